```python
import jax, jax.numpy as jnp
from jax import lax
import numpy as np

D_MODEL = 1024
BATCH = 8
SEQ = 4096
DEPTH = 1
DEC_BATCH = 128
DEC_SEQ = 8
PAST_LEN = 16384
PAGE_SIZE = 128

MIX_WIDTH = D_MODEL
HG_WIDTH = MIX_WIDTH // 2
HG_EXPAND = 128
HG_HEADS = HG_WIDTH // HG_EXPAND
HG_DK = HG_EXPAND
HG_DV = HG_WIDTH // HG_HEADS
HG_CHUNK = 32
SWA_WIDTH = MIX_WIDTH - HG_WIDTH
SWA_HEAD_DIM = 64
SWA_Q_HEADS = SWA_WIDTH // SWA_HEAD_DIM
SWA_KV_HEADS = 2
SWA_GROUP = SWA_Q_HEADS // SWA_KV_HEADS
WINDOW = 128
SWA_SCALE = SWA_HEAD_DIM ** -0.5
D_FF = 4 * D_MODEL
EPS = 1e-6
IN_SPLITS = (HG_WIDTH, HG_WIDTH, HG_WIDTH, HG_WIDTH,
             SWA_Q_HEADS * SWA_HEAD_DIM, SWA_KV_HEADS * SWA_HEAD_DIM, SWA_KV_HEADS * SWA_HEAD_DIM)
N_IN = sum(IN_SPLITS)

kernel_name = 'hymba_hgrn2_swa_sink_decoder_step'


def _rmsnorm(x, g):
    xf = x.astype(jnp.float32)
    y = xf * lax.rsqrt(jnp.mean(xf * xf, axis=-1, keepdims=True) + EPS)
    return (y * g.astype(jnp.float32)).astype(x.dtype)


def _split(z):
    outs, off = [], 0
    for w in IN_SPLITS:
        outs.append(z[..., off:off + w])
        off += w
    return outs


def _gla_chunked(q, k, v, log_f, S0):
    B, T, H, DK = q.shape
    DV = v.shape[-1]
    L = min(HG_CHUNK, T)
    pad = (-T) % L
    if pad:
        pw = ((0, 0), (0, pad), (0, 0), (0, 0))
        q, k, v, log_f = [jnp.pad(a, pw) for a in (q, k, v, log_f)]
    NC = (T + pad) // L

    def blocks(a):
        return a.reshape(B, NC, L, H, a.shape[-1]).transpose(1, 0, 3, 2, 4)

    q, k, v, g = blocks(q), blocks(k), blocks(v), blocks(log_f)
    b = jnp.cumsum(g, axis=3)
    b_last = b[:, :, :, -1:, :]
    qt = q * jnp.exp(b)
    kt = k * jnp.exp(-b)
    ke = k * jnp.exp(b_last - b)
    decay = jnp.exp(b_last[:, :, :, 0, :])
    causal = jnp.tril(jnp.ones((L, L), dtype=bool))
    A = jnp.where(causal, jnp.einsum('nbhtd,nbhsd->nbhts', qt, kt), 0.0)
    o_intra = jnp.einsum('nbhts,nbhsv->nbhtv', A, v)

    def step(S, xs):
        qt_n, ke_n, v_n, dec_n = xs
        o_n = jnp.einsum('bhtd,bhdv->bhtv', qt_n, S)
        S = dec_n[..., None] * S + jnp.einsum('bhsd,bhsv->bhdv', ke_n, v_n)
        return S, o_n

    S_fin, o_inter = lax.scan(step, S0, (qt, ke, v, decay))
    o = (o_intra + o_inter).transpose(1, 0, 3, 2, 4).reshape(B, NC * L, H, DV)[:, :T]
    return o, S_fin


def _hgrn2(zq, zf, zi, zg, lb, g_norm, S0):
    B, T, _ = zq.shape
    f32 = jnp.float32
    q = jax.nn.silu(zq.astype(f32)).reshape(B, T, HG_HEADS, HG_DK)
    fl = zf.astype(f32).reshape(B, T, HG_HEADS, HG_DK)
    lb = lb.astype(f32)
    f = lb + (1.0 - lb) * jax.nn.sigmoid(fl)
    k = (1.0 - lb) * jax.nn.sigmoid(-fl)
    v = zi.astype(f32).reshape(B, T, HG_HEADS, HG_DV)
    o, S = _gla_chunked(q, k, v, jnp.log(f), S0.astype(f32))
    o = _rmsnorm(o, g_norm)
    o = o.reshape(B, T, HG_WIDTH) * jax.nn.silu(zg.astype(f32))
    return o.astype(zq.dtype), S.astype(S0.dtype)


def _sink_softmax(s, mask, sink):
    s = jnp.where(mask, s, -jnp.inf)
    m = jnp.maximum(jnp.max(s, axis=-1, keepdims=True), sink)
    p = jnp.exp(s - m)
    return p / (jnp.sum(p, axis=-1, keepdims=True) + jnp.exp(sink - m))


def _swa_prompt(q, k, v, sinks):
    B, T = q.shape[:2]
    W = WINDOW
    NB = T // W
    f32 = jnp.float32
    qb = q.astype(f32).reshape(B, NB, W, SWA_KV_HEADS, SWA_GROUP, SWA_HEAD_DIM)
    kb = k.astype(f32).reshape(B, NB, W, SWA_KV_HEADS, SWA_HEAD_DIM)
    vb = v.astype(f32).reshape(B, NB, W, SWA_KV_HEADS, SWA_HEAD_DIM)
    prev = lambda a: jnp.pad(a[:, :-1], ((0, 0), (1, 0), (0, 0), (0, 0), (0, 0)))
    kk = jnp.concatenate([prev(kb), kb], axis=2)
    vv = jnp.concatenate([prev(vb), vb], axis=2)
    s = jnp.einsum('bnqkgd,bnskd->bnkgqs', qb, kk) * SWA_SCALE
    n = jnp.arange(NB)[:, None, None]
    i = jnp.arange(W)[None, :, None]
    j = jnp.arange(2 * W)[None, None, :]
    kpos = (n - 1) * W + j
    d = n * W + i - kpos
    mask = ((d >= 0) & (d < WINDOW) & (kpos >= 0))[None, :, None, None]
    sink = sinks.astype(f32).reshape(SWA_KV_HEADS, SWA_GROUP)[None, None, :, :, None, None]
    p = _sink_softmax(s, mask, sink)
    o = jnp.einsum('bnkgqs,bnskd->bnqkgd', p, vv)
    return o.reshape(B, T, SWA_WIDTH).astype(q.dtype)


def _swa_sample(q, k_new, v_new, ck, cv, sinks):
    DB, S = q.shape[:2]
    WB = ck.shape[1]
    f32 = jnp.float32
    kk = jnp.concatenate([ck, k_new.astype(ck.dtype)], axis=1)
    vv = jnp.concatenate([cv, v_new.astype(cv.dtype)], axis=1)
    qpos = PAST_LEN + jnp.arange(S)
    kpos = jnp.concatenate([PAST_LEN - WB + jnp.arange(WB), PAST_LEN + jnp.arange(S)])
    d = qpos[:, None] - kpos[None, :]
    mask = ((d >= 0) & (d < WINDOW))[None, None, None]
    qg = q.astype(f32).reshape(DB, S, SWA_KV_HEADS, SWA_GROUP, SWA_HEAD_DIM)
    s = jnp.einsum('bqkgd,bskd->bkgqs', qg, kk.astype(f32)) * SWA_SCALE
    sink = sinks.astype(f32).reshape(SWA_KV_HEADS, SWA_GROUP)[None, :, :, None, None]
    p = _sink_softmax(s, mask, sink)
    o = jnp.einsum('bkgqs,bskd->bqkgd', p, vv.astype(f32))
    return o.reshape(DB, S, SWA_WIDTH).astype(q.dtype), kk[:, -WB:], vv[:, -WB:]


def _layer(h, ln_mix, w_in, lb, hg_norm, sinks, w_out, ln_mlp, w_up, w_down, S0, ck=None, cv=None):
    B, T, _ = h.shape
    xn = _rmsnorm(h, ln_mix)
    zq, zf, zi, zg, sq, sk, sv = _split(xn @ w_in)
    o_hg, S_new = _hgrn2(zq, zf, zi, zg, lb, hg_norm, S0)
    q = sq.reshape(B, T, SWA_Q_HEADS, SWA_HEAD_DIM)
    k = sk.reshape(B, T, SWA_KV_HEADS, SWA_HEAD_DIM)
    v = sv.reshape(B, T, SWA_KV_HEADS, SWA_HEAD_DIM)
    if ck is None:
        o_swa = _swa_prompt(q, k, v, sinks)
        wb = min(WINDOW, T)
        k_buf, v_buf = k[:, -wb:], v[:, -wb:]
    else:
        o_swa, k_buf, v_buf = _swa_sample(q, k, v, ck, cv, sinks)
    h = h + jnp.concatenate([o_hg, o_swa], axis=-1) @ w_out
    u = _rmsnorm(h, ln_mlp) @ w_up
    h = h + jnp.square(jax.nn.relu(u)) @ w_down
    return h, S_new, k_buf, v_buf


def setup_inputs(seed: int = 0) -> dict:
    key = jax.random.key(seed)
    ks = jax.random.split(key, 16)
    WB = min(WINDOW, PAST_LEN)
    nrm = jax.random.normal
    f32 = jnp.float32
    return {
        'x_prompt': nrm(ks[0], (BATCH, SEQ, D_MODEL), f32),
        'x_sample': nrm(ks[1], (DEC_BATCH, DEC_SEQ, D_MODEL), f32),
        'state_hgrn': 0.5 * nrm(ks[2], (DEPTH, DEC_BATCH, HG_HEADS, HG_DK, HG_DV), f32),
        'cache_swa_k': nrm(ks[3], (DEPTH, DEC_BATCH, WB, SWA_KV_HEADS, SWA_HEAD_DIM), f32),
        'cache_swa_v': nrm(ks[4], (DEPTH, DEC_BATCH, WB, SWA_KV_HEADS, SWA_HEAD_DIM), f32),
        'ln_mix': 1.0 + 0.02 * nrm(ks[5], (DEPTH, D_MODEL), f32),
        'w_in': nrm(ks[6], (DEPTH, D_MODEL, N_IN), f32) * D_MODEL ** -0.5,
        'lb_logits': 0.1 * nrm(ks[7], (DEPTH + 1, HG_HEADS * HG_DK), f32),
        'hg_norm': 1.0 + 0.02 * nrm(ks[8], (DEPTH, HG_DV), f32),
        'sinks': nrm(ks[9], (DEPTH, SWA_Q_HEADS), f32),
        'w_out': nrm(ks[10], (DEPTH, MIX_WIDTH, D_MODEL), f32) * MIX_WIDTH ** -0.5,
        'ln_mlp': 1.0 + 0.02 * nrm(ks[11], (DEPTH, D_MODEL), f32),
        'w_up': nrm(ks[12], (DEPTH, D_MODEL, D_FF), f32) * D_MODEL ** -0.5,
        'w_down': nrm(ks[13], (DEPTH, D_FF, D_MODEL), f32) * D_FF ** -0.5,
        'ln_final': 1.0 + 0.02 * nrm(ks[14], (D_MODEL,), f32),
    }


def reference(x_prompt, x_sample, state_hgrn, cache_swa_k, cache_swa_v, ln_mix, w_in, lb_logits,
              hg_norm, sinks, w_out, ln_mlp, w_up, w_down, ln_final):
    lbs = jnp.cumsum(jax.nn.softmax(lb_logits.astype(jnp.float32), axis=0), axis=0)
    hp, hs = x_prompt, x_sample
    sp_l, kp_l, vp_l, ss_l, ks_l, vs_l = [], [], [], [], [], []
    for l in range(DEPTH):
        lb = lbs[l].reshape(HG_HEADS, HG_DK)
        w = (ln_mix[l], w_in[l], lb, hg_norm[l], sinks[l], w_out[l], ln_mlp[l], w_up[l], w_down[l])
        S0p = jnp.zeros((hp.shape[0], HG_HEADS, HG_DK, HG_DV), state_hgrn.dtype)
        hp, sp, kp, vp = _layer(hp, *w, S0p)
        hs, ss, kss, vss = _layer(hs, *w, state_hgrn[l], cache_swa_k[l], cache_swa_v[l])
        sp_l.append(sp); kp_l.append(kp.astype(cache_swa_k.dtype)); vp_l.append(vp.astype(cache_swa_v.dtype))
        ss_l.append(ss); ks_l.append(kss); vs_l.append(vss)
    y_prompt = _rmsnorm(hp, ln_final)
    y_sample = _rmsnorm(hs, ln_final)
    return (y_prompt, y_sample, jnp.stack(sp_l), jnp.stack(kp_l), jnp.stack(vp_l),
            jnp.stack(ss_l), jnp.stack(ks_l), jnp.stack(vs_l))
```

```python
import functools

import jax
import jax.numpy as jnp
from jax import lax
from jax.experimental import pallas as pl
from jax.experimental.pallas import tpu as pltpu

F32 = jnp.float32
BF16 = jnp.bfloat16

D_MODEL = 1024
HG_WIDTH = 512
HG_HEADS = 4
HG_DK = 128
HG_DV = 128
SWA_WIDTH = 512
SWA_HEAD_DIM = 64
SWA_Q_HEADS = 8
SWA_KV_HEADS = 2
SWA_GROUP = SWA_Q_HEADS // SWA_KV_HEADS
SWA_KV_WIDTH = SWA_KV_HEADS * SWA_HEAD_DIM
WINDOW = 128
SWA_SCALE = SWA_HEAD_DIM ** -0.5
D_FF = 4 * D_MODEL
EPS = 1e-6
N_HG_IN = 4 * HG_WIDTH
N_SWA_IN = SWA_WIDTH + 2 * SWA_KV_WIDTH
N_IN = N_HG_IN + N_SWA_IN
NEG_BIG = -1e30

VMEM_LIMIT_BYTES = 56 * 1024 * 1024


def _rms(x, g):
    return x * lax.rsqrt(jnp.mean(x * x, axis=-1, keepdims=True) + EPS) * g


def _sigmoid(x):
    return 1.0 / (1.0 + jnp.exp(-x))


def _dot(a, b):
    return jnp.dot(a.astype(BF16), b.astype(BF16), preferred_element_type=F32)


def _dot_nt(a, b):
    return lax.dot_general(a.astype(BF16), b.astype(BF16), (((1,), (1,)), ((), ())),
                           preferred_element_type=F32)


def _dot_tn(a, b):
    return lax.dot_general(a.astype(BF16), b.astype(BF16), (((0,), (0,)), ((), ())),
                           preferred_element_type=F32)


def _inproj_body(x_ref, g_ref, w_ref, zh_ref, zs_ref):
    xn = _rms(x_ref[...], g_ref[...]).astype(BF16)
    zh_ref[...] = jnp.dot(xn, w_ref[:, :N_HG_IN], preferred_element_type=F32)
    zs_ref[...] = jnp.dot(xn, w_ref[:, N_HG_IN:], preferred_element_type=F32)


def _inproj(x2d, ln, w_in_bf16, tm):
    n = x2d.shape[0]
    return pl.pallas_call(
        _inproj_body,
        grid=(n // tm,),
        in_specs=[
            pl.BlockSpec((tm, D_MODEL), lambda i: (i, 0)),
            pl.BlockSpec((1, D_MODEL), lambda i: (0, 0)),
            pl.BlockSpec((D_MODEL, N_IN), lambda i: (0, 0)),
        ],
        out_specs=[
            pl.BlockSpec((tm, N_HG_IN), lambda i: (i, 0)),
            pl.BlockSpec((tm, N_SWA_IN), lambda i: (i, 0)),
        ],
        out_shape=[
            jax.ShapeDtypeStruct((n, N_HG_IN), F32),
            jax.ShapeDtypeStruct((n, N_SWA_IN), F32),
        ],
        compiler_params=pltpu.CompilerParams(
            dimension_semantics=("arbitrary",), vmem_limit_bytes=VMEM_LIMIT_BYTES),
        name="inproj",
    )(x2d, ln, w_in_bf16)


def _cumsum_rows(x):
    n = x.shape[0]
    row = lax.broadcasted_iota(jnp.int32, x.shape, 0)
    s = 1
    while s < n:
        x = x + jnp.where(row >= s, pltpu.roll(x, s, axis=0), 0.0)
        s *= 2
    return x


def _lower_bound(lb_logits):
    m = jnp.max(lb_logits, axis=0, keepdims=True)
    e = jnp.exp(lb_logits - m)
    return e[0:1, :] / jnp.sum(e, axis=0, keepdims=True)


def _gla_chunk(zh, lb, g_norm, st_ref, j, chunk):
    zq = zh[:, 0:HG_WIDTH]
    zf = zh[:, HG_WIDTH:2 * HG_WIDTH]
    v = zh[:, 2 * HG_WIDTH:3 * HG_WIDTH]
    zg = zh[:, 3 * HG_WIDTH:4 * HG_WIDTH]

    q = zq * _sigmoid(zq)
    t = jnp.exp(-jnp.abs(zf))
    r = 1.0 / (1.0 + t)
    pos = zf >= 0.0
    sig = jnp.where(pos, r, t * r)
    nsig = jnp.where(pos, t * r, r)
    f = lb + (1.0 - lb) * sig
    k = (1.0 - lb) * nsig
    b = _cumsum_rows(jnp.log(f))
    mid = chunk // 2 - 1
    b_mid = b[mid:mid + 1, :]
    b_last = b[chunk - 1:chunk, :]
    qt = q * jnp.exp(b)
    qf = q * jnp.exp(b - b_mid)
    kf = k * jnp.exp(b_mid - b)
    ke = k * jnp.exp(b_last - b)
    dec = jnp.exp(b_last)

    row = lax.broadcasted_iota(jnp.int32, (chunk, chunk), 0)
    col = lax.broadcasted_iota(jnp.int32, (chunk, chunk), 1)
    causal = row >= col
    gate = zg * _sigmoid(zg)

    outs = []
    for h in range(HG_HEADS):
        sl = slice(h * HG_DK, (h + 1) * HG_DK)
        v_h = v[:, sl]
        a = jnp.where(causal, _dot_nt(qf[:, sl], kf[:, sl]), 0.0)
        st = st_ref[j, h]
        o = _dot(a, v_h) + _dot_nt(qt[:, sl], st)
        st_ref[j, h] = st * dec[:, sl] + _dot_tn(v_h, ke[:, sl])
        outs.append(_rms(o, g_norm) * gate[:, sl])
    return jnp.concatenate(outs, axis=-1)


def _gla_body(*refs, nb, tb, chunk, has_s0):
    if has_s0:
        zh_ref, lbl_ref, gn_ref, s0_ref, o_ref, so_ref, st_ref = refs
    else:
        zh_ref, lbl_ref, gn_ref, o_ref, so_ref, st_ref = refs
        s0_ref = None
    t_idx = pl.program_id(1)
    n_t = pl.num_programs(1)

    @pl.when(t_idx == 0)
    def _init():
        for j in range(nb):
            for h in range(HG_HEADS):
                if has_s0:
                    st_ref[j, h] = s0_ref[j, h].T
                else:
                    st_ref[j, h] = jnp.zeros((HG_DV, HG_DK), F32)

    lb = _lower_bound(lbl_ref[...])
    g_norm = gn_ref[...]
    n_chunks = tb // chunk
    for j in range(nb):
        if n_chunks == 1:
            o_ref[j] = _gla_chunk(zh_ref[j], lb, g_norm, st_ref, j, chunk)
        else:
            def step(c, carry, j=j):
                r0 = pl.multiple_of(c * chunk, chunk)
                o_ref[j, pl.ds(r0, chunk), :] = _gla_chunk(
                    zh_ref[j, pl.ds(r0, chunk), :], lb, g_norm, st_ref, j, chunk)
                return carry
            lax.fori_loop(0, n_chunks, step, 0)

    @pl.when(t_idx == n_t - 1)
    def _fin():
        for j in range(nb):
            for h in range(HG_HEADS):
                so_ref[j, h] = st_ref[j, h].T


def _gla(zh3d, lb_logits, hg_norm, s0, nb, tb, chunk):
    bsz, t, _ = zh3d.shape
    has_s0 = s0 is not None
    in_specs = [
        pl.BlockSpec((nb, tb, N_HG_IN), lambda b, i: (b, i, 0)),
        pl.BlockSpec(lb_logits.shape, lambda b, i: (0, 0)),
        pl.BlockSpec((1, HG_DV), lambda b, i: (0, 0)),
    ]
    args = [zh3d, lb_logits, hg_norm]
    if has_s0:
        in_specs.append(pl.BlockSpec((nb, HG_HEADS, HG_DK, HG_DV), lambda b, i: (b, 0, 0, 0)))
        args.append(s0)
    return pl.pallas_call(
        functools.partial(_gla_body, nb=nb, tb=tb, chunk=chunk, has_s0=has_s0),
        grid=(bsz // nb, t // tb),
        in_specs=in_specs,
        out_specs=[
            pl.BlockSpec((nb, tb, HG_WIDTH), lambda b, i: (b, i, 0)),
            pl.BlockSpec((nb, HG_HEADS, HG_DK, HG_DV), lambda b, i: (b, 0, 0, 0)),
        ],
        out_shape=[
            jax.ShapeDtypeStruct((bsz, t, HG_WIDTH), F32),
            jax.ShapeDtypeStruct((bsz, HG_HEADS, HG_DK, HG_DV), F32),
        ],
        scratch_shapes=[pltpu.VMEM((nb, HG_HEADS, HG_DV, HG_DK), F32)],
        compiler_params=pltpu.CompilerParams(
            dimension_semantics=("arbitrary", "arbitrary"), vmem_limit_bytes=VMEM_LIMIT_BYTES),
        name="gla_s0" if has_s0 else "gla",
    )(*args)


def _sink_attend(q_h, kk, vv, mask, sink):
    s = jnp.where(mask, _dot_nt(q_h, kk), NEG_BIG)
    m = jnp.maximum(jnp.max(s, axis=-1, keepdims=True), sink)
    p = jnp.where(mask, jnp.exp(s - m), 0.0)
    den = jnp.sum(p, axis=-1, keepdims=True) + jnp.exp(sink - m)
    return _dot(p, vv) / den


def _swa_prompt_body(sink_ref, zc_ref, zp_ref, o_ref):
    n = pl.program_id(1)
    zc = zc_ref[0]
    zp = zp_ref[0]
    q = zc[:, :SWA_WIDTH] * SWA_SCALE
    i = lax.broadcasted_iota(jnp.int32, (WINDOW, 2 * WINDOW), 0)
    j = lax.broadcasted_iota(jnp.int32, (WINDOW, 2 * WINDOW), 1)
    mask = (j > i) & (j <= i + WINDOW) & ((j >= WINDOW) | (n > 0))
    for kh in range(SWA_KV_HEADS):
        ks = slice(SWA_WIDTH + kh * SWA_HEAD_DIM, SWA_WIDTH + (kh + 1) * SWA_HEAD_DIM)
        vs = slice(SWA_WIDTH + SWA_KV_WIDTH + kh * SWA_HEAD_DIM,
                   SWA_WIDTH + SWA_KV_WIDTH + (kh + 1) * SWA_HEAD_DIM)
        kk = jnp.concatenate([zp[:, ks], zc[:, ks]], axis=0)
        vv = jnp.concatenate([zp[:, vs], zc[:, vs]], axis=0)
        for g in range(SWA_GROUP):
            h = kh * SWA_GROUP + g
            hs = slice(h * SWA_HEAD_DIM, (h + 1) * SWA_HEAD_DIM)
            o_ref[0, :, hs] = _sink_attend(q[:, hs], kk, vv, mask, sink_ref[h])


def _swa_prompt(zs3d, sinks):
    bsz, t, _ = zs3d.shape
    nblk = t // WINDOW
    return pl.pallas_call(
        _swa_prompt_body,
        grid=(bsz, nblk),
        in_specs=[
            pl.BlockSpec(memory_space=pltpu.SMEM),
            pl.BlockSpec((1, WINDOW, N_SWA_IN), lambda b, n: (b, n, 0)),
            pl.BlockSpec((1, WINDOW, N_SWA_IN), lambda b, n: (b, jnp.maximum(n - 1, 0), 0)),
        ],
        out_specs=pl.BlockSpec((1, WINDOW, SWA_WIDTH), lambda b, n: (b, n, 0)),
        out_shape=jax.ShapeDtypeStruct((bsz, t, SWA_WIDTH), F32),
        compiler_params=pltpu.CompilerParams(
            dimension_semantics=("arbitrary", "arbitrary"), vmem_limit_bytes=VMEM_LIMIT_BYTES),
        name="swa_prompt",
    )(sinks, zs3d, zs3d)


def _swa_sample_body(sink_ref, zs_ref, ck_ref, cv_ref, o_ref, cko_ref, cvo_ref, *, nb, s_new, wb):
    mask_c = (lax.broadcasted_iota(jnp.int32, (s_new, wb), 1)
              > lax.broadcasted_iota(jnp.int32, (s_new, wb), 0) + (wb - WINDOW))
    mask_n = (lax.broadcasted_iota(jnp.int32, (s_new, s_new), 1)
              <= lax.broadcasted_iota(jnp.int32, (s_new, s_new), 0))
    for b in range(nb):
        z = zs_ref[b]
        q = z[:, :SWA_WIDTH] * SWA_SCALE
        k_new = z[:, SWA_WIDTH:SWA_WIDTH + SWA_KV_WIDTH]
        v_new = z[:, SWA_WIDTH + SWA_KV_WIDTH:]
        ck = ck_ref[b]
        cv = cv_ref[b]
        cko_ref[b, 0:wb - s_new, :] = ck[s_new:, :]
        cko_ref[b, wb - s_new:, :] = k_new
        cvo_ref[b, 0:wb - s_new, :] = cv[s_new:, :]
        cvo_ref[b, wb - s_new:, :] = v_new
        for kh in range(SWA_KV_HEADS):
            cs = slice(kh * SWA_HEAD_DIM, (kh + 1) * SWA_HEAD_DIM)
            for g in range(SWA_GROUP):
                h = kh * SWA_GROUP + g
                hs = slice(h * SWA_HEAD_DIM, (h + 1) * SWA_HEAD_DIM)
                sink = sink_ref[h]
                q_h = q[:, hs]
                sc = jnp.where(mask_c, _dot_nt(q_h, ck[:, cs]), NEG_BIG)
                sn = jnp.where(mask_n, _dot_nt(q_h, k_new[:, cs]), NEG_BIG)
                m = jnp.maximum(jnp.maximum(jnp.max(sc, axis=-1, keepdims=True),
                                            jnp.max(sn, axis=-1, keepdims=True)), sink)
                pc = jnp.where(mask_c, jnp.exp(sc - m), 0.0)
                pn = jnp.where(mask_n, jnp.exp(sn - m), 0.0)
                den = (jnp.sum(pc, axis=-1, keepdims=True) + jnp.sum(pn, axis=-1, keepdims=True)
                       + jnp.exp(sink - m))
                o_ref[b, :, hs] = (_dot(pc, cv[:, cs]) + _dot(pn, v_new[:, cs])) / den


def _swa_sample(zs3d, ck, cv, sinks, nb):
    bsz, s_new, _ = zs3d.shape
    wb = ck.shape[1]
    cache_spec = pl.BlockSpec((nb, wb, SWA_KV_WIDTH), lambda b: (b, 0, 0))
    return pl.pallas_call(
        functools.partial(_swa_sample_body, nb=nb, s_new=s_new, wb=wb),
        grid=(bsz // nb,),
        in_specs=[
            pl.BlockSpec(memory_space=pltpu.SMEM),
            pl.BlockSpec((nb, s_new, N_SWA_IN), lambda b: (b, 0, 0)),
            cache_spec,
            cache_spec,
        ],
        out_specs=[
            pl.BlockSpec((nb, s_new, SWA_WIDTH), lambda b: (b, 0, 0)),
            cache_spec,
            cache_spec,
        ],
        out_shape=[
            jax.ShapeDtypeStruct((bsz, s_new, SWA_WIDTH), F32),
            jax.ShapeDtypeStruct(ck.shape, ck.dtype),
            jax.ShapeDtypeStruct(cv.shape, cv.dtype),
        ],
        compiler_params=pltpu.CompilerParams(
            dimension_semantics=("arbitrary",), vmem_limit_bytes=VMEM_LIMIT_BYTES),
        name="swa_sample",
    )(sinks, zs3d, ck, cv)


def _out_mlp_body(x_ref, oh_ref, os_ref, wo_ref, lm_ref, wu_ref, wd_ref, lf_ref, y_ref):
    o = jnp.concatenate([oh_ref[...].astype(BF16), os_ref[...].astype(BF16)], axis=-1)
    h = x_ref[...] + jnp.dot(o, wo_ref[...], preferred_element_type=F32)
    hn = _rms(h, lm_ref[...]).astype(BF16)
    u = jnp.dot(hn, wu_ref[...], preferred_element_type=F32)
    a = jnp.square(jnp.maximum(u, 0.0)).astype(BF16)
    y_ref[...] = _rms(h + jnp.dot(a, wd_ref[...], preferred_element_type=F32), lf_ref[...])


def _out_mlp(x2d, oh2d, os2d, w_out, ln_mlp, w_up, w_down, ln_final, tm):
    n = x2d.shape[0]
    const = lambda i: (0, 0)
    single = pl.Buffered(1)
    return pl.pallas_call(
        _out_mlp_body,
        grid=(n // tm,),
        in_specs=[
            pl.BlockSpec((tm, D_MODEL), lambda i: (i, 0)),
            pl.BlockSpec((tm, HG_WIDTH), lambda i: (i, 0)),
            pl.BlockSpec((tm, SWA_WIDTH), lambda i: (i, 0)),
            pl.BlockSpec((D_MODEL, D_MODEL), const, pipeline_mode=single),
            pl.BlockSpec((1, D_MODEL), const),
            pl.BlockSpec((D_MODEL, D_FF), const, pipeline_mode=single),
            pl.BlockSpec((D_FF, D_MODEL), const, pipeline_mode=single),
            pl.BlockSpec((1, D_MODEL), const),
        ],
        out_specs=pl.BlockSpec((tm, D_MODEL), lambda i: (i, 0)),
        out_shape=jax.ShapeDtypeStruct((n, D_MODEL), F32),
        compiler_params=pltpu.CompilerParams(
            dimension_semantics=("arbitrary",), vmem_limit_bytes=VMEM_LIMIT_BYTES),
        name="out_mlp",
    )(x2d, oh2d, os2d, w_out, ln_mlp, w_up, w_down, ln_final)


def kernel(x_prompt, x_sample, state_hgrn, cache_swa_k, cache_swa_v, ln_mix, w_in, lb_logits,
           hg_norm, sinks, w_out, ln_mlp, w_up, w_down, ln_final):
    depth = w_in.shape[0]
    assert depth == 1 and lb_logits.shape[0] == depth + 1
    bsz, seq, _ = x_prompt.shape
    dbsz, dseq, _ = x_sample.shape
    wb = cache_swa_k.shape[2]

    w_in_b = w_in[0].astype(BF16)
    w_out_b = w_out[0].astype(BF16)
    w_up_b = w_up[0].astype(BF16)
    w_down_b = w_down[0].astype(BF16)
    ln_mix2 = ln_mix[0].reshape(1, D_MODEL)
    ln_mlp2 = ln_mlp[0].reshape(1, D_MODEL)
    ln_fin2 = ln_final.reshape(1, D_MODEL)
    gn2 = hg_norm[0].reshape(1, HG_DV)
    sink1 = sinks[0]

    def mix_and_mlp(x2d, oh, osw, tm):
        return _out_mlp(x2d, oh, osw, w_out_b, ln_mlp2, w_up_b, w_down_b, ln_fin2, tm)

    xp = x_prompt.reshape(bsz * seq, D_MODEL)
    zh_p, zs_p = _inproj(xp, ln_mix2, w_in_b, 512)
    oh_p, s_p = _gla(zh_p.reshape(bsz, seq, N_HG_IN), lb_logits, gn2, None, 1, 512, 64)
    zs_p3 = zs_p.reshape(bsz, seq, N_SWA_IN)
    os_p = _swa_prompt(zs_p3, sink1)
    y_p = mix_and_mlp(xp, oh_p.reshape(-1, HG_WIDTH), os_p.reshape(-1, SWA_WIDTH), 512)
    wp = min(WINDOW, seq)
    k_p = zs_p3[:, seq - wp:, SWA_WIDTH:SWA_WIDTH + SWA_KV_WIDTH]
    v_p = zs_p3[:, seq - wp:, SWA_WIDTH + SWA_KV_WIDTH:]

    xs = x_sample.reshape(dbsz * dseq, D_MODEL)
    zh_s, zs_s = _inproj(xs, ln_mix2, w_in_b, 512)
    oh_s, s_s = _gla(zh_s.reshape(dbsz, dseq, N_HG_IN), lb_logits, gn2, state_hgrn[0], 8, dseq, dseq)
    os_s, ck_s, cv_s = _swa_sample(
        zs_s.reshape(dbsz, dseq, N_SWA_IN),
        cache_swa_k[0].reshape(dbsz, wb, SWA_KV_WIDTH),
        cache_swa_v[0].reshape(dbsz, wb, SWA_KV_WIDTH), sink1, 8)
    y_s = mix_and_mlp(xs, oh_s.reshape(-1, HG_WIDTH), os_s.reshape(-1, SWA_WIDTH), 512)

    kv_shape_p = (1, bsz, wp, SWA_KV_HEADS, SWA_HEAD_DIM)
    kv_shape_s = (1, dbsz, wb, SWA_KV_HEADS, SWA_HEAD_DIM)
    return (y_p.reshape(bsz, seq, D_MODEL),
            y_s.reshape(dbsz, dseq, D_MODEL),
            s_p[None],
            k_p.reshape(kv_shape_p).astype(cache_swa_k.dtype),
            v_p.reshape(kv_shape_p).astype(cache_swa_v.dtype),
            s_s[None],
            ck_s.reshape(kv_shape_s),
            cv_s.reshape(kv_shape_s))
```

```python
import functools

import jax
import jax.numpy as jnp
from jax import lax
from jax.experimental import pallas as pl
from jax.experimental.pallas import tpu as pltpu

F32 = jnp.float32
BF16 = jnp.bfloat16

D_MODEL = 1024
HG_WIDTH = 512
HG_HEADS = 4
HG_DK = 128
HG_DV = 128
SWA_WIDTH = 512
SWA_HEAD_DIM = 64
SWA_Q_HEADS = 8
SWA_KV_HEADS = 2
SWA_GROUP = SWA_Q_HEADS // SWA_KV_HEADS
SWA_KV_WIDTH = SWA_KV_HEADS * SWA_HEAD_DIM
WINDOW = 128
SWA_SCALE = SWA_HEAD_DIM ** -0.5
D_FF = 4 * D_MODEL
EPS = 1e-6
N_HG_IN = 4 * HG_WIDTH
N_SWA_IN = SWA_WIDTH + 2 * SWA_KV_WIDTH
N_IN = N_HG_IN + N_SWA_IN
NEG_BIG = -1e30

VMEM_LIMIT_BYTES = 56 * 1024 * 1024


def _rms(x, g):
    return x * lax.rsqrt(jnp.mean(x * x, axis=-1, keepdims=True) + EPS) * g


def _sigmoid(x):
    return 1.0 / (1.0 + jnp.exp(-x))


def _dot(a, b):
    return jnp.dot(a.astype(BF16), b.astype(BF16), preferred_element_type=F32)


def _dot_nt(a, b):
    return lax.dot_general(a.astype(BF16), b.astype(BF16), (((1,), (1,)), ((), ())),
                           preferred_element_type=F32)


def _dot_tn(a, b):
    return lax.dot_general(a.astype(BF16), b.astype(BF16), (((0,), (0,)), ((), ())),
                           preferred_element_type=F32)


def _inproj_body(x_ref, g_ref, w_ref, zh_ref, zs_ref):
    xn = _rms(x_ref[...], g_ref[...]).astype(BF16)
    zh_ref[...] = jnp.dot(xn, w_ref[:, :N_HG_IN], preferred_element_type=F32)
    zs_ref[...] = jnp.dot(xn, w_ref[:, N_HG_IN:], preferred_element_type=F32)


def _inproj(x2d, ln, w_in_bf16, tm):
    n = x2d.shape[0]
    return pl.pallas_call(
        _inproj_body,
        grid=(n // tm,),
        in_specs=[
            pl.BlockSpec((tm, D_MODEL), lambda i: (i, 0)),
            pl.BlockSpec((1, D_MODEL), lambda i: (0, 0)),
            pl.BlockSpec((D_MODEL, N_IN), lambda i: (0, 0)),
        ],
        out_specs=[
            pl.BlockSpec((tm, N_HG_IN), lambda i: (i, 0)),
            pl.BlockSpec((tm, N_SWA_IN), lambda i: (i, 0)),
        ],
        out_shape=[
            jax.ShapeDtypeStruct((n, N_HG_IN), F32),
            jax.ShapeDtypeStruct((n, N_SWA_IN), F32),
        ],
        compiler_params=pltpu.CompilerParams(
            dimension_semantics=("arbitrary",), vmem_limit_bytes=VMEM_LIMIT_BYTES),
        name="inproj",
    )(x2d, ln, w_in_bf16)


def _cumsum_rows(x):
    n = x.shape[0]
    row = lax.broadcasted_iota(jnp.int32, x.shape, 0)
    s = 1
    while s < n:
        x = x + jnp.where(row >= s, pltpu.roll(x, s, axis=0), 0.0)
        s *= 2
    return x


def _lower_bound(lb_logits):
    m = jnp.max(lb_logits, axis=0, keepdims=True)
    e = jnp.exp(lb_logits - m)
    return e[0:1, :] / jnp.sum(e, axis=0, keepdims=True)


def _gla_chunk(zh, lb, g_norm, st_ref, j, chunk):
    zq = zh[:, 0:HG_WIDTH]
    zf = zh[:, HG_WIDTH:2 * HG_WIDTH]
    v = zh[:, 2 * HG_WIDTH:3 * HG_WIDTH]
    zg = zh[:, 3 * HG_WIDTH:4 * HG_WIDTH]

    q = zq * _sigmoid(zq)
    t = jnp.exp(-jnp.abs(zf))
    r = 1.0 / (1.0 + t)
    pos = zf >= 0.0
    sig = jnp.where(pos, r, t * r)
    nsig = jnp.where(pos, t * r, r)
    f = lb + (1.0 - lb) * sig
    k = (1.0 - lb) * nsig
    b = _cumsum_rows(jnp.log(f))
    mid = chunk // 2 - 1
    b_mid = b[mid:mid + 1, :]
    b_last = b[chunk - 1:chunk, :]
    qt = q * jnp.exp(b)
    qf = q * jnp.exp(b - b_mid)
    kf = k * jnp.exp(b_mid - b)
    ke = k * jnp.exp(b_last - b)
    dec = jnp.exp(b_last)

    row = lax.broadcasted_iota(jnp.int32, (chunk, chunk), 0)
    col = lax.broadcasted_iota(jnp.int32, (chunk, chunk), 1)
    causal = row >= col
    gate = zg * _sigmoid(zg)

    outs = []
    for h in range(HG_HEADS):
        sl = slice(h * HG_DK, (h + 1) * HG_DK)
        v_h = v[:, sl]
        a = jnp.where(causal, _dot_nt(qf[:, sl], kf[:, sl]), 0.0)
        st = st_ref[j, h]
        o = _dot(a, v_h) + _dot_nt(qt[:, sl], st)
        st_ref[j, h] = st * dec[:, sl] + _dot_tn(v_h, ke[:, sl])
        outs.append(_rms(o, g_norm) * gate[:, sl])
    return jnp.concatenate(outs, axis=-1)


def _gla_body(*refs, nb, tb, chunk, has_s0):
    if has_s0:
        zh_ref, lbl_ref, gn_ref, s0_ref, o_ref, so_ref, st_ref = refs
    else:
        zh_ref, lbl_ref, gn_ref, o_ref, so_ref, st_ref = refs
        s0_ref = None
    t_idx = pl.program_id(1)
    n_t = pl.num_programs(1)

    @pl.when(t_idx == 0)
    def _init():
        for j in range(nb):
            for h in range(HG_HEADS):
                if has_s0:
                    st_ref[j, h] = s0_ref[j, h].T
                else:
                    st_ref[j, h] = jnp.zeros((HG_DV, HG_DK), F32)

    lb = _lower_bound(lbl_ref[...])
    g_norm = gn_ref[...]
    n_chunks = tb // chunk
    for j in range(nb):
        if n_chunks == 1:
            o_ref[j] = _gla_chunk(zh_ref[j], lb, g_norm, st_ref, j, chunk)
        else:
            def step(c, carry, j=j):
                r0 = pl.multiple_of(c * chunk, chunk)
                o_ref[j, pl.ds(r0, chunk), :] = _gla_chunk(
                    zh_ref[j, pl.ds(r0, chunk), :], lb, g_norm, st_ref, j, chunk)
                return carry
            lax.fori_loop(0, n_chunks, step, 0, unroll=True)

    @pl.when(t_idx == n_t - 1)
    def _fin():
        for j in range(nb):
            for h in range(HG_HEADS):
                so_ref[j, h] = st_ref[j, h].T


def _gla(zh3d, lb_logits, hg_norm, s0, nb, tb, chunk):
    bsz, t, _ = zh3d.shape
    has_s0 = s0 is not None
    in_specs = [
        pl.BlockSpec((nb, tb, N_HG_IN), lambda b, i: (b, i, 0)),
        pl.BlockSpec(lb_logits.shape, lambda b, i: (0, 0)),
        pl.BlockSpec((1, HG_DV), lambda b, i: (0, 0)),
    ]
    args = [zh3d, lb_logits, hg_norm]
    if has_s0:
        in_specs.append(pl.BlockSpec((nb, HG_HEADS, HG_DK, HG_DV), lambda b, i: (b, 0, 0, 0)))
        args.append(s0)
    return pl.pallas_call(
        functools.partial(_gla_body, nb=nb, tb=tb, chunk=chunk, has_s0=has_s0),
        grid=(bsz // nb, t // tb),
        in_specs=in_specs,
        out_specs=[
            pl.BlockSpec((nb, tb, HG_WIDTH), lambda b, i: (b, i, 0)),
            pl.BlockSpec((nb, HG_HEADS, HG_DK, HG_DV), lambda b, i: (b, 0, 0, 0)),
        ],
        out_shape=[
            jax.ShapeDtypeStruct((bsz, t, HG_WIDTH), F32),
            jax.ShapeDtypeStruct((bsz, HG_HEADS, HG_DK, HG_DV), F32),
        ],
        scratch_shapes=[pltpu.VMEM((nb, HG_HEADS, HG_DV, HG_DK), F32)],
        compiler_params=pltpu.CompilerParams(
            dimension_semantics=("arbitrary", "arbitrary"), vmem_limit_bytes=VMEM_LIMIT_BYTES),
        name="gla_s0" if has_s0 else "gla",
    )(*args)


def _log2(n):
    assert n > 0 and n & (n - 1) == 0, n
    return n.bit_length() - 1


def _stack_heads(x, kh):
    return jnp.concatenate(
        [x[:, (kh * SWA_GROUP + g) * SWA_HEAD_DIM:(kh * SWA_GROUP + g + 1) * SWA_HEAD_DIM]
         for g in range(SWA_GROUP)], axis=0)


def _sink_column(sink_ref, kh, rows_per_head):
    r = lax.broadcasted_iota(jnp.int32, (SWA_GROUP * rows_per_head, 1), 0)
    col = jnp.full(r.shape, sink_ref[kh * SWA_GROUP], F32)
    for g in range(1, SWA_GROUP):
        col = jnp.where(r >= g * rows_per_head, sink_ref[kh * SWA_GROUP + g], col)
    return col


def _swa_prompt_body(sink_ref, zc_ref, zp_ref, o_ref, *, n_blk):
    first = pl.program_id(1) == 0
    rows = SWA_GROUP * WINDOW
    i = lax.broadcasted_iota(jnp.int32, (rows, WINDOW), 0) & (WINDOW - 1)
    j = lax.broadcasted_iota(jnp.int32, (rows, WINDOW), 1)
    use_prev = j > i
    for kh in range(SWA_KV_HEADS):
        ks = slice(SWA_WIDTH + kh * SWA_HEAD_DIM, SWA_WIDTH + (kh + 1) * SWA_HEAD_DIM)
        vs = slice(SWA_WIDTH + SWA_KV_WIDTH + kh * SWA_HEAD_DIM,
                   SWA_WIDTH + SWA_KV_WIDTH + (kh + 1) * SWA_HEAD_DIM)
        sink = _sink_column(sink_ref, kh, WINDOW)
        for n in range(n_blk):
            rs = slice(n * WINDOW, (n + 1) * WINDOW)
            zc = zc_ref[0, rs, :]
            zp = zp_ref[0] if n == 0 else zc_ref[0, (n - 1) * WINDOW:n * WINDOW, :]
            q = _stack_heads(zc, kh) * SWA_SCALE
            s_prev = _dot_nt(q, zp[:, ks])
            if n == 0:
                s_prev = jnp.where(first, NEG_BIG, s_prev)
            s = jnp.where(use_prev, s_prev, _dot_nt(q, zc[:, ks]))
            m = jnp.maximum(jnp.max(s, axis=-1, keepdims=True), sink)
            p = jnp.exp(s - m)
            den = jnp.sum(p, axis=-1, keepdims=True) + jnp.exp(sink - m)
            o = (_dot(jnp.where(use_prev, p, 0.0), zp[:, vs])
                 + _dot(jnp.where(use_prev, 0.0, p), zc[:, vs])) / den
            for g in range(SWA_GROUP):
                h = kh * SWA_GROUP + g
                o_ref[0, rs, h * SWA_HEAD_DIM:(h + 1) * SWA_HEAD_DIM] = o[g * WINDOW:(g + 1) * WINDOW]


def _swa_prompt(zs3d, sinks, n_blk):
    bsz, t, _ = zs3d.shape
    tq = n_blk * WINDOW
    return pl.pallas_call(
        functools.partial(_swa_prompt_body, n_blk=n_blk),
        grid=(bsz, t // tq),
        in_specs=[
            pl.BlockSpec(memory_space=pltpu.SMEM),
            pl.BlockSpec((1, tq, N_SWA_IN), lambda b, i: (b, i, 0)),
            pl.BlockSpec((1, WINDOW, N_SWA_IN), lambda b, i: (b, jnp.maximum(i * n_blk - 1, 0), 0)),
        ],
        out_specs=pl.BlockSpec((1, tq, SWA_WIDTH), lambda b, i: (b, i, 0)),
        out_shape=jax.ShapeDtypeStruct((bsz, t, SWA_WIDTH), F32),
        compiler_params=pltpu.CompilerParams(
            dimension_semantics=("arbitrary", "arbitrary"), vmem_limit_bytes=VMEM_LIMIT_BYTES),
        name="swa_prompt",
    )(sinks, zs3d, zs3d)


def _swa_sample_body(sink_ref, zs_ref, ck_ref, cv_ref, o_ref, cko_ref, cvo_ref, *, nb, s_new, wb):
    nq = nb * s_new
    rows = SWA_GROUP * nq
    ls, lw = _log2(s_new), _log2(wb)
    _log2(nb)
    r = lax.broadcasted_iota(jnp.int32, (rows, nb * wb), 0)
    c = lax.broadcasted_iota(jnp.int32, (rows, nb * wb), 1)
    mask_c = (((c >> lw) == ((r >> ls) & (nb - 1)))
              & ((c & (wb - 1)) > (r & (s_new - 1)) + (wb - WINDOW)))
    r = lax.broadcasted_iota(jnp.int32, (rows, nq), 0)
    c = lax.broadcasted_iota(jnp.int32, (rows, nq), 1)
    mask_n = ((c >> ls) == ((r >> ls) & (nb - 1))) & ((c & (s_new - 1)) <= (r & (s_new - 1)))

    z = zs_ref[...].reshape(nq, N_SWA_IN)
    k_new = z[:, SWA_WIDTH:SWA_WIDTH + SWA_KV_WIDTH]
    v_new = z[:, SWA_WIDTH + SWA_KV_WIDTH:]
    for b in range(nb):
        cko_ref[b, 0:wb - s_new, :] = ck_ref[b, s_new:, :]
        cko_ref[b, wb - s_new:, :] = k_new[b * s_new:(b + 1) * s_new]
        cvo_ref[b, 0:wb - s_new, :] = cv_ref[b, s_new:, :]
        cvo_ref[b, wb - s_new:, :] = v_new[b * s_new:(b + 1) * s_new]
    ck = ck_ref[...].reshape(nb * wb, SWA_KV_WIDTH)
    cv = cv_ref[...].reshape(nb * wb, SWA_KV_WIDTH)
    for kh in range(SWA_KV_HEADS):
        cs = slice(kh * SWA_HEAD_DIM, (kh + 1) * SWA_HEAD_DIM)
        sink = _sink_column(sink_ref, kh, nq)
        q = _stack_heads(z, kh) * SWA_SCALE
        sc = jnp.where(mask_c, _dot_nt(q, ck[:, cs]), NEG_BIG)
        sn = jnp.where(mask_n, _dot_nt(q, k_new[:, cs]), NEG_BIG)
        m = jnp.maximum(jnp.maximum(jnp.max(sc, axis=-1, keepdims=True),
                                    jnp.max(sn, axis=-1, keepdims=True)), sink)
        pc = jnp.where(mask_c, jnp.exp(sc - m), 0.0)
        pn = jnp.where(mask_n, jnp.exp(sn - m), 0.0)
        den = (jnp.sum(pc, axis=-1, keepdims=True) + jnp.sum(pn, axis=-1, keepdims=True)
               + jnp.exp(sink - m))
        o = (_dot(pc, cv[:, cs]) + _dot(pn, v_new[:, cs])) / den
        for g in range(SWA_GROUP):
            h = kh * SWA_GROUP + g
            o_ref[:, :, h * SWA_HEAD_DIM:(h + 1) * SWA_HEAD_DIM] = (
                o[g * nq:(g + 1) * nq].reshape(nb, s_new, SWA_HEAD_DIM))


def _swa_sample(zs3d, ck, cv, sinks, nb):
    bsz, s_new, _ = zs3d.shape
    wb = ck.shape[1]
    cache_spec = pl.BlockSpec((nb, wb, SWA_KV_WIDTH), lambda b: (b, 0, 0))
    return pl.pallas_call(
        functools.partial(_swa_sample_body, nb=nb, s_new=s_new, wb=wb),
        grid=(bsz // nb,),
        in_specs=[
            pl.BlockSpec(memory_space=pltpu.SMEM),
            pl.BlockSpec((nb, s_new, N_SWA_IN), lambda b: (b, 0, 0)),
            cache_spec,
            cache_spec,
        ],
        out_specs=[
            pl.BlockSpec((nb, s_new, SWA_WIDTH), lambda b: (b, 0, 0)),
            cache_spec,
            cache_spec,
        ],
        out_shape=[
            jax.ShapeDtypeStruct((bsz, s_new, SWA_WIDTH), F32),
            jax.ShapeDtypeStruct(ck.shape, ck.dtype),
            jax.ShapeDtypeStruct(cv.shape, cv.dtype),
        ],
        compiler_params=pltpu.CompilerParams(
            dimension_semantics=("arbitrary",), vmem_limit_bytes=VMEM_LIMIT_BYTES),
        name="swa_sample",
    )(sinks, zs3d, ck, cv)


def _out_mlp_body(x_ref, oh_ref, os_ref, wo_ref, lm_ref, wu_ref, wd_ref, lf_ref, y_ref):
    o = jnp.concatenate([oh_ref[...].astype(BF16), os_ref[...].astype(BF16)], axis=-1)
    h = x_ref[...] + jnp.dot(o, wo_ref[...], preferred_element_type=F32)
    hn = _rms(h, lm_ref[...]).astype(BF16)
    u = jnp.dot(hn, wu_ref[...], preferred_element_type=F32)
    a = jnp.square(jnp.maximum(u, 0.0)).astype(BF16)
    y_ref[...] = _rms(h + jnp.dot(a, wd_ref[...], preferred_element_type=F32), lf_ref[...])


def _out_mlp(x2d, oh2d, os2d, w_out, ln_mlp, w_up, w_down, ln_final, tm):
    n = x2d.shape[0]
    const = lambda i: (0, 0)
    single = pl.Buffered(1)
    return pl.pallas_call(
        _out_mlp_body,
        grid=(n // tm,),
        in_specs=[
            pl.BlockSpec((tm, D_MODEL), lambda i: (i, 0)),
            pl.BlockSpec((tm, HG_WIDTH), lambda i: (i, 0)),
            pl.BlockSpec((tm, SWA_WIDTH), lambda i: (i, 0)),
            pl.BlockSpec((D_MODEL, D_MODEL), const, pipeline_mode=single),
            pl.BlockSpec((1, D_MODEL), const),
            pl.BlockSpec((D_MODEL, D_FF), const, pipeline_mode=single),
            pl.BlockSpec((D_FF, D_MODEL), const, pipeline_mode=single),
            pl.BlockSpec((1, D_MODEL), const),
        ],
        out_specs=pl.BlockSpec((tm, D_MODEL), lambda i: (i, 0)),
        out_shape=jax.ShapeDtypeStruct((n, D_MODEL), F32),
        compiler_params=pltpu.CompilerParams(
            dimension_semantics=("arbitrary",), vmem_limit_bytes=VMEM_LIMIT_BYTES),
        name="out_mlp",
    )(x2d, oh2d, os2d, w_out, ln_mlp, w_up, w_down, ln_final)


def kernel(x_prompt, x_sample, state_hgrn, cache_swa_k, cache_swa_v, ln_mix, w_in, lb_logits,
           hg_norm, sinks, w_out, ln_mlp, w_up, w_down, ln_final):
    depth = w_in.shape[0]
    assert depth == 1 and lb_logits.shape[0] == depth + 1
    bsz, seq, _ = x_prompt.shape
    dbsz, dseq, _ = x_sample.shape
    wb = cache_swa_k.shape[2]

    w_in_b = w_in[0].astype(BF16)
    w_out_b = w_out[0].astype(BF16)
    w_up_b = w_up[0].astype(BF16)
    w_down_b = w_down[0].astype(BF16)
    ln_mix2 = ln_mix[0].reshape(1, D_MODEL)
    ln_mlp2 = ln_mlp[0].reshape(1, D_MODEL)
    ln_fin2 = ln_final.reshape(1, D_MODEL)
    gn2 = hg_norm[0].reshape(1, HG_DV)
    sink1 = sinks[0]

    def mix_and_mlp(x2d, oh, osw, tm):
        return _out_mlp(x2d, oh, osw, w_out_b, ln_mlp2, w_up_b, w_down_b, ln_fin2, tm)

    xp = x_prompt.reshape(bsz * seq, D_MODEL)
    zh_p, zs_p = _inproj(xp, ln_mix2, w_in_b, 512)
    oh_p, s_p = _gla(zh_p.reshape(bsz, seq, N_HG_IN), lb_logits, gn2, None, 1, 512, 64)
    zs_p3 = zs_p.reshape(bsz, seq, N_SWA_IN)
    os_p = _swa_prompt(zs_p3, sink1, 4)
    y_p = mix_and_mlp(xp, oh_p.reshape(-1, HG_WIDTH), os_p.reshape(-1, SWA_WIDTH), 512)
    wp = min(WINDOW, seq)
    k_p = zs_p3[:, seq - wp:, SWA_WIDTH:SWA_WIDTH + SWA_KV_WIDTH]
    v_p = zs_p3[:, seq - wp:, SWA_WIDTH + SWA_KV_WIDTH:]

    xs = x_sample.reshape(dbsz * dseq, D_MODEL)
    zh_s, zs_s = _inproj(xs, ln_mix2, w_in_b, 512)
    oh_s, s_s = _gla(zh_s.reshape(dbsz, dseq, N_HG_IN), lb_logits, gn2, state_hgrn[0], 8, dseq, dseq)
    os_s, ck_s, cv_s = _swa_sample(
        zs_s.reshape(dbsz, dseq, N_SWA_IN),
        cache_swa_k[0].reshape(dbsz, wb, SWA_KV_WIDTH),
        cache_swa_v[0].reshape(dbsz, wb, SWA_KV_WIDTH), sink1, 8)
    y_s = mix_and_mlp(xs, oh_s.reshape(-1, HG_WIDTH), os_s.reshape(-1, SWA_WIDTH), 512)

    kv_shape_p = (1, bsz, wp, SWA_KV_HEADS, SWA_HEAD_DIM)
    kv_shape_s = (1, dbsz, wb, SWA_KV_HEADS, SWA_HEAD_DIM)
    return (y_p.reshape(bsz, seq, D_MODEL),
            y_s.reshape(dbsz, dseq, D_MODEL),
            s_p[None],
            k_p.reshape(kv_shape_p).astype(cache_swa_k.dtype),
            v_p.reshape(kv_shape_p).astype(cache_swa_v.dtype),
            s_s[None],
            ck_s.reshape(kv_shape_s),
            cv_s.reshape(kv_shape_s))
```

```python
import functools

import jax
import jax.numpy as jnp
from jax import lax
from jax.experimental import pallas as pl
from jax.experimental.pallas import tpu as pltpu

F32 = jnp.float32
BF16 = jnp.bfloat16

D_MODEL = 1024
HG_WIDTH = 512
HG_HEADS = 4
HG_DK = 128
HG_DV = 128
SWA_WIDTH = 512
SWA_HEAD_DIM = 64
SWA_Q_HEADS = 8
SWA_KV_HEADS = 2
SWA_GROUP = SWA_Q_HEADS // SWA_KV_HEADS
SWA_KV_WIDTH = SWA_KV_HEADS * SWA_HEAD_DIM
WINDOW = 128
SWA_SCALE = SWA_HEAD_DIM ** -0.5
D_FF = 4 * D_MODEL
EPS = 1e-6
N_HG_IN = 4 * HG_WIDTH
N_SWA_IN = SWA_WIDTH + 2 * SWA_KV_WIDTH
N_IN = N_HG_IN + N_SWA_IN
NEG_BIG = -1e30

VMEM_LIMIT_BYTES = 56 * 1024 * 1024


def _rms(x, g):
    return x * lax.rsqrt(jnp.mean(x * x, axis=-1, keepdims=True) + EPS) * g


def _sigmoid(x):
    return 1.0 / (1.0 + jnp.exp(-x))


def _dot(a, b):
    return jnp.dot(a.astype(BF16), b.astype(BF16), preferred_element_type=F32)


def _dot_nt(a, b):
    return lax.dot_general(a.astype(BF16), b.astype(BF16), (((1,), (1,)), ((), ())),
                           preferred_element_type=F32)


def _dot_tn(a, b):
    return lax.dot_general(a.astype(BF16), b.astype(BF16), (((0,), (0,)), ((), ())),
                           preferred_element_type=F32)


def _inproj_body(x_ref, g_ref, w_ref, zh_ref, zs_ref):
    xn = _rms(x_ref[...], g_ref[...]).astype(BF16)
    zh_ref[...] = jnp.dot(xn, w_ref[:, :N_HG_IN], preferred_element_type=F32)
    zs_ref[...] = jnp.dot(xn, w_ref[:, N_HG_IN:], preferred_element_type=F32)


def _inproj(x2d, ln, w_in_bf16, tm):
    n = x2d.shape[0]
    return pl.pallas_call(
        _inproj_body,
        grid=(n // tm,),
        in_specs=[
            pl.BlockSpec((tm, D_MODEL), lambda i: (i, 0)),
            pl.BlockSpec((1, D_MODEL), lambda i: (0, 0)),
            pl.BlockSpec((D_MODEL, N_IN), lambda i: (0, 0)),
        ],
        out_specs=[
            pl.BlockSpec((tm, N_HG_IN), lambda i: (i, 0)),
            pl.BlockSpec((tm, N_SWA_IN), lambda i: (i, 0)),
        ],
        out_shape=[
            jax.ShapeDtypeStruct((n, N_HG_IN), F32),
            jax.ShapeDtypeStruct((n, N_SWA_IN), F32),
        ],
        compiler_params=pltpu.CompilerParams(
            dimension_semantics=("arbitrary",), vmem_limit_bytes=VMEM_LIMIT_BYTES),
        name="inproj",
    )(x2d, ln, w_in_bf16)


def _cumsum_rows(x):
    n = x.shape[0]
    row = lax.broadcasted_iota(jnp.int32, x.shape, 0)
    s = 1
    while s < n:
        x = x + jnp.where(row >= s, pltpu.roll(x, s, axis=0), 0.0)
        s *= 2
    return x


def _lower_bound(lb_logits):
    m = jnp.max(lb_logits, axis=0, keepdims=True)
    e = jnp.exp(lb_logits - m)
    return e[0:1, :] / jnp.sum(e, axis=0, keepdims=True)


def _gla_chunk(zh, lb, g_norm, st_ref, j, chunk):
    zq = zh[:, 0:HG_WIDTH]
    zf = zh[:, HG_WIDTH:2 * HG_WIDTH]
    v = zh[:, 2 * HG_WIDTH:3 * HG_WIDTH]
    zg = zh[:, 3 * HG_WIDTH:4 * HG_WIDTH]

    q = zq * _sigmoid(zq)
    t = jnp.exp(-jnp.abs(zf))
    r = 1.0 / (1.0 + t)
    pos = zf >= 0.0
    sig = jnp.where(pos, r, t * r)
    nsig = jnp.where(pos, t * r, r)
    f = lb + (1.0 - lb) * sig
    k = (1.0 - lb) * nsig
    b = _cumsum_rows(jnp.log(f))
    mid = chunk // 2 - 1
    b_mid = b[mid:mid + 1, :]
    b_last = b[chunk - 1:chunk, :]
    qt = q * jnp.exp(b)
    qf = q * jnp.exp(b - b_mid)
    kf = k * jnp.exp(b_mid - b)
    ke = k * jnp.exp(b_last - b)
    dec = jnp.exp(b_last)

    row = lax.broadcasted_iota(jnp.int32, (chunk, chunk), 0)
    col = lax.broadcasted_iota(jnp.int32, (chunk, chunk), 1)
    causal = row >= col
    gate = zg * _sigmoid(zg)

    outs = []
    for h in range(HG_HEADS):
        sl = slice(h * HG_DK, (h + 1) * HG_DK)
        v_h = v[:, sl]
        a = jnp.where(causal, _dot_nt(qf[:, sl], kf[:, sl]), 0.0)
        st = st_ref[j, h]
        o = _dot(a, v_h) + _dot_nt(qt[:, sl], st)
        st_ref[j, h] = st * dec[:, sl] + _dot_tn(v_h, ke[:, sl])
        outs.append(_rms(o, g_norm) * gate[:, sl])
    return jnp.concatenate(outs, axis=-1)


def _gla_body(*refs, nb, tb, chunk, has_s0):
    if has_s0:
        zh_ref, lbl_ref, gn_ref, s0_ref, o_ref, so_ref, st_ref = refs
    else:
        zh_ref, lbl_ref, gn_ref, o_ref, so_ref, st_ref = refs
        s0_ref = None
    t_idx = pl.program_id(1)
    n_t = pl.num_programs(1)

    @pl.when(t_idx == 0)
    def _init():
        for j in range(nb):
            for h in range(HG_HEADS):
                if has_s0:
                    st_ref[j, h] = s0_ref[j, h].T
                else:
                    st_ref[j, h] = jnp.zeros((HG_DV, HG_DK), F32)

    lb = _lower_bound(lbl_ref[...])
    g_norm = gn_ref[...]
    n_chunks = tb // chunk
    for j in range(nb):
        if n_chunks == 1:
            o_ref[j] = _gla_chunk(zh_ref[j], lb, g_norm, st_ref, j, chunk)
        else:
            def step(c, carry, j=j):
                r0 = pl.multiple_of(c * chunk, chunk)
                o_ref[j, pl.ds(r0, chunk), :] = _gla_chunk(
                    zh_ref[j, pl.ds(r0, chunk), :], lb, g_norm, st_ref, j, chunk)
                return carry
            lax.fori_loop(0, n_chunks, step, 0, unroll=True)

    @pl.when(t_idx == n_t - 1)
    def _fin():
        for j in range(nb):
            for h in range(HG_HEADS):
                so_ref[j, h] = st_ref[j, h].T


def _gla(zh3d, lb_logits, hg_norm, s0, nb, tb, chunk):
    bsz, t, _ = zh3d.shape
    has_s0 = s0 is not None
    in_specs = [
        pl.BlockSpec((nb, tb, N_HG_IN), lambda b, i: (b, i, 0)),
        pl.BlockSpec(lb_logits.shape, lambda b, i: (0, 0)),
        pl.BlockSpec((1, HG_DV), lambda b, i: (0, 0)),
    ]
    args = [zh3d, lb_logits, hg_norm]
    if has_s0:
        in_specs.append(pl.BlockSpec((nb, HG_HEADS, HG_DK, HG_DV), lambda b, i: (b, 0, 0, 0)))
        args.append(s0)
    return pl.pallas_call(
        functools.partial(_gla_body, nb=nb, tb=tb, chunk=chunk, has_s0=has_s0),
        grid=(bsz // nb, t // tb),
        in_specs=in_specs,
        out_specs=[
            pl.BlockSpec((nb, tb, HG_WIDTH), lambda b, i: (b, i, 0)),
            pl.BlockSpec((nb, HG_HEADS, HG_DK, HG_DV), lambda b, i: (b, 0, 0, 0)),
        ],
        out_shape=[
            jax.ShapeDtypeStruct((bsz, t, HG_WIDTH), F32),
            jax.ShapeDtypeStruct((bsz, HG_HEADS, HG_DK, HG_DV), F32),
        ],
        scratch_shapes=[pltpu.VMEM((nb, HG_HEADS, HG_DV, HG_DK), F32)],
        compiler_params=pltpu.CompilerParams(
            dimension_semantics=("arbitrary", "arbitrary"), vmem_limit_bytes=VMEM_LIMIT_BYTES),
        name="gla_s0" if has_s0 else "gla",
    )(*args)


def _log2(n):
    assert n > 0 and n & (n - 1) == 0, n
    return n.bit_length() - 1


def _stack_heads(x, kh):
    return jnp.concatenate(
        [x[:, (kh * SWA_GROUP + g) * SWA_HEAD_DIM:(kh * SWA_GROUP + g + 1) * SWA_HEAD_DIM]
         for g in range(SWA_GROUP)], axis=0)


def _sink_column(sink_ref, kh, rows_per_head):
    r = lax.broadcasted_iota(jnp.int32, (SWA_GROUP * rows_per_head, 1), 0)
    col = jnp.full(r.shape, sink_ref[kh * SWA_GROUP], F32)
    for g in range(1, SWA_GROUP):
        col = jnp.where(r >= g * rows_per_head, sink_ref[kh * SWA_GROUP + g], col)
    return col


def _swa_block(zq, kv_cur, kv_prev, sink_ref, no_prev):
    rows = SWA_GROUP * WINDOW
    i = lax.broadcasted_iota(jnp.int32, (rows, WINDOW), 0) & (WINDOW - 1)
    j = lax.broadcasted_iota(jnp.int32, (rows, WINDOW), 1)
    use_prev = j > i
    outs = []
    for kh in range(SWA_KV_HEADS):
        ks = slice(kh * SWA_HEAD_DIM, (kh + 1) * SWA_HEAD_DIM)
        vs = slice(SWA_KV_WIDTH + kh * SWA_HEAD_DIM, SWA_KV_WIDTH + (kh + 1) * SWA_HEAD_DIM)
        sink = _sink_column(sink_ref, kh, WINDOW)
        q = _stack_heads(zq, kh) * SWA_SCALE
        s_prev = _dot_nt(q, kv_prev[:, ks])
        if no_prev is not None:
            s_prev = jnp.where(no_prev, NEG_BIG, s_prev)
        s = jnp.where(use_prev, s_prev, _dot_nt(q, kv_cur[:, ks]))
        m = jnp.maximum(jnp.max(s, axis=-1, keepdims=True), sink)
        p = jnp.exp(s - m)
        den = jnp.sum(p, axis=-1, keepdims=True) + jnp.exp(sink - m)
        o = (_dot(jnp.where(use_prev, p, 0.0), kv_prev[:, vs])
             + _dot(jnp.where(use_prev, 0.0, p), kv_cur[:, vs])) / den
        outs.extend(o[g * WINDOW:(g + 1) * WINDOW] for g in range(SWA_GROUP))
    return jnp.concatenate(outs, axis=-1)


def _swa_sample_body(sink_ref, zs_ref, ck_ref, cv_ref, o_ref, cko_ref, cvo_ref, *, nb, s_new, wb):
    nq = nb * s_new
    rows = SWA_GROUP * nq
    ls, lw = _log2(s_new), _log2(wb)
    _log2(nb)
    r = lax.broadcasted_iota(jnp.int32, (rows, nb * wb), 0)
    c = lax.broadcasted_iota(jnp.int32, (rows, nb * wb), 1)
    mask_c = (((c >> lw) == ((r >> ls) & (nb - 1)))
              & ((c & (wb - 1)) > (r & (s_new - 1)) + (wb - WINDOW)))
    r = lax.broadcasted_iota(jnp.int32, (rows, nq), 0)
    c = lax.broadcasted_iota(jnp.int32, (rows, nq), 1)
    mask_n = ((c >> ls) == ((r >> ls) & (nb - 1))) & ((c & (s_new - 1)) <= (r & (s_new - 1)))

    z = zs_ref[...].reshape(nq, N_SWA_IN)
    k_new = z[:, SWA_WIDTH:SWA_WIDTH + SWA_KV_WIDTH]
    v_new = z[:, SWA_WIDTH + SWA_KV_WIDTH:]
    for b in range(nb):
        cko_ref[b, 0:wb - s_new, :] = ck_ref[b, s_new:, :]
        cko_ref[b, wb - s_new:, :] = k_new[b * s_new:(b + 1) * s_new]
        cvo_ref[b, 0:wb - s_new, :] = cv_ref[b, s_new:, :]
        cvo_ref[b, wb - s_new:, :] = v_new[b * s_new:(b + 1) * s_new]
    ck = ck_ref[...].reshape(nb * wb, SWA_KV_WIDTH)
    cv = cv_ref[...].reshape(nb * wb, SWA_KV_WIDTH)
    for kh in range(SWA_KV_HEADS):
        cs = slice(kh * SWA_HEAD_DIM, (kh + 1) * SWA_HEAD_DIM)
        sink = _sink_column(sink_ref, kh, nq)
        q = _stack_heads(z, kh) * SWA_SCALE
        sc = jnp.where(mask_c, _dot_nt(q, ck[:, cs]), NEG_BIG)
        sn = jnp.where(mask_n, _dot_nt(q, k_new[:, cs]), NEG_BIG)
        m = jnp.maximum(jnp.maximum(jnp.max(sc, axis=-1, keepdims=True),
                                    jnp.max(sn, axis=-1, keepdims=True)), sink)
        pc = jnp.where(mask_c, jnp.exp(sc - m), 0.0)
        pn = jnp.where(mask_n, jnp.exp(sn - m), 0.0)
        den = (jnp.sum(pc, axis=-1, keepdims=True) + jnp.sum(pn, axis=-1, keepdims=True)
               + jnp.exp(sink - m))
        o = (_dot(pc, cv[:, cs]) + _dot(pn, v_new[:, cs])) / den
        for g in range(SWA_GROUP):
            h = kh * SWA_GROUP + g
            o_ref[:, :, h * SWA_HEAD_DIM:(h + 1) * SWA_HEAD_DIM] = (
                o[g * nq:(g + 1) * nq].reshape(nb, s_new, SWA_HEAD_DIM))


def _swa_sample(zs3d, ck, cv, sinks, nb):
    bsz, s_new, _ = zs3d.shape
    wb = ck.shape[1]
    cache_spec = pl.BlockSpec((nb, wb, SWA_KV_WIDTH), lambda b: (b, 0, 0))
    return pl.pallas_call(
        functools.partial(_swa_sample_body, nb=nb, s_new=s_new, wb=wb),
        grid=(bsz // nb,),
        in_specs=[
            pl.BlockSpec(memory_space=pltpu.SMEM),
            pl.BlockSpec((nb, s_new, N_SWA_IN), lambda b: (b, 0, 0)),
            cache_spec,
            cache_spec,
        ],
        out_specs=[
            pl.BlockSpec((nb, s_new, SWA_WIDTH), lambda b: (b, 0, 0)),
            cache_spec,
            cache_spec,
        ],
        out_shape=[
            jax.ShapeDtypeStruct((bsz, s_new, SWA_WIDTH), F32),
            jax.ShapeDtypeStruct(ck.shape, ck.dtype),
            jax.ShapeDtypeStruct(cv.shape, cv.dtype),
        ],
        compiler_params=pltpu.CompilerParams(
            dimension_semantics=("arbitrary",), vmem_limit_bytes=VMEM_LIMIT_BYTES),
        name="swa_sample",
    )(sinks, zs3d, ck, cv)


def _out_mlp_body(x_ref, oh_ref, os_ref, wo_ref, lm_ref, wu_ref, wd_ref, lf_ref, y_ref):
    o = jnp.concatenate([oh_ref[...].astype(BF16), os_ref[...].astype(BF16)], axis=-1)
    h = x_ref[...] + jnp.dot(o, wo_ref[...], preferred_element_type=F32)
    hn = _rms(h, lm_ref[...]).astype(BF16)
    u = jnp.dot(hn, wu_ref[...], preferred_element_type=F32)
    a = jnp.square(jnp.maximum(u, 0.0)).astype(BF16)
    y_ref[...] = _rms(h + jnp.dot(a, wd_ref[...], preferred_element_type=F32), lf_ref[...])


def _out_mlp(x2d, oh2d, os2d, w_out, ln_mlp, w_up, w_down, ln_final, tm):
    n = x2d.shape[0]
    const = lambda i: (0, 0)
    single = pl.Buffered(1)
    return pl.pallas_call(
        _out_mlp_body,
        grid=(n // tm,),
        in_specs=[
            pl.BlockSpec((tm, D_MODEL), lambda i: (i, 0)),
            pl.BlockSpec((tm, HG_WIDTH), lambda i: (i, 0)),
            pl.BlockSpec((tm, SWA_WIDTH), lambda i: (i, 0)),
            pl.BlockSpec((D_MODEL, D_MODEL), const, pipeline_mode=single),
            pl.BlockSpec((1, D_MODEL), const),
            pl.BlockSpec((D_MODEL, D_FF), const, pipeline_mode=single),
            pl.BlockSpec((D_FF, D_MODEL), const, pipeline_mode=single),
            pl.BlockSpec((1, D_MODEL), const),
        ],
        out_specs=pl.BlockSpec((tm, D_MODEL), lambda i: (i, 0)),
        out_shape=jax.ShapeDtypeStruct((n, D_MODEL), F32),
        compiler_params=pltpu.CompilerParams(
            dimension_semantics=("arbitrary",), vmem_limit_bytes=VMEM_LIMIT_BYTES),
        name="out_mlp",
    )(x2d, oh2d, os2d, w_out, ln_mlp, w_up, w_down, ln_final)


PROMPT_TILE = 512
GLA_CHUNK = 64
FF_CHUNK = 1024
KV_COLS = slice(N_HG_IN + SWA_WIDTH, N_IN)


def _prompt_body(sink_ref, xc_ref, xp_ref, lmix_ref, win_ref, lbl_ref, gn_ref, wo_ref, lmlp_ref,
                 wu_ref, wd_ref, lfin_ref, y_ref, so_ref, ko_ref, vo_ref,
                 z_ref, o_ref, st_ref, kvp_ref, *, tiles_per_seq, n_tiles):
    s = pl.program_id(0)
    pos = s % tiles_per_seq
    seq_start = pos == 0
    tm = PROMPT_TILE

    @pl.when(s == 0)
    def _first():
        o_ref[...] = jnp.zeros(o_ref.shape, o_ref.dtype)

    @pl.when(seq_start)
    def _reset():
        st_ref[...] = jnp.zeros(st_ref.shape, st_ref.dtype)
        kvp_ref[...] = jnp.zeros(kvp_ref.shape, kvp_ref.dtype)

    y_ref[...] = xp_ref[...] + jnp.dot(o_ref[...], wo_ref[...], preferred_element_type=F32)

    xn = _rms(xc_ref[...], lmix_ref[...]).astype(BF16)
    z_ref[...] = jnp.dot(xn, win_ref[...], preferred_element_type=F32)
    lb = _lower_bound(lbl_ref[...])
    g_norm = gn_ref[...]
    for c in range(tm // GLA_CHUNK):
        rows = slice(c * GLA_CHUNK, (c + 1) * GLA_CHUNK)
        o_ref[rows, 0:HG_WIDTH] = _gla_chunk(
            z_ref[rows, 0:N_HG_IN], lb, g_norm, st_ref, 0, GLA_CHUNK).astype(BF16)
    for n in range(tm // WINDOW):
        rows = slice(n * WINDOW, (n + 1) * WINDOW)
        kv_prev = kvp_ref[...] if n == 0 else z_ref[(n - 1) * WINDOW:n * WINDOW, KV_COLS]
        o_ref[rows, HG_WIDTH:] = _swa_block(
            z_ref[rows, N_HG_IN:N_HG_IN + SWA_WIDTH], z_ref[rows, KV_COLS], kv_prev, sink_ref,
            seq_start if n == 0 else None).astype(BF16)
    kvp_ref[...] = z_ref[tm - WINDOW:tm, KV_COLS]

    hn = _rms(y_ref[...], lmlp_ref[...]).astype(BF16)
    for c in range(D_FF // FF_CHUNK):
        cs = slice(c * FF_CHUNK, (c + 1) * FF_CHUNK)
        u = jnp.dot(hn, wu_ref[:, cs], preferred_element_type=F32)
        a = jnp.square(jnp.maximum(u, 0.0)).astype(BF16)
        y_ref[...] += jnp.dot(a, wd_ref[cs, :], preferred_element_type=F32)
    y_ref[...] = _rms(y_ref[...], lfin_ref[...])

    @pl.when((pos == tiles_per_seq - 1) & (s < n_tiles))
    def _seq_end():
        for h in range(HG_HEADS):
            so_ref[0, h] = st_ref[0, h].T
        ko_ref[0] = z_ref[tm - WINDOW:tm, N_HG_IN + SWA_WIDTH:N_HG_IN + SWA_WIDTH + SWA_KV_WIDTH]
        vo_ref[0] = z_ref[tm - WINDOW:tm, N_HG_IN + SWA_WIDTH + SWA_KV_WIDTH:N_IN]


def _prompt_layer(x2d, bsz, sinks, ln_mix, w_in, lb_logits, hg_norm, w_out, ln_mlp, w_up, w_down, ln_final):
    n = x2d.shape[0]
    tm = PROMPT_TILE
    n_tiles = n // tm
    tiles_per_seq = n_tiles // bsz
    assert tiles_per_seq * bsz * tm == n
    const = lambda s: (0, 0)
    single = pl.Buffered(1)
    seq_of = lambda s: jnp.minimum(s, n_tiles - 1) // tiles_per_seq
    return pl.pallas_call(
        functools.partial(_prompt_body, tiles_per_seq=tiles_per_seq, n_tiles=n_tiles),
        grid=(n_tiles + 1,),
        in_specs=[
            pl.BlockSpec(memory_space=pltpu.SMEM),
            pl.BlockSpec((tm, D_MODEL), lambda s: (jnp.minimum(s, n_tiles - 1), 0)),
            pl.BlockSpec((tm, D_MODEL), lambda s: (jnp.maximum(s - 1, 0), 0)),
            pl.BlockSpec((1, D_MODEL), const),
            pl.BlockSpec((D_MODEL, N_IN), const, pipeline_mode=single),
            pl.BlockSpec(lb_logits.shape, const),
            pl.BlockSpec((1, HG_DV), const),
            pl.BlockSpec((D_MODEL, D_MODEL), const, pipeline_mode=single),
            pl.BlockSpec((1, D_MODEL), const),
            pl.BlockSpec((D_MODEL, D_FF), const, pipeline_mode=single),
            pl.BlockSpec((D_FF, D_MODEL), const, pipeline_mode=single),
            pl.BlockSpec((1, D_MODEL), const),
        ],
        out_specs=[
            pl.BlockSpec((tm, D_MODEL), lambda s: (jnp.maximum(s - 1, 0), 0)),
            pl.BlockSpec((1, HG_HEADS, HG_DK, HG_DV), lambda s: (seq_of(s), 0, 0, 0)),
            pl.BlockSpec((1, WINDOW, SWA_KV_WIDTH), lambda s: (seq_of(s), 0, 0)),
            pl.BlockSpec((1, WINDOW, SWA_KV_WIDTH), lambda s: (seq_of(s), 0, 0)),
        ],
        out_shape=[
            jax.ShapeDtypeStruct((n, D_MODEL), F32),
            jax.ShapeDtypeStruct((bsz, HG_HEADS, HG_DK, HG_DV), F32),
            jax.ShapeDtypeStruct((bsz, WINDOW, SWA_KV_WIDTH), F32),
            jax.ShapeDtypeStruct((bsz, WINDOW, SWA_KV_WIDTH), F32),
        ],
        scratch_shapes=[
            pltpu.VMEM((tm, N_IN), F32),
            pltpu.VMEM((tm, D_MODEL), BF16),
            pltpu.VMEM((1, HG_HEADS, HG_DV, HG_DK), F32),
            pltpu.VMEM((WINDOW, 2 * SWA_KV_WIDTH), F32),
        ],
        compiler_params=pltpu.CompilerParams(
            dimension_semantics=("arbitrary",), vmem_limit_bytes=VMEM_LIMIT_BYTES),
        name="prompt_layer",
    )(sinks, x2d, x2d, ln_mix, w_in, lb_logits, hg_norm, w_out, ln_mlp, w_up, w_down, ln_final)


def kernel(x_prompt, x_sample, state_hgrn, cache_swa_k, cache_swa_v, ln_mix, w_in, lb_logits,
           hg_norm, sinks, w_out, ln_mlp, w_up, w_down, ln_final):
    depth = w_in.shape[0]
    assert depth == 1 and lb_logits.shape[0] == depth + 1
    bsz, seq, _ = x_prompt.shape
    dbsz, dseq, _ = x_sample.shape
    wb = cache_swa_k.shape[2]

    w_in_b = w_in[0].astype(BF16)
    w_out_b = w_out[0].astype(BF16)
    w_up_b = w_up[0].astype(BF16)
    w_down_b = w_down[0].astype(BF16)
    ln_mix2 = ln_mix[0].reshape(1, D_MODEL)
    ln_mlp2 = ln_mlp[0].reshape(1, D_MODEL)
    ln_fin2 = ln_final.reshape(1, D_MODEL)
    gn2 = hg_norm[0].reshape(1, HG_DV)
    sink1 = sinks[0]

    def mix_and_mlp(x2d, oh, osw, tm):
        return _out_mlp(x2d, oh, osw, w_out_b, ln_mlp2, w_up_b, w_down_b, ln_fin2, tm)

    assert seq % PROMPT_TILE == 0 and seq >= WINDOW
    wp = WINDOW
    xp = x_prompt.reshape(bsz * seq, D_MODEL)
    y_p, s_p, k_p, v_p = _prompt_layer(xp, bsz, sink1, ln_mix2, w_in_b, lb_logits, gn2, w_out_b,
                                       ln_mlp2, w_up_b, w_down_b, ln_fin2)

    xs = x_sample.reshape(dbsz * dseq, D_MODEL)
    zh_s, zs_s = _inproj(xs, ln_mix2, w_in_b, 512)
    oh_s, s_s = _gla(zh_s.reshape(dbsz, dseq, N_HG_IN), lb_logits, gn2, state_hgrn[0], 8, dseq, dseq)
    os_s, ck_s, cv_s = _swa_sample(
        zs_s.reshape(dbsz, dseq, N_SWA_IN),
        cache_swa_k[0].reshape(dbsz, wb, SWA_KV_WIDTH),
        cache_swa_v[0].reshape(dbsz, wb, SWA_KV_WIDTH), sink1, 8)
    y_s = mix_and_mlp(xs, oh_s.reshape(-1, HG_WIDTH), os_s.reshape(-1, SWA_WIDTH), 512)

    kv_shape_p = (1, bsz, wp, SWA_KV_HEADS, SWA_HEAD_DIM)
    kv_shape_s = (1, dbsz, wb, SWA_KV_HEADS, SWA_HEAD_DIM)
    return (y_p.reshape(bsz, seq, D_MODEL),
            y_s.reshape(dbsz, dseq, D_MODEL),
            s_p[None],
            k_p.reshape(kv_shape_p).astype(cache_swa_k.dtype),
            v_p.reshape(kv_shape_p).astype(cache_swa_v.dtype),
            s_s[None],
            ck_s.reshape(kv_shape_s),
            cv_s.reshape(kv_shape_s))
```

```python
import functools

import jax
import jax.numpy as jnp
from jax import lax
from jax.experimental import pallas as pl
from jax.experimental.pallas import tpu as pltpu

F32 = jnp.float32
BF16 = jnp.bfloat16

D_MODEL = 1024
HG_WIDTH = 512
HG_HEADS = 4
HG_DK = 128
HG_DV = 128
SWA_WIDTH = 512
SWA_HEAD_DIM = 64
SWA_Q_HEADS = 8
SWA_KV_HEADS = 2
SWA_GROUP = SWA_Q_HEADS // SWA_KV_HEADS
SWA_KV_WIDTH = SWA_KV_HEADS * SWA_HEAD_DIM
WINDOW = 128
SWA_SCALE = SWA_HEAD_DIM ** -0.5
D_FF = 4 * D_MODEL
EPS = 1e-6
N_HG_IN = 4 * HG_WIDTH
N_SWA_IN = SWA_WIDTH + 2 * SWA_KV_WIDTH
N_IN = N_HG_IN + N_SWA_IN
NEG_BIG = -1e30

VMEM_LIMIT_BYTES = 56 * 1024 * 1024


def _rms(x, g):
    return x * lax.rsqrt(jnp.mean(x * x, axis=-1, keepdims=True) + EPS) * g


def _sigmoid(x):
    return 1.0 / (1.0 + jnp.exp(-x))


def _dot(a, b):
    return jnp.dot(a.astype(BF16), b.astype(BF16), preferred_element_type=F32)


def _dot_nt(a, b):
    return lax.dot_general(a.astype(BF16), b.astype(BF16), (((1,), (1,)), ((), ())),
                           preferred_element_type=F32)


def _dot_tn(a, b):
    return lax.dot_general(a.astype(BF16), b.astype(BF16), (((0,), (0,)), ((), ())),
                           preferred_element_type=F32)


def _inproj_body(x_ref, g_ref, w_ref, zh_ref, zs_ref):
    xn = _rms(x_ref[...], g_ref[...]).astype(BF16)
    zh_ref[...] = jnp.dot(xn, w_ref[:, :N_HG_IN], preferred_element_type=F32)
    zs_ref[...] = jnp.dot(xn, w_ref[:, N_HG_IN:], preferred_element_type=F32)


def _inproj(x2d, ln, w_in_bf16, tm):
    n = x2d.shape[0]
    return pl.pallas_call(
        _inproj_body,
        grid=(n // tm,),
        in_specs=[
            pl.BlockSpec((tm, D_MODEL), lambda i: (i, 0)),
            pl.BlockSpec((1, D_MODEL), lambda i: (0, 0)),
            pl.BlockSpec((D_MODEL, N_IN), lambda i: (0, 0)),
        ],
        out_specs=[
            pl.BlockSpec((tm, N_HG_IN), lambda i: (i, 0)),
            pl.BlockSpec((tm, N_SWA_IN), lambda i: (i, 0)),
        ],
        out_shape=[
            jax.ShapeDtypeStruct((n, N_HG_IN), F32),
            jax.ShapeDtypeStruct((n, N_SWA_IN), F32),
        ],
        compiler_params=pltpu.CompilerParams(
            dimension_semantics=("arbitrary",), vmem_limit_bytes=VMEM_LIMIT_BYTES),
        name="inproj",
    )(x2d, ln, w_in_bf16)


def _cumsum_rows(x):
    n = x.shape[0]
    row = lax.broadcasted_iota(jnp.int32, x.shape, 0)
    s = 1
    while s < n:
        x = x + jnp.where(row >= s, pltpu.roll(x, s, axis=0), 0.0)
        s *= 2
    return x


def _lower_bound(lb_logits):
    m = jnp.max(lb_logits, axis=0, keepdims=True)
    e = jnp.exp(lb_logits - m)
    return e[0:1, :] / jnp.sum(e, axis=0, keepdims=True)


def _gla_front(zh, lb):
    chunk = zh.shape[0]
    zq = zh[:, 0:HG_WIDTH]
    zf = zh[:, HG_WIDTH:2 * HG_WIDTH]
    v = zh[:, 2 * HG_WIDTH:3 * HG_WIDTH]
    zg = zh[:, 3 * HG_WIDTH:4 * HG_WIDTH]

    q = zq * _sigmoid(zq)
    t = jnp.exp(-jnp.abs(zf))
    r = 1.0 / (1.0 + t)
    pos = zf >= 0.0
    sig = jnp.where(pos, r, t * r)
    nsig = jnp.where(pos, t * r, r)
    f = lb + (1.0 - lb) * sig
    k = (1.0 - lb) * nsig
    b = _cumsum_rows(jnp.log(f))
    mid = chunk // 2 - 1
    b_mid = b[mid:mid + 1, :]
    b_last = b[chunk - 1:chunk, :]
    qt = q * jnp.exp(b)
    qf = q * jnp.exp(b - b_mid)
    kf = k * jnp.exp(b_mid - b)
    ke = k * jnp.exp(b_last - b)
    dec = jnp.exp(b_last)
    gate = zg * _sigmoid(zg)
    v = v.astype(BF16)
    heads = [slice(h * HG_DK, (h + 1) * HG_DK) for h in range(HG_HEADS)]
    scores = [_dot_nt(qf[:, sl], kf[:, sl]) for sl in heads]
    kv = [_dot_tn(v[:, sl], ke[:, sl]) for sl in heads]
    return scores, kv, qt.astype(BF16), v, dec, gate


def _gla_back(front, g_norm, st_ref, j):
    scores, kv, qt, v, dec, gate = front
    chunk = qt.shape[0]
    row = lax.broadcasted_iota(jnp.int32, (chunk, chunk), 0)
    col = lax.broadcasted_iota(jnp.int32, (chunk, chunk), 1)
    causal = row >= col
    outs = []
    for h in range(HG_HEADS):
        sl = slice(h * HG_DK, (h + 1) * HG_DK)
        a = jnp.where(causal, scores[h], 0.0)
        st = st_ref[j, h]
        o = _dot(a, v[:, sl]) + _dot_nt(qt[:, sl], st)
        st_ref[j, h] = st * dec[:, sl] + kv[h]
        outs.append(_rms(o, g_norm) * gate[:, sl])
    return jnp.concatenate(outs, axis=-1)


def _gla_chunk(zh, lb, g_norm, st_ref, j, chunk):
    assert zh.shape[0] == chunk
    return _gla_back(_gla_front(zh, lb), g_norm, st_ref, j)


def _gla_body(*refs, nb, tb, chunk, has_s0):
    if has_s0:
        zh_ref, lbl_ref, gn_ref, s0_ref, o_ref, so_ref, st_ref = refs
    else:
        zh_ref, lbl_ref, gn_ref, o_ref, so_ref, st_ref = refs
        s0_ref = None
    t_idx = pl.program_id(1)
    n_t = pl.num_programs(1)

    @pl.when(t_idx == 0)
    def _init():
        for j in range(nb):
            for h in range(HG_HEADS):
                if has_s0:
                    st_ref[j, h] = s0_ref[j, h].T
                else:
                    st_ref[j, h] = jnp.zeros((HG_DV, HG_DK), F32)

    lb = _lower_bound(lbl_ref[...])
    g_norm = gn_ref[...]
    n_chunks = tb // chunk
    for j in range(nb):
        if n_chunks == 1:
            o_ref[j] = _gla_chunk(zh_ref[j], lb, g_norm, st_ref, j, chunk)
        else:
            def step(c, carry, j=j):
                r0 = pl.multiple_of(c * chunk, chunk)
                o_ref[j, pl.ds(r0, chunk), :] = _gla_chunk(
                    zh_ref[j, pl.ds(r0, chunk), :], lb, g_norm, st_ref, j, chunk)
                return carry
            lax.fori_loop(0, n_chunks, step, 0, unroll=True)

    @pl.when(t_idx == n_t - 1)
    def _fin():
        for j in range(nb):
            for h in range(HG_HEADS):
                so_ref[j, h] = st_ref[j, h].T


def _gla(zh3d, lb_logits, hg_norm, s0, nb, tb, chunk):
    bsz, t, _ = zh3d.shape
    has_s0 = s0 is not None
    in_specs = [
        pl.BlockSpec((nb, tb, N_HG_IN), lambda b, i: (b, i, 0)),
        pl.BlockSpec(lb_logits.shape, lambda b, i: (0, 0)),
        pl.BlockSpec((1, HG_DV), lambda b, i: (0, 0)),
    ]
    args = [zh3d, lb_logits, hg_norm]
    if has_s0:
        in_specs.append(pl.BlockSpec((nb, HG_HEADS, HG_DK, HG_DV), lambda b, i: (b, 0, 0, 0)))
        args.append(s0)
    return pl.pallas_call(
        functools.partial(_gla_body, nb=nb, tb=tb, chunk=chunk, has_s0=has_s0),
        grid=(bsz // nb, t // tb),
        in_specs=in_specs,
        out_specs=[
            pl.BlockSpec((nb, tb, HG_WIDTH), lambda b, i: (b, i, 0)),
            pl.BlockSpec((nb, HG_HEADS, HG_DK, HG_DV), lambda b, i: (b, 0, 0, 0)),
        ],
        out_shape=[
            jax.ShapeDtypeStruct((bsz, t, HG_WIDTH), F32),
            jax.ShapeDtypeStruct((bsz, HG_HEADS, HG_DK, HG_DV), F32),
        ],
        scratch_shapes=[pltpu.VMEM((nb, HG_HEADS, HG_DV, HG_DK), F32)],
        compiler_params=pltpu.CompilerParams(
            dimension_semantics=("arbitrary", "arbitrary"), vmem_limit_bytes=VMEM_LIMIT_BYTES),
        name="gla_s0" if has_s0 else "gla",
    )(*args)


def _log2(n):
    assert n > 0 and n & (n - 1) == 0, n
    return n.bit_length() - 1


def _stack_heads(x, kh):
    return jnp.concatenate(
        [x[:, (kh * SWA_GROUP + g) * SWA_HEAD_DIM:(kh * SWA_GROUP + g + 1) * SWA_HEAD_DIM]
         for g in range(SWA_GROUP)], axis=0)


def _sink_column(sink_ref, kh, rows_per_head):
    r = lax.broadcasted_iota(jnp.int32, (SWA_GROUP * rows_per_head, 1), 0)
    col = jnp.full(r.shape, sink_ref[kh * SWA_GROUP], F32)
    for g in range(1, SWA_GROUP):
        col = jnp.where(r >= g * rows_per_head, sink_ref[kh * SWA_GROUP + g], col)
    return col


def _swa_front(zq, kv_cur, kv_prev):
    scores = []
    for kh in range(SWA_KV_HEADS):
        ks = slice(kh * SWA_HEAD_DIM, (kh + 1) * SWA_HEAD_DIM)
        q = (_stack_heads(zq, kh) * SWA_SCALE).astype(BF16)
        scores.append((_dot_nt(q, kv_prev[:, ks]), _dot_nt(q, kv_cur[:, ks])))
    return scores


def _swa_back(scores, kv_cur, kv_prev, sink_ref, no_prev):
    rows = SWA_GROUP * WINDOW
    i = lax.broadcasted_iota(jnp.int32, (rows, WINDOW), 0) & (WINDOW - 1)
    j = lax.broadcasted_iota(jnp.int32, (rows, WINDOW), 1)
    use_prev = j > i
    outs = []
    for kh in range(SWA_KV_HEADS):
        vs = slice(SWA_KV_WIDTH + kh * SWA_HEAD_DIM, SWA_KV_WIDTH + (kh + 1) * SWA_HEAD_DIM)
        sink = _sink_column(sink_ref, kh, WINDOW)
        s_prev, s_cur = scores[kh]
        if no_prev is not None:
            s_prev = jnp.where(no_prev, NEG_BIG, s_prev)
        s = jnp.where(use_prev, s_prev, s_cur)
        m = jnp.maximum(jnp.max(s, axis=-1, keepdims=True), sink)
        p = jnp.exp(s - m)
        den = jnp.sum(p, axis=-1, keepdims=True) + jnp.exp(sink - m)
        o = (_dot(jnp.where(use_prev, p, 0.0), kv_prev[:, vs])
             + _dot(jnp.where(use_prev, 0.0, p), kv_cur[:, vs])) / den
        outs.extend(o[g * WINDOW:(g + 1) * WINDOW] for g in range(SWA_GROUP))
    return jnp.concatenate(outs, axis=-1)


def _swa_sample_body(sink_ref, zs_ref, ck_ref, cv_ref, o_ref, cko_ref, cvo_ref, *, nb, s_new, wb):
    nq = nb * s_new
    rows = SWA_GROUP * nq
    ls, lw = _log2(s_new), _log2(wb)
    _log2(nb)
    r = lax.broadcasted_iota(jnp.int32, (rows, nb * wb), 0)
    c = lax.broadcasted_iota(jnp.int32, (rows, nb * wb), 1)
    mask_c = (((c >> lw) == ((r >> ls) & (nb - 1)))
              & ((c & (wb - 1)) > (r & (s_new - 1)) + (wb - WINDOW)))
    r = lax.broadcasted_iota(jnp.int32, (rows, nq), 0)
    c = lax.broadcasted_iota(jnp.int32, (rows, nq), 1)
    mask_n = ((c >> ls) == ((r >> ls) & (nb - 1))) & ((c & (s_new - 1)) <= (r & (s_new - 1)))

    z = zs_ref[...].reshape(nq, N_SWA_IN)
    k_new = z[:, SWA_WIDTH:SWA_WIDTH + SWA_KV_WIDTH]
    v_new = z[:, SWA_WIDTH + SWA_KV_WIDTH:]
    for b in range(nb):
        cko_ref[b, 0:wb - s_new, :] = ck_ref[b, s_new:, :]
        cko_ref[b, wb - s_new:, :] = k_new[b * s_new:(b + 1) * s_new]
        cvo_ref[b, 0:wb - s_new, :] = cv_ref[b, s_new:, :]
        cvo_ref[b, wb - s_new:, :] = v_new[b * s_new:(b + 1) * s_new]
    ck = ck_ref[...].reshape(nb * wb, SWA_KV_WIDTH)
    cv = cv_ref[...].reshape(nb * wb, SWA_KV_WIDTH)
    for kh in range(SWA_KV_HEADS):
        cs = slice(kh * SWA_HEAD_DIM, (kh + 1) * SWA_HEAD_DIM)
        sink = _sink_column(sink_ref, kh, nq)
        q = _stack_heads(z, kh) * SWA_SCALE
        sc = jnp.where(mask_c, _dot_nt(q, ck[:, cs]), NEG_BIG)
        sn = jnp.where(mask_n, _dot_nt(q, k_new[:, cs]), NEG_BIG)
        m = jnp.maximum(jnp.maximum(jnp.max(sc, axis=-1, keepdims=True),
                                    jnp.max(sn, axis=-1, keepdims=True)), sink)
        pc = jnp.where(mask_c, jnp.exp(sc - m), 0.0)
        pn = jnp.where(mask_n, jnp.exp(sn - m), 0.0)
        den = (jnp.sum(pc, axis=-1, keepdims=True) + jnp.sum(pn, axis=-1, keepdims=True)
               + jnp.exp(sink - m))
        o = (_dot(pc, cv[:, cs]) + _dot(pn, v_new[:, cs])) / den
        for g in range(SWA_GROUP):
            h = kh * SWA_GROUP + g
            o_ref[:, :, h * SWA_HEAD_DIM:(h + 1) * SWA_HEAD_DIM] = (
                o[g * nq:(g + 1) * nq].reshape(nb, s_new, SWA_HEAD_DIM))


def _swa_sample(zs3d, ck, cv, sinks, nb):
    bsz, s_new, _ = zs3d.shape
    wb = ck.shape[1]
    cache_spec = pl.BlockSpec((nb, wb, SWA_KV_WIDTH), lambda b: (b, 0, 0))
    return pl.pallas_call(
        functools.partial(_swa_sample_body, nb=nb, s_new=s_new, wb=wb),
        grid=(bsz // nb,),
        in_specs=[
            pl.BlockSpec(memory_space=pltpu.SMEM),
            pl.BlockSpec((nb, s_new, N_SWA_IN), lambda b: (b, 0, 0)),
            cache_spec,
            cache_spec,
        ],
        out_specs=[
            pl.BlockSpec((nb, s_new, SWA_WIDTH), lambda b: (b, 0, 0)),
            cache_spec,
            cache_spec,
        ],
        out_shape=[
            jax.ShapeDtypeStruct((bsz, s_new, SWA_WIDTH), F32),
            jax.ShapeDtypeStruct(ck.shape, ck.dtype),
            jax.ShapeDtypeStruct(cv.shape, cv.dtype),
        ],
        compiler_params=pltpu.CompilerParams(
            dimension_semantics=("arbitrary",), vmem_limit_bytes=VMEM_LIMIT_BYTES),
        name="swa_sample",
    )(sinks, zs3d, ck, cv)


def _out_mlp_body(x_ref, oh_ref, os_ref, wo_ref, lm_ref, wu_ref, wd_ref, lf_ref, y_ref):
    o = jnp.concatenate([oh_ref[...].astype(BF16), os_ref[...].astype(BF16)], axis=-1)
    h = x_ref[...] + jnp.dot(o, wo_ref[...], preferred_element_type=F32)
    hn = _rms(h, lm_ref[...]).astype(BF16)
    u = jnp.dot(hn, wu_ref[...], preferred_element_type=F32)
    a = jnp.square(jnp.maximum(u, 0.0)).astype(BF16)
    y_ref[...] = _rms(h + jnp.dot(a, wd_ref[...], preferred_element_type=F32), lf_ref[...])


def _out_mlp(x2d, oh2d, os2d, w_out, ln_mlp, w_up, w_down, ln_final, tm):
    n = x2d.shape[0]
    const = lambda i: (0, 0)
    single = pl.Buffered(1)
    return pl.pallas_call(
        _out_mlp_body,
        grid=(n // tm,),
        in_specs=[
            pl.BlockSpec((tm, D_MODEL), lambda i: (i, 0)),
            pl.BlockSpec((tm, HG_WIDTH), lambda i: (i, 0)),
            pl.BlockSpec((tm, SWA_WIDTH), lambda i: (i, 0)),
            pl.BlockSpec((D_MODEL, D_MODEL), const, pipeline_mode=single),
            pl.BlockSpec((1, D_MODEL), const),
            pl.BlockSpec((D_MODEL, D_FF), const, pipeline_mode=single),
            pl.BlockSpec((D_FF, D_MODEL), const, pipeline_mode=single),
            pl.BlockSpec((1, D_MODEL), const),
        ],
        out_specs=pl.BlockSpec((tm, D_MODEL), lambda i: (i, 0)),
        out_shape=jax.ShapeDtypeStruct((n, D_MODEL), F32),
        compiler_params=pltpu.CompilerParams(
            dimension_semantics=("arbitrary",), vmem_limit_bytes=VMEM_LIMIT_BYTES),
        name="out_mlp",
    )(x2d, oh2d, os2d, w_out, ln_mlp, w_up, w_down, ln_final)


PROMPT_TILE = 512
GLA_CHUNK = 64
FF_CHUNK = 1024
KV_COLS = slice(N_HG_IN + SWA_WIDTH, N_IN)


def _prompt_body(sink_ref, xc_ref, xp_ref, lmix_ref, win_ref, lbl_ref, gn_ref, wo_ref, lmlp_ref,
                 wu_ref, wd_ref, lfin_ref, y_ref, so_ref, ko_ref, vo_ref,
                 z_ref, o_ref, st_ref, kvp_ref, acc_ref, *, tiles_per_seq, n_tiles):
    s = pl.program_id(0)
    pos = s % tiles_per_seq
    seq_start = pos == 0
    tm = PROMPT_TILE

    @pl.when(s == 0)
    def _first():
        o_ref[...] = jnp.zeros(o_ref.shape, o_ref.dtype)

    @pl.when(seq_start)
    def _reset():
        st_ref[...] = jnp.zeros(st_ref.shape, st_ref.dtype)
        kvp_ref[...] = jnp.zeros(kvp_ref.shape, kvp_ref.dtype)

    acc_ref[...] = xp_ref[...] + jnp.dot(o_ref[...], wo_ref[...], preferred_element_type=F32)
    hn = _rms(acc_ref[...], lmlp_ref[...]).astype(BF16)

    xn = _rms(xc_ref[...], lmix_ref[...]).astype(BF16)
    z_ref[...] = jnp.dot(xn, win_ref[...], preferred_element_type=F32)
    lb = _lower_bound(lbl_ref[...])
    g_norm = gn_ref[...]

    n_ff = D_FF // FF_CHUNK
    n_blk = tm // WINDOW
    chunks_per_blk = WINDOW // GLA_CHUNK

    def mlp_up(c):
        cs = slice(c * FF_CHUNK, (c + 1) * FF_CHUNK)
        u = jnp.dot(hn, wu_ref[:, cs], preferred_element_type=F32)
        return jnp.square(jnp.maximum(u, 0.0)).astype(BF16)

    def mlp_down(c, a):
        cs = slice(c * FF_CHUNK, (c + 1) * FF_CHUNK)
        acc_ref[...] += jnp.dot(a, wd_ref[cs, :], preferred_element_type=F32)

    def kv_blocks(n):
        kv_prev = kvp_ref[...] if n == 0 else z_ref[(n - 1) * WINDOW:n * WINDOW, KV_COLS]
        return z_ref[n * WINDOW:(n + 1) * WINDOW, KV_COLS], kv_prev

    def mix_front(n):
        gla = [_gla_front(z_ref[c * GLA_CHUNK:(c + 1) * GLA_CHUNK, 0:N_HG_IN], lb)
               for c in range(n * chunks_per_blk, (n + 1) * chunks_per_blk)]
        kv_cur, kv_prev = kv_blocks(n)
        swa = _swa_front(z_ref[n * WINDOW:(n + 1) * WINDOW, N_HG_IN:N_HG_IN + SWA_WIDTH],
                         kv_cur, kv_prev)
        return gla, swa

    def mix_back(n, front):
        gla, swa = front
        for i, c in enumerate(range(n * chunks_per_blk, (n + 1) * chunks_per_blk)):
            o_ref[c * GLA_CHUNK:(c + 1) * GLA_CHUNK, 0:HG_WIDTH] = _gla_back(
                gla[i], g_norm, st_ref, 0).astype(BF16)
        kv_cur, kv_prev = kv_blocks(n)
        o_ref[n * WINDOW:(n + 1) * WINDOW, HG_WIDTH:] = _swa_back(
            swa, kv_cur, kv_prev, sink_ref, seq_start if n == 0 else None).astype(BF16)

    work = []
    for c in range(n_ff):
        work += [("up", c), ("down", c)]
    for n in range(n_blk):
        work.insert(3 * n + 1, ("front", n))
        work.insert(3 * n + 3, ("back", n))
    acts, fronts = {}, {}
    for kind, i in work:
        if kind == "up":
            acts[i] = mlp_up(i)
        elif kind == "down":
            mlp_down(i, acts.pop(i))
        elif kind == "front":
            fronts[i] = mix_front(i)
        else:
            mix_back(i, fronts.pop(i))
    kvp_ref[...] = z_ref[tm - WINDOW:tm, KV_COLS]
    y_ref[...] = _rms(acc_ref[...], lfin_ref[...])

    @pl.when((pos == tiles_per_seq - 1) & (s < n_tiles))
    def _seq_end():
        for h in range(HG_HEADS):
            so_ref[0, h] = st_ref[0, h].T
        ko_ref[0] = z_ref[tm - WINDOW:tm, N_HG_IN + SWA_WIDTH:N_HG_IN + SWA_WIDTH + SWA_KV_WIDTH]
        vo_ref[0] = z_ref[tm - WINDOW:tm, N_HG_IN + SWA_WIDTH + SWA_KV_WIDTH:N_IN]


def _prompt_layer(x2d, bsz, sinks, ln_mix, w_in, lb_logits, hg_norm, w_out, ln_mlp, w_up, w_down, ln_final):
    n = x2d.shape[0]
    tm = PROMPT_TILE
    n_tiles = n // tm
    tiles_per_seq = n_tiles // bsz
    assert tiles_per_seq * bsz * tm == n
    const = lambda s: (0, 0)
    single = pl.Buffered(1)
    seq_of = lambda s: jnp.minimum(s, n_tiles - 1) // tiles_per_seq
    return pl.pallas_call(
        functools.partial(_prompt_body, tiles_per_seq=tiles_per_seq, n_tiles=n_tiles),
        grid=(n_tiles + 1,),
        in_specs=[
            pl.BlockSpec(memory_space=pltpu.SMEM),
            pl.BlockSpec((tm, D_MODEL), lambda s: (jnp.minimum(s, n_tiles - 1), 0)),
            pl.BlockSpec((tm, D_MODEL), lambda s: (jnp.maximum(s - 1, 0), 0)),
            pl.BlockSpec((1, D_MODEL), const),
            pl.BlockSpec((D_MODEL, N_IN), const, pipeline_mode=single),
            pl.BlockSpec(lb_logits.shape, const),
            pl.BlockSpec((1, HG_DV), const),
            pl.BlockSpec((D_MODEL, D_MODEL), const, pipeline_mode=single),
            pl.BlockSpec((1, D_MODEL), const),
            pl.BlockSpec((D_MODEL, D_FF), const, pipeline_mode=single),
            pl.BlockSpec((D_FF, D_MODEL), const, pipeline_mode=single),
            pl.BlockSpec((1, D_MODEL), const),
        ],
        out_specs=[
            pl.BlockSpec((tm, D_MODEL), lambda s: (jnp.maximum(s - 1, 0), 0)),
            pl.BlockSpec((1, HG_HEADS, HG_DK, HG_DV), lambda s: (seq_of(s), 0, 0, 0)),
            pl.BlockSpec((1, WINDOW, SWA_KV_WIDTH), lambda s: (seq_of(s), 0, 0)),
            pl.BlockSpec((1, WINDOW, SWA_KV_WIDTH), lambda s: (seq_of(s), 0, 0)),
        ],
        out_shape=[
            jax.ShapeDtypeStruct((n, D_MODEL), F32),
            jax.ShapeDtypeStruct((bsz, HG_HEADS, HG_DK, HG_DV), F32),
            jax.ShapeDtypeStruct((bsz, WINDOW, SWA_KV_WIDTH), F32),
            jax.ShapeDtypeStruct((bsz, WINDOW, SWA_KV_WIDTH), F32),
        ],
        scratch_shapes=[
            pltpu.VMEM((tm, N_IN), F32),
            pltpu.VMEM((tm, D_MODEL), BF16),
            pltpu.VMEM((1, HG_HEADS, HG_DV, HG_DK), F32),
            pltpu.VMEM((WINDOW, 2 * SWA_KV_WIDTH), F32),
            pltpu.VMEM((tm, D_MODEL), F32),
        ],
        compiler_params=pltpu.CompilerParams(
            dimension_semantics=("arbitrary",), vmem_limit_bytes=VMEM_LIMIT_BYTES),
        name="prompt_layer",
    )(sinks, x2d, x2d, ln_mix, w_in, lb_logits, hg_norm, w_out, ln_mlp, w_up, w_down, ln_final)


def kernel(x_prompt, x_sample, state_hgrn, cache_swa_k, cache_swa_v, ln_mix, w_in, lb_logits,
           hg_norm, sinks, w_out, ln_mlp, w_up, w_down, ln_final):
    depth = w_in.shape[0]
    assert depth == 1 and lb_logits.shape[0] == depth + 1
    bsz, seq, _ = x_prompt.shape
    dbsz, dseq, _ = x_sample.shape
    wb = cache_swa_k.shape[2]

    w_in_b = w_in[0].astype(BF16)
    w_out_b = w_out[0].astype(BF16)
    w_up_b = w_up[0].astype(BF16)
    w_down_b = w_down[0].astype(BF16)
    ln_mix2 = ln_mix[0].reshape(1, D_MODEL)
    ln_mlp2 = ln_mlp[0].reshape(1, D_MODEL)
    ln_fin2 = ln_final.reshape(1, D_MODEL)
    gn2 = hg_norm[0].reshape(1, HG_DV)
    sink1 = sinks[0]

    def mix_and_mlp(x2d, oh, osw, tm):
        return _out_mlp(x2d, oh, osw, w_out_b, ln_mlp2, w_up_b, w_down_b, ln_fin2, tm)

    assert seq % PROMPT_TILE == 0 and seq >= WINDOW
    wp = WINDOW
    xp = x_prompt.reshape(bsz * seq, D_MODEL)
    y_p, s_p, k_p, v_p = _prompt_layer(xp, bsz, sink1, ln_mix2, w_in_b, lb_logits, gn2, w_out_b,
                                       ln_mlp2, w_up_b, w_down_b, ln_fin2)

    xs = x_sample.reshape(dbsz * dseq, D_MODEL)
    zh_s, zs_s = _inproj(xs, ln_mix2, w_in_b, 512)
    oh_s, s_s = _gla(zh_s.reshape(dbsz, dseq, N_HG_IN), lb_logits, gn2, state_hgrn[0], 8, dseq, dseq)
    os_s, ck_s, cv_s = _swa_sample(
        zs_s.reshape(dbsz, dseq, N_SWA_IN),
        cache_swa_k[0].reshape(dbsz, wb, SWA_KV_WIDTH),
        cache_swa_v[0].reshape(dbsz, wb, SWA_KV_WIDTH), sink1, 8)
    y_s = mix_and_mlp(xs, oh_s.reshape(-1, HG_WIDTH), os_s.reshape(-1, SWA_WIDTH), 512)

    kv_shape_p = (1, bsz, wp, SWA_KV_HEADS, SWA_HEAD_DIM)
    kv_shape_s = (1, dbsz, wb, SWA_KV_HEADS, SWA_HEAD_DIM)
    return (y_p.reshape(bsz, seq, D_MODEL),
            y_s.reshape(dbsz, dseq, D_MODEL),
            s_p[None],
            k_p.reshape(kv_shape_p).astype(cache_swa_k.dtype),
            v_p.reshape(kv_shape_p).astype(cache_swa_v.dtype),
            s_s[None],
            ck_s.reshape(kv_shape_s),
            cv_s.reshape(kv_shape_s))
```

```python
import functools

import jax
import jax.numpy as jnp
from jax import lax
from jax.experimental import pallas as pl
from jax.experimental.pallas import tpu as pltpu

F32 = jnp.float32
BF16 = jnp.bfloat16

D_MODEL = 1024
HG_WIDTH = 512
HG_HEADS = 4
HG_DK = 128
HG_DV = 128
SWA_WIDTH = 512
SWA_HEAD_DIM = 64
SWA_Q_HEADS = 8
SWA_KV_HEADS = 2
SWA_GROUP = SWA_Q_HEADS // SWA_KV_HEADS
SWA_KV_WIDTH = SWA_KV_HEADS * SWA_HEAD_DIM
WINDOW = 128
SWA_SCALE = SWA_HEAD_DIM ** -0.5
D_FF = 4 * D_MODEL
EPS = 1e-6
N_HG_IN = 4 * HG_WIDTH
N_SWA_IN = SWA_WIDTH + 2 * SWA_KV_WIDTH
N_IN = N_HG_IN + N_SWA_IN
NEG_BIG = -1e30

VMEM_LIMIT_BYTES = 56 * 1024 * 1024


def _rms(x, g):
    return x * lax.rsqrt(jnp.mean(x * x, axis=-1, keepdims=True) + EPS) * g


def _sigmoid(x):
    return 1.0 / (1.0 + jnp.exp(-x))


def _dot(a, b):
    return jnp.dot(a.astype(BF16), b.astype(BF16), preferred_element_type=F32)


def _dot_nt(a, b):
    return lax.dot_general(a.astype(BF16), b.astype(BF16), (((1,), (1,)), ((), ())),
                           preferred_element_type=F32)


def _dot_tn(a, b):
    return lax.dot_general(a.astype(BF16), b.astype(BF16), (((0,), (0,)), ((), ())),
                           preferred_element_type=F32)


def _inproj_body(x_ref, g_ref, w_ref, zh_ref, zs_ref):
    xn = _rms(x_ref[...], g_ref[...]).astype(BF16)
    zh_ref[...] = jnp.dot(xn, w_ref[:, :N_HG_IN], preferred_element_type=F32)
    zs_ref[...] = jnp.dot(xn, w_ref[:, N_HG_IN:], preferred_element_type=F32)


def _inproj(x2d, ln, w_in_bf16, tm):
    n = x2d.shape[0]
    return pl.pallas_call(
        _inproj_body,
        grid=(n // tm,),
        in_specs=[
            pl.BlockSpec((tm, D_MODEL), lambda i: (i, 0)),
            pl.BlockSpec((1, D_MODEL), lambda i: (0, 0)),
            pl.BlockSpec((D_MODEL, N_IN), lambda i: (0, 0)),
        ],
        out_specs=[
            pl.BlockSpec((tm, N_HG_IN), lambda i: (i, 0)),
            pl.BlockSpec((tm, N_SWA_IN), lambda i: (i, 0)),
        ],
        out_shape=[
            jax.ShapeDtypeStruct((n, N_HG_IN), F32),
            jax.ShapeDtypeStruct((n, N_SWA_IN), F32),
        ],
        compiler_params=pltpu.CompilerParams(
            dimension_semantics=("arbitrary",), vmem_limit_bytes=VMEM_LIMIT_BYTES),
        name="inproj",
    )(x2d, ln, w_in_bf16)


def _cumsum_rows(x):
    n = x.shape[0]
    row = lax.broadcasted_iota(jnp.int32, x.shape, 0)
    s = 1
    while s < n:
        x = x + jnp.where(row >= s, pltpu.roll(x, s, axis=0), 0.0)
        s *= 2
    return x


def _lower_bound(lb_logits):
    m = jnp.max(lb_logits, axis=0, keepdims=True)
    e = jnp.exp(lb_logits - m)
    return e[0:1, :] / jnp.sum(e, axis=0, keepdims=True)


def _gla_front(zh, lb):
    chunk = zh.shape[0]
    zq = zh[:, 0:HG_WIDTH]
    zf = zh[:, HG_WIDTH:2 * HG_WIDTH]
    v = zh[:, 2 * HG_WIDTH:3 * HG_WIDTH]
    zg = zh[:, 3 * HG_WIDTH:4 * HG_WIDTH]

    q = zq * _sigmoid(zq)
    f = lb + (1.0 - lb) * _sigmoid(zf)
    k = 1.0 - f
    b = _cumsum_rows(jnp.log(f))
    mid = chunk // 2 - 1
    b_mid = b[mid:mid + 1, :]
    b_last = b[chunk - 1:chunk, :]
    qf = q * jnp.exp(b - b_mid)
    kf = k * jnp.exp(b_mid - b)
    qt = qf * jnp.exp(b_mid)
    ke = kf * jnp.exp(b_last - b_mid)
    dec = jnp.exp(b_last)
    gate = zg * _sigmoid(zg)
    v = v.astype(BF16)
    heads = [slice(h * HG_DK, (h + 1) * HG_DK) for h in range(HG_HEADS)]
    scores = [_dot_nt(qf[:, sl], kf[:, sl]) for sl in heads]
    kv = [_dot_tn(v[:, sl], ke[:, sl]) for sl in heads]
    return scores, kv, qt.astype(BF16), v, dec, gate


def _gla_back(front, g_norm, st_ref, j):
    scores, kv, qt, v, dec, gate = front
    chunk = qt.shape[0]
    row = lax.broadcasted_iota(jnp.int32, (chunk, chunk), 0)
    col = lax.broadcasted_iota(jnp.int32, (chunk, chunk), 1)
    causal = row >= col
    outs = []
    for h in range(HG_HEADS):
        sl = slice(h * HG_DK, (h + 1) * HG_DK)
        a = jnp.where(causal, scores[h], 0.0)
        st = st_ref[j, h]
        o = _dot(a, v[:, sl]) + _dot_nt(qt[:, sl], st)
        st_ref[j, h] = st * dec[:, sl] + kv[h]
        outs.append(_rms(o, g_norm) * gate[:, sl])
    return jnp.concatenate(outs, axis=-1)


def _gla_chunk(zh, lb, g_norm, st_ref, j, chunk):
    assert zh.shape[0] == chunk
    return _gla_back(_gla_front(zh, lb), g_norm, st_ref, j)


def _gla_body(*refs, nb, tb, chunk, has_s0):
    if has_s0:
        zh_ref, lbl_ref, gn_ref, s0_ref, o_ref, so_ref, st_ref = refs
    else:
        zh_ref, lbl_ref, gn_ref, o_ref, so_ref, st_ref = refs
        s0_ref = None
    t_idx = pl.program_id(1)
    n_t = pl.num_programs(1)

    @pl.when(t_idx == 0)
    def _init():
        for j in range(nb):
            for h in range(HG_HEADS):
                if has_s0:
                    st_ref[j, h] = s0_ref[j, h].T
                else:
                    st_ref[j, h] = jnp.zeros((HG_DV, HG_DK), F32)

    lb = _lower_bound(lbl_ref[...])
    g_norm = gn_ref[...]
    n_chunks = tb // chunk
    for j in range(nb):
        if n_chunks == 1:
            o_ref[j] = _gla_chunk(zh_ref[j], lb, g_norm, st_ref, j, chunk)
        else:
            def step(c, carry, j=j):
                r0 = pl.multiple_of(c * chunk, chunk)
                o_ref[j, pl.ds(r0, chunk), :] = _gla_chunk(
                    zh_ref[j, pl.ds(r0, chunk), :], lb, g_norm, st_ref, j, chunk)
                return carry
            lax.fori_loop(0, n_chunks, step, 0, unroll=True)

    @pl.when(t_idx == n_t - 1)
    def _fin():
        for j in range(nb):
            for h in range(HG_HEADS):
                so_ref[j, h] = st_ref[j, h].T


def _gla(zh3d, lb_logits, hg_norm, s0, nb, tb, chunk):
    bsz, t, _ = zh3d.shape
    has_s0 = s0 is not None
    in_specs = [
        pl.BlockSpec((nb, tb, N_HG_IN), lambda b, i: (b, i, 0)),
        pl.BlockSpec(lb_logits.shape, lambda b, i: (0, 0)),
        pl.BlockSpec((1, HG_DV), lambda b, i: (0, 0)),
    ]
    args = [zh3d, lb_logits, hg_norm]
    if has_s0:
        in_specs.append(pl.BlockSpec((nb, HG_HEADS, HG_DK, HG_DV), lambda b, i: (b, 0, 0, 0)))
        args.append(s0)
    return pl.pallas_call(
        functools.partial(_gla_body, nb=nb, tb=tb, chunk=chunk, has_s0=has_s0),
        grid=(bsz // nb, t // tb),
        in_specs=in_specs,
        out_specs=[
            pl.BlockSpec((nb, tb, HG_WIDTH), lambda b, i: (b, i, 0)),
            pl.BlockSpec((nb, HG_HEADS, HG_DK, HG_DV), lambda b, i: (b, 0, 0, 0)),
        ],
        out_shape=[
            jax.ShapeDtypeStruct((bsz, t, HG_WIDTH), F32),
            jax.ShapeDtypeStruct((bsz, HG_HEADS, HG_DK, HG_DV), F32),
        ],
        scratch_shapes=[pltpu.VMEM((nb, HG_HEADS, HG_DV, HG_DK), F32)],
        compiler_params=pltpu.CompilerParams(
            dimension_semantics=("arbitrary", "arbitrary"), vmem_limit_bytes=VMEM_LIMIT_BYTES),
        name="gla_s0" if has_s0 else "gla",
    )(*args)


def _log2(n):
    assert n > 0 and n & (n - 1) == 0, n
    return n.bit_length() - 1


def _stack_heads(x, kh):
    return jnp.concatenate(
        [x[:, (kh * SWA_GROUP + g) * SWA_HEAD_DIM:(kh * SWA_GROUP + g + 1) * SWA_HEAD_DIM]
         for g in range(SWA_GROUP)], axis=0)


def _sink_column(sink_ref, kh, rows_per_head):
    r = lax.broadcasted_iota(jnp.int32, (SWA_GROUP * rows_per_head, 1), 0)
    col = jnp.full(r.shape, sink_ref[kh * SWA_GROUP], F32)
    for g in range(1, SWA_GROUP):
        col = jnp.where(r >= g * rows_per_head, sink_ref[kh * SWA_GROUP + g], col)
    return col


def _swa_front(zq, kv_cur, kv_prev):
    scores = []
    for kh in range(SWA_KV_HEADS):
        ks = slice(kh * SWA_HEAD_DIM, (kh + 1) * SWA_HEAD_DIM)
        q = (_stack_heads(zq, kh) * SWA_SCALE).astype(BF16)
        scores.append((_dot_nt(q, kv_prev[:, ks]), _dot_nt(q, kv_cur[:, ks])))
    return scores


def _swa_back(scores, kv_cur, kv_prev, sink_ref, no_prev):
    rows = SWA_GROUP * WINDOW
    i = lax.broadcasted_iota(jnp.int32, (rows, WINDOW), 0) & (WINDOW - 1)
    j = lax.broadcasted_iota(jnp.int32, (rows, WINDOW), 1)
    use_prev = j > i
    outs = []
    for kh in range(SWA_KV_HEADS):
        vs = slice(SWA_KV_WIDTH + kh * SWA_HEAD_DIM, SWA_KV_WIDTH + (kh + 1) * SWA_HEAD_DIM)
        sink = _sink_column(sink_ref, kh, WINDOW)
        s_prev, s_cur = scores[kh]
        if no_prev is not None:
            s_prev = jnp.where(no_prev, NEG_BIG, s_prev)
        s = jnp.where(use_prev, s_prev, s_cur)
        m = jnp.maximum(jnp.max(s, axis=-1, keepdims=True), sink)
        p = jnp.exp(s - m)
        den = jnp.sum(p, axis=-1, keepdims=True) + jnp.exp(sink - m)
        o = (_dot(jnp.where(use_prev, p, 0.0), kv_prev[:, vs])
             + _dot(jnp.where(use_prev, 0.0, p), kv_cur[:, vs])) / den
        outs.extend(o[g * WINDOW:(g + 1) * WINDOW] for g in range(SWA_GROUP))
    return jnp.concatenate(outs, axis=-1)


def _swa_sample_body(sink_ref, zs_ref, ck_ref, cv_ref, o_ref, cko_ref, cvo_ref, *, nb, s_new, wb):
    nq = nb * s_new
    rows = SWA_GROUP * nq
    ls, lw = _log2(s_new), _log2(wb)
    _log2(nb)
    r = lax.broadcasted_iota(jnp.int32, (rows, nb * wb), 0)
    c = lax.broadcasted_iota(jnp.int32, (rows, nb * wb), 1)
    mask_c = (((c >> lw) == ((r >> ls) & (nb - 1)))
              & ((c & (wb - 1)) > (r & (s_new - 1)) + (wb - WINDOW)))
    r = lax.broadcasted_iota(jnp.int32, (rows, nq), 0)
    c = lax.broadcasted_iota(jnp.int32, (rows, nq), 1)
    mask_n = ((c >> ls) == ((r >> ls) & (nb - 1))) & ((c & (s_new - 1)) <= (r & (s_new - 1)))

    z = zs_ref[...].reshape(nq, N_SWA_IN)
    k_new = z[:, SWA_WIDTH:SWA_WIDTH + SWA_KV_WIDTH]
    v_new = z[:, SWA_WIDTH + SWA_KV_WIDTH:]
    for b in range(nb):
        cko_ref[b, 0:wb - s_new, :] = ck_ref[b, s_new:, :]
        cko_ref[b, wb - s_new:, :] = k_new[b * s_new:(b + 1) * s_new]
        cvo_ref[b, 0:wb - s_new, :] = cv_ref[b, s_new:, :]
        cvo_ref[b, wb - s_new:, :] = v_new[b * s_new:(b + 1) * s_new]
    ck = ck_ref[...].reshape(nb * wb, SWA_KV_WIDTH)
    cv = cv_ref[...].reshape(nb * wb, SWA_KV_WIDTH)
    for kh in range(SWA_KV_HEADS):
        cs = slice(kh * SWA_HEAD_DIM, (kh + 1) * SWA_HEAD_DIM)
        sink = _sink_column(sink_ref, kh, nq)
        q = _stack_heads(z, kh) * SWA_SCALE
        sc = jnp.where(mask_c, _dot_nt(q, ck[:, cs]), NEG_BIG)
        sn = jnp.where(mask_n, _dot_nt(q, k_new[:, cs]), NEG_BIG)
        m = jnp.maximum(jnp.maximum(jnp.max(sc, axis=-1, keepdims=True),
                                    jnp.max(sn, axis=-1, keepdims=True)), sink)
        pc = jnp.where(mask_c, jnp.exp(sc - m), 0.0)
        pn = jnp.where(mask_n, jnp.exp(sn - m), 0.0)
        den = (jnp.sum(pc, axis=-1, keepdims=True) + jnp.sum(pn, axis=-1, keepdims=True)
               + jnp.exp(sink - m))
        o = (_dot(pc, cv[:, cs]) + _dot(pn, v_new[:, cs])) / den
        for g in range(SWA_GROUP):
            h = kh * SWA_GROUP + g
            o_ref[:, :, h * SWA_HEAD_DIM:(h + 1) * SWA_HEAD_DIM] = (
                o[g * nq:(g + 1) * nq].reshape(nb, s_new, SWA_HEAD_DIM))


def _swa_sample(zs3d, ck, cv, sinks, nb):
    bsz, s_new, _ = zs3d.shape
    wb = ck.shape[1]
    cache_spec = pl.BlockSpec((nb, wb, SWA_KV_WIDTH), lambda b: (b, 0, 0))
    return pl.pallas_call(
        functools.partial(_swa_sample_body, nb=nb, s_new=s_new, wb=wb),
        grid=(bsz // nb,),
        in_specs=[
            pl.BlockSpec(memory_space=pltpu.SMEM),
            pl.BlockSpec((nb, s_new, N_SWA_IN), lambda b: (b, 0, 0)),
            cache_spec,
            cache_spec,
        ],
        out_specs=[
            pl.BlockSpec((nb, s_new, SWA_WIDTH), lambda b: (b, 0, 0)),
            cache_spec,
            cache_spec,
        ],
        out_shape=[
            jax.ShapeDtypeStruct((bsz, s_new, SWA_WIDTH), F32),
            jax.ShapeDtypeStruct(ck.shape, ck.dtype),
            jax.ShapeDtypeStruct(cv.shape, cv.dtype),
        ],
        compiler_params=pltpu.CompilerParams(
            dimension_semantics=("arbitrary",), vmem_limit_bytes=VMEM_LIMIT_BYTES),
        name="swa_sample",
    )(sinks, zs3d, ck, cv)


def _out_mlp_body(x_ref, oh_ref, os_ref, wo_ref, lm_ref, wu_ref, wd_ref, lf_ref, y_ref):
    o = jnp.concatenate([oh_ref[...].astype(BF16), os_ref[...].astype(BF16)], axis=-1)
    h = x_ref[...] + jnp.dot(o, wo_ref[...], preferred_element_type=F32)
    hn = _rms(h, lm_ref[...]).astype(BF16)
    u = jnp.dot(hn, wu_ref[...], preferred_element_type=F32)
    a = jnp.square(jnp.maximum(u, 0.0)).astype(BF16)
    y_ref[...] = _rms(h + jnp.dot(a, wd_ref[...], preferred_element_type=F32), lf_ref[...])


def _out_mlp(x2d, oh2d, os2d, w_out, ln_mlp, w_up, w_down, ln_final, tm):
    n = x2d.shape[0]
    const = lambda i: (0, 0)
    single = pl.Buffered(1)
    return pl.pallas_call(
        _out_mlp_body,
        grid=(n // tm,),
        in_specs=[
            pl.BlockSpec((tm, D_MODEL), lambda i: (i, 0)),
            pl.BlockSpec((tm, HG_WIDTH), lambda i: (i, 0)),
            pl.BlockSpec((tm, SWA_WIDTH), lambda i: (i, 0)),
            pl.BlockSpec((D_MODEL, D_MODEL), const, pipeline_mode=single),
            pl.BlockSpec((1, D_MODEL), const),
            pl.BlockSpec((D_MODEL, D_FF), const, pipeline_mode=single),
            pl.BlockSpec((D_FF, D_MODEL), const, pipeline_mode=single),
            pl.BlockSpec((1, D_MODEL), const),
        ],
        out_specs=pl.BlockSpec((tm, D_MODEL), lambda i: (i, 0)),
        out_shape=jax.ShapeDtypeStruct((n, D_MODEL), F32),
        compiler_params=pltpu.CompilerParams(
            dimension_semantics=("arbitrary",), vmem_limit_bytes=VMEM_LIMIT_BYTES),
        name="out_mlp",
    )(x2d, oh2d, os2d, w_out, ln_mlp, w_up, w_down, ln_final)


PROMPT_TILE = 512
GLA_CHUNK = 64
FF_CHUNK = 1024
KV_COLS = slice(N_HG_IN + SWA_WIDTH, N_IN)


def _prompt_body(sink_ref, xc_ref, lmix_ref, win_ref, lbl_ref, gn_ref, wo_ref, lmlp_ref,
                 wu_ref, wd_ref, lfin_ref, y_ref, so_ref, ko_ref, vo_ref,
                 z_ref, o_ref, st_ref, kvp_ref, h_ref, acc_ref, *, tiles_per_seq, n_tiles):
    s = pl.program_id(0)
    pos = s % tiles_per_seq
    seq_start = pos == 0
    tm = PROMPT_TILE

    @pl.when(s == 0)
    def _first():
        h_ref[...] = jnp.zeros(h_ref.shape, h_ref.dtype)

    @pl.when(seq_start)
    def _reset():
        st_ref[...] = jnp.zeros(st_ref.shape, st_ref.dtype)
        kvp_ref[...] = jnp.zeros(kvp_ref.shape, kvp_ref.dtype)

    hn = _rms(h_ref[...], lmlp_ref[...]).astype(BF16)

    xn = _rms(xc_ref[...], lmix_ref[...]).astype(BF16)
    z_ref[...] = jnp.dot(xn, win_ref[...], preferred_element_type=F32)
    lb = _lower_bound(lbl_ref[...])
    g_norm = gn_ref[...]

    n_ff = D_FF // FF_CHUNK
    n_blk = tm // WINDOW
    chunks_per_blk = WINDOW // GLA_CHUNK

    def mlp_up(c):
        cs = slice(c * FF_CHUNK, (c + 1) * FF_CHUNK)
        u = jnp.dot(hn, wu_ref[:, cs], preferred_element_type=F32)
        return jnp.square(jnp.maximum(u, 0.0)).astype(BF16)

    def mlp_down(c, a):
        cs = slice(c * FF_CHUNK, (c + 1) * FF_CHUNK)
        base = h_ref if c == 0 else acc_ref
        acc_ref[...] = base[...] + jnp.dot(a, wd_ref[cs, :], preferred_element_type=F32)

    def kv_blocks(n):
        kv_prev = kvp_ref[...] if n == 0 else z_ref[(n - 1) * WINDOW:n * WINDOW, KV_COLS]
        return z_ref[n * WINDOW:(n + 1) * WINDOW, KV_COLS], kv_prev

    def mix_front(n):
        gla = [_gla_front(z_ref[c * GLA_CHUNK:(c + 1) * GLA_CHUNK, 0:N_HG_IN], lb)
               for c in range(n * chunks_per_blk, (n + 1) * chunks_per_blk)]
        kv_cur, kv_prev = kv_blocks(n)
        swa = _swa_front(z_ref[n * WINDOW:(n + 1) * WINDOW, N_HG_IN:N_HG_IN + SWA_WIDTH],
                         kv_cur, kv_prev)
        return gla, swa

    def mix_back(n, front):
        gla, swa = front
        for i, c in enumerate(range(n * chunks_per_blk, (n + 1) * chunks_per_blk)):
            o_ref[c * GLA_CHUNK:(c + 1) * GLA_CHUNK, 0:HG_WIDTH] = _gla_back(
                gla[i], g_norm, st_ref, 0).astype(BF16)
        kv_cur, kv_prev = kv_blocks(n)
        o_ref[n * WINDOW:(n + 1) * WINDOW, HG_WIDTH:] = _swa_back(
            swa, kv_cur, kv_prev, sink_ref, seq_start if n == 0 else None).astype(BF16)

    work = []
    for c in range(n_ff):
        work += [("up", c), ("down", c)]
    for n in range(n_blk):
        work.insert(3 * n + 1, ("front", n))
        work.insert(3 * n + 3, ("back", n))
    acts, fronts = {}, {}
    for kind, i in work:
        if kind == "up":
            acts[i] = mlp_up(i)
        elif kind == "down":
            mlp_down(i, acts.pop(i))
        elif kind == "front":
            fronts[i] = mix_front(i)
        else:
            mix_back(i, fronts.pop(i))
    kvp_ref[...] = z_ref[tm - WINDOW:tm, KV_COLS]
    h_ref[...] = xc_ref[...] + jnp.dot(o_ref[...], wo_ref[...], preferred_element_type=F32)
    y_ref[...] = _rms(acc_ref[...], lfin_ref[...])

    @pl.when((pos == tiles_per_seq - 1) & (s < n_tiles))
    def _seq_end():
        for h in range(HG_HEADS):
            so_ref[0, h] = st_ref[0, h].T
        ko_ref[0] = z_ref[tm - WINDOW:tm, N_HG_IN + SWA_WIDTH:N_HG_IN + SWA_WIDTH + SWA_KV_WIDTH]
        vo_ref[0] = z_ref[tm - WINDOW:tm, N_HG_IN + SWA_WIDTH + SWA_KV_WIDTH:N_IN]


def _prompt_layer(x2d, bsz, sinks, ln_mix, w_in, lb_logits, hg_norm, w_out, ln_mlp, w_up, w_down, ln_final):
    n = x2d.shape[0]
    tm = PROMPT_TILE
    n_tiles = n // tm
    tiles_per_seq = n_tiles // bsz
    assert tiles_per_seq * bsz * tm == n
    const = lambda s: (0, 0)
    single = pl.Buffered(1)
    seq_of = lambda s: jnp.minimum(s, n_tiles - 1) // tiles_per_seq
    return pl.pallas_call(
        functools.partial(_prompt_body, tiles_per_seq=tiles_per_seq, n_tiles=n_tiles),
        grid=(n_tiles + 1,),
        in_specs=[
            pl.BlockSpec(memory_space=pltpu.SMEM),
            pl.BlockSpec((tm, D_MODEL), lambda s: (jnp.minimum(s, n_tiles - 1), 0)),
            pl.BlockSpec((1, D_MODEL), const),
            pl.BlockSpec((D_MODEL, N_IN), const, pipeline_mode=single),
            pl.BlockSpec(lb_logits.shape, const),
            pl.BlockSpec((1, HG_DV), const),
            pl.BlockSpec((D_MODEL, D_MODEL), const, pipeline_mode=single),
            pl.BlockSpec((1, D_MODEL), const),
            pl.BlockSpec((D_MODEL, D_FF), const, pipeline_mode=single),
            pl.BlockSpec((D_FF, D_MODEL), const, pipeline_mode=single),
            pl.BlockSpec((1, D_MODEL), const),
        ],
        out_specs=[
            pl.BlockSpec((tm, D_MODEL), lambda s: (jnp.maximum(s - 1, 0), 0)),
            pl.BlockSpec((1, HG_HEADS, HG_DK, HG_DV), lambda s: (seq_of(s), 0, 0, 0)),
            pl.BlockSpec((1, WINDOW, SWA_KV_WIDTH), lambda s: (seq_of(s), 0, 0)),
            pl.BlockSpec((1, WINDOW, SWA_KV_WIDTH), lambda s: (seq_of(s), 0, 0)),
        ],
        out_shape=[
            jax.ShapeDtypeStruct((n, D_MODEL), F32),
            jax.ShapeDtypeStruct((bsz, HG_HEADS, HG_DK, HG_DV), F32),
            jax.ShapeDtypeStruct((bsz, WINDOW, SWA_KV_WIDTH), F32),
            jax.ShapeDtypeStruct((bsz, WINDOW, SWA_KV_WIDTH), F32),
        ],
        scratch_shapes=[
            pltpu.VMEM((tm, N_IN), F32),
            pltpu.VMEM((tm, D_MODEL), BF16),
            pltpu.VMEM((1, HG_HEADS, HG_DV, HG_DK), F32),
            pltpu.VMEM((WINDOW, 2 * SWA_KV_WIDTH), F32),
            pltpu.VMEM((tm, D_MODEL), F32),
            pltpu.VMEM((tm, D_MODEL), F32),
        ],
        compiler_params=pltpu.CompilerParams(
            dimension_semantics=("arbitrary",), vmem_limit_bytes=VMEM_LIMIT_BYTES),
        name="prompt_layer",
    )(sinks, x2d, ln_mix, w_in, lb_logits, hg_norm, w_out, ln_mlp, w_up, w_down, ln_final)


def kernel(x_prompt, x_sample, state_hgrn, cache_swa_k, cache_swa_v, ln_mix, w_in, lb_logits,
           hg_norm, sinks, w_out, ln_mlp, w_up, w_down, ln_final):
    depth = w_in.shape[0]
    assert depth == 1 and lb_logits.shape[0] == depth + 1
    bsz, seq, _ = x_prompt.shape
    dbsz, dseq, _ = x_sample.shape
    wb = cache_swa_k.shape[2]

    w_in_b = w_in[0].astype(BF16)
    w_out_b = w_out[0].astype(BF16)
    w_up_b = w_up[0].astype(BF16)
    w_down_b = w_down[0].astype(BF16)
    ln_mix2 = ln_mix[0].reshape(1, D_MODEL)
    ln_mlp2 = ln_mlp[0].reshape(1, D_MODEL)
    ln_fin2 = ln_final.reshape(1, D_MODEL)
    gn2 = hg_norm[0].reshape(1, HG_DV)
    sink1 = sinks[0]

    def mix_and_mlp(x2d, oh, osw, tm):
        return _out_mlp(x2d, oh, osw, w_out_b, ln_mlp2, w_up_b, w_down_b, ln_fin2, tm)

    assert seq % PROMPT_TILE == 0 and seq >= WINDOW
    wp = WINDOW
    xp = x_prompt.reshape(bsz * seq, D_MODEL)
    y_p, s_p, k_p, v_p = _prompt_layer(xp, bsz, sink1, ln_mix2, w_in_b, lb_logits, gn2, w_out_b,
                                       ln_mlp2, w_up_b, w_down_b, ln_fin2)

    xs = x_sample.reshape(dbsz * dseq, D_MODEL)
    zh_s, zs_s = _inproj(xs, ln_mix2, w_in_b, 512)
    oh_s, s_s = _gla(zh_s.reshape(dbsz, dseq, N_HG_IN), lb_logits, gn2, state_hgrn[0], 8, dseq, dseq)
    os_s, ck_s, cv_s = _swa_sample(
        zs_s.reshape(dbsz, dseq, N_SWA_IN),
        cache_swa_k[0].reshape(dbsz, wb, SWA_KV_WIDTH),
        cache_swa_v[0].reshape(dbsz, wb, SWA_KV_WIDTH), sink1, 8)
    y_s = mix_and_mlp(xs, oh_s.reshape(-1, HG_WIDTH), os_s.reshape(-1, SWA_WIDTH), 512)

    kv_shape_p = (1, bsz, wp, SWA_KV_HEADS, SWA_HEAD_DIM)
    kv_shape_s = (1, dbsz, wb, SWA_KV_HEADS, SWA_HEAD_DIM)
    return (y_p.reshape(bsz, seq, D_MODEL),
            y_s.reshape(dbsz, dseq, D_MODEL),
            s_p[None],
            k_p.reshape(kv_shape_p).astype(cache_swa_k.dtype),
            v_p.reshape(kv_shape_p).astype(cache_swa_v.dtype),
            s_s[None],
            ck_s.reshape(kv_shape_s),
            cv_s.reshape(kv_shape_s))
```

```python
import functools

import jax
import jax.numpy as jnp
from jax import lax
from jax.experimental import pallas as pl
from jax.experimental.pallas import tpu as pltpu

F32 = jnp.float32
BF16 = jnp.bfloat16

D_MODEL = 1024
HG_WIDTH = 512
HG_HEADS = 4
HG_DK = 128
HG_DV = 128
SWA_WIDTH = 512
SWA_HEAD_DIM = 64
SWA_Q_HEADS = 8
SWA_KV_HEADS = 2
SWA_GROUP = SWA_Q_HEADS // SWA_KV_HEADS
SWA_KV_WIDTH = SWA_KV_HEADS * SWA_HEAD_DIM
WINDOW = 128
SWA_SCALE = SWA_HEAD_DIM ** -0.5
D_FF = 4 * D_MODEL
EPS = 1e-6
N_HG_IN = 4 * HG_WIDTH
N_SWA_IN = SWA_WIDTH + 2 * SWA_KV_WIDTH
N_IN = N_HG_IN + N_SWA_IN
KV_COLS = slice(N_HG_IN + SWA_WIDTH, N_IN)
NEG_BIG = -1e30

VMEM_LIMIT_BYTES = 56 * 1024 * 1024
DENSE_TILE = 512
GLA_CHUNK = 64
FF_CHUNK = 1024
SAMPLE_SEQS = 8
MATMUL_ORDER = "U0 F0 D0 B0 F1 U1 B1 F2 D1 B2 F3 U2 B3 D2 U3 D3"


def _rms(x, g):
    return x * lax.rsqrt(jnp.mean(x * x, axis=-1, keepdims=True) + EPS) * g


def _sigmoid(x):
    return 1.0 / (1.0 + jnp.exp(-x))


def _dot(a, b):
    return jnp.dot(a.astype(BF16), b.astype(BF16), preferred_element_type=F32)


def _dot_nt(a, b):
    return lax.dot_general(a.astype(BF16), b.astype(BF16), (((1,), (1,)), ((), ())),
                           preferred_element_type=F32)


def _dot_tn(a, b):
    return lax.dot_general(a.astype(BF16), b.astype(BF16), (((0,), (0,)), ((), ())),
                           preferred_element_type=F32)


def _log2(n):
    assert n > 0 and n & (n - 1) == 0, n
    return n.bit_length() - 1


def _cumsum_rows(x):
    n = x.shape[0]
    row = lax.broadcasted_iota(jnp.int32, x.shape, 0)
    s = 1
    while s < n:
        x = x + jnp.where(row >= s, pltpu.roll(x, s, axis=0), 0.0)
        s *= 2
    return x


def _lower_bound(lb_logits):
    m = jnp.max(lb_logits, axis=0, keepdims=True)
    e = jnp.exp(lb_logits - m)
    return e[0:1, :] / jnp.sum(e, axis=0, keepdims=True)


def _gla_front(zh, lb, state_transposed):
    chunk = zh.shape[0]
    zq = zh[:, 0:HG_WIDTH]
    zf = zh[:, HG_WIDTH:2 * HG_WIDTH]
    v = zh[:, 2 * HG_WIDTH:3 * HG_WIDTH].astype(BF16)
    zg = zh[:, 3 * HG_WIDTH:4 * HG_WIDTH]

    q = zq * _sigmoid(zq)
    f = lb + (1.0 - lb) * _sigmoid(zf)
    k = 1.0 - f
    b = _cumsum_rows(jnp.log(f))
    mid = chunk // 2 - 1
    b_mid = b[mid:mid + 1, :]
    b_last = b[chunk - 1:chunk, :]
    qf = q * jnp.exp(b - b_mid)
    kf = k * jnp.exp(b_mid - b)
    qt = qf * jnp.exp(b_mid)
    ke = kf * jnp.exp(b_last - b_mid)
    dec = jnp.exp(b_last)
    gate = zg * _sigmoid(zg)
    heads = [slice(h * HG_DK, (h + 1) * HG_DK) for h in range(HG_HEADS)]
    scores = [_dot_nt(qf[:, sl], kf[:, sl]) for sl in heads]
    if state_transposed:
        kv = [_dot_tn(v[:, sl], ke[:, sl]) for sl in heads]
    else:
        kv = [_dot_tn(ke[:, sl], v[:, sl]) for sl in heads]
    return scores, kv, qt.astype(BF16), v, dec, gate


def _gla_back(front, g_norm, states, decays, state_transposed):
    scores, kv, qt, v, _, gate = front
    chunk = qt.shape[0]
    row = lax.broadcasted_iota(jnp.int32, (chunk, chunk), 0)
    col = lax.broadcasted_iota(jnp.int32, (chunk, chunk), 1)
    causal = row >= col
    outs, new_states = [], []
    for h in range(HG_HEADS):
        sl = slice(h * HG_DK, (h + 1) * HG_DK)
        a = jnp.where(causal, scores[h], 0.0)
        inter = _dot_nt(qt[:, sl], states[h]) if state_transposed else _dot(qt[:, sl], states[h])
        o = _dot(a, v[:, sl]) + inter
        new_states.append(states[h] * decays[h] + kv[h])
        outs.append(_rms(o, g_norm) * gate[:, sl])
    return jnp.concatenate(outs, axis=-1), new_states


def _stack_heads(x, kh):
    return jnp.concatenate(
        [x[:, (kh * SWA_GROUP + g) * SWA_HEAD_DIM:(kh * SWA_GROUP + g + 1) * SWA_HEAD_DIM]
         for g in range(SWA_GROUP)], axis=0)


def _sink_column(sink_ref, kh, rows_per_head):
    r = lax.broadcasted_iota(jnp.int32, (SWA_GROUP * rows_per_head, 1), 0)
    col = jnp.full(r.shape, sink_ref[kh * SWA_GROUP], F32)
    for g in range(1, SWA_GROUP):
        col = jnp.where(r >= g * rows_per_head, sink_ref[kh * SWA_GROUP + g], col)
    return col


def _swa_front(zq, kv_cur, kv_prev):
    scores = []
    for kh in range(SWA_KV_HEADS):
        ks = slice(kh * SWA_HEAD_DIM, (kh + 1) * SWA_HEAD_DIM)
        q = (_stack_heads(zq, kh) * SWA_SCALE).astype(BF16)
        scores.append((_dot_nt(q, kv_prev[:, ks]), _dot_nt(q, kv_cur[:, ks])))
    return scores


def _swa_back(scores, kv_cur, kv_prev, sink_ref, no_prev):
    rows = SWA_GROUP * WINDOW
    i = lax.broadcasted_iota(jnp.int32, (rows, WINDOW), 0) & (WINDOW - 1)
    j = lax.broadcasted_iota(jnp.int32, (rows, WINDOW), 1)
    use_prev = j > i
    outs = []
    for kh in range(SWA_KV_HEADS):
        vs = slice(SWA_KV_WIDTH + kh * SWA_HEAD_DIM, SWA_KV_WIDTH + (kh + 1) * SWA_HEAD_DIM)
        sink = _sink_column(sink_ref, kh, WINDOW)
        s_prev, s_cur = scores[kh]
        if no_prev is not None:
            s_prev = jnp.where(no_prev, NEG_BIG, s_prev)
        s = jnp.where(use_prev, s_prev, s_cur)
        m = jnp.maximum(jnp.max(s, axis=-1, keepdims=True), sink)
        p = jnp.exp(s - m)
        den = jnp.sum(p, axis=-1, keepdims=True) + jnp.exp(sink - m)
        o = (_dot(jnp.where(use_prev, p, 0.0), kv_prev[:, vs])
             + _dot(jnp.where(use_prev, 0.0, p), kv_cur[:, vs])) / den
        outs.extend(o[g * WINDOW:(g + 1) * WINDOW] for g in range(SWA_GROUP))
    return jnp.concatenate(outs, axis=-1)


def _swa_sample(sink_ref, zs_ref, ck_ref, cv_ref, o_ref, cko_ref, cvo_ref, nb, s_new, wb):
    nq = nb * s_new
    rows = SWA_GROUP * nq
    ls, lw = _log2(s_new), _log2(wb)
    _log2(nb)
    r = lax.broadcasted_iota(jnp.int32, (rows, nb * wb), 0)
    c = lax.broadcasted_iota(jnp.int32, (rows, nb * wb), 1)
    mask_c = (((c >> lw) == ((r >> ls) & (nb - 1)))
              & ((c & (wb - 1)) > (r & (s_new - 1)) + (wb - WINDOW)))
    r = lax.broadcasted_iota(jnp.int32, (rows, nq), 0)
    c = lax.broadcasted_iota(jnp.int32, (rows, nq), 1)
    mask_n = ((c >> ls) == ((r >> ls) & (nb - 1))) & ((c & (s_new - 1)) <= (r & (s_new - 1)))

    z = zs_ref[...].reshape(nq, N_SWA_IN)
    k_new = z[:, SWA_WIDTH:SWA_WIDTH + SWA_KV_WIDTH]
    v_new = z[:, SWA_WIDTH + SWA_KV_WIDTH:]
    for b in range(nb):
        cko_ref[b, 0:wb - s_new, :] = ck_ref[b, s_new:, :]
        cko_ref[b, wb - s_new:, :] = k_new[b * s_new:(b + 1) * s_new]
        cvo_ref[b, 0:wb - s_new, :] = cv_ref[b, s_new:, :]
        cvo_ref[b, wb - s_new:, :] = v_new[b * s_new:(b + 1) * s_new]
    ck = ck_ref[...].reshape(nb * wb, SWA_KV_WIDTH)
    cv = cv_ref[...].reshape(nb * wb, SWA_KV_WIDTH)
    for kh in range(SWA_KV_HEADS):
        cs = slice(kh * SWA_HEAD_DIM, (kh + 1) * SWA_HEAD_DIM)
        sink = _sink_column(sink_ref, kh, nq)
        q = _stack_heads(z, kh) * SWA_SCALE
        sc = jnp.where(mask_c, _dot_nt(q, ck[:, cs]), NEG_BIG)
        sn = jnp.where(mask_n, _dot_nt(q, k_new[:, cs]), NEG_BIG)
        m = jnp.maximum(jnp.maximum(jnp.max(sc, axis=-1, keepdims=True),
                                    jnp.max(sn, axis=-1, keepdims=True)), sink)
        pc = jnp.where(mask_c, jnp.exp(sc - m), 0.0)
        pn = jnp.where(mask_n, jnp.exp(sn - m), 0.0)
        den = (jnp.sum(pc, axis=-1, keepdims=True) + jnp.sum(pn, axis=-1, keepdims=True)
               + jnp.exp(sink - m))
        o = (_dot(pc, cv[:, cs]) + _dot(pn, v_new[:, cs])) / den
        for g in range(SWA_GROUP):
            h = kh * SWA_GROUP + g
            o_ref[:, :, h * SWA_HEAD_DIM:(h + 1) * SWA_HEAD_DIM] = (
                o[g * nq:(g + 1) * nq].reshape(nb, s_new, SWA_HEAD_DIM))


def _prompt_body(sink_ref, xc_ref, lmix_ref, win_ref, lbl_ref, gn_ref, wo_ref, lmlp_ref,
                 wu_ref, wd_ref, lfin_ref, y_ref, so_ref, ko_ref, vo_ref,
                 z_ref, o_ref, st_ref, kvp_ref, h_ref, acc_ref, *, tiles_per_seq, n_tiles):
    s = pl.program_id(0)
    pos = s % tiles_per_seq
    seq_start = pos == 0
    tm = DENSE_TILE

    @pl.when(s == 0)
    def _first():
        h_ref[...] = jnp.zeros(h_ref.shape, h_ref.dtype)

    @pl.when(seq_start)
    def _reset():
        st_ref[...] = jnp.zeros(st_ref.shape, st_ref.dtype)
        kvp_ref[...] = jnp.zeros(kvp_ref.shape, kvp_ref.dtype)

    hn = _rms(h_ref[...], lmlp_ref[...]).astype(BF16)

    xn = _rms(xc_ref[...], lmix_ref[...]).astype(BF16)
    z_ref[...] = jnp.dot(xn, win_ref[...], preferred_element_type=F32)
    lb = _lower_bound(lbl_ref[...])
    g_norm = gn_ref[...]

    n_ff = D_FF // FF_CHUNK
    n_blk = tm // WINDOW
    chunks_per_blk = WINDOW // GLA_CHUNK

    def mlp_up(c):
        cs = slice(c * FF_CHUNK, (c + 1) * FF_CHUNK)
        u = jnp.dot(hn, wu_ref[:, cs], preferred_element_type=F32)
        return jnp.square(jnp.maximum(u, 0.0)).astype(BF16)

    def mlp_down(c, a):
        cs = slice(c * FF_CHUNK, (c + 1) * FF_CHUNK)
        base = h_ref if c == 0 else acc_ref
        acc_ref[...] = base[...] + jnp.dot(a, wd_ref[cs, :], preferred_element_type=F32)

    def kv_blocks(n):
        kv_prev = kvp_ref[...] if n == 0 else z_ref[(n - 1) * WINDOW:n * WINDOW, KV_COLS]
        return z_ref[n * WINDOW:(n + 1) * WINDOW, KV_COLS], kv_prev

    def mix_front(n):
        gla = [_gla_front(z_ref[c * GLA_CHUNK:(c + 1) * GLA_CHUNK, 0:N_HG_IN], lb, True)
               for c in range(n * chunks_per_blk, (n + 1) * chunks_per_blk)]
        kv_cur, kv_prev = kv_blocks(n)
        swa = _swa_front(z_ref[n * WINDOW:(n + 1) * WINDOW, N_HG_IN:N_HG_IN + SWA_WIDTH],
                         kv_cur, kv_prev)
        return gla, swa

    def mix_back(n, front):
        gla, swa = front
        for i, c in enumerate(range(n * chunks_per_blk, (n + 1) * chunks_per_blk)):
            dec = gla[i][4]
            o, new_states = _gla_back(
                gla[i], g_norm, [st_ref[h] for h in range(HG_HEADS)],
                [dec[:, h * HG_DK:(h + 1) * HG_DK] for h in range(HG_HEADS)], True)
            for h in range(HG_HEADS):
                st_ref[h] = new_states[h]
            o_ref[c * GLA_CHUNK:(c + 1) * GLA_CHUNK, 0:HG_WIDTH] = o.astype(BF16)
        kv_cur, kv_prev = kv_blocks(n)
        o_ref[n * WINDOW:(n + 1) * WINDOW, HG_WIDTH:] = _swa_back(
            swa, kv_cur, kv_prev, sink_ref, seq_start if n == 0 else None).astype(BF16)

    assert n_ff == 4 and n_blk == 4, "MATMUL_ORDER is written for 4 MLP slices and 4 mixer slices"
    acts, fronts = {}, {}
    for step in MATMUL_ORDER.split():
        kind, i = step[0], int(step[1])
        if kind == "U":
            acts[i] = mlp_up(i)
        elif kind == "D":
            mlp_down(i, acts.pop(i))
        elif kind == "F":
            fronts[i] = mix_front(i)
        else:
            mix_back(i, fronts.pop(i))
    assert not acts and not fronts
    kvp_ref[...] = z_ref[tm - WINDOW:tm, KV_COLS]
    h_ref[...] = xc_ref[...] + jnp.dot(o_ref[...], wo_ref[...], preferred_element_type=F32)
    y_ref[...] = _rms(acc_ref[...], lfin_ref[...])

    @pl.when((pos == tiles_per_seq - 1) & (s < n_tiles))
    def _seq_end():
        for h in range(HG_HEADS):
            so_ref[0, h] = st_ref[h].T
        ko_ref[0] = z_ref[tm - WINDOW:tm, N_HG_IN + SWA_WIDTH:N_HG_IN + SWA_WIDTH + SWA_KV_WIDTH]
        vo_ref[0] = z_ref[tm - WINDOW:tm, N_HG_IN + SWA_WIDTH + SWA_KV_WIDTH:N_IN]


def _prompt_layer(x2d, bsz, sinks, ln_mix, w_in, lb_logits, hg_norm, w_out, ln_mlp, w_up, w_down, ln_final):
    n = x2d.shape[0]
    tm = DENSE_TILE
    n_tiles = n // tm
    tiles_per_seq = n_tiles // bsz
    assert tiles_per_seq * bsz * tm == n
    const = lambda s: (0, 0)
    single = pl.Buffered(1)
    seq_of = lambda s: jnp.minimum(s, n_tiles - 1) // tiles_per_seq
    return pl.pallas_call(
        functools.partial(_prompt_body, tiles_per_seq=tiles_per_seq, n_tiles=n_tiles),
        grid=(n_tiles + 1,),
        in_specs=[
            pl.BlockSpec(memory_space=pltpu.SMEM),
            pl.BlockSpec((tm, D_MODEL), lambda s: (jnp.minimum(s, n_tiles - 1), 0)),
            pl.BlockSpec((1, D_MODEL), const),
            pl.BlockSpec((D_MODEL, N_IN), const, pipeline_mode=single),
            pl.BlockSpec(lb_logits.shape, const),
            pl.BlockSpec((1, HG_DV), const),
            pl.BlockSpec((D_MODEL, D_MODEL), const, pipeline_mode=single),
            pl.BlockSpec((1, D_MODEL), const),
            pl.BlockSpec((D_MODEL, D_FF), const, pipeline_mode=single),
            pl.BlockSpec((D_FF, D_MODEL), const, pipeline_mode=single),
            pl.BlockSpec((1, D_MODEL), const),
        ],
        out_specs=[
            pl.BlockSpec((tm, D_MODEL), lambda s: (jnp.maximum(s - 1, 0), 0)),
            pl.BlockSpec((1, HG_HEADS, HG_DK, HG_DV), lambda s: (seq_of(s), 0, 0, 0)),
            pl.BlockSpec((1, WINDOW, SWA_KV_WIDTH), lambda s: (seq_of(s), 0, 0)),
            pl.BlockSpec((1, WINDOW, SWA_KV_WIDTH), lambda s: (seq_of(s), 0, 0)),
        ],
        out_shape=[
            jax.ShapeDtypeStruct((n, D_MODEL), F32),
            jax.ShapeDtypeStruct((bsz, HG_HEADS, HG_DK, HG_DV), F32),
            jax.ShapeDtypeStruct((bsz, WINDOW, SWA_KV_WIDTH), F32),
            jax.ShapeDtypeStruct((bsz, WINDOW, SWA_KV_WIDTH), F32),
        ],
        scratch_shapes=[
            pltpu.VMEM((tm, N_IN), F32),
            pltpu.VMEM((tm, D_MODEL), BF16),
            pltpu.VMEM((HG_HEADS, HG_DV, HG_DK), F32),
            pltpu.VMEM((WINDOW, 2 * SWA_KV_WIDTH), F32),
            pltpu.VMEM((tm, D_MODEL), F32),
            pltpu.VMEM((tm, D_MODEL), F32),
        ],
        compiler_params=pltpu.CompilerParams(
            dimension_semantics=("arbitrary",), vmem_limit_bytes=VMEM_LIMIT_BYTES),
        name="prompt_layer",
    )(sinks, x2d, ln_mix, w_in, lb_logits, hg_norm, w_out, ln_mlp, w_up, w_down, ln_final)


def _inproj_body(x_ref, g_ref, w_ref, zh_ref, zs_ref):
    xn = _rms(x_ref[...], g_ref[...]).astype(BF16)
    zh_ref[...] = jnp.dot(xn, w_ref[:, :N_HG_IN], preferred_element_type=F32)
    zs_ref[...] = jnp.dot(xn, w_ref[:, N_HG_IN:], preferred_element_type=F32)


def _inproj(x2d, ln, w_in_bf16):
    n = x2d.shape[0]
    tm = DENSE_TILE
    return pl.pallas_call(
        _inproj_body,
        grid=(n // tm,),
        in_specs=[
            pl.BlockSpec((tm, D_MODEL), lambda i: (i, 0)),
            pl.BlockSpec((1, D_MODEL), lambda i: (0, 0)),
            pl.BlockSpec((D_MODEL, N_IN), lambda i: (0, 0)),
        ],
        out_specs=[
            pl.BlockSpec((tm, N_HG_IN), lambda i: (i, 0)),
            pl.BlockSpec((tm, N_SWA_IN), lambda i: (i, 0)),
        ],
        out_shape=[
            jax.ShapeDtypeStruct((n, N_HG_IN), F32),
            jax.ShapeDtypeStruct((n, N_SWA_IN), F32),
        ],
        compiler_params=pltpu.CompilerParams(
            dimension_semantics=("arbitrary",), vmem_limit_bytes=VMEM_LIMIT_BYTES),
        name="inproj",
    )(x2d, ln, w_in_bf16)


def _sample_mixers_body(sink_ref, zh_ref, zs_ref, lbl_ref, gn_ref, s0_ref, ck_ref, cv_ref,
                        oh_ref, os_ref, so_ref, cko_ref, cvo_ref, *, nb, s_new, wb):
    lb = _lower_bound(lbl_ref[...])
    g_norm = gn_ref[...]
    fronts = [_gla_front(zh_ref[j], lb, False) for j in range(nb)]
    dec_rows = jnp.concatenate([f[4] for f in fronts] + [jnp.zeros((HG_DK - nb, HG_WIDTH), F32)], axis=0)
    dec_cols = dec_rows.T
    for j in range(nb):
        o, new_states = _gla_back(
            fronts[j], g_norm, [s0_ref[j, h] for h in range(HG_HEADS)],
            [dec_cols[h * HG_DK:(h + 1) * HG_DK, j:j + 1] for h in range(HG_HEADS)], False)
        oh_ref[j] = o
        for h in range(HG_HEADS):
            so_ref[j, h] = new_states[h]
    _swa_sample(sink_ref, zs_ref, ck_ref, cv_ref, os_ref, cko_ref, cvo_ref, nb, s_new, wb)


def _sample_mixers(zh3d, zs3d, lb_logits, hg_norm, s0, ck, cv, sinks):
    bsz, s_new, _ = zh3d.shape
    wb = ck.shape[1]
    nb = SAMPLE_SEQS
    assert bsz % nb == 0 and nb <= HG_DK
    const = lambda b: (0, 0)
    row3 = lambda b: (b, 0, 0)
    state_spec = pl.BlockSpec((nb, HG_HEADS, HG_DK, HG_DV), lambda b: (b, 0, 0, 0))
    cache_spec = pl.BlockSpec((nb, wb, SWA_KV_WIDTH), row3)
    return pl.pallas_call(
        functools.partial(_sample_mixers_body, nb=nb, s_new=s_new, wb=wb),
        grid=(bsz // nb,),
        in_specs=[
            pl.BlockSpec(memory_space=pltpu.SMEM),
            pl.BlockSpec((nb, s_new, N_HG_IN), row3),
            pl.BlockSpec((nb, s_new, N_SWA_IN), row3),
            pl.BlockSpec(lb_logits.shape, const),
            pl.BlockSpec((1, HG_DV), const),
            state_spec,
            cache_spec,
            cache_spec,
        ],
        out_specs=[
            pl.BlockSpec((nb, s_new, HG_WIDTH), row3),
            pl.BlockSpec((nb, s_new, SWA_WIDTH), row3),
            state_spec,
            cache_spec,
            cache_spec,
        ],
        out_shape=[
            jax.ShapeDtypeStruct((bsz, s_new, HG_WIDTH), F32),
            jax.ShapeDtypeStruct((bsz, s_new, SWA_WIDTH), F32),
            jax.ShapeDtypeStruct(s0.shape, s0.dtype),
            jax.ShapeDtypeStruct(ck.shape, ck.dtype),
            jax.ShapeDtypeStruct(cv.shape, cv.dtype),
        ],
        compiler_params=pltpu.CompilerParams(
            dimension_semantics=("arbitrary",), vmem_limit_bytes=VMEM_LIMIT_BYTES),
        name="sample_mixers",
    )(sinks, zh3d, zs3d, lb_logits, hg_norm, s0, ck, cv)


def _out_mlp_body(x_ref, oh_ref, os_ref, wo_ref, lm_ref, wu_ref, wd_ref, lf_ref, y_ref):
    o = jnp.concatenate([oh_ref[...].astype(BF16), os_ref[...].astype(BF16)], axis=-1)
    h = x_ref[...] + jnp.dot(o, wo_ref[...], preferred_element_type=F32)
    hn = _rms(h, lm_ref[...]).astype(BF16)
    u = jnp.dot(hn, wu_ref[...], preferred_element_type=F32)
    a = jnp.square(jnp.maximum(u, 0.0)).astype(BF16)
    y_ref[...] = _rms(h + jnp.dot(a, wd_ref[...], preferred_element_type=F32), lf_ref[...])


def _out_mlp(x2d, oh2d, os2d, w_out, ln_mlp, w_up, w_down, ln_final):
    n = x2d.shape[0]
    tm = DENSE_TILE
    const = lambda i: (0, 0)
    single = pl.Buffered(1)
    return pl.pallas_call(
        _out_mlp_body,
        grid=(n // tm,),
        in_specs=[
            pl.BlockSpec((tm, D_MODEL), lambda i: (i, 0)),
            pl.BlockSpec((tm, HG_WIDTH), lambda i: (i, 0)),
            pl.BlockSpec((tm, SWA_WIDTH), lambda i: (i, 0)),
            pl.BlockSpec((D_MODEL, D_MODEL), const, pipeline_mode=single),
            pl.BlockSpec((1, D_MODEL), const),
            pl.BlockSpec((D_MODEL, D_FF), const, pipeline_mode=single),
            pl.BlockSpec((D_FF, D_MODEL), const, pipeline_mode=single),
            pl.BlockSpec((1, D_MODEL), const),
        ],
        out_specs=pl.BlockSpec((tm, D_MODEL), lambda i: (i, 0)),
        out_shape=jax.ShapeDtypeStruct((n, D_MODEL), F32),
        compiler_params=pltpu.CompilerParams(
            dimension_semantics=("arbitrary",), vmem_limit_bytes=VMEM_LIMIT_BYTES),
        name="out_mlp",
    )(x2d, oh2d, os2d, w_out, ln_mlp, w_up, w_down, ln_final)


def kernel(x_prompt, x_sample, state_hgrn, cache_swa_k, cache_swa_v, ln_mix, w_in, lb_logits,
           hg_norm, sinks, w_out, ln_mlp, w_up, w_down, ln_final):
    depth = w_in.shape[0]
    assert depth == 1 and lb_logits.shape[0] == depth + 1
    bsz, seq, _ = x_prompt.shape
    dbsz, dseq, _ = x_sample.shape
    wb = cache_swa_k.shape[2]
    assert seq % DENSE_TILE == 0 and seq >= WINDOW and (dbsz * dseq) % DENSE_TILE == 0

    w_in_b = w_in[0].astype(BF16)
    w_out_b = w_out[0].astype(BF16)
    w_up_b = w_up[0].astype(BF16)
    w_down_b = w_down[0].astype(BF16)
    ln_mix2 = ln_mix[0].reshape(1, D_MODEL)
    ln_mlp2 = ln_mlp[0].reshape(1, D_MODEL)
    ln_fin2 = ln_final.reshape(1, D_MODEL)
    gn2 = hg_norm[0].reshape(1, HG_DV)
    sink1 = sinks[0]

    xp = x_prompt.reshape(bsz * seq, D_MODEL)
    y_p, s_p, k_p, v_p = _prompt_layer(xp, bsz, sink1, ln_mix2, w_in_b, lb_logits, gn2, w_out_b,
                                       ln_mlp2, w_up_b, w_down_b, ln_fin2)

    xs = x_sample.reshape(dbsz * dseq, D_MODEL)
    zh_s, zs_s = _inproj(xs, ln_mix2, w_in_b)
    oh_s, os_s, s_s, ck_s, cv_s = _sample_mixers(
        zh_s.reshape(dbsz, dseq, N_HG_IN), zs_s.reshape(dbsz, dseq, N_SWA_IN), lb_logits, gn2,
        state_hgrn[0], cache_swa_k[0].reshape(dbsz, wb, SWA_KV_WIDTH),
        cache_swa_v[0].reshape(dbsz, wb, SWA_KV_WIDTH), sink1)
    y_s = _out_mlp(xs, oh_s.reshape(-1, HG_WIDTH), os_s.reshape(-1, SWA_WIDTH), w_out_b, ln_mlp2,
                   w_up_b, w_down_b, ln_fin2)

    kv_shape_p = (1, bsz, WINDOW, SWA_KV_HEADS, SWA_HEAD_DIM)
    kv_shape_s = (1, dbsz, wb, SWA_KV_HEADS, SWA_HEAD_DIM)
    return (y_p.reshape(bsz, seq, D_MODEL),
            y_s.reshape(dbsz, dseq, D_MODEL),
            s_p[None],
            k_p.reshape(kv_shape_p).astype(cache_swa_k.dtype),
            v_p.reshape(kv_shape_p).astype(cache_swa_v.dtype),
            s_s[None],
            ck_s.reshape(kv_shape_s),
            cv_s.reshape(kv_shape_s))
```

```python
import functools

import jax
import jax.numpy as jnp
from jax import lax
from jax.experimental import pallas as pl
from jax.experimental.pallas import tpu as pltpu

F32 = jnp.float32
BF16 = jnp.bfloat16

D_MODEL = 1024
HG_WIDTH = 512
HG_HEADS = 4
HG_DK = 128
HG_DV = 128
SWA_WIDTH = 512
SWA_HEAD_DIM = 64
SWA_Q_HEADS = 8
SWA_KV_HEADS = 2
SWA_GROUP = SWA_Q_HEADS // SWA_KV_HEADS
SWA_KV_WIDTH = SWA_KV_HEADS * SWA_HEAD_DIM
WINDOW = 128
SWA_SCALE = SWA_HEAD_DIM ** -0.5
D_FF = 4 * D_MODEL
EPS = 1e-6
N_HG_IN = 4 * HG_WIDTH
N_SWA_IN = SWA_WIDTH + 2 * SWA_KV_WIDTH
N_IN = N_HG_IN + N_SWA_IN
KV_COLS = slice(N_HG_IN + SWA_WIDTH, N_IN)
NEG_BIG = -1e30

VMEM_LIMIT_BYTES = 56 * 1024 * 1024
DENSE_TILE = 512
GLA_CHUNK = 64
FF_CHUNK = 1024
SAMPLE_SEQS = 8
MATMUL_ORDER = "U0 F0 D0 B0 F1 U1 B1 F2 D1 B2 F3 U2 B3 D2 U3 D3"


def _rms(x, g):
    return x * lax.rsqrt(jnp.mean(x * x, axis=-1, keepdims=True) + EPS) * g


def _sigmoid(x):
    return 1.0 / (1.0 + jnp.exp(-x))


def _dot(a, b):
    return jnp.dot(a.astype(BF16), b.astype(BF16), preferred_element_type=F32)


def _dot_nt(a, b):
    return lax.dot_general(a.astype(BF16), b.astype(BF16), (((1,), (1,)), ((), ())),
                           preferred_element_type=F32)


def _dot_tn(a, b):
    return lax.dot_general(a.astype(BF16), b.astype(BF16), (((0,), (0,)), ((), ())),
                           preferred_element_type=F32)


def _log2(n):
    assert n > 0 and n & (n - 1) == 0, n
    return n.bit_length() - 1


def _cumsum_rows(x):
    n = x.shape[0]
    row = lax.broadcasted_iota(jnp.int32, x.shape, 0)
    s = 1
    while s < n:
        x = x + jnp.where(row >= s, pltpu.roll(x, s, axis=0), 0.0)
        s *= 2
    return x


def _lower_bound(lb_logits):
    m = jnp.max(lb_logits, axis=0, keepdims=True)
    e = jnp.exp(lb_logits - m)
    return e[0:1, :] / jnp.sum(e, axis=0, keepdims=True)


def _gla_front(zh, lb, state_transposed):
    chunk = zh.shape[0]
    zq = zh[:, 0:HG_WIDTH]
    zf = zh[:, HG_WIDTH:2 * HG_WIDTH]
    v = zh[:, 2 * HG_WIDTH:3 * HG_WIDTH].astype(BF16)
    zg = zh[:, 3 * HG_WIDTH:4 * HG_WIDTH]

    q = zq * _sigmoid(zq)
    f = lb + (1.0 - lb) * _sigmoid(zf)
    k = 1.0 - f
    b = _cumsum_rows(jnp.log(f))
    mid = chunk // 2 - 1
    b_mid = b[mid:mid + 1, :]
    b_last = b[chunk - 1:chunk, :]
    qf = q * jnp.exp(b - b_mid)
    kf = k * jnp.exp(b_mid - b)
    qt = qf * jnp.exp(b_mid)
    ke = kf * jnp.exp(b_last - b_mid)
    dec = jnp.exp(b_last)
    gate = zg * _sigmoid(zg)
    heads = [slice(h * HG_DK, (h + 1) * HG_DK) for h in range(HG_HEADS)]
    scores = [_dot_nt(qf[:, sl], kf[:, sl]) for sl in heads]
    if state_transposed:
        kv = [_dot_tn(v[:, sl], ke[:, sl]) for sl in heads]
    else:
        kv = [_dot_tn(ke[:, sl], v[:, sl]) for sl in heads]
    return scores, kv, qt.astype(BF16), v, dec, gate


def _gla_back(front, g_norm, states, decays, state_transposed):
    scores, kv, qt, v, _, gate = front
    chunk = qt.shape[0]
    row = lax.broadcasted_iota(jnp.int32, (chunk, chunk), 0)
    col = lax.broadcasted_iota(jnp.int32, (chunk, chunk), 1)
    causal = row >= col
    outs, new_states = [], []
    for h in range(HG_HEADS):
        sl = slice(h * HG_DK, (h + 1) * HG_DK)
        a = jnp.where(causal, scores[h], 0.0)
        inter = _dot_nt(qt[:, sl], states[h]) if state_transposed else _dot(qt[:, sl], states[h])
        o = _dot(a, v[:, sl]) + inter
        new_states.append(states[h] * decays[h] + kv[h])
        outs.append(_rms(o, g_norm) * gate[:, sl])
    return jnp.concatenate(outs, axis=-1), new_states


def _stack_heads(x, kh):
    return jnp.concatenate(
        [x[:, (kh * SWA_GROUP + g) * SWA_HEAD_DIM:(kh * SWA_GROUP + g + 1) * SWA_HEAD_DIM]
         for g in range(SWA_GROUP)], axis=0)


def _sink_column(sink_ref, kh, rows_per_head):
    r = lax.broadcasted_iota(jnp.int32, (SWA_GROUP * rows_per_head, 1), 0)
    col = jnp.full(r.shape, sink_ref[kh * SWA_GROUP], F32)
    for g in range(1, SWA_GROUP):
        col = jnp.where(r >= g * rows_per_head, sink_ref[kh * SWA_GROUP + g], col)
    return col


def _swa_front(zq, kv_cur, kv_prev):
    scores = []
    for kh in range(SWA_KV_HEADS):
        ks = slice(kh * SWA_HEAD_DIM, (kh + 1) * SWA_HEAD_DIM)
        q = (_stack_heads(zq, kh) * SWA_SCALE).astype(BF16)
        scores.append((_dot_nt(q, kv_prev[:, ks]), _dot_nt(q, kv_cur[:, ks])))
    return scores


def _swa_back(scores, kv_cur, kv_prev, sink_ref, no_prev):
    rows = SWA_GROUP * WINDOW
    i = lax.broadcasted_iota(jnp.int32, (rows, WINDOW), 0) & (WINDOW - 1)
    j = lax.broadcasted_iota(jnp.int32, (rows, WINDOW), 1)
    use_prev = j > i
    outs = []
    for kh in range(SWA_KV_HEADS):
        vs = slice(SWA_KV_WIDTH + kh * SWA_HEAD_DIM, SWA_KV_WIDTH + (kh + 1) * SWA_HEAD_DIM)
        sink = _sink_column(sink_ref, kh, WINDOW)
        s_prev, s_cur = scores[kh]
        if no_prev is not None:
            s_prev = jnp.where(no_prev, NEG_BIG, s_prev)
        s = jnp.where(use_prev, s_prev, s_cur)
        m = jnp.maximum(jnp.max(s, axis=-1, keepdims=True), sink)
        p = jnp.exp(s - m)
        den = jnp.sum(p, axis=-1, keepdims=True) + jnp.exp(sink - m)
        o = (_dot(jnp.where(use_prev, p, 0.0), kv_prev[:, vs])
             + _dot(jnp.where(use_prev, 0.0, p), kv_cur[:, vs])) / den
        outs.extend(o[g * WINDOW:(g + 1) * WINDOW] for g in range(SWA_GROUP))
    return jnp.concatenate(outs, axis=-1)


def _swa_sample(sink_ref, zs_ref, ck_ref, cv_ref, o_ref, cko_ref, cvo_ref, nb, s_new, wb):
    nq = nb * s_new
    rows = SWA_GROUP * nq
    ls, lw = _log2(s_new), _log2(wb)
    _log2(nb)
    r = lax.broadcasted_iota(jnp.int32, (rows, nb * wb), 0)
    c = lax.broadcasted_iota(jnp.int32, (rows, nb * wb), 1)
    mask_c = (((c >> lw) == ((r >> ls) & (nb - 1)))
              & ((c & (wb - 1)) > (r & (s_new - 1)) + (wb - WINDOW)))
    r = lax.broadcasted_iota(jnp.int32, (rows, nq), 0)
    c = lax.broadcasted_iota(jnp.int32, (rows, nq), 1)
    mask_n = ((c >> ls) == ((r >> ls) & (nb - 1))) & ((c & (s_new - 1)) <= (r & (s_new - 1)))

    z = zs_ref[...].reshape(nq, N_SWA_IN)
    k_new = z[:, SWA_WIDTH:SWA_WIDTH + SWA_KV_WIDTH]
    v_new = z[:, SWA_WIDTH + SWA_KV_WIDTH:]
    assert nq <= SWA_KV_WIDTH and wb == SWA_KV_WIDTH
    pad = jnp.zeros((SWA_KV_WIDTH - nq, SWA_KV_WIDTH), F32)
    lane = lax.broadcasted_iota(jnp.int32, (SWA_KV_WIDTH, wb), 1)
    for new, c_ref, co_ref in ((k_new, ck_ref, cko_ref), (v_new, cv_ref, cvo_ref)):
        new_t = jnp.concatenate([new, pad], axis=0).T
        for b in range(nb):
            kept = pltpu.roll(c_ref[b], wb - s_new, axis=1)
            fresh = pltpu.roll(new_t, (wb - s_new - b * s_new) % wb, axis=1)
            co_ref[b] = jnp.where(lane >= wb - s_new, fresh, kept)
    for kh in range(SWA_KV_HEADS):
        cs = slice(kh * SWA_HEAD_DIM, (kh + 1) * SWA_HEAD_DIM)
        kt = jnp.concatenate([ck_ref[b, cs, :] for b in range(nb)], axis=1)
        vt = jnp.concatenate([cv_ref[b, cs, :] for b in range(nb)], axis=1)
        sink = _sink_column(sink_ref, kh, nq)
        q = _stack_heads(z, kh) * SWA_SCALE
        sc = jnp.where(mask_c, _dot(q, kt), NEG_BIG)
        sn = jnp.where(mask_n, _dot_nt(q, k_new[:, cs]), NEG_BIG)
        m = jnp.maximum(jnp.maximum(jnp.max(sc, axis=-1, keepdims=True),
                                    jnp.max(sn, axis=-1, keepdims=True)), sink)
        pc = jnp.where(mask_c, jnp.exp(sc - m), 0.0)
        pn = jnp.where(mask_n, jnp.exp(sn - m), 0.0)
        den = (jnp.sum(pc, axis=-1, keepdims=True) + jnp.sum(pn, axis=-1, keepdims=True)
               + jnp.exp(sink - m))
        o = (_dot_nt(pc, vt) + _dot(pn, v_new[:, cs])) / den
        for g in range(SWA_GROUP):
            h = kh * SWA_GROUP + g
            o_ref[:, :, h * SWA_HEAD_DIM:(h + 1) * SWA_HEAD_DIM] = (
                o[g * nq:(g + 1) * nq].reshape(nb, s_new, SWA_HEAD_DIM))


def _prompt_body(sink_ref, xc_ref, lmix_ref, win_ref, lbl_ref, gn_ref, wo_ref, lmlp_ref,
                 wu_ref, wd_ref, lfin_ref, y_ref, so_ref, ko_ref, vo_ref,
                 z_ref, o_ref, st_ref, kvp_ref, h_ref, acc_ref, *, tiles_per_seq, n_tiles):
    s = pl.program_id(0)
    pos = s % tiles_per_seq
    seq_start = pos == 0
    tm = DENSE_TILE

    @pl.when(s == 0)
    def _first():
        h_ref[...] = jnp.zeros(h_ref.shape, h_ref.dtype)

    @pl.when(seq_start)
    def _reset():
        st_ref[...] = jnp.zeros(st_ref.shape, st_ref.dtype)
        kvp_ref[...] = jnp.zeros(kvp_ref.shape, kvp_ref.dtype)

    hn = _rms(h_ref[...], lmlp_ref[...]).astype(BF16)

    xn = _rms(xc_ref[...], lmix_ref[...]).astype(BF16)
    z_ref[...] = jnp.dot(xn, win_ref[...], preferred_element_type=F32)
    lb = _lower_bound(lbl_ref[...])
    g_norm = gn_ref[...]

    n_ff = D_FF // FF_CHUNK
    n_blk = tm // WINDOW
    chunks_per_blk = WINDOW // GLA_CHUNK

    def mlp_up(c):
        cs = slice(c * FF_CHUNK, (c + 1) * FF_CHUNK)
        u = jnp.dot(hn, wu_ref[:, cs], preferred_element_type=F32)
        return jnp.square(jnp.maximum(u, 0.0)).astype(BF16)

    def mlp_down(c, a):
        cs = slice(c * FF_CHUNK, (c + 1) * FF_CHUNK)
        base = h_ref if c == 0 else acc_ref
        acc_ref[...] = base[...] + jnp.dot(a, wd_ref[cs, :], preferred_element_type=F32)

    def kv_blocks(n):
        kv_prev = kvp_ref[...] if n == 0 else z_ref[(n - 1) * WINDOW:n * WINDOW, KV_COLS]
        return z_ref[n * WINDOW:(n + 1) * WINDOW, KV_COLS], kv_prev

    def mix_front(n):
        gla = [_gla_front(z_ref[c * GLA_CHUNK:(c + 1) * GLA_CHUNK, 0:N_HG_IN], lb, True)
               for c in range(n * chunks_per_blk, (n + 1) * chunks_per_blk)]
        kv_cur, kv_prev = kv_blocks(n)
        swa = _swa_front(z_ref[n * WINDOW:(n + 1) * WINDOW, N_HG_IN:N_HG_IN + SWA_WIDTH],
                         kv_cur, kv_prev)
        return gla, swa

    def mix_back(n, front):
        gla, swa = front
        for i, c in enumerate(range(n * chunks_per_blk, (n + 1) * chunks_per_blk)):
            dec = gla[i][4]
            o, new_states = _gla_back(
                gla[i], g_norm, [st_ref[h] for h in range(HG_HEADS)],
                [dec[:, h * HG_DK:(h + 1) * HG_DK] for h in range(HG_HEADS)], True)
            for h in range(HG_HEADS):
                st_ref[h] = new_states[h]
            o_ref[c * GLA_CHUNK:(c + 1) * GLA_CHUNK, 0:HG_WIDTH] = o.astype(BF16)
        kv_cur, kv_prev = kv_blocks(n)
        o_ref[n * WINDOW:(n + 1) * WINDOW, HG_WIDTH:] = _swa_back(
            swa, kv_cur, kv_prev, sink_ref, seq_start if n == 0 else None).astype(BF16)

    assert n_ff == 4 and n_blk == 4, "MATMUL_ORDER is written for 4 MLP slices and 4 mixer slices"
    acts, fronts = {}, {}
    for step in MATMUL_ORDER.split():
        kind, i = step[0], int(step[1])
        if kind == "U":
            acts[i] = mlp_up(i)
        elif kind == "D":
            mlp_down(i, acts.pop(i))
        elif kind == "F":
            fronts[i] = mix_front(i)
        else:
            mix_back(i, fronts.pop(i))
    assert not acts and not fronts
    kvp_ref[...] = z_ref[tm - WINDOW:tm, KV_COLS]
    h_ref[...] = xc_ref[...] + jnp.dot(o_ref[...], wo_ref[...], preferred_element_type=F32)
    y_ref[...] = _rms(acc_ref[...], lfin_ref[...])

    @pl.when((pos == tiles_per_seq - 1) & (s < n_tiles))
    def _seq_end():
        for h in range(HG_HEADS):
            so_ref[0, h] = st_ref[h].T
        ko_ref[0] = z_ref[tm - WINDOW:tm, N_HG_IN + SWA_WIDTH:N_HG_IN + SWA_WIDTH + SWA_KV_WIDTH].T
        vo_ref[0] = z_ref[tm - WINDOW:tm, N_HG_IN + SWA_WIDTH + SWA_KV_WIDTH:N_IN].T


def _prompt_layer(x2d, bsz, sinks, ln_mix, w_in, lb_logits, hg_norm, w_out, ln_mlp, w_up, w_down, ln_final):
    n = x2d.shape[0]
    tm = DENSE_TILE
    n_tiles = n // tm
    tiles_per_seq = n_tiles // bsz
    assert tiles_per_seq * bsz * tm == n
    const = lambda s: (0, 0)
    single = pl.Buffered(1)
    seq_of = lambda s: jnp.minimum(s, n_tiles - 1) // tiles_per_seq
    return pl.pallas_call(
        functools.partial(_prompt_body, tiles_per_seq=tiles_per_seq, n_tiles=n_tiles),
        grid=(n_tiles + 1,),
        in_specs=[
            pl.BlockSpec(memory_space=pltpu.SMEM),
            pl.BlockSpec((tm, D_MODEL), lambda s: (jnp.minimum(s, n_tiles - 1), 0)),
            pl.BlockSpec((1, D_MODEL), const),
            pl.BlockSpec((D_MODEL, N_IN), const, pipeline_mode=single),
            pl.BlockSpec(lb_logits.shape, const),
            pl.BlockSpec((1, HG_DV), const),
            pl.BlockSpec((D_MODEL, D_MODEL), const, pipeline_mode=single),
            pl.BlockSpec((1, D_MODEL), const),
            pl.BlockSpec((D_MODEL, D_FF), const, pipeline_mode=single),
            pl.BlockSpec((D_FF, D_MODEL), const, pipeline_mode=single),
            pl.BlockSpec((1, D_MODEL), const),
        ],
        out_specs=[
            pl.BlockSpec((tm, D_MODEL), lambda s: (jnp.maximum(s - 1, 0), 0)),
            pl.BlockSpec((1, HG_HEADS, HG_DK, HG_DV), lambda s: (seq_of(s), 0, 0, 0)),
            pl.BlockSpec((1, WINDOW, SWA_KV_WIDTH), lambda s: (seq_of(s), 0, 0)),
            pl.BlockSpec((1, WINDOW, SWA_KV_WIDTH), lambda s: (seq_of(s), 0, 0)),
        ],
        out_shape=[
            jax.ShapeDtypeStruct((n, D_MODEL), F32),
            jax.ShapeDtypeStruct((bsz, HG_HEADS, HG_DK, HG_DV), F32),
            jax.ShapeDtypeStruct((bsz, WINDOW, SWA_KV_WIDTH), F32),
            jax.ShapeDtypeStruct((bsz, WINDOW, SWA_KV_WIDTH), F32),
        ],
        scratch_shapes=[
            pltpu.VMEM((tm, N_IN), F32),
            pltpu.VMEM((tm, D_MODEL), BF16),
            pltpu.VMEM((HG_HEADS, HG_DV, HG_DK), F32),
            pltpu.VMEM((WINDOW, 2 * SWA_KV_WIDTH), F32),
            pltpu.VMEM((tm, D_MODEL), F32),
            pltpu.VMEM((tm, D_MODEL), F32),
        ],
        compiler_params=pltpu.CompilerParams(
            dimension_semantics=("arbitrary",), vmem_limit_bytes=VMEM_LIMIT_BYTES),
        name="prompt_layer",
    )(sinks, x2d, ln_mix, w_in, lb_logits, hg_norm, w_out, ln_mlp, w_up, w_down, ln_final)


def _inproj_body(x_ref, g_ref, w_ref, zh_ref, zs_ref):
    xn = _rms(x_ref[...], g_ref[...]).astype(BF16)
    zh_ref[...] = jnp.dot(xn, w_ref[:, :N_HG_IN], preferred_element_type=F32)
    zs_ref[...] = jnp.dot(xn, w_ref[:, N_HG_IN:], preferred_element_type=F32)


def _inproj(x2d, ln, w_in_bf16):
    n = x2d.shape[0]
    tm = DENSE_TILE
    return pl.pallas_call(
        _inproj_body,
        grid=(n // tm,),
        in_specs=[
            pl.BlockSpec((tm, D_MODEL), lambda i: (i, 0)),
            pl.BlockSpec((1, D_MODEL), lambda i: (0, 0)),
            pl.BlockSpec((D_MODEL, N_IN), lambda i: (0, 0)),
        ],
        out_specs=[
            pl.BlockSpec((tm, N_HG_IN), lambda i: (i, 0)),
            pl.BlockSpec((tm, N_SWA_IN), lambda i: (i, 0)),
        ],
        out_shape=[
            jax.ShapeDtypeStruct((n, N_HG_IN), F32),
            jax.ShapeDtypeStruct((n, N_SWA_IN), F32),
        ],
        compiler_params=pltpu.CompilerParams(
            dimension_semantics=("arbitrary",), vmem_limit_bytes=VMEM_LIMIT_BYTES),
        name="inproj",
    )(x2d, ln, w_in_bf16)


def _sample_mixers_body(sink_ref, zh_ref, zs_ref, lbl_ref, gn_ref, s0_ref, ck_ref, cv_ref,
                        oh_ref, os_ref, so_ref, cko_ref, cvo_ref, *, nb, s_new, wb):
    lb = _lower_bound(lbl_ref[...])
    g_norm = gn_ref[...]
    fronts = [_gla_front(zh_ref[j], lb, False) for j in range(nb)]
    dec_rows = jnp.concatenate([f[4] for f in fronts] + [jnp.zeros((HG_DK - nb, HG_WIDTH), F32)], axis=0)
    dec_cols = dec_rows.T
    for j in range(nb):
        o, new_states = _gla_back(
            fronts[j], g_norm, [s0_ref[j, h] for h in range(HG_HEADS)],
            [dec_cols[h * HG_DK:(h + 1) * HG_DK, j:j + 1] for h in range(HG_HEADS)], False)
        oh_ref[j] = o
        for h in range(HG_HEADS):
            so_ref[j, h] = new_states[h]
    _swa_sample(sink_ref, zs_ref, ck_ref, cv_ref, os_ref, cko_ref, cvo_ref, nb, s_new, wb)


def _sample_mixers(zh3d, zs3d, lb_logits, hg_norm, s0, ck, cv, sinks):
    bsz, s_new, _ = zh3d.shape
    wb = ck.shape[2]
    nb = SAMPLE_SEQS
    assert bsz % nb == 0 and nb <= HG_DK
    const = lambda b: (0, 0)
    row3 = lambda b: (b, 0, 0)
    state_spec = pl.BlockSpec((nb, HG_HEADS, HG_DK, HG_DV), lambda b: (b, 0, 0, 0))
    cache_spec = pl.BlockSpec((nb, SWA_KV_WIDTH, wb), row3)
    return pl.pallas_call(
        functools.partial(_sample_mixers_body, nb=nb, s_new=s_new, wb=wb),
        grid=(bsz // nb,),
        in_specs=[
            pl.BlockSpec(memory_space=pltpu.SMEM),
            pl.BlockSpec((nb, s_new, N_HG_IN), row3),
            pl.BlockSpec((nb, s_new, N_SWA_IN), row3),
            pl.BlockSpec(lb_logits.shape, const),
            pl.BlockSpec((1, HG_DV), const),
            state_spec,
            cache_spec,
            cache_spec,
        ],
        out_specs=[
            pl.BlockSpec((nb, s_new, HG_WIDTH), row3),
            pl.BlockSpec((nb, s_new, SWA_WIDTH), row3),
            state_spec,
            cache_spec,
            cache_spec,
        ],
        out_shape=[
            jax.ShapeDtypeStruct((bsz, s_new, HG_WIDTH), F32),
            jax.ShapeDtypeStruct((bsz, s_new, SWA_WIDTH), F32),
            jax.ShapeDtypeStruct(s0.shape, s0.dtype),
            jax.ShapeDtypeStruct(ck.shape, ck.dtype),
            jax.ShapeDtypeStruct(cv.shape, cv.dtype),
        ],
        compiler_params=pltpu.CompilerParams(
            dimension_semantics=("arbitrary",), vmem_limit_bytes=VMEM_LIMIT_BYTES),
        name="sample_mixers",
    )(sinks, zh3d, zs3d, lb_logits, hg_norm, s0, ck, cv)


def _out_mlp_body(x_ref, oh_ref, os_ref, wo_ref, lm_ref, wu_ref, wd_ref, lf_ref, y_ref):
    o = jnp.concatenate([oh_ref[...].astype(BF16), os_ref[...].astype(BF16)], axis=-1)
    h = x_ref[...] + jnp.dot(o, wo_ref[...], preferred_element_type=F32)
    hn = _rms(h, lm_ref[...]).astype(BF16)
    u = jnp.dot(hn, wu_ref[...], preferred_element_type=F32)
    a = jnp.square(jnp.maximum(u, 0.0)).astype(BF16)
    y_ref[...] = _rms(h + jnp.dot(a, wd_ref[...], preferred_element_type=F32), lf_ref[...])


def _out_mlp(x2d, oh2d, os2d, w_out, ln_mlp, w_up, w_down, ln_final):
    n = x2d.shape[0]
    tm = DENSE_TILE
    const = lambda i: (0, 0)
    single = pl.Buffered(1)
    return pl.pallas_call(
        _out_mlp_body,
        grid=(n // tm,),
        in_specs=[
            pl.BlockSpec((tm, D_MODEL), lambda i: (i, 0)),
            pl.BlockSpec((tm, HG_WIDTH), lambda i: (i, 0)),
            pl.BlockSpec((tm, SWA_WIDTH), lambda i: (i, 0)),
            pl.BlockSpec((D_MODEL, D_MODEL), const, pipeline_mode=single),
            pl.BlockSpec((1, D_MODEL), const),
            pl.BlockSpec((D_MODEL, D_FF), const, pipeline_mode=single),
            pl.BlockSpec((D_FF, D_MODEL), const, pipeline_mode=single),
            pl.BlockSpec((1, D_MODEL), const),
        ],
        out_specs=pl.BlockSpec((tm, D_MODEL), lambda i: (i, 0)),
        out_shape=jax.ShapeDtypeStruct((n, D_MODEL), F32),
        compiler_params=pltpu.CompilerParams(
            dimension_semantics=("arbitrary",), vmem_limit_bytes=VMEM_LIMIT_BYTES),
        name="out_mlp",
    )(x2d, oh2d, os2d, w_out, ln_mlp, w_up, w_down, ln_final)


def kernel(x_prompt, x_sample, state_hgrn, cache_swa_k, cache_swa_v, ln_mix, w_in, lb_logits,
           hg_norm, sinks, w_out, ln_mlp, w_up, w_down, ln_final):
    depth = w_in.shape[0]
    assert depth == 1 and lb_logits.shape[0] == depth + 1
    bsz, seq, _ = x_prompt.shape
    dbsz, dseq, _ = x_sample.shape
    wb = cache_swa_k.shape[2]
    assert seq % DENSE_TILE == 0 and seq >= WINDOW and (dbsz * dseq) % DENSE_TILE == 0

    w_in_b = w_in[0].astype(BF16)
    w_out_b = w_out[0].astype(BF16)
    w_up_b = w_up[0].astype(BF16)
    w_down_b = w_down[0].astype(BF16)
    ln_mix2 = ln_mix[0].reshape(1, D_MODEL)
    ln_mlp2 = ln_mlp[0].reshape(1, D_MODEL)
    ln_fin2 = ln_final.reshape(1, D_MODEL)
    gn2 = hg_norm[0].reshape(1, HG_DV)
    sink1 = sinks[0]

    def feature_major(c):
        return jnp.transpose(c, (0, 2, 3, 1)).reshape(c.shape[0], SWA_KV_WIDTH, c.shape[1])

    def position_major(c):
        return jnp.transpose(c.reshape(c.shape[0], SWA_KV_HEADS, SWA_HEAD_DIM, c.shape[2]), (0, 3, 1, 2))

    xp = x_prompt.reshape(bsz * seq, D_MODEL)
    y_p, s_p, k_p, v_p = _prompt_layer(xp, bsz, sink1, ln_mix2, w_in_b, lb_logits, gn2, w_out_b,
                                       ln_mlp2, w_up_b, w_down_b, ln_fin2)

    xs = x_sample.reshape(dbsz * dseq, D_MODEL)
    zh_s, zs_s = _inproj(xs, ln_mix2, w_in_b)
    oh_s, os_s, s_s, ck_s, cv_s = _sample_mixers(
        zh_s.reshape(dbsz, dseq, N_HG_IN), zs_s.reshape(dbsz, dseq, N_SWA_IN), lb_logits, gn2,
        state_hgrn[0], feature_major(cache_swa_k[0]), feature_major(cache_swa_v[0]), sink1)
    y_s = _out_mlp(xs, oh_s.reshape(-1, HG_WIDTH), os_s.reshape(-1, SWA_WIDTH), w_out_b, ln_mlp2,
                   w_up_b, w_down_b, ln_fin2)

    return (y_p.reshape(bsz, seq, D_MODEL),
            y_s.reshape(dbsz, dseq, D_MODEL),
            s_p[None],
            position_major(k_p)[None].astype(cache_swa_k.dtype),
            position_major(v_p)[None].astype(cache_swa_v.dtype),
            s_s[None],
            position_major(ck_s)[None],
            position_major(cv_s)[None])
```

```python
import functools

import jax
import jax.numpy as jnp
from jax import lax
from jax.experimental import pallas as pl
from jax.experimental.pallas import tpu as pltpu

F32 = jnp.float32
BF16 = jnp.bfloat16

D_MODEL = 1024
HG_WIDTH = 512
HG_HEADS = 4
HG_DK = 128
HG_DV = 128
SWA_WIDTH = 512
SWA_HEAD_DIM = 64
SWA_Q_HEADS = 8
SWA_KV_HEADS = 2
SWA_GROUP = SWA_Q_HEADS // SWA_KV_HEADS
SWA_KV_WIDTH = SWA_KV_HEADS * SWA_HEAD_DIM
WINDOW = 128
SWA_SCALE = SWA_HEAD_DIM ** -0.5
D_FF = 4 * D_MODEL
EPS = 1e-6
N_HG_IN = 4 * HG_WIDTH
N_SWA_IN = SWA_WIDTH + 2 * SWA_KV_WIDTH
N_IN = N_HG_IN + N_SWA_IN
KV_COLS = slice(N_HG_IN + SWA_WIDTH, N_IN)
NEG_BIG = -1e30

VMEM_LIMIT_BYTES = 56 * 1024 * 1024
DENSE_TILE = 512
GLA_CHUNK = 64
FF_CHUNK = 1024
MATMUL_ORDER = "sF U0 F0 D0 sB B0 F1 U1 B1 F2 D1 B2 F3 U2 B3 D2 U3 D3"


def _rms(x, g):
    return x * lax.rsqrt(jnp.mean(x * x, axis=-1, keepdims=True) + EPS) * g


def _sigmoid(x):
    return 1.0 / (1.0 + jnp.exp(-x))


def _dot(a, b):
    return jnp.dot(a.astype(BF16), b.astype(BF16), preferred_element_type=F32)


def _dot_nt(a, b):
    return lax.dot_general(a.astype(BF16), b.astype(BF16), (((1,), (1,)), ((), ())),
                           preferred_element_type=F32)


def _dot_tn(a, b):
    return lax.dot_general(a.astype(BF16), b.astype(BF16), (((0,), (0,)), ((), ())),
                           preferred_element_type=F32)


def _log2(n):
    assert n > 0 and n & (n - 1) == 0, n
    return n.bit_length() - 1


def _cumsum_rows(x):
    n = x.shape[0]
    row = lax.broadcasted_iota(jnp.int32, x.shape, 0)
    s = 1
    while s < n:
        x = x + jnp.where(row >= s, pltpu.roll(x, s, axis=0), 0.0)
        s *= 2
    return x


def _lower_bound(lb_logits):
    m = jnp.max(lb_logits, axis=0, keepdims=True)
    e = jnp.exp(lb_logits - m)
    return e[0:1, :] / jnp.sum(e, axis=0, keepdims=True)


def _gla_front(zh, lb, state_transposed):
    chunk = zh.shape[0]
    zq = zh[:, 0:HG_WIDTH]
    zf = zh[:, HG_WIDTH:2 * HG_WIDTH]
    v = zh[:, 2 * HG_WIDTH:3 * HG_WIDTH].astype(BF16)
    zg = zh[:, 3 * HG_WIDTH:4 * HG_WIDTH]

    q = zq * _sigmoid(zq)
    f = lb + (1.0 - lb) * _sigmoid(zf)
    k = 1.0 - f
    b = _cumsum_rows(jnp.log(f))
    mid = chunk // 2 - 1
    b_mid = b[mid:mid + 1, :]
    b_last = b[chunk - 1:chunk, :]
    qf = q * jnp.exp(b - b_mid)
    kf = k * jnp.exp(b_mid - b)
    qt = qf * jnp.exp(b_mid)
    ke = kf * jnp.exp(b_last - b_mid)
    dec = jnp.exp(b_last)
    gate = zg * _sigmoid(zg)
    heads = [slice(h * HG_DK, (h + 1) * HG_DK) for h in range(HG_HEADS)]
    scores = [_dot_nt(qf[:, sl], kf[:, sl]) for sl in heads]
    if state_transposed:
        kv = [_dot_tn(v[:, sl], ke[:, sl]) for sl in heads]
    else:
        kv = [_dot_tn(ke[:, sl], v[:, sl]) for sl in heads]
    return scores, kv, qt.astype(BF16), v, dec, gate


def _gla_back(front, g_norm, states, decays, state_transposed):
    scores, kv, qt, v, _, gate = front
    chunk = qt.shape[0]
    row = lax.broadcasted_iota(jnp.int32, (chunk, chunk), 0)
    col = lax.broadcasted_iota(jnp.int32, (chunk, chunk), 1)
    causal = row >= col
    outs, new_states = [], []
    for h in range(HG_HEADS):
        sl = slice(h * HG_DK, (h + 1) * HG_DK)
        a = jnp.where(causal, scores[h], 0.0)
        inter = _dot_nt(qt[:, sl], states[h]) if state_transposed else _dot(qt[:, sl], states[h])
        o = _dot(a, v[:, sl]) + inter
        new_states.append(states[h] * decays[h] + kv[h])
        outs.append(_rms(o, g_norm) * gate[:, sl])
    return jnp.concatenate(outs, axis=-1), new_states


def _stack_heads(x, kh):
    return jnp.concatenate(
        [x[:, (kh * SWA_GROUP + g) * SWA_HEAD_DIM:(kh * SWA_GROUP + g + 1) * SWA_HEAD_DIM]
         for g in range(SWA_GROUP)], axis=0)


def _sink_column(sink_ref, kh, rows_per_head):
    r = lax.broadcasted_iota(jnp.int32, (SWA_GROUP * rows_per_head, 1), 0)
    col = jnp.full(r.shape, sink_ref[kh * SWA_GROUP], F32)
    for g in range(1, SWA_GROUP):
        col = jnp.where(r >= g * rows_per_head, sink_ref[kh * SWA_GROUP + g], col)
    return col


def _swa_front(zq, kv_cur, kv_prev):
    scores = []
    for kh in range(SWA_KV_HEADS):
        ks = slice(kh * SWA_HEAD_DIM, (kh + 1) * SWA_HEAD_DIM)
        q = (_stack_heads(zq, kh) * SWA_SCALE).astype(BF16)
        scores.append((_dot_nt(q, kv_prev[:, ks]), _dot_nt(q, kv_cur[:, ks])))
    return scores


def _swa_back(scores, kv_cur, kv_prev, sink_ref, no_prev):
    rows = SWA_GROUP * WINDOW
    i = lax.broadcasted_iota(jnp.int32, (rows, WINDOW), 0) & (WINDOW - 1)
    j = lax.broadcasted_iota(jnp.int32, (rows, WINDOW), 1)
    use_prev = j > i
    outs = []
    for kh in range(SWA_KV_HEADS):
        vs = slice(SWA_KV_WIDTH + kh * SWA_HEAD_DIM, SWA_KV_WIDTH + (kh + 1) * SWA_HEAD_DIM)
        sink = _sink_column(sink_ref, kh, WINDOW)
        s_prev, s_cur = scores[kh]
        if no_prev is not None:
            s_prev = jnp.where(no_prev, NEG_BIG, s_prev)
        s = jnp.where(use_prev, s_prev, s_cur)
        m = jnp.maximum(jnp.max(s, axis=-1, keepdims=True), sink)
        p = jnp.exp(s - m)
        den = jnp.sum(p, axis=-1, keepdims=True) + jnp.exp(sink - m)
        o = (_dot(jnp.where(use_prev, p, 0.0), kv_prev[:, vs])
             + _dot(jnp.where(use_prev, 0.0, p), kv_cur[:, vs])) / den
        outs.extend(o[g * WINDOW:(g + 1) * WINDOW] for g in range(SWA_GROUP))
    return jnp.concatenate(outs, axis=-1)


def _swa_sample_front(zs_ref, ck_ref, cv_ref, cko_ref, cvo_ref, nb, s_new, wb):
    nq = nb * s_new
    z = zs_ref[...].reshape(nq, N_SWA_IN)
    k_new = z[:, SWA_WIDTH:SWA_WIDTH + SWA_KV_WIDTH]
    v_new = z[:, SWA_WIDTH + SWA_KV_WIDTH:]
    assert nq <= SWA_KV_WIDTH and wb == SWA_KV_WIDTH
    pad = jnp.zeros((SWA_KV_WIDTH - nq, SWA_KV_WIDTH), F32)
    lane = lax.broadcasted_iota(jnp.int32, (SWA_KV_WIDTH, wb), 1)
    for new, c_ref, co_ref in ((k_new, ck_ref, cko_ref), (v_new, cv_ref, cvo_ref)):
        new_t = jnp.concatenate([new, pad], axis=0).T
        for b in range(nb):
            kept = pltpu.roll(c_ref[b], wb - s_new, axis=1)
            fresh = pltpu.roll(new_t, (wb - s_new - b * s_new) % wb, axis=1)
            co_ref[b] = jnp.where(lane >= wb - s_new, fresh, kept)
    scores = []
    for kh in range(SWA_KV_HEADS):
        cs = slice(kh * SWA_HEAD_DIM, (kh + 1) * SWA_HEAD_DIM)
        kt = jnp.concatenate([ck_ref[b, cs, :] for b in range(nb)], axis=1)
        q = (_stack_heads(z, kh) * SWA_SCALE).astype(BF16)
        scores.append((_dot(q, kt), _dot_nt(q, k_new[:, cs])))
    return scores, v_new.astype(BF16)


def _swa_sample_back(front, sink_ref, cv_ref, o_ref, nb, s_new, wb):
    scores, v_new = front
    nq = nb * s_new
    rows = SWA_GROUP * nq
    ls, lw = _log2(s_new), _log2(wb)
    _log2(nb)
    r = lax.broadcasted_iota(jnp.int32, (rows, nb * wb), 0)
    c = lax.broadcasted_iota(jnp.int32, (rows, nb * wb), 1)
    mask_c = (((c >> lw) == ((r >> ls) & (nb - 1)))
              & ((c & (wb - 1)) > (r & (s_new - 1)) + (wb - WINDOW)))
    r = lax.broadcasted_iota(jnp.int32, (rows, nq), 0)
    c = lax.broadcasted_iota(jnp.int32, (rows, nq), 1)
    mask_n = ((c >> ls) == ((r >> ls) & (nb - 1))) & ((c & (s_new - 1)) <= (r & (s_new - 1)))
    for kh in range(SWA_KV_HEADS):
        cs = slice(kh * SWA_HEAD_DIM, (kh + 1) * SWA_HEAD_DIM)
        vt = jnp.concatenate([cv_ref[b, cs, :] for b in range(nb)], axis=1)
        sink = _sink_column(sink_ref, kh, nq)
        sc = jnp.where(mask_c, scores[kh][0], NEG_BIG)
        sn = jnp.where(mask_n, scores[kh][1], NEG_BIG)
        m = jnp.maximum(jnp.maximum(jnp.max(sc, axis=-1, keepdims=True),
                                    jnp.max(sn, axis=-1, keepdims=True)), sink)
        pc = jnp.where(mask_c, jnp.exp(sc - m), 0.0)
        pn = jnp.where(mask_n, jnp.exp(sn - m), 0.0)
        den = (jnp.sum(pc, axis=-1, keepdims=True) + jnp.sum(pn, axis=-1, keepdims=True)
               + jnp.exp(sink - m))
        o = (_dot_nt(pc, vt) + _dot(pn, v_new[:, cs])) / den
        for g in range(SWA_GROUP):
            h = kh * SWA_GROUP + g
            o_ref[:, :, h * SWA_HEAD_DIM:(h + 1) * SWA_HEAD_DIM] = (
                o[g * nq:(g + 1) * nq].reshape(nb, s_new, SWA_HEAD_DIM))


def _sample_front(refs, lb):
    zh_ref, zs_ref, _, ck_ref, cv_ref, _, _, _, cko_ref, cvo_ref = refs
    nb, s_new, _ = zh_ref.shape
    gla = [_gla_front(zh_ref[j], lb, False) for j in range(nb)]
    swa = _swa_sample_front(zs_ref, ck_ref, cv_ref, cko_ref, cvo_ref, nb, s_new, ck_ref.shape[2])
    return gla, swa


def _sample_back(front, refs, sink_ref, g_norm):
    zh_ref, _, s0_ref, ck_ref, cv_ref, oh_ref, os_ref, so_ref, _, _ = refs
    nb, s_new, _ = zh_ref.shape
    gla, swa = front
    dec_rows = jnp.concatenate([f[4] for f in gla] + [jnp.zeros((HG_DK - nb, HG_WIDTH), F32)], axis=0)
    dec_cols = dec_rows.T
    for j in range(nb):
        o, new_states = _gla_back(
            gla[j], g_norm, [s0_ref[j, h] for h in range(HG_HEADS)],
            [dec_cols[h * HG_DK:(h + 1) * HG_DK, j:j + 1] for h in range(HG_HEADS)], False)
        oh_ref[j] = o
        for h in range(HG_HEADS):
            so_ref[j, h] = new_states[h]
    _swa_sample_back(swa, sink_ref, cv_ref, os_ref, nb, s_new, ck_ref.shape[2])


def _prompt_body(sink_ref, xc_ref, lmix_ref, win_ref, lbl_ref, gn_ref, wo_ref, lmlp_ref,
                 wu_ref, wd_ref, lfin_ref, zhs_ref, zss_ref, s0_ref, cks_ref, cvs_ref,
                 y_ref, so_ref, ko_ref, vo_ref, ohs_ref, oss_ref, sso_ref, ckso_ref, cvso_ref,
                 z_ref, o_ref, st_ref, kvp_ref, h_ref, acc_ref, *, tiles_per_seq, n_tiles):
    sample_refs = (zhs_ref, zss_ref, s0_ref, cks_ref, cvs_ref,
                   ohs_ref, oss_ref, sso_ref, ckso_ref, cvso_ref)
    s = pl.program_id(0)
    pos = s % tiles_per_seq
    seq_start = pos == 0
    tm = DENSE_TILE

    @pl.when(s == 0)
    def _first():
        h_ref[...] = jnp.zeros(h_ref.shape, h_ref.dtype)

    @pl.when(seq_start)
    def _reset():
        st_ref[...] = jnp.zeros(st_ref.shape, st_ref.dtype)
        kvp_ref[...] = jnp.zeros(kvp_ref.shape, kvp_ref.dtype)

    hn = _rms(h_ref[...], lmlp_ref[...]).astype(BF16)

    xn = _rms(xc_ref[...], lmix_ref[...]).astype(BF16)
    z_ref[...] = jnp.dot(xn, win_ref[...], preferred_element_type=F32)
    lb = _lower_bound(lbl_ref[...])
    g_norm = gn_ref[...]

    n_ff = D_FF // FF_CHUNK
    n_blk = tm // WINDOW
    chunks_per_blk = WINDOW // GLA_CHUNK

    def mlp_up(c):
        cs = slice(c * FF_CHUNK, (c + 1) * FF_CHUNK)
        u = jnp.dot(hn, wu_ref[:, cs], preferred_element_type=F32)
        return jnp.square(jnp.maximum(u, 0.0)).astype(BF16)

    def mlp_down(c, a):
        cs = slice(c * FF_CHUNK, (c + 1) * FF_CHUNK)
        base = h_ref if c == 0 else acc_ref
        acc_ref[...] = base[...] + jnp.dot(a, wd_ref[cs, :], preferred_element_type=F32)

    def kv_blocks(n):
        kv_prev = kvp_ref[...] if n == 0 else z_ref[(n - 1) * WINDOW:n * WINDOW, KV_COLS]
        return z_ref[n * WINDOW:(n + 1) * WINDOW, KV_COLS], kv_prev

    def mix_front(n):
        gla = [_gla_front(z_ref[c * GLA_CHUNK:(c + 1) * GLA_CHUNK, 0:N_HG_IN], lb, True)
               for c in range(n * chunks_per_blk, (n + 1) * chunks_per_blk)]
        kv_cur, kv_prev = kv_blocks(n)
        swa = _swa_front(z_ref[n * WINDOW:(n + 1) * WINDOW, N_HG_IN:N_HG_IN + SWA_WIDTH],
                         kv_cur, kv_prev)
        return gla, swa

    def mix_back(n, front):
        gla, swa = front
        for i, c in enumerate(range(n * chunks_per_blk, (n + 1) * chunks_per_blk)):
            dec = gla[i][4]
            o, new_states = _gla_back(
                gla[i], g_norm, [st_ref[h] for h in range(HG_HEADS)],
                [dec[:, h * HG_DK:(h + 1) * HG_DK] for h in range(HG_HEADS)], True)
            for h in range(HG_HEADS):
                st_ref[h] = new_states[h]
            o_ref[c * GLA_CHUNK:(c + 1) * GLA_CHUNK, 0:HG_WIDTH] = o.astype(BF16)
        kv_cur, kv_prev = kv_blocks(n)
        o_ref[n * WINDOW:(n + 1) * WINDOW, HG_WIDTH:] = _swa_back(
            swa, kv_cur, kv_prev, sink_ref, seq_start if n == 0 else None).astype(BF16)

    assert n_ff == 4 and n_blk == 4, "MATMUL_ORDER is written for 4 MLP slices and 4 mixer slices"
    acts, fronts = {}, {}
    for step in MATMUL_ORDER.split():
        kind, i = step[0], step[1]
        if kind == "U":
            acts[i] = mlp_up(int(i))
        elif kind == "D":
            mlp_down(int(i), acts.pop(i))
        elif kind == "F":
            fronts[i] = mix_front(int(i))
        elif kind == "B":
            mix_back(int(i), fronts.pop(i))
        elif step == "sF":
            fronts[step] = _sample_front(sample_refs, lb)
        else:
            assert step == "sB", step
            _sample_back(fronts.pop("sF"), sample_refs, sink_ref, g_norm)
    assert not acts and not fronts
    kvp_ref[...] = z_ref[tm - WINDOW:tm, KV_COLS]
    h_ref[...] = xc_ref[...] + jnp.dot(o_ref[...], wo_ref[...], preferred_element_type=F32)
    y_ref[...] = _rms(acc_ref[...], lfin_ref[...])

    @pl.when((pos == tiles_per_seq - 1) & (s < n_tiles))
    def _seq_end():
        for h in range(HG_HEADS):
            so_ref[0, h] = st_ref[h].T
        ko_ref[0] = z_ref[tm - WINDOW:tm, N_HG_IN + SWA_WIDTH:N_HG_IN + SWA_WIDTH + SWA_KV_WIDTH].T
        vo_ref[0] = z_ref[tm - WINDOW:tm, N_HG_IN + SWA_WIDTH + SWA_KV_WIDTH:N_IN].T


def _fused_layer(x2d, bsz, sinks, ln_mix, w_in, lb_logits, hg_norm, w_out, ln_mlp, w_up, w_down,
                 ln_final, zh_s, zs_s, s0, ck, cv):
    n = x2d.shape[0]
    tm = DENSE_TILE
    n_tiles = n // tm
    tiles_per_seq = n_tiles // bsz
    assert tiles_per_seq * bsz * tm == n
    dbsz, s_new, _ = zh_s.shape
    wb = ck.shape[2]
    nb = pl.cdiv(dbsz, n_tiles)
    assert dbsz % nb == 0 and nb <= HG_DK
    n_sblk = dbsz // nb
    const = lambda s: (0, 0)
    single = pl.Buffered(1)
    seq_of = lambda s: jnp.minimum(s, n_tiles - 1) // tiles_per_seq
    srow3 = lambda s: (jnp.minimum(s, n_sblk - 1), 0, 0)
    srow4 = lambda s: (jnp.minimum(s, n_sblk - 1), 0, 0, 0)
    state_spec = pl.BlockSpec((nb, HG_HEADS, HG_DK, HG_DV), srow4)
    cache_spec = pl.BlockSpec((nb, SWA_KV_WIDTH, wb), srow3)
    return pl.pallas_call(
        functools.partial(_prompt_body, tiles_per_seq=tiles_per_seq, n_tiles=n_tiles),
        grid=(n_tiles + 1,),
        in_specs=[
            pl.BlockSpec(memory_space=pltpu.SMEM),
            pl.BlockSpec((tm, D_MODEL), lambda s: (jnp.minimum(s, n_tiles - 1), 0)),
            pl.BlockSpec((1, D_MODEL), const),
            pl.BlockSpec((D_MODEL, N_IN), const, pipeline_mode=single),
            pl.BlockSpec(lb_logits.shape, const),
            pl.BlockSpec((1, HG_DV), const),
            pl.BlockSpec((D_MODEL, D_MODEL), const, pipeline_mode=single),
            pl.BlockSpec((1, D_MODEL), const),
            pl.BlockSpec((D_MODEL, D_FF), const, pipeline_mode=single),
            pl.BlockSpec((D_FF, D_MODEL), const, pipeline_mode=single),
            pl.BlockSpec((1, D_MODEL), const),
            pl.BlockSpec((nb, s_new, N_HG_IN), srow3),
            pl.BlockSpec((nb, s_new, N_SWA_IN), srow3),
            state_spec,
            cache_spec,
            cache_spec,
        ],
        out_specs=[
            pl.BlockSpec((tm, D_MODEL), lambda s: (jnp.maximum(s - 1, 0), 0)),
            pl.BlockSpec((1, HG_HEADS, HG_DK, HG_DV), lambda s: (seq_of(s), 0, 0, 0)),
            pl.BlockSpec((1, WINDOW, SWA_KV_WIDTH), lambda s: (seq_of(s), 0, 0)),
            pl.BlockSpec((1, WINDOW, SWA_KV_WIDTH), lambda s: (seq_of(s), 0, 0)),
            pl.BlockSpec((nb, s_new, HG_WIDTH), srow3),
            pl.BlockSpec((nb, s_new, SWA_WIDTH), srow3),
            state_spec,
            cache_spec,
            cache_spec,
        ],
        out_shape=[
            jax.ShapeDtypeStruct((n, D_MODEL), F32),
            jax.ShapeDtypeStruct((bsz, HG_HEADS, HG_DK, HG_DV), F32),
            jax.ShapeDtypeStruct((bsz, WINDOW, SWA_KV_WIDTH), F32),
            jax.ShapeDtypeStruct((bsz, WINDOW, SWA_KV_WIDTH), F32),
            jax.ShapeDtypeStruct((dbsz, s_new, HG_WIDTH), F32),
            jax.ShapeDtypeStruct((dbsz, s_new, SWA_WIDTH), F32),
            jax.ShapeDtypeStruct(s0.shape, s0.dtype),
            jax.ShapeDtypeStruct(ck.shape, ck.dtype),
            jax.ShapeDtypeStruct(cv.shape, cv.dtype),
        ],
        scratch_shapes=[
            pltpu.VMEM((tm, N_IN), F32),
            pltpu.VMEM((tm, D_MODEL), BF16),
            pltpu.VMEM((HG_HEADS, HG_DV, HG_DK), F32),
            pltpu.VMEM((WINDOW, 2 * SWA_KV_WIDTH), F32),
            pltpu.VMEM((tm, D_MODEL), F32),
            pltpu.VMEM((tm, D_MODEL), F32),
        ],
        compiler_params=pltpu.CompilerParams(
            dimension_semantics=("arbitrary",), vmem_limit_bytes=VMEM_LIMIT_BYTES),
        name="fused_layer",
    )(sinks, x2d, ln_mix, w_in, lb_logits, hg_norm, w_out, ln_mlp, w_up, w_down, ln_final,
      zh_s, zs_s, s0, ck, cv)


def _inproj_body(x_ref, g_ref, w_ref, zh_ref, zs_ref):
    xn = _rms(x_ref[...], g_ref[...]).astype(BF16)
    zh_ref[...] = jnp.dot(xn, w_ref[:, :N_HG_IN], preferred_element_type=F32)
    zs_ref[...] = jnp.dot(xn, w_ref[:, N_HG_IN:], preferred_element_type=F32)


def _inproj(x2d, ln, w_in_bf16):
    n = x2d.shape[0]
    tm = DENSE_TILE
    return pl.pallas_call(
        _inproj_body,
        grid=(n // tm,),
        in_specs=[
            pl.BlockSpec((tm, D_MODEL), lambda i: (i, 0)),
            pl.BlockSpec((1, D_MODEL), lambda i: (0, 0)),
            pl.BlockSpec((D_MODEL, N_IN), lambda i: (0, 0)),
        ],
        out_specs=[
            pl.BlockSpec((tm, N_HG_IN), lambda i: (i, 0)),
            pl.BlockSpec((tm, N_SWA_IN), lambda i: (i, 0)),
        ],
        out_shape=[
            jax.ShapeDtypeStruct((n, N_HG_IN), F32),
            jax.ShapeDtypeStruct((n, N_SWA_IN), F32),
        ],
        compiler_params=pltpu.CompilerParams(
            dimension_semantics=("arbitrary",), vmem_limit_bytes=VMEM_LIMIT_BYTES),
        name="inproj",
    )(x2d, ln, w_in_bf16)


def _out_mlp_body(x_ref, oh_ref, os_ref, wo_ref, lm_ref, wu_ref, wd_ref, lf_ref, y_ref):
    o = jnp.concatenate([oh_ref[...].astype(BF16), os_ref[...].astype(BF16)], axis=-1)
    h = x_ref[...] + jnp.dot(o, wo_ref[...], preferred_element_type=F32)
    hn = _rms(h, lm_ref[...]).astype(BF16)
    u = jnp.dot(hn, wu_ref[...], preferred_element_type=F32)
    a = jnp.square(jnp.maximum(u, 0.0)).astype(BF16)
    y_ref[...] = _rms(h + jnp.dot(a, wd_ref[...], preferred_element_type=F32), lf_ref[...])


def _out_mlp(x2d, oh2d, os2d, w_out, ln_mlp, w_up, w_down, ln_final):
    n = x2d.shape[0]
    tm = DENSE_TILE
    const = lambda i: (0, 0)
    single = pl.Buffered(1)
    return pl.pallas_call(
        _out_mlp_body,
        grid=(n // tm,),
        in_specs=[
            pl.BlockSpec((tm, D_MODEL), lambda i: (i, 0)),
            pl.BlockSpec((tm, HG_WIDTH), lambda i: (i, 0)),
            pl.BlockSpec((tm, SWA_WIDTH), lambda i: (i, 0)),
            pl.BlockSpec((D_MODEL, D_MODEL), const, pipeline_mode=single),
            pl.BlockSpec((1, D_MODEL), const),
            pl.BlockSpec((D_MODEL, D_FF), const, pipeline_mode=single),
            pl.BlockSpec((D_FF, D_MODEL), const, pipeline_mode=single),
            pl.BlockSpec((1, D_MODEL), const),
        ],
        out_specs=pl.BlockSpec((tm, D_MODEL), lambda i: (i, 0)),
        out_shape=jax.ShapeDtypeStruct((n, D_MODEL), F32),
        compiler_params=pltpu.CompilerParams(
            dimension_semantics=("arbitrary",), vmem_limit_bytes=VMEM_LIMIT_BYTES),
        name="out_mlp",
    )(x2d, oh2d, os2d, w_out, ln_mlp, w_up, w_down, ln_final)


def kernel(x_prompt, x_sample, state_hgrn, cache_swa_k, cache_swa_v, ln_mix, w_in, lb_logits,
           hg_norm, sinks, w_out, ln_mlp, w_up, w_down, ln_final):
    depth = w_in.shape[0]
    assert depth == 1 and lb_logits.shape[0] == depth + 1
    bsz, seq, _ = x_prompt.shape
    dbsz, dseq, _ = x_sample.shape
    assert seq % DENSE_TILE == 0 and seq >= WINDOW and (dbsz * dseq) % DENSE_TILE == 0

    w_in_b = w_in[0].astype(BF16)
    w_out_b = w_out[0].astype(BF16)
    w_up_b = w_up[0].astype(BF16)
    w_down_b = w_down[0].astype(BF16)
    ln_mix2 = ln_mix[0].reshape(1, D_MODEL)
    ln_mlp2 = ln_mlp[0].reshape(1, D_MODEL)
    ln_fin2 = ln_final.reshape(1, D_MODEL)
    gn2 = hg_norm[0].reshape(1, HG_DV)
    sink1 = sinks[0]

    def feature_major(c):
        return jnp.transpose(c, (0, 2, 3, 1)).reshape(c.shape[0], SWA_KV_WIDTH, c.shape[1])

    def position_major(c):
        return jnp.transpose(c.reshape(c.shape[0], SWA_KV_HEADS, SWA_HEAD_DIM, c.shape[2]), (0, 3, 1, 2))

    xs = x_sample.reshape(dbsz * dseq, D_MODEL)
    zh_s, zs_s = _inproj(xs, ln_mix2, w_in_b)
    xp = x_prompt.reshape(bsz * seq, D_MODEL)
    y_p, s_p, k_p, v_p, oh_s, os_s, s_s, ck_s, cv_s = _fused_layer(
        xp, bsz, sink1, ln_mix2, w_in_b, lb_logits, gn2, w_out_b, ln_mlp2, w_up_b, w_down_b, ln_fin2,
        zh_s.reshape(dbsz, dseq, N_HG_IN), zs_s.reshape(dbsz, dseq, N_SWA_IN),
        state_hgrn[0], feature_major(cache_swa_k[0]), feature_major(cache_swa_v[0]))
    y_s = _out_mlp(xs, oh_s.reshape(-1, HG_WIDTH), os_s.reshape(-1, SWA_WIDTH), w_out_b, ln_mlp2,
                   w_up_b, w_down_b, ln_fin2)

    return (y_p.reshape(bsz, seq, D_MODEL),
            y_s.reshape(dbsz, dseq, D_MODEL),
            s_p[None],
            position_major(k_p)[None].astype(cache_swa_k.dtype),
            position_major(v_p)[None].astype(cache_swa_v.dtype),
            s_s[None],
            position_major(ck_s)[None],
            position_major(cv_s)[None])
```

```python
import functools

import jax
import jax.numpy as jnp
from jax import lax
from jax.experimental import pallas as pl
from jax.experimental.pallas import tpu as pltpu

F32 = jnp.float32
BF16 = jnp.bfloat16

D_MODEL = 1024
HG_WIDTH = 512
HG_HEADS = 4
HG_DK = 128
HG_DV = 128
SWA_WIDTH = 512
SWA_HEAD_DIM = 64
SWA_Q_HEADS = 8
SWA_KV_HEADS = 2
SWA_GROUP = SWA_Q_HEADS // SWA_KV_HEADS
SWA_KV_WIDTH = SWA_KV_HEADS * SWA_HEAD_DIM
WINDOW = 128
SWA_SCALE = SWA_HEAD_DIM ** -0.5
D_FF = 4 * D_MODEL
EPS = 1e-6
N_HG_IN = 4 * HG_WIDTH
N_SWA_IN = SWA_WIDTH + 2 * SWA_KV_WIDTH
N_IN = N_HG_IN + N_SWA_IN
KV_COLS = slice(N_HG_IN + SWA_WIDTH, N_IN)
NEG_BIG = -1e30

V7X_VMEM_BYTES = 64 * 1024 * 1024
VMEM_LIMIT_BYTES = V7X_VMEM_BYTES - 8 * 1024 * 1024
DENSE_TILE = 512
GLA_CHUNK = 64
FF_CHUNK = 1024
MATMUL_ORDER = "sF U0 F0 D0 sB B0 F1 U1 B1 F2 D1 B2 F3 U2 B3 D2 U3 D3"


def _rms(x, g):
    return x * lax.rsqrt(jnp.mean(x * x, axis=-1, keepdims=True) + EPS) * g


def _sigmoid(x):
    return 1.0 / (1.0 + jnp.exp(-x))


def _dot(a, b):
    return jnp.dot(a.astype(BF16), b.astype(BF16), preferred_element_type=F32)


def _dot_nt(a, b):
    return lax.dot_general(a.astype(BF16), b.astype(BF16), (((1,), (1,)), ((), ())),
                           preferred_element_type=F32)


def _dot_tn(a, b):
    return lax.dot_general(a.astype(BF16), b.astype(BF16), (((0,), (0,)), ((), ())),
                           preferred_element_type=F32)


def _log2(n):
    assert n > 0 and n & (n - 1) == 0, n
    return n.bit_length() - 1


def _cumsum_rows(x):
    n = x.shape[0]
    row = lax.broadcasted_iota(jnp.int32, x.shape, 0)
    s = 1
    while s < n:
        x = x + jnp.where(row >= s, pltpu.roll(x, s, axis=0), 0.0)
        s *= 2
    return x


def _lower_bound(lb_logits):
    m = jnp.max(lb_logits, axis=0, keepdims=True)
    e = jnp.exp(lb_logits - m)
    return e[0:1, :] / jnp.sum(e, axis=0, keepdims=True)


def _gla_front(zh, lb, state_transposed):
    chunk = zh.shape[0]
    zq = zh[:, 0:HG_WIDTH]
    zf = zh[:, HG_WIDTH:2 * HG_WIDTH]
    v = zh[:, 2 * HG_WIDTH:3 * HG_WIDTH].astype(BF16)
    zg = zh[:, 3 * HG_WIDTH:4 * HG_WIDTH]

    q = zq * _sigmoid(zq)
    f = lb + (1.0 - lb) * _sigmoid(zf)
    k = 1.0 - f
    b = _cumsum_rows(jnp.log(f))
    mid = chunk // 2 - 1
    b_mid = b[mid:mid + 1, :]
    b_last = b[chunk - 1:chunk, :]
    qf = q * jnp.exp(b - b_mid)
    kf = k * jnp.exp(b_mid - b)
    qt = q * jnp.exp(b)
    ke = k * jnp.exp(b_last - b)
    dec = jnp.exp(b_last)
    gate = zg * _sigmoid(zg)
    heads = [slice(h * HG_DK, (h + 1) * HG_DK) for h in range(HG_HEADS)]
    scores = [_dot_nt(qf[:, sl], kf[:, sl]) for sl in heads]
    if state_transposed:
        kv = [_dot_tn(v[:, sl], ke[:, sl]) for sl in heads]
    else:
        kv = [_dot_tn(ke[:, sl], v[:, sl]) for sl in heads]
    return scores, kv, qt.astype(BF16), v, dec, gate


def _gla_back(front, g_norm, states, decays, state_transposed):
    scores, kv, qt, v, _, gate = front
    chunk = qt.shape[0]
    row = lax.broadcasted_iota(jnp.int32, (chunk, chunk), 0)
    col = lax.broadcasted_iota(jnp.int32, (chunk, chunk), 1)
    causal = row >= col
    outs, new_states = [], []
    for h in range(HG_HEADS):
        sl = slice(h * HG_DK, (h + 1) * HG_DK)
        a = jnp.where(causal, scores[h], 0.0)
        inter = _dot_nt(qt[:, sl], states[h]) if state_transposed else _dot(qt[:, sl], states[h])
        o = _dot(a, v[:, sl]) + inter
        new_states.append(states[h] * decays[h] + kv[h])
        outs.append(_rms(o, g_norm) * gate[:, sl])
    return jnp.concatenate(outs, axis=-1), new_states


def _stack_heads(x, kh):
    return jnp.concatenate(
        [x[:, (kh * SWA_GROUP + g) * SWA_HEAD_DIM:(kh * SWA_GROUP + g + 1) * SWA_HEAD_DIM]
         for g in range(SWA_GROUP)], axis=0)


def _sink_column(sink_ref, kh, rows_per_head):
    r = lax.broadcasted_iota(jnp.int32, (SWA_GROUP * rows_per_head, 1), 0)
    col = jnp.full(r.shape, sink_ref[kh * SWA_GROUP], F32)
    for g in range(1, SWA_GROUP):
        col = jnp.where(r >= g * rows_per_head, sink_ref[kh * SWA_GROUP + g], col)
    return col


def _swa_front(zq, kv_cur, kv_prev):
    scores = []
    for kh in range(SWA_KV_HEADS):
        ks = slice(kh * SWA_HEAD_DIM, (kh + 1) * SWA_HEAD_DIM)
        q = (_stack_heads(zq, kh) * SWA_SCALE).astype(BF16)
        scores.append((_dot_nt(q, kv_prev[:, ks]), _dot_nt(q, kv_cur[:, ks])))
    return scores


def _swa_back(scores, kv_cur, kv_prev, sink_ref, no_prev):
    rows = SWA_GROUP * WINDOW
    i = lax.broadcasted_iota(jnp.int32, (rows, WINDOW), 0) & (WINDOW - 1)
    j = lax.broadcasted_iota(jnp.int32, (rows, WINDOW), 1)
    use_prev = j > i
    outs = []
    for kh in range(SWA_KV_HEADS):
        vs = slice(SWA_KV_WIDTH + kh * SWA_HEAD_DIM, SWA_KV_WIDTH + (kh + 1) * SWA_HEAD_DIM)
        sink = _sink_column(sink_ref, kh, WINDOW)
        s_prev, s_cur = scores[kh]
        if no_prev is not None:
            s_prev = jnp.where(no_prev, NEG_BIG, s_prev)
        s = jnp.where(use_prev, s_prev, s_cur)
        m = jnp.maximum(jnp.max(s, axis=-1, keepdims=True), sink)
        p = jnp.exp(s - m)
        den = jnp.sum(p, axis=-1, keepdims=True) + jnp.exp(sink - m)
        o = (_dot(jnp.where(use_prev, p, 0.0), kv_prev[:, vs])
             + _dot(jnp.where(use_prev, 0.0, p), kv_cur[:, vs])) / den
        outs.extend(o[g * WINDOW:(g + 1) * WINDOW] for g in range(SWA_GROUP))
    return jnp.concatenate(outs, axis=-1)


def _swa_sample_front(zs_ref, ck_ref, cv_ref, cko_ref, cvo_ref, nb, s_new, wb):
    nq = nb * s_new
    z = zs_ref[...].reshape(nq, N_SWA_IN)
    k_new = z[:, SWA_WIDTH:SWA_WIDTH + SWA_KV_WIDTH]
    v_new = z[:, SWA_WIDTH + SWA_KV_WIDTH:]
    assert nq <= SWA_KV_WIDTH and wb == SWA_KV_WIDTH
    pad = jnp.zeros((SWA_KV_WIDTH - nq, SWA_KV_WIDTH), F32)
    lane = lax.broadcasted_iota(jnp.int32, (SWA_KV_WIDTH, wb), 1)
    for new, c_ref, co_ref in ((k_new, ck_ref, cko_ref), (v_new, cv_ref, cvo_ref)):
        new_t = jnp.concatenate([new, pad], axis=0).T
        for b in range(nb):
            kept = pltpu.roll(c_ref[b], wb - s_new, axis=1)
            fresh = pltpu.roll(new_t, (wb - s_new - b * s_new) % wb, axis=1)
            co_ref[b] = jnp.where(lane >= wb - s_new, fresh, kept)
    scores = []
    for kh in range(SWA_KV_HEADS):
        cs = slice(kh * SWA_HEAD_DIM, (kh + 1) * SWA_HEAD_DIM)
        kt = jnp.concatenate([ck_ref[b, cs, :] for b in range(nb)], axis=1)
        q = (_stack_heads(z, kh) * SWA_SCALE).astype(BF16)
        scores.append((_dot(q, kt), _dot_nt(q, k_new[:, cs])))
    return scores, v_new.astype(BF16)


def _swa_sample_back(front, sink_ref, cv_ref, o_ref, nb, s_new, wb):
    scores, v_new = front
    nq = nb * s_new
    rows = SWA_GROUP * nq
    ls, lw = _log2(s_new), _log2(wb)
    _log2(nb)
    r = lax.broadcasted_iota(jnp.int32, (rows, nb * wb), 0)
    c = lax.broadcasted_iota(jnp.int32, (rows, nb * wb), 1)
    mask_c = (((c >> lw) == ((r >> ls) & (nb - 1)))
              & ((c & (wb - 1)) > (r & (s_new - 1)) + (wb - WINDOW)))
    r = lax.broadcasted_iota(jnp.int32, (rows, nq), 0)
    c = lax.broadcasted_iota(jnp.int32, (rows, nq), 1)
    mask_n = ((c >> ls) == ((r >> ls) & (nb - 1))) & ((c & (s_new - 1)) <= (r & (s_new - 1)))
    for kh in range(SWA_KV_HEADS):
        cs = slice(kh * SWA_HEAD_DIM, (kh + 1) * SWA_HEAD_DIM)
        vt = jnp.concatenate([cv_ref[b, cs, :] for b in range(nb)], axis=1)
        sink = _sink_column(sink_ref, kh, nq)
        sc = jnp.where(mask_c, scores[kh][0], NEG_BIG)
        sn = jnp.where(mask_n, scores[kh][1], NEG_BIG)
        m = jnp.maximum(jnp.maximum(jnp.max(sc, axis=-1, keepdims=True),
                                    jnp.max(sn, axis=-1, keepdims=True)), sink)
        pc = jnp.where(mask_c, jnp.exp(sc - m), 0.0)
        pn = jnp.where(mask_n, jnp.exp(sn - m), 0.0)
        den = (jnp.sum(pc, axis=-1, keepdims=True) + jnp.sum(pn, axis=-1, keepdims=True)
               + jnp.exp(sink - m))
        o = (_dot_nt(pc, vt) + _dot(pn, v_new[:, cs])) / den
        for g in range(SWA_GROUP):
            h = kh * SWA_GROUP + g
            o_ref[:, :, h * SWA_HEAD_DIM:(h + 1) * SWA_HEAD_DIM] = (
                o[g * nq:(g + 1) * nq].reshape(nb, s_new, SWA_HEAD_DIM))


def _sample_front(refs, lb):
    zh_ref, zs_ref, _, ck_ref, cv_ref, _, _, _, cko_ref, cvo_ref = refs
    nb, s_new, _ = zh_ref.shape
    gla = [_gla_front(zh_ref[j], lb, False) for j in range(nb)]
    swa = _swa_sample_front(zs_ref, ck_ref, cv_ref, cko_ref, cvo_ref, nb, s_new, ck_ref.shape[2])
    return gla, swa


def _sample_back(front, refs, sink_ref, g_norm):
    zh_ref, _, s0_ref, ck_ref, cv_ref, oh_ref, os_ref, so_ref, _, _ = refs
    nb, s_new, _ = zh_ref.shape
    gla, swa = front
    dec_rows = jnp.concatenate([f[4] for f in gla] + [jnp.zeros((HG_DK - nb, HG_WIDTH), F32)], axis=0)
    dec_cols = dec_rows.T
    for j in range(nb):
        o, new_states = _gla_back(
            gla[j], g_norm, [s0_ref[j, h] for h in range(HG_HEADS)],
            [dec_cols[h * HG_DK:(h + 1) * HG_DK, j:j + 1] for h in range(HG_HEADS)], False)
        oh_ref[j] = o
        for h in range(HG_HEADS):
            so_ref[j, h] = new_states[h]
    _swa_sample_back(swa, sink_ref, cv_ref, os_ref, nb, s_new, ck_ref.shape[2])


def _prompt_body(sink_ref, xc_ref, lmix_ref, win_ref, lbl_ref, gn_ref, wo_ref, lmlp_ref,
                 wu_ref, wd_ref, lfin_ref, zhs_ref, zss_ref, s0_ref, cks_ref, cvs_ref,
                 y_ref, so_ref, ko_ref, vo_ref, ohs_ref, oss_ref, sso_ref, ckso_ref, cvso_ref,
                 z_ref, o_ref, st_ref, kvp_ref, h_ref, acc_ref, *, tiles_per_seq, n_tiles):
    sample_refs = (zhs_ref, zss_ref, s0_ref, cks_ref, cvs_ref,
                   ohs_ref, oss_ref, sso_ref, ckso_ref, cvso_ref)
    s = pl.program_id(0)
    pos = s % tiles_per_seq
    seq_start = pos == 0
    tm = DENSE_TILE

    @pl.when(s == 0)
    def _first():
        h_ref[...] = jnp.zeros(h_ref.shape, h_ref.dtype)

    @pl.when(seq_start)
    def _reset():
        st_ref[...] = jnp.zeros(st_ref.shape, st_ref.dtype)
        kvp_ref[...] = jnp.zeros(kvp_ref.shape, kvp_ref.dtype)

    hn = _rms(h_ref[...], lmlp_ref[...]).astype(BF16)

    xn = _rms(xc_ref[...], lmix_ref[...]).astype(BF16)
    z_ref[...] = jnp.dot(xn, win_ref[...], preferred_element_type=F32)
    lb = _lower_bound(lbl_ref[...])
    g_norm = gn_ref[...]

    n_ff = D_FF // FF_CHUNK
    n_blk = tm // WINDOW
    chunks_per_blk = WINDOW // GLA_CHUNK

    def mlp_up(c):
        cs = slice(c * FF_CHUNK, (c + 1) * FF_CHUNK)
        u = jnp.dot(hn, wu_ref[:, cs], preferred_element_type=F32)
        return jnp.square(jnp.maximum(u, 0.0)).astype(BF16)

    def mlp_down(c, a):
        cs = slice(c * FF_CHUNK, (c + 1) * FF_CHUNK)
        base = h_ref if c == 0 else acc_ref
        acc_ref[...] = base[...] + jnp.dot(a, wd_ref[cs, :], preferred_element_type=F32)

    def kv_blocks(n):
        kv_prev = kvp_ref[...] if n == 0 else z_ref[(n - 1) * WINDOW:n * WINDOW, KV_COLS]
        return z_ref[n * WINDOW:(n + 1) * WINDOW, KV_COLS], kv_prev

    def mix_front(n):
        gla = [_gla_front(z_ref[c * GLA_CHUNK:(c + 1) * GLA_CHUNK, 0:N_HG_IN], lb, True)
               for c in range(n * chunks_per_blk, (n + 1) * chunks_per_blk)]
        kv_cur, kv_prev = kv_blocks(n)
        swa = _swa_front(z_ref[n * WINDOW:(n + 1) * WINDOW, N_HG_IN:N_HG_IN + SWA_WIDTH],
                         kv_cur, kv_prev)
        return gla, swa

    def mix_back(n, front):
        gla, swa = front
        for i, c in enumerate(range(n * chunks_per_blk, (n + 1) * chunks_per_blk)):
            dec = gla[i][4]
            o, new_states = _gla_back(
                gla[i], g_norm, [st_ref[h] for h in range(HG_HEADS)],
                [dec[:, h * HG_DK:(h + 1) * HG_DK] for h in range(HG_HEADS)], True)
            for h in range(HG_HEADS):
                st_ref[h] = new_states[h]
            o_ref[c * GLA_CHUNK:(c + 1) * GLA_CHUNK, 0:HG_WIDTH] = o.astype(BF16)
        kv_cur, kv_prev = kv_blocks(n)
        o_ref[n * WINDOW:(n + 1) * WINDOW, HG_WIDTH:] = _swa_back(
            swa, kv_cur, kv_prev, sink_ref, seq_start if n == 0 else None).astype(BF16)

    assert n_ff == 4 and n_blk == 4, "MATMUL_ORDER is written for 4 MLP slices and 4 mixer slices"
    acts, fronts = {}, {}
    for step in MATMUL_ORDER.split():
        kind, i = step[0], step[1]
        if kind == "U":
            acts[i] = mlp_up(int(i))
        elif kind == "D":
            mlp_down(int(i), acts.pop(i))
        elif kind == "F":
            fronts[i] = mix_front(int(i))
        elif kind == "B":
            mix_back(int(i), fronts.pop(i))
        elif step == "sF":
            fronts[step] = _sample_front(sample_refs, lb)
        else:
            assert step == "sB", step
            _sample_back(fronts.pop("sF"), sample_refs, sink_ref, g_norm)
    assert not acts and not fronts
    kvp_ref[...] = z_ref[tm - WINDOW:tm, KV_COLS]
    h_ref[...] = xc_ref[...] + jnp.dot(o_ref[...], wo_ref[...], preferred_element_type=F32)
    y_ref[...] = _rms(acc_ref[...], lfin_ref[...])

    @pl.when((pos == tiles_per_seq - 1) & (s < n_tiles))
    def _seq_end():
        for h in range(HG_HEADS):
            so_ref[0, h] = st_ref[h].T
        ko_ref[0] = z_ref[tm - WINDOW:tm, N_HG_IN + SWA_WIDTH:N_HG_IN + SWA_WIDTH + SWA_KV_WIDTH].T
        vo_ref[0] = z_ref[tm - WINDOW:tm, N_HG_IN + SWA_WIDTH + SWA_KV_WIDTH:N_IN].T


def _fused_layer(x2d, bsz, sinks, ln_mix, w_in, lb_logits, hg_norm, w_out, ln_mlp, w_up, w_down,
                 ln_final, zh_s, zs_s, s0, ck, cv):
    n = x2d.shape[0]
    tm = DENSE_TILE
    n_tiles = n // tm
    tiles_per_seq = n_tiles // bsz
    assert tiles_per_seq * bsz * tm == n
    dbsz, s_new, _ = zh_s.shape
    wb = ck.shape[2]
    nb = pl.cdiv(dbsz, n_tiles)
    assert dbsz % nb == 0 and nb <= HG_DK
    n_sblk = dbsz // nb
    const = lambda s: (0, 0)
    single = pl.Buffered(1)
    seq_of = lambda s: jnp.minimum(s, n_tiles - 1) // tiles_per_seq
    srow3 = lambda s: (jnp.minimum(s, n_sblk - 1), 0, 0)
    srow4 = lambda s: (jnp.minimum(s, n_sblk - 1), 0, 0, 0)
    state_spec = pl.BlockSpec((nb, HG_HEADS, HG_DK, HG_DV), srow4)
    cache_spec = pl.BlockSpec((nb, SWA_KV_WIDTH, wb), srow3)
    return pl.pallas_call(
        functools.partial(_prompt_body, tiles_per_seq=tiles_per_seq, n_tiles=n_tiles),
        grid=(n_tiles + 1,),
        in_specs=[
            pl.BlockSpec(memory_space=pltpu.SMEM),
            pl.BlockSpec((tm, D_MODEL), lambda s: (jnp.minimum(s, n_tiles - 1), 0)),
            pl.BlockSpec((1, D_MODEL), const),
            pl.BlockSpec((D_MODEL, N_IN), const, pipeline_mode=single),
            pl.BlockSpec(lb_logits.shape, const),
            pl.BlockSpec((1, HG_DV), const),
            pl.BlockSpec((D_MODEL, D_MODEL), const, pipeline_mode=single),
            pl.BlockSpec((1, D_MODEL), const),
            pl.BlockSpec((D_MODEL, D_FF), const, pipeline_mode=single),
            pl.BlockSpec((D_FF, D_MODEL), const, pipeline_mode=single),
            pl.BlockSpec((1, D_MODEL), const),
            pl.BlockSpec((nb, s_new, N_HG_IN), srow3),
            pl.BlockSpec((nb, s_new, N_SWA_IN), srow3),
            state_spec,
            cache_spec,
            cache_spec,
        ],
        out_specs=[
            pl.BlockSpec((tm, D_MODEL), lambda s: (jnp.maximum(s - 1, 0), 0)),
            pl.BlockSpec((1, HG_HEADS, HG_DK, HG_DV), lambda s: (seq_of(s), 0, 0, 0)),
            pl.BlockSpec((1, WINDOW, SWA_KV_WIDTH), lambda s: (seq_of(s), 0, 0)),
            pl.BlockSpec((1, WINDOW, SWA_KV_WIDTH), lambda s: (seq_of(s), 0, 0)),
            pl.BlockSpec((nb, s_new, HG_WIDTH), srow3),
            pl.BlockSpec((nb, s_new, SWA_WIDTH), srow3),
            state_spec,
            cache_spec,
            cache_spec,
        ],
        out_shape=[
            jax.ShapeDtypeStruct((n, D_MODEL), F32),
            jax.ShapeDtypeStruct((bsz, HG_HEADS, HG_DK, HG_DV), F32),
            jax.ShapeDtypeStruct((bsz, WINDOW, SWA_KV_WIDTH), F32),
            jax.ShapeDtypeStruct((bsz, WINDOW, SWA_KV_WIDTH), F32),
            jax.ShapeDtypeStruct((dbsz, s_new, HG_WIDTH), F32),
            jax.ShapeDtypeStruct((dbsz, s_new, SWA_WIDTH), F32),
            jax.ShapeDtypeStruct(s0.shape, s0.dtype),
            jax.ShapeDtypeStruct(ck.shape, ck.dtype),
            jax.ShapeDtypeStruct(cv.shape, cv.dtype),
        ],
        scratch_shapes=[
            pltpu.VMEM((tm, N_IN), F32),
            pltpu.VMEM((tm, D_MODEL), BF16),
            pltpu.VMEM((HG_HEADS, HG_DV, HG_DK), F32),
            pltpu.VMEM((WINDOW, 2 * SWA_KV_WIDTH), F32),
            pltpu.VMEM((tm, D_MODEL), F32),
            pltpu.VMEM((tm, D_MODEL), F32),
        ],
        compiler_params=pltpu.CompilerParams(
            dimension_semantics=("arbitrary",), vmem_limit_bytes=VMEM_LIMIT_BYTES),
        name="fused_layer",
    )(sinks, x2d, ln_mix, w_in, lb_logits, hg_norm, w_out, ln_mlp, w_up, w_down, ln_final,
      zh_s, zs_s, s0, ck, cv)


def _inproj_body(x_ref, g_ref, w_ref, zh_ref, zs_ref):
    xn = _rms(x_ref[...], g_ref[...]).astype(BF16)
    zh_ref[...] = jnp.dot(xn, w_ref[:, :N_HG_IN], preferred_element_type=F32)
    zs_ref[...] = jnp.dot(xn, w_ref[:, N_HG_IN:], preferred_element_type=F32)


def _inproj(x2d, ln, w_in_bf16):
    n = x2d.shape[0]
    tm = DENSE_TILE
    return pl.pallas_call(
        _inproj_body,
        grid=(n // tm,),
        in_specs=[
            pl.BlockSpec((tm, D_MODEL), lambda i: (i, 0)),
            pl.BlockSpec((1, D_MODEL), lambda i: (0, 0)),
            pl.BlockSpec((D_MODEL, N_IN), lambda i: (0, 0)),
        ],
        out_specs=[
            pl.BlockSpec((tm, N_HG_IN), lambda i: (i, 0)),
            pl.BlockSpec((tm, N_SWA_IN), lambda i: (i, 0)),
        ],
        out_shape=[
            jax.ShapeDtypeStruct((n, N_HG_IN), F32),
            jax.ShapeDtypeStruct((n, N_SWA_IN), F32),
        ],
        compiler_params=pltpu.CompilerParams(
            dimension_semantics=("arbitrary",), vmem_limit_bytes=VMEM_LIMIT_BYTES),
        name="inproj",
    )(x2d, ln, w_in_bf16)


def _out_mlp_body(x_ref, oh_ref, os_ref, wo_ref, lm_ref, wu_ref, wd_ref, lf_ref, y_ref):
    o = jnp.concatenate([oh_ref[...].astype(BF16), os_ref[...].astype(BF16)], axis=-1)
    h = x_ref[...] + jnp.dot(o, wo_ref[...], preferred_element_type=F32)
    hn = _rms(h, lm_ref[...]).astype(BF16)
    u = jnp.dot(hn, wu_ref[...], preferred_element_type=F32)
    a = jnp.square(jnp.maximum(u, 0.0)).astype(BF16)
    y_ref[...] = _rms(h + jnp.dot(a, wd_ref[...], preferred_element_type=F32), lf_ref[...])


def _out_mlp(x2d, oh2d, os2d, w_out, ln_mlp, w_up, w_down, ln_final):
    n = x2d.shape[0]
    tm = DENSE_TILE
    const = lambda i: (0, 0)
    single = pl.Buffered(1)
    return pl.pallas_call(
        _out_mlp_body,
        grid=(n // tm,),
        in_specs=[
            pl.BlockSpec((tm, D_MODEL), lambda i: (i, 0)),
            pl.BlockSpec((tm, HG_WIDTH), lambda i: (i, 0)),
            pl.BlockSpec((tm, SWA_WIDTH), lambda i: (i, 0)),
            pl.BlockSpec((D_MODEL, D_MODEL), const, pipeline_mode=single),
            pl.BlockSpec((1, D_MODEL), const),
            pl.BlockSpec((D_MODEL, D_FF), const, pipeline_mode=single),
            pl.BlockSpec((D_FF, D_MODEL), const, pipeline_mode=single),
            pl.BlockSpec((1, D_MODEL), const),
        ],
        out_specs=pl.BlockSpec((tm, D_MODEL), lambda i: (i, 0)),
        out_shape=jax.ShapeDtypeStruct((n, D_MODEL), F32),
        compiler_params=pltpu.CompilerParams(
            dimension_semantics=("arbitrary",), vmem_limit_bytes=VMEM_LIMIT_BYTES),
        name="out_mlp",
    )(x2d, oh2d, os2d, w_out, ln_mlp, w_up, w_down, ln_final)


def kernel(x_prompt, x_sample, state_hgrn, cache_swa_k, cache_swa_v, ln_mix, w_in, lb_logits,
           hg_norm, sinks, w_out, ln_mlp, w_up, w_down, ln_final):
    depth = w_in.shape[0]
    assert depth == 1 and lb_logits.shape[0] == depth + 1
    bsz, seq, _ = x_prompt.shape
    dbsz, dseq, _ = x_sample.shape
    assert seq % DENSE_TILE == 0 and seq >= WINDOW and (dbsz * dseq) % DENSE_TILE == 0

    w_in_b = w_in[0].astype(BF16)
    w_out_b = w_out[0].astype(BF16)
    w_up_b = w_up[0].astype(BF16)
    w_down_b = w_down[0].astype(BF16)
    ln_mix2 = ln_mix[0].reshape(1, D_MODEL)
    ln_mlp2 = ln_mlp[0].reshape(1, D_MODEL)
    ln_fin2 = ln_final.reshape(1, D_MODEL)
    gn2 = hg_norm[0].reshape(1, HG_DV)
    sink1 = sinks[0]

    def feature_major(c):
        return jnp.transpose(c, (0, 2, 3, 1)).reshape(c.shape[0], SWA_KV_WIDTH, c.shape[1])

    def position_major(c):
        return jnp.transpose(c.reshape(c.shape[0], SWA_KV_HEADS, SWA_HEAD_DIM, c.shape[2]), (0, 3, 1, 2))

    xs = x_sample.reshape(dbsz * dseq, D_MODEL)
    zh_s, zs_s = _inproj(xs, ln_mix2, w_in_b)
    xp = x_prompt.reshape(bsz * seq, D_MODEL)
    y_p, s_p, k_p, v_p, oh_s, os_s, s_s, ck_s, cv_s = _fused_layer(
        xp, bsz, sink1, ln_mix2, w_in_b, lb_logits, gn2, w_out_b, ln_mlp2, w_up_b, w_down_b, ln_fin2,
        zh_s.reshape(dbsz, dseq, N_HG_IN), zs_s.reshape(dbsz, dseq, N_SWA_IN),
        state_hgrn[0], feature_major(cache_swa_k[0]), feature_major(cache_swa_v[0]))
    y_s = _out_mlp(xs, oh_s.reshape(-1, HG_WIDTH), os_s.reshape(-1, SWA_WIDTH), w_out_b, ln_mlp2,
                   w_up_b, w_down_b, ln_fin2)

    return (y_p.reshape(bsz, seq, D_MODEL),
            y_s.reshape(dbsz, dseq, D_MODEL),
            s_p[None],
            position_major(k_p)[None].astype(cache_swa_k.dtype),
            position_major(v_p)[None].astype(cache_swa_v.dtype),
            s_s[None],
            position_major(ck_s)[None],
            position_major(cv_s)[None])
```

```python
import functools

import jax
import jax.numpy as jnp
from jax import lax
from jax.experimental import pallas as pl
from jax.experimental.pallas import tpu as pltpu

F32 = jnp.float32
BF16 = jnp.bfloat16

D_MODEL = 1024
HG_WIDTH = 512
HG_HEADS = 4
HG_DK = 128
HG_DV = 128
SWA_WIDTH = 512
SWA_HEAD_DIM = 64
SWA_Q_HEADS = 8
SWA_KV_HEADS = 2
SWA_GROUP = SWA_Q_HEADS // SWA_KV_HEADS
SWA_KV_WIDTH = SWA_KV_HEADS * SWA_HEAD_DIM
WINDOW = 128
SWA_SCALE = SWA_HEAD_DIM ** -0.5
D_FF = 4 * D_MODEL
EPS = 1e-6
N_HG_IN = 4 * HG_WIDTH
N_SWA_IN = SWA_WIDTH + 2 * SWA_KV_WIDTH
N_IN = N_HG_IN + N_SWA_IN
KV_COLS = slice(N_HG_IN + SWA_WIDTH, N_IN)
NEG_BIG = -1e30

V7X_VMEM_BYTES = 64 * 1024 * 1024
VMEM_LIMIT_BYTES = V7X_VMEM_BYTES - 4 * 1024 * 1024
DENSE_TILE = 512
GLA_CHUNK = 64
FF_CHUNK = 2048
MATMUL_ORDER = "sF U0 F0 F1 D0 sB B0 B1 F2 F3 U1 B2 B3 D1"


def _rms(x, g):
    return x * lax.rsqrt(jnp.mean(x * x, axis=-1, keepdims=True) + EPS) * g


def _sigmoid(x):
    return 1.0 / (1.0 + jnp.exp(-x))


def _dot(a, b):
    return jnp.dot(a.astype(BF16), b.astype(BF16), preferred_element_type=F32)


def _dot_nt(a, b):
    return lax.dot_general(a.astype(BF16), b.astype(BF16), (((1,), (1,)), ((), ())),
                           preferred_element_type=F32)


def _dot_tn(a, b):
    return lax.dot_general(a.astype(BF16), b.astype(BF16), (((0,), (0,)), ((), ())),
                           preferred_element_type=F32)


def _log2(n):
    assert n > 0 and n & (n - 1) == 0, n
    return n.bit_length() - 1


def _cumsum_rows(x):
    n = x.shape[0]
    row = lax.broadcasted_iota(jnp.int32, x.shape, 0)
    s = 1
    while s < n:
        x = x + jnp.where(row >= s, pltpu.roll(x, s, axis=0), 0.0)
        s *= 2
    return x


def _lower_bound(lb_logits):
    m = jnp.max(lb_logits, axis=0, keepdims=True)
    e = jnp.exp(lb_logits - m)
    return e[0:1, :] / jnp.sum(e, axis=0, keepdims=True)


def _gla_front(zh, lb, state_transposed):
    chunk = zh.shape[0]
    zq = zh[:, 0:HG_WIDTH]
    zf = zh[:, HG_WIDTH:2 * HG_WIDTH]
    v = zh[:, 2 * HG_WIDTH:3 * HG_WIDTH].astype(BF16)
    zg = zh[:, 3 * HG_WIDTH:4 * HG_WIDTH]

    q = zq * _sigmoid(zq)
    f = lb + (1.0 - lb) * _sigmoid(zf)
    k = 1.0 - f
    b = _cumsum_rows(jnp.log(f))
    mid = chunk // 2 - 1
    b_mid = b[mid:mid + 1, :]
    b_last = b[chunk - 1:chunk, :]
    qf = q * jnp.exp(b - b_mid)
    kf = k * jnp.exp(b_mid - b)
    qt = q * jnp.exp(b)
    ke = k * jnp.exp(b_last - b)
    dec = jnp.exp(b_last)
    gate = zg * _sigmoid(zg)
    heads = [slice(h * HG_DK, (h + 1) * HG_DK) for h in range(HG_HEADS)]
    scores = [_dot_nt(qf[:, sl], kf[:, sl]) for sl in heads]
    if state_transposed:
        kv = [_dot_tn(v[:, sl], ke[:, sl]) for sl in heads]
    else:
        kv = [_dot_tn(ke[:, sl], v[:, sl]) for sl in heads]
    return scores, kv, qt.astype(BF16), v, dec, gate


def _gla_back(front, g_norm, states, decays, state_transposed):
    scores, kv, qt, v, _, gate = front
    chunk = qt.shape[0]
    row = lax.broadcasted_iota(jnp.int32, (chunk, chunk), 0)
    col = lax.broadcasted_iota(jnp.int32, (chunk, chunk), 1)
    causal = row >= col
    outs, new_states = [], []
    for h in range(HG_HEADS):
        sl = slice(h * HG_DK, (h + 1) * HG_DK)
        a = jnp.where(causal, scores[h], 0.0)
        inter = _dot_nt(qt[:, sl], states[h]) if state_transposed else _dot(qt[:, sl], states[h])
        o = _dot(a, v[:, sl]) + inter
        new_states.append(states[h] * decays[h] + kv[h])
        outs.append(_rms(o, g_norm) * gate[:, sl])
    return jnp.concatenate(outs, axis=-1), new_states


def _stack_heads(x, kh):
    return jnp.concatenate(
        [x[:, (kh * SWA_GROUP + g) * SWA_HEAD_DIM:(kh * SWA_GROUP + g + 1) * SWA_HEAD_DIM]
         for g in range(SWA_GROUP)], axis=0)


def _sink_column(sink_ref, kh, rows_per_head):
    r = lax.broadcasted_iota(jnp.int32, (SWA_GROUP * rows_per_head, 1), 0)
    col = jnp.full(r.shape, sink_ref[kh * SWA_GROUP], F32)
    for g in range(1, SWA_GROUP):
        col = jnp.where(r >= g * rows_per_head, sink_ref[kh * SWA_GROUP + g], col)
    return col


def _swa_front(zq, kv_cur, kv_prev):
    scores = []
    for kh in range(SWA_KV_HEADS):
        ks = slice(kh * SWA_HEAD_DIM, (kh + 1) * SWA_HEAD_DIM)
        q = (_stack_heads(zq, kh) * SWA_SCALE).astype(BF16)
        scores.append((_dot_nt(q, kv_prev[:, ks]), _dot_nt(q, kv_cur[:, ks])))
    return scores


def _swa_back(scores, kv_cur, kv_prev, sink_ref, no_prev):
    rows = SWA_GROUP * WINDOW
    i = lax.broadcasted_iota(jnp.int32, (rows, WINDOW), 0) & (WINDOW - 1)
    j = lax.broadcasted_iota(jnp.int32, (rows, WINDOW), 1)
    use_prev = j > i
    outs = []
    for kh in range(SWA_KV_HEADS):
        vs = slice(SWA_KV_WIDTH + kh * SWA_HEAD_DIM, SWA_KV_WIDTH + (kh + 1) * SWA_HEAD_DIM)
        sink = _sink_column(sink_ref, kh, WINDOW)
        s_prev, s_cur = scores[kh]
        if no_prev is not None:
            s_prev = jnp.where(no_prev, NEG_BIG, s_prev)
        s = jnp.where(use_prev, s_prev, s_cur)
        m = jnp.maximum(jnp.max(s, axis=-1, keepdims=True), sink)
        p = jnp.exp(s - m)
        den = jnp.sum(p, axis=-1, keepdims=True) + jnp.exp(sink - m)
        o = (_dot(jnp.where(use_prev, p, 0.0), kv_prev[:, vs])
             + _dot(jnp.where(use_prev, 0.0, p), kv_cur[:, vs])) / den
        outs.extend(o[g * WINDOW:(g + 1) * WINDOW] for g in range(SWA_GROUP))
    return jnp.concatenate(outs, axis=-1)


def _swa_sample_front(zs_ref, ck_ref, cv_ref, cko_ref, cvo_ref, nb, s_new, wb):
    nq = nb * s_new
    z = zs_ref[...].reshape(nq, N_SWA_IN)
    k_new = z[:, SWA_WIDTH:SWA_WIDTH + SWA_KV_WIDTH]
    v_new = z[:, SWA_WIDTH + SWA_KV_WIDTH:]
    assert nq <= SWA_KV_WIDTH and wb == SWA_KV_WIDTH
    pad = jnp.zeros((SWA_KV_WIDTH - nq, SWA_KV_WIDTH), F32)
    lane = lax.broadcasted_iota(jnp.int32, (SWA_KV_WIDTH, wb), 1)
    for new, c_ref, co_ref in ((k_new, ck_ref, cko_ref), (v_new, cv_ref, cvo_ref)):
        new_t = jnp.concatenate([new, pad], axis=0).T
        for b in range(nb):
            kept = pltpu.roll(c_ref[b], wb - s_new, axis=1)
            fresh = pltpu.roll(new_t, (wb - s_new - b * s_new) % wb, axis=1)
            co_ref[b] = jnp.where(lane >= wb - s_new, fresh, kept)
    scores = []
    for kh in range(SWA_KV_HEADS):
        cs = slice(kh * SWA_HEAD_DIM, (kh + 1) * SWA_HEAD_DIM)
        kt = jnp.concatenate([ck_ref[b, cs, :] for b in range(nb)], axis=1)
        q = (_stack_heads(z, kh) * SWA_SCALE).astype(BF16)
        scores.append((_dot(q, kt), _dot_nt(q, k_new[:, cs])))
    return scores, v_new.astype(BF16)


def _swa_sample_back(front, sink_ref, cv_ref, o_ref, nb, s_new, wb):
    scores, v_new = front
    nq = nb * s_new
    rows = SWA_GROUP * nq
    ls, lw = _log2(s_new), _log2(wb)
    _log2(nb)
    r = lax.broadcasted_iota(jnp.int32, (rows, nb * wb), 0)
    c = lax.broadcasted_iota(jnp.int32, (rows, nb * wb), 1)
    mask_c = (((c >> lw) == ((r >> ls) & (nb - 1)))
              & ((c & (wb - 1)) > (r & (s_new - 1)) + (wb - WINDOW)))
    r = lax.broadcasted_iota(jnp.int32, (rows, nq), 0)
    c = lax.broadcasted_iota(jnp.int32, (rows, nq), 1)
    mask_n = ((c >> ls) == ((r >> ls) & (nb - 1))) & ((c & (s_new - 1)) <= (r & (s_new - 1)))
    for kh in range(SWA_KV_HEADS):
        cs = slice(kh * SWA_HEAD_DIM, (kh + 1) * SWA_HEAD_DIM)
        vt = jnp.concatenate([cv_ref[b, cs, :] for b in range(nb)], axis=1)
        sink = _sink_column(sink_ref, kh, nq)
        sc = jnp.where(mask_c, scores[kh][0], NEG_BIG)
        sn = jnp.where(mask_n, scores[kh][1], NEG_BIG)
        m = jnp.maximum(jnp.maximum(jnp.max(sc, axis=-1, keepdims=True),
                                    jnp.max(sn, axis=-1, keepdims=True)), sink)
        pc = jnp.where(mask_c, jnp.exp(sc - m), 0.0)
        pn = jnp.where(mask_n, jnp.exp(sn - m), 0.0)
        den = (jnp.sum(pc, axis=-1, keepdims=True) + jnp.sum(pn, axis=-1, keepdims=True)
               + jnp.exp(sink - m))
        o = (_dot_nt(pc, vt) + _dot(pn, v_new[:, cs])) / den
        for g in range(SWA_GROUP):
            h = kh * SWA_GROUP + g
            o_ref[:, :, h * SWA_HEAD_DIM:(h + 1) * SWA_HEAD_DIM] = (
                o[g * nq:(g + 1) * nq].reshape(nb, s_new, SWA_HEAD_DIM))


def _sample_front(refs, lb):
    zh_ref, zs_ref, _, ck_ref, cv_ref, _, _, _, cko_ref, cvo_ref = refs
    nb, s_new, _ = zh_ref.shape
    gla = [_gla_front(zh_ref[j], lb, False) for j in range(nb)]
    swa = _swa_sample_front(zs_ref, ck_ref, cv_ref, cko_ref, cvo_ref, nb, s_new, ck_ref.shape[2])
    return gla, swa


def _sample_back(front, refs, sink_ref, g_norm):
    zh_ref, _, s0_ref, ck_ref, cv_ref, oh_ref, os_ref, so_ref, _, _ = refs
    nb, s_new, _ = zh_ref.shape
    gla, swa = front
    dec_rows = jnp.concatenate([f[4] for f in gla] + [jnp.zeros((HG_DK - nb, HG_WIDTH), F32)], axis=0)
    dec_cols = dec_rows.T
    for j in range(nb):
        o, new_states = _gla_back(
            gla[j], g_norm, [s0_ref[j, h] for h in range(HG_HEADS)],
            [dec_cols[h * HG_DK:(h + 1) * HG_DK, j:j + 1] for h in range(HG_HEADS)], False)
        oh_ref[j] = o
        for h in range(HG_HEADS):
            so_ref[j, h] = new_states[h]
    _swa_sample_back(swa, sink_ref, cv_ref, os_ref, nb, s_new, ck_ref.shape[2])


def _prompt_body(sink_ref, xc_ref, lmix_ref, win_ref, lbl_ref, gn_ref, wo_ref, lmlp_ref,
                 wu_ref, wd_ref, lfin_ref, zhs_ref, zss_ref, s0_ref, cks_ref, cvs_ref,
                 y_ref, so_ref, ko_ref, vo_ref, ohs_ref, oss_ref, sso_ref, ckso_ref, cvso_ref,
                 z_ref, o_ref, st_ref, kvp_ref, h_ref, acc_ref, *, tiles_per_seq, n_tiles):
    sample_refs = (zhs_ref, zss_ref, s0_ref, cks_ref, cvs_ref,
                   ohs_ref, oss_ref, sso_ref, ckso_ref, cvso_ref)
    s = pl.program_id(0)
    pos = s % tiles_per_seq
    seq_start = pos == 0
    tm = DENSE_TILE

    @pl.when(seq_start)
    def _reset():
        st_ref[...] = jnp.zeros(st_ref.shape, st_ref.dtype)
        kvp_ref[...] = jnp.zeros(kvp_ref.shape, kvp_ref.dtype)

    n_ff = D_FF // FF_CHUNK
    n_blk = tm // WINDOW
    chunks_per_blk = WINDOW // GLA_CHUNK
    assert n_ff == 2 and n_blk == 4, "MATMUL_ORDER is written for 2 MLP slices and 4 mixer slices"

    def run(do_mlp, do_mix):
        if do_mlp:
            hn = _rms(h_ref[...], lmlp_ref[...]).astype(BF16)
        if do_mix:
            xn = _rms(xc_ref[...], lmix_ref[...]).astype(BF16)
            z_ref[...] = jnp.dot(xn, win_ref[...], preferred_element_type=F32)
            lb = _lower_bound(lbl_ref[...])
            g_norm = gn_ref[...]

        def mlp_up(c):
            cs = slice(c * FF_CHUNK, (c + 1) * FF_CHUNK)
            u = jnp.dot(hn, wu_ref[:, cs], preferred_element_type=F32)
            return jnp.square(jnp.maximum(u, 0.0)).astype(BF16)

        def mlp_down(c, a):
            cs = slice(c * FF_CHUNK, (c + 1) * FF_CHUNK)
            base = h_ref if c == 0 else acc_ref
            acc_ref[...] = base[...] + jnp.dot(a, wd_ref[cs, :], preferred_element_type=F32)

        def kv_blocks(n):
            kv_prev = kvp_ref[...] if n == 0 else z_ref[(n - 1) * WINDOW:n * WINDOW, KV_COLS]
            return z_ref[n * WINDOW:(n + 1) * WINDOW, KV_COLS], kv_prev

        def mix_front(n):
            gla = [_gla_front(z_ref[c * GLA_CHUNK:(c + 1) * GLA_CHUNK, 0:N_HG_IN], lb, True)
                   for c in range(n * chunks_per_blk, (n + 1) * chunks_per_blk)]
            kv_cur, kv_prev = kv_blocks(n)
            swa = _swa_front(z_ref[n * WINDOW:(n + 1) * WINDOW, N_HG_IN:N_HG_IN + SWA_WIDTH],
                             kv_cur, kv_prev)
            return gla, swa

        def mix_back(n, front):
            gla, swa = front
            for i, c in enumerate(range(n * chunks_per_blk, (n + 1) * chunks_per_blk)):
                dec = gla[i][4]
                o, new_states = _gla_back(
                    gla[i], g_norm, [st_ref[h] for h in range(HG_HEADS)],
                    [dec[:, h * HG_DK:(h + 1) * HG_DK] for h in range(HG_HEADS)], True)
                for h in range(HG_HEADS):
                    st_ref[h] = new_states[h]
                o_ref[c * GLA_CHUNK:(c + 1) * GLA_CHUNK, 0:HG_WIDTH] = o.astype(BF16)
            kv_cur, kv_prev = kv_blocks(n)
            o_ref[n * WINDOW:(n + 1) * WINDOW, HG_WIDTH:] = _swa_back(
                swa, kv_cur, kv_prev, sink_ref, seq_start if n == 0 else None).astype(BF16)

        acts, fronts = {}, {}
        for step in MATMUL_ORDER.split():
            kind, i = step[0], step[1]
            if kind in "UD":
                if not do_mlp:
                    continue
                if kind == "U":
                    acts[i] = mlp_up(int(i))
                else:
                    mlp_down(int(i), acts.pop(i))
            elif not do_mix:
                continue
            elif kind == "F":
                fronts[i] = mix_front(int(i))
            elif kind == "B":
                mix_back(int(i), fronts.pop(i))
            elif step == "sF":
                fronts[step] = _sample_front(sample_refs, lb)
            else:
                assert step == "sB", step
                _sample_back(fronts.pop("sF"), sample_refs, sink_ref, g_norm)
        assert not acts and not fronts
        if do_mix:
            kvp_ref[...] = z_ref[tm - WINDOW:tm, KV_COLS]
            h_ref[...] = xc_ref[...] + jnp.dot(o_ref[...], wo_ref[...], preferred_element_type=F32)
        if do_mlp:
            y_ref[...] = _rms(acc_ref[...], lfin_ref[...])

    pl.when((s > 0) & (s < n_tiles))(functools.partial(run, True, True))
    pl.when(s == 0)(functools.partial(run, False, True))
    pl.when(s == n_tiles)(functools.partial(run, True, False))

    @pl.when((pos == tiles_per_seq - 1) & (s < n_tiles))
    def _seq_end():
        for h in range(HG_HEADS):
            so_ref[0, h] = st_ref[h].T
        ko_ref[0] = z_ref[tm - WINDOW:tm, N_HG_IN + SWA_WIDTH:N_HG_IN + SWA_WIDTH + SWA_KV_WIDTH].T
        vo_ref[0] = z_ref[tm - WINDOW:tm, N_HG_IN + SWA_WIDTH + SWA_KV_WIDTH:N_IN].T


def _fused_layer(x2d, bsz, sinks, ln_mix, w_in, lb_logits, hg_norm, w_out, ln_mlp, w_up, w_down,
                 ln_final, zh_s, zs_s, s0, ck, cv):
    n = x2d.shape[0]
    tm = DENSE_TILE
    n_tiles = n // tm
    tiles_per_seq = n_tiles // bsz
    assert tiles_per_seq * bsz * tm == n
    dbsz, s_new, _ = zh_s.shape
    wb = ck.shape[2]
    nb = pl.cdiv(dbsz, n_tiles)
    assert dbsz % nb == 0 and nb <= HG_DK
    n_sblk = dbsz // nb
    const = lambda s: (0, 0)
    single = pl.Buffered(1)
    seq_of = lambda s: jnp.minimum(s, n_tiles - 1) // tiles_per_seq
    srow3 = lambda s: (jnp.minimum(s, n_sblk - 1), 0, 0)
    srow4 = lambda s: (jnp.minimum(s, n_sblk - 1), 0, 0, 0)
    state_spec = pl.BlockSpec((nb, HG_HEADS, HG_DK, HG_DV), srow4)
    cache_spec = pl.BlockSpec((nb, SWA_KV_WIDTH, wb), srow3)
    return pl.pallas_call(
        functools.partial(_prompt_body, tiles_per_seq=tiles_per_seq, n_tiles=n_tiles),
        grid=(n_tiles + 1,),
        in_specs=[
            pl.BlockSpec(memory_space=pltpu.SMEM),
            pl.BlockSpec((tm, D_MODEL), lambda s: (jnp.minimum(s, n_tiles - 1), 0)),
            pl.BlockSpec((1, D_MODEL), const),
            pl.BlockSpec((D_MODEL, N_IN), const, pipeline_mode=single),
            pl.BlockSpec(lb_logits.shape, const),
            pl.BlockSpec((1, HG_DV), const),
            pl.BlockSpec((D_MODEL, D_MODEL), const, pipeline_mode=single),
            pl.BlockSpec((1, D_MODEL), const),
            pl.BlockSpec((D_MODEL, D_FF), const, pipeline_mode=single),
            pl.BlockSpec((D_FF, D_MODEL), const, pipeline_mode=single),
            pl.BlockSpec((1, D_MODEL), const),
            pl.BlockSpec((nb, s_new, N_HG_IN), srow3),
            pl.BlockSpec((nb, s_new, N_SWA_IN), srow3),
            state_spec,
            cache_spec,
            cache_spec,
        ],
        out_specs=[
            pl.BlockSpec((tm, D_MODEL), lambda s: (jnp.maximum(s - 1, 0), 0)),
            pl.BlockSpec((1, HG_HEADS, HG_DK, HG_DV), lambda s: (seq_of(s), 0, 0, 0)),
            pl.BlockSpec((1, WINDOW, SWA_KV_WIDTH), lambda s: (seq_of(s), 0, 0)),
            pl.BlockSpec((1, WINDOW, SWA_KV_WIDTH), lambda s: (seq_of(s), 0, 0)),
            pl.BlockSpec((nb, s_new, HG_WIDTH), srow3),
            pl.BlockSpec((nb, s_new, SWA_WIDTH), srow3),
            state_spec,
            cache_spec,
            cache_spec,
        ],
        out_shape=[
            jax.ShapeDtypeStruct((n, D_MODEL), F32),
            jax.ShapeDtypeStruct((bsz, HG_HEADS, HG_DK, HG_DV), F32),
            jax.ShapeDtypeStruct((bsz, WINDOW, SWA_KV_WIDTH), F32),
            jax.ShapeDtypeStruct((bsz, WINDOW, SWA_KV_WIDTH), F32),
            jax.ShapeDtypeStruct((dbsz, s_new, HG_WIDTH), F32),
            jax.ShapeDtypeStruct((dbsz, s_new, SWA_WIDTH), F32),
            jax.ShapeDtypeStruct(s0.shape, s0.dtype),
            jax.ShapeDtypeStruct(ck.shape, ck.dtype),
            jax.ShapeDtypeStruct(cv.shape, cv.dtype),
        ],
        scratch_shapes=[
            pltpu.VMEM((tm, N_IN), F32),
            pltpu.VMEM((tm, D_MODEL), BF16),
            pltpu.VMEM((HG_HEADS, HG_DV, HG_DK), F32),
            pltpu.VMEM((WINDOW, 2 * SWA_KV_WIDTH), F32),
            pltpu.VMEM((tm, D_MODEL), F32),
            pltpu.VMEM((tm, D_MODEL), F32),
        ],
        compiler_params=pltpu.CompilerParams(
            dimension_semantics=("arbitrary",), vmem_limit_bytes=VMEM_LIMIT_BYTES),
        name="fused_layer",
    )(sinks, x2d, ln_mix, w_in, lb_logits, hg_norm, w_out, ln_mlp, w_up, w_down, ln_final,
      zh_s, zs_s, s0, ck, cv)


def _inproj_body(x_ref, g_ref, w_ref, zh_ref, zs_ref):
    xn = _rms(x_ref[...], g_ref[...]).astype(BF16)
    zh_ref[...] = jnp.dot(xn, w_ref[:, :N_HG_IN], preferred_element_type=F32)
    zs_ref[...] = jnp.dot(xn, w_ref[:, N_HG_IN:], preferred_element_type=F32)


def _inproj(x2d, ln, w_in_bf16):
    n = x2d.shape[0]
    tm = DENSE_TILE
    return pl.pallas_call(
        _inproj_body,
        grid=(n // tm,),
        in_specs=[
            pl.BlockSpec((tm, D_MODEL), lambda i: (i, 0)),
            pl.BlockSpec((1, D_MODEL), lambda i: (0, 0)),
            pl.BlockSpec((D_MODEL, N_IN), lambda i: (0, 0)),
        ],
        out_specs=[
            pl.BlockSpec((tm, N_HG_IN), lambda i: (i, 0)),
            pl.BlockSpec((tm, N_SWA_IN), lambda i: (i, 0)),
        ],
        out_shape=[
            jax.ShapeDtypeStruct((n, N_HG_IN), F32),
            jax.ShapeDtypeStruct((n, N_SWA_IN), F32),
        ],
        compiler_params=pltpu.CompilerParams(
            dimension_semantics=("arbitrary",), vmem_limit_bytes=VMEM_LIMIT_BYTES),
        name="inproj",
    )(x2d, ln, w_in_bf16)


def _out_mlp_body(x_ref, oh_ref, os_ref, wo_ref, lm_ref, wu_ref, wd_ref, lf_ref, y_ref):
    o = jnp.concatenate([oh_ref[...].astype(BF16), os_ref[...].astype(BF16)], axis=-1)
    h = x_ref[...] + jnp.dot(o, wo_ref[...], preferred_element_type=F32)
    hn = _rms(h, lm_ref[...]).astype(BF16)
    u = jnp.dot(hn, wu_ref[...], preferred_element_type=F32)
    a = jnp.square(jnp.maximum(u, 0.0)).astype(BF16)
    y_ref[...] = _rms(h + jnp.dot(a, wd_ref[...], preferred_element_type=F32), lf_ref[...])


def _out_mlp(x2d, oh2d, os2d, w_out, ln_mlp, w_up, w_down, ln_final):
    n = x2d.shape[0]
    tm = DENSE_TILE
    const = lambda i: (0, 0)
    single = pl.Buffered(1)
    return pl.pallas_call(
        _out_mlp_body,
        grid=(n // tm,),
        in_specs=[
            pl.BlockSpec((tm, D_MODEL), lambda i: (i, 0)),
            pl.BlockSpec((tm, HG_WIDTH), lambda i: (i, 0)),
            pl.BlockSpec((tm, SWA_WIDTH), lambda i: (i, 0)),
            pl.BlockSpec((D_MODEL, D_MODEL), const, pipeline_mode=single),
            pl.BlockSpec((1, D_MODEL), const),
            pl.BlockSpec((D_MODEL, D_FF), const, pipeline_mode=single),
            pl.BlockSpec((D_FF, D_MODEL), const, pipeline_mode=single),
            pl.BlockSpec((1, D_MODEL), const),
        ],
        out_specs=pl.BlockSpec((tm, D_MODEL), lambda i: (i, 0)),
        out_shape=jax.ShapeDtypeStruct((n, D_MODEL), F32),
        compiler_params=pltpu.CompilerParams(
            dimension_semantics=("arbitrary",), vmem_limit_bytes=VMEM_LIMIT_BYTES),
        name="out_mlp",
    )(x2d, oh2d, os2d, w_out, ln_mlp, w_up, w_down, ln_final)


def kernel(x_prompt, x_sample, state_hgrn, cache_swa_k, cache_swa_v, ln_mix, w_in, lb_logits,
           hg_norm, sinks, w_out, ln_mlp, w_up, w_down, ln_final):
    depth = w_in.shape[0]
    assert depth == 1 and lb_logits.shape[0] == depth + 1
    bsz, seq, _ = x_prompt.shape
    dbsz, dseq, _ = x_sample.shape
    assert seq % DENSE_TILE == 0 and seq >= WINDOW and (dbsz * dseq) % DENSE_TILE == 0

    w_in_b = w_in[0].astype(BF16)
    w_out_b = w_out[0].astype(BF16)
    w_up_b = w_up[0].astype(BF16)
    w_down_b = w_down[0].astype(BF16)
    ln_mix2 = ln_mix[0].reshape(1, D_MODEL)
    ln_mlp2 = ln_mlp[0].reshape(1, D_MODEL)
    ln_fin2 = ln_final.reshape(1, D_MODEL)
    gn2 = hg_norm[0].reshape(1, HG_DV)
    sink1 = sinks[0]

    def feature_major(c):
        return jnp.transpose(c, (0, 2, 3, 1)).reshape(c.shape[0], SWA_KV_WIDTH, c.shape[1])

    def position_major(c):
        return jnp.transpose(c.reshape(c.shape[0], SWA_KV_HEADS, SWA_HEAD_DIM, c.shape[2]), (0, 3, 1, 2))

    xs = x_sample.reshape(dbsz * dseq, D_MODEL)
    zh_s, zs_s = _inproj(xs, ln_mix2, w_in_b)
    xp = x_prompt.reshape(bsz * seq, D_MODEL)
    y_p, s_p, k_p, v_p, oh_s, os_s, s_s, ck_s, cv_s = _fused_layer(
        xp, bsz, sink1, ln_mix2, w_in_b, lb_logits, gn2, w_out_b, ln_mlp2, w_up_b, w_down_b, ln_fin2,
        zh_s.reshape(dbsz, dseq, N_HG_IN), zs_s.reshape(dbsz, dseq, N_SWA_IN),
        state_hgrn[0], feature_major(cache_swa_k[0]), feature_major(cache_swa_v[0]))
    y_s = _out_mlp(xs, oh_s.reshape(-1, HG_WIDTH), os_s.reshape(-1, SWA_WIDTH), w_out_b, ln_mlp2,
                   w_up_b, w_down_b, ln_fin2)

    return (y_p.reshape(bsz, seq, D_MODEL),
            y_s.reshape(dbsz, dseq, D_MODEL),
            s_p[None],
            position_major(k_p)[None].astype(cache_swa_k.dtype),
            position_major(v_p)[None].astype(cache_swa_v.dtype),
            s_s[None],
            position_major(ck_s)[None],
            position_major(cv_s)[None])
```

```python
import functools

import jax
import jax.numpy as jnp
from jax import lax
from jax.experimental import pallas as pl
from jax.experimental.pallas import tpu as pltpu

F32 = jnp.float32
BF16 = jnp.bfloat16

D_MODEL = 1024
HG_WIDTH = 512
HG_HEADS = 4
HG_DK = 128
HG_DV = 128
SWA_WIDTH = 512
SWA_HEAD_DIM = 64
SWA_Q_HEADS = 8
SWA_KV_HEADS = 2
SWA_GROUP = SWA_Q_HEADS // SWA_KV_HEADS
SWA_KV_WIDTH = SWA_KV_HEADS * SWA_HEAD_DIM
WINDOW = 128
SWA_SCALE = SWA_HEAD_DIM ** -0.5
D_FF = 4 * D_MODEL
EPS = 1e-6
N_HG_IN = 4 * HG_WIDTH
N_SWA_IN = SWA_WIDTH + 2 * SWA_KV_WIDTH
N_IN = N_HG_IN + N_SWA_IN
KV_COLS = slice(N_HG_IN + SWA_WIDTH, N_IN)
NEG_BIG = -1e30

V7X_VMEM_BYTES = 64 * 1024 * 1024
VMEM_LIMIT_BYTES = V7X_VMEM_BYTES - 4 * 1024 * 1024
DENSE_TILE = 512
GLA_CHUNK = 64
FF_CHUNK = 2048
MATMUL_ORDER = "sF U0 F0 F1 D0 sB B0 B1 F2 F3 U1 B2 B3 D1"


def _rms(x, g):
    return x * lax.rsqrt(jnp.mean(x * x, axis=-1, keepdims=True) + EPS) * g


def _sigmoid(x):
    return 1.0 / (1.0 + jnp.exp(-x))


def _dot(a, b):
    return jnp.dot(a.astype(BF16), b.astype(BF16), preferred_element_type=F32)


def _dot_nt(a, b):
    return lax.dot_general(a.astype(BF16), b.astype(BF16), (((1,), (1,)), ((), ())),
                           preferred_element_type=F32)


def _dot_tn(a, b):
    return lax.dot_general(a.astype(BF16), b.astype(BF16), (((0,), (0,)), ((), ())),
                           preferred_element_type=F32)


def _log2(n):
    assert n > 0 and n & (n - 1) == 0, n
    return n.bit_length() - 1


def _cumsum_rows(x):
    n = x.shape[0]
    row = lax.broadcasted_iota(jnp.int32, x.shape, 0)
    s = 1
    while s < n:
        x = x + jnp.where(row >= s, pltpu.roll(x, s, axis=0), 0.0)
        s *= 2
    return x


def _lower_bound(lb_logits):
    m = jnp.max(lb_logits, axis=0, keepdims=True)
    e = jnp.exp(lb_logits - m)
    return e[0:1, :] / jnp.sum(e, axis=0, keepdims=True)


def _gla_front(zh, lb):
    chunk = zh.shape[0]
    zq = zh[:, 0:HG_WIDTH]
    zf = zh[:, HG_WIDTH:2 * HG_WIDTH]
    v = zh[:, 2 * HG_WIDTH:3 * HG_WIDTH].astype(BF16)
    zg = zh[:, 3 * HG_WIDTH:4 * HG_WIDTH]

    q = zq * _sigmoid(zq)
    f = lb + (1.0 - lb) * _sigmoid(zf)
    k = 1.0 - f
    b = _cumsum_rows(jnp.log(f))
    mid = chunk // 2 - 1
    b_mid = b[mid:mid + 1, :]
    b_last = b[chunk - 1:chunk, :]
    qf = q * jnp.exp(b - b_mid)
    kf = k * jnp.exp(b_mid - b)
    qt = q * jnp.exp(b)
    ke = k * jnp.exp(b_last - b)
    dec = jnp.exp(b_last)
    gate = zg * _sigmoid(zg)
    heads = [slice(h * HG_DK, (h + 1) * HG_DK) for h in range(HG_HEADS)]
    scores = [_dot_nt(qf[:, sl], kf[:, sl]) for sl in heads]
    kv = [_dot_tn(ke[:, sl], v[:, sl]) for sl in heads]
    return scores, kv, qt.astype(BF16), v, dec, gate


def _decay_columns(dec_rows):
    pad = jnp.zeros((HG_DK - len(dec_rows), HG_WIDTH), F32)
    return jnp.concatenate(list(dec_rows) + [pad], axis=0).T


def _gla_back(front, g_norm, states, decays):
    scores, kv, qt, v, _, gate = front
    chunk = qt.shape[0]
    row = lax.broadcasted_iota(jnp.int32, (chunk, chunk), 0)
    col = lax.broadcasted_iota(jnp.int32, (chunk, chunk), 1)
    causal = row >= col
    outs, new_states = [], []
    for h in range(HG_HEADS):
        sl = slice(h * HG_DK, (h + 1) * HG_DK)
        a = jnp.where(causal, scores[h], 0.0)
        o = _dot(a, v[:, sl]) + _dot(qt[:, sl], states[h])
        new_states.append(states[h] * decays[h] + kv[h])
        outs.append(_rms(o, g_norm) * gate[:, sl])
    return jnp.concatenate(outs, axis=-1), new_states


def _stack_heads(x, kh):
    return jnp.concatenate(
        [x[:, (kh * SWA_GROUP + g) * SWA_HEAD_DIM:(kh * SWA_GROUP + g + 1) * SWA_HEAD_DIM]
         for g in range(SWA_GROUP)], axis=0)


def _sink_column(sink_ref, kh, rows_per_head):
    r = lax.broadcasted_iota(jnp.int32, (SWA_GROUP * rows_per_head, 1), 0)
    col = jnp.full(r.shape, sink_ref[kh * SWA_GROUP], F32)
    for g in range(1, SWA_GROUP):
        col = jnp.where(r >= g * rows_per_head, sink_ref[kh * SWA_GROUP + g], col)
    return col


def _swa_front(zq, kv_cur, kv_prev):
    scores = []
    for kh in range(SWA_KV_HEADS):
        ks = slice(kh * SWA_HEAD_DIM, (kh + 1) * SWA_HEAD_DIM)
        q = (_stack_heads(zq, kh) * SWA_SCALE).astype(BF16)
        scores.append((_dot_nt(q, kv_prev[:, ks]), _dot_nt(q, kv_cur[:, ks])))
    return scores


def _swa_back(scores, kv_cur, kv_prev, sink_ref, no_prev):
    rows = SWA_GROUP * WINDOW
    i = lax.broadcasted_iota(jnp.int32, (rows, WINDOW), 0) & (WINDOW - 1)
    j = lax.broadcasted_iota(jnp.int32, (rows, WINDOW), 1)
    use_prev = j > i
    outs = []
    for kh in range(SWA_KV_HEADS):
        vs = slice(SWA_KV_WIDTH + kh * SWA_HEAD_DIM, SWA_KV_WIDTH + (kh + 1) * SWA_HEAD_DIM)
        sink = _sink_column(sink_ref, kh, WINDOW)
        s_prev, s_cur = scores[kh]
        if no_prev is not None:
            s_prev = jnp.where(no_prev, NEG_BIG, s_prev)
        s = jnp.where(use_prev, s_prev, s_cur)
        m = jnp.maximum(jnp.max(s, axis=-1, keepdims=True), sink)
        p = jnp.exp(s - m)
        den = jnp.sum(p, axis=-1, keepdims=True) + jnp.exp(sink - m)
        o = (_dot(jnp.where(use_prev, p, 0.0), kv_prev[:, vs])
             + _dot(jnp.where(use_prev, 0.0, p), kv_cur[:, vs])) / den
        outs.extend(o[g * WINDOW:(g + 1) * WINDOW] for g in range(SWA_GROUP))
    return jnp.concatenate(outs, axis=-1)


def _swa_sample_front(z_ref, ck_ref, cv_ref, cko_ref, cvo_ref, nb, s_new, wb):
    nq = nb * s_new
    z = z_ref[:, :, N_HG_IN:].reshape(nq, N_SWA_IN)
    k_new = z[:, SWA_WIDTH:SWA_WIDTH + SWA_KV_WIDTH]
    v_new = z[:, SWA_WIDTH + SWA_KV_WIDTH:]
    assert nq <= SWA_KV_WIDTH and wb == SWA_KV_WIDTH
    pad = jnp.zeros((SWA_KV_WIDTH - nq, SWA_KV_WIDTH), F32)
    lane = lax.broadcasted_iota(jnp.int32, (SWA_KV_WIDTH, wb), 1)
    for new, c_ref, co_ref in ((k_new, ck_ref, cko_ref), (v_new, cv_ref, cvo_ref)):
        new_t = jnp.concatenate([new, pad], axis=0).T
        for b in range(nb):
            kept = pltpu.roll(c_ref[b], wb - s_new, axis=1)
            fresh = pltpu.roll(new_t, (wb - s_new - b * s_new) % wb, axis=1)
            co_ref[b] = jnp.where(lane >= wb - s_new, fresh, kept)
    scores = []
    for kh in range(SWA_KV_HEADS):
        cs = slice(kh * SWA_HEAD_DIM, (kh + 1) * SWA_HEAD_DIM)
        kt = jnp.concatenate([ck_ref[b, cs, :] for b in range(nb)], axis=1)
        q = (_stack_heads(z, kh) * SWA_SCALE).astype(BF16)
        scores.append((_dot(q, kt), _dot_nt(q, k_new[:, cs])))
    return scores, v_new.astype(BF16)


def _swa_sample_back(front, sink_ref, cv_ref, o_ref, nb, s_new, wb):
    scores, v_new = front
    nq = nb * s_new
    rows = SWA_GROUP * nq
    ls, lw = _log2(s_new), _log2(wb)
    _log2(nb)
    r = lax.broadcasted_iota(jnp.int32, (rows, nb * wb), 0)
    c = lax.broadcasted_iota(jnp.int32, (rows, nb * wb), 1)
    mask_c = (((c >> lw) == ((r >> ls) & (nb - 1)))
              & ((c & (wb - 1)) > (r & (s_new - 1)) + (wb - WINDOW)))
    r = lax.broadcasted_iota(jnp.int32, (rows, nq), 0)
    c = lax.broadcasted_iota(jnp.int32, (rows, nq), 1)
    mask_n = ((c >> ls) == ((r >> ls) & (nb - 1))) & ((c & (s_new - 1)) <= (r & (s_new - 1)))
    for kh in range(SWA_KV_HEADS):
        cs = slice(kh * SWA_HEAD_DIM, (kh + 1) * SWA_HEAD_DIM)
        vt = jnp.concatenate([cv_ref[b, cs, :] for b in range(nb)], axis=1)
        sink = _sink_column(sink_ref, kh, nq)
        sc = jnp.where(mask_c, scores[kh][0], NEG_BIG)
        sn = jnp.where(mask_n, scores[kh][1], NEG_BIG)
        m = jnp.maximum(jnp.maximum(jnp.max(sc, axis=-1, keepdims=True),
                                    jnp.max(sn, axis=-1, keepdims=True)), sink)
        pc = jnp.where(mask_c, jnp.exp(sc - m), 0.0)
        pn = jnp.where(mask_n, jnp.exp(sn - m), 0.0)
        den = (jnp.sum(pc, axis=-1, keepdims=True) + jnp.sum(pn, axis=-1, keepdims=True)
               + jnp.exp(sink - m))
        o = (_dot_nt(pc, vt) + _dot(pn, v_new[:, cs])) / den
        for g in range(SWA_GROUP):
            c0 = HG_WIDTH + (kh * SWA_GROUP + g) * SWA_HEAD_DIM
            o_ref[:, :, c0:c0 + SWA_HEAD_DIM] = o[g * nq:(g + 1) * nq].reshape(nb, s_new, SWA_HEAD_DIM)


def _sample_front(refs, lb):
    z_ref, _, ck_ref, cv_ref, _, _, cko_ref, cvo_ref = refs
    nb, s_new, _ = z_ref.shape
    gla = [_gla_front(z_ref[j, :, 0:N_HG_IN], lb) for j in range(nb)]
    swa = _swa_sample_front(z_ref, ck_ref, cv_ref, cko_ref, cvo_ref, nb, s_new, ck_ref.shape[2])
    return gla, swa


def _sample_back(front, refs, sink_ref, g_norm):
    z_ref, s0_ref, ck_ref, cv_ref, o_ref, so_ref, _, _ = refs
    nb, s_new, _ = z_ref.shape
    gla, swa = front
    dec_cols = _decay_columns([f[4] for f in gla])
    for j in range(nb):
        o, new_states = _gla_back(
            gla[j], g_norm, [s0_ref[j, h] for h in range(HG_HEADS)],
            [dec_cols[h * HG_DK:(h + 1) * HG_DK, j:j + 1] for h in range(HG_HEADS)])
        o_ref[j, :, 0:HG_WIDTH] = o
        for h in range(HG_HEADS):
            so_ref[j, h] = new_states[h]
    _swa_sample_back(swa, sink_ref, cv_ref, o_ref, nb, s_new, ck_ref.shape[2])


def _prompt_body(sink_ref, xc_ref, lmix_ref, win_ref, lbl_ref, gn_ref, wo_ref, lmlp_ref,
                 wu_ref, wd_ref, lfin_ref, zs_ref, s0_ref, cks_ref, cvs_ref,
                 y_ref, so_ref, ko_ref, vo_ref, os_ref, sso_ref, ckso_ref, cvso_ref,
                 z_ref, o_ref, st_ref, kvp_ref, h_ref, acc_ref, *, tiles_per_seq, n_tiles):
    sample_refs = (zs_ref, s0_ref, cks_ref, cvs_ref, os_ref, sso_ref, ckso_ref, cvso_ref)
    s = pl.program_id(0)
    pos = s % tiles_per_seq
    seq_start = pos == 0
    tm = DENSE_TILE

    @pl.when(seq_start)
    def _reset():
        st_ref[...] = jnp.zeros(st_ref.shape, st_ref.dtype)
        kvp_ref[...] = jnp.zeros(kvp_ref.shape, kvp_ref.dtype)

    n_ff = D_FF // FF_CHUNK
    n_blk = tm // WINDOW
    chunks_per_blk = WINDOW // GLA_CHUNK
    assert n_ff == 2 and n_blk == 4, "MATMUL_ORDER is written for 2 MLP slices and 4 mixer slices"

    def run(do_mlp, do_mix):
        if do_mlp:
            hn = _rms(h_ref[...], lmlp_ref[...]).astype(BF16)
        if do_mix:
            xn = _rms(xc_ref[...], lmix_ref[...]).astype(BF16)
            z_ref[...] = jnp.dot(xn, win_ref[...], preferred_element_type=F32)
            lb = _lower_bound(lbl_ref[...])
            g_norm = gn_ref[...]

        def mlp_up(c):
            cs = slice(c * FF_CHUNK, (c + 1) * FF_CHUNK)
            u = jnp.dot(hn, wu_ref[:, cs], preferred_element_type=F32)
            return jnp.square(jnp.maximum(u, 0.0)).astype(BF16)

        def mlp_down(c, a):
            cs = slice(c * FF_CHUNK, (c + 1) * FF_CHUNK)
            base = h_ref if c == 0 else acc_ref
            acc_ref[...] = base[...] + jnp.dot(a, wd_ref[cs, :], preferred_element_type=F32)

        def kv_blocks(n):
            kv_prev = kvp_ref[...] if n == 0 else z_ref[(n - 1) * WINDOW:n * WINDOW, KV_COLS]
            return z_ref[n * WINDOW:(n + 1) * WINDOW, KV_COLS], kv_prev

        def mix_front(n):
            gla = [_gla_front(z_ref[c * GLA_CHUNK:(c + 1) * GLA_CHUNK, 0:N_HG_IN], lb)
                   for c in range(n * chunks_per_blk, (n + 1) * chunks_per_blk)]
            kv_cur, kv_prev = kv_blocks(n)
            swa = _swa_front(z_ref[n * WINDOW:(n + 1) * WINDOW, N_HG_IN:N_HG_IN + SWA_WIDTH],
                             kv_cur, kv_prev)
            return gla, swa

        def mix_back(n, front):
            gla, swa = front
            dec_cols = _decay_columns([g[4] for g in gla])
            for i, c in enumerate(range(n * chunks_per_blk, (n + 1) * chunks_per_blk)):
                o, new_states = _gla_back(
                    gla[i], g_norm, [st_ref[h] for h in range(HG_HEADS)],
                    [dec_cols[h * HG_DK:(h + 1) * HG_DK, i:i + 1] for h in range(HG_HEADS)])
                for h in range(HG_HEADS):
                    st_ref[h] = new_states[h]
                o_ref[c * GLA_CHUNK:(c + 1) * GLA_CHUNK, 0:HG_WIDTH] = o.astype(BF16)
            kv_cur, kv_prev = kv_blocks(n)
            o_ref[n * WINDOW:(n + 1) * WINDOW, HG_WIDTH:] = _swa_back(
                swa, kv_cur, kv_prev, sink_ref, seq_start if n == 0 else None).astype(BF16)

        acts, fronts = {}, {}
        for step in MATMUL_ORDER.split():
            kind, i = step[0], step[1]
            if kind in "UD":
                if not do_mlp:
                    continue
                if kind == "U":
                    acts[i] = mlp_up(int(i))
                else:
                    mlp_down(int(i), acts.pop(i))
            elif not do_mix:
                continue
            elif kind == "F":
                fronts[i] = mix_front(int(i))
            elif kind == "B":
                mix_back(int(i), fronts.pop(i))
            elif step == "sF":
                fronts[step] = _sample_front(sample_refs, lb)
            else:
                assert step == "sB", step
                _sample_back(fronts.pop("sF"), sample_refs, sink_ref, g_norm)
        assert not acts and not fronts
        if do_mix:
            kvp_ref[...] = z_ref[tm - WINDOW:tm, KV_COLS]
            h_ref[...] = xc_ref[...] + jnp.dot(o_ref[...], wo_ref[...], preferred_element_type=F32)
        if do_mlp:
            y_ref[...] = _rms(acc_ref[...], lfin_ref[...])

    pl.when((s > 0) & (s < n_tiles))(functools.partial(run, True, True))
    pl.when(s == 0)(functools.partial(run, False, True))
    pl.when(s == n_tiles)(functools.partial(run, True, False))

    @pl.when((pos == tiles_per_seq - 1) & (s < n_tiles))
    def _seq_end():
        for h in range(HG_HEADS):
            so_ref[0, h] = st_ref[h]
        ko_ref[0] = z_ref[tm - WINDOW:tm, N_HG_IN + SWA_WIDTH:N_HG_IN + SWA_WIDTH + SWA_KV_WIDTH].T
        vo_ref[0] = z_ref[tm - WINDOW:tm, N_HG_IN + SWA_WIDTH + SWA_KV_WIDTH:N_IN].T


def _fused_layer(x2d, bsz, sinks, ln_mix, w_in, lb_logits, hg_norm, w_out, ln_mlp, w_up, w_down,
                 ln_final, z_s, s0, ck, cv):
    n = x2d.shape[0]
    tm = DENSE_TILE
    n_tiles = n // tm
    tiles_per_seq = n_tiles // bsz
    assert tiles_per_seq * bsz * tm == n
    dbsz, s_new, _ = z_s.shape
    wb = ck.shape[2]
    nb = pl.cdiv(dbsz, n_tiles)
    assert dbsz % nb == 0 and nb <= HG_DK
    n_sblk = dbsz // nb
    const = lambda s: (0, 0)
    single = pl.Buffered(1)
    seq_of = lambda s: jnp.minimum(s, n_tiles - 1) // tiles_per_seq
    srow3 = lambda s: (jnp.minimum(s, n_sblk - 1), 0, 0)
    srow4 = lambda s: (jnp.minimum(s, n_sblk - 1), 0, 0, 0)
    state_spec = pl.BlockSpec((nb, HG_HEADS, HG_DK, HG_DV), srow4)
    cache_spec = pl.BlockSpec((nb, SWA_KV_WIDTH, wb), srow3)
    return pl.pallas_call(
        functools.partial(_prompt_body, tiles_per_seq=tiles_per_seq, n_tiles=n_tiles),
        grid=(n_tiles + 1,),
        in_specs=[
            pl.BlockSpec(memory_space=pltpu.SMEM),
            pl.BlockSpec((tm, D_MODEL), lambda s: (jnp.minimum(s, n_tiles - 1), 0)),
            pl.BlockSpec((1, D_MODEL), const),
            pl.BlockSpec((D_MODEL, N_IN), const, pipeline_mode=single),
            pl.BlockSpec(lb_logits.shape, const),
            pl.BlockSpec((1, HG_DV), const),
            pl.BlockSpec((D_MODEL, D_MODEL), const, pipeline_mode=single),
            pl.BlockSpec((1, D_MODEL), const),
            pl.BlockSpec((D_MODEL, D_FF), const, pipeline_mode=single),
            pl.BlockSpec((D_FF, D_MODEL), const, pipeline_mode=single),
            pl.BlockSpec((1, D_MODEL), const),
            pl.BlockSpec((nb, s_new, N_IN), srow3),
            state_spec,
            cache_spec,
            cache_spec,
        ],
        out_specs=[
            pl.BlockSpec((tm, D_MODEL), lambda s: (jnp.maximum(s - 1, 0), 0)),
            pl.BlockSpec((1, HG_HEADS, HG_DK, HG_DV), lambda s: (seq_of(s), 0, 0, 0)),
            pl.BlockSpec((1, WINDOW, SWA_KV_WIDTH), lambda s: (seq_of(s), 0, 0)),
            pl.BlockSpec((1, WINDOW, SWA_KV_WIDTH), lambda s: (seq_of(s), 0, 0)),
            pl.BlockSpec((nb, s_new, D_MODEL), srow3),
            state_spec,
            cache_spec,
            cache_spec,
        ],
        out_shape=[
            jax.ShapeDtypeStruct((n, D_MODEL), F32),
            jax.ShapeDtypeStruct((bsz, HG_HEADS, HG_DK, HG_DV), F32),
            jax.ShapeDtypeStruct((bsz, WINDOW, SWA_KV_WIDTH), F32),
            jax.ShapeDtypeStruct((bsz, WINDOW, SWA_KV_WIDTH), F32),
            jax.ShapeDtypeStruct((dbsz, s_new, D_MODEL), F32),
            jax.ShapeDtypeStruct(s0.shape, s0.dtype),
            jax.ShapeDtypeStruct(ck.shape, ck.dtype),
            jax.ShapeDtypeStruct(cv.shape, cv.dtype),
        ],
        scratch_shapes=[
            pltpu.VMEM((tm, N_IN), F32),
            pltpu.VMEM((tm, D_MODEL), BF16),
            pltpu.VMEM((HG_HEADS, HG_DK, HG_DV), F32),
            pltpu.VMEM((WINDOW, 2 * SWA_KV_WIDTH), F32),
            pltpu.VMEM((tm, D_MODEL), F32),
            pltpu.VMEM((tm, D_MODEL), F32),
        ],
        compiler_params=pltpu.CompilerParams(
            dimension_semantics=("arbitrary",), vmem_limit_bytes=VMEM_LIMIT_BYTES),
        name="fused_layer",
    )(sinks, x2d, ln_mix, w_in, lb_logits, hg_norm, w_out, ln_mlp, w_up, w_down, ln_final,
      z_s, s0, ck, cv)


def _inproj_body(x_ref, g_ref, w_ref, z_ref):
    xn = _rms(x_ref[...], g_ref[...]).astype(BF16)
    z_ref[...] = jnp.dot(xn, w_ref[...], preferred_element_type=F32)


def _inproj(x2d, ln, w_in_bf16):
    n = x2d.shape[0]
    tm = DENSE_TILE
    return pl.pallas_call(
        _inproj_body,
        grid=(n // tm,),
        in_specs=[
            pl.BlockSpec((tm, D_MODEL), lambda i: (i, 0)),
            pl.BlockSpec((1, D_MODEL), lambda i: (0, 0)),
            pl.BlockSpec((D_MODEL, N_IN), lambda i: (0, 0)),
        ],
        out_specs=pl.BlockSpec((tm, N_IN), lambda i: (i, 0)),
        out_shape=jax.ShapeDtypeStruct((n, N_IN), F32),
        compiler_params=pltpu.CompilerParams(
            dimension_semantics=("arbitrary",), vmem_limit_bytes=VMEM_LIMIT_BYTES),
        name="inproj",
    )(x2d, ln, w_in_bf16)


def _out_mlp_body(x_ref, o_ref, wo_ref, lm_ref, wu_ref, wd_ref, lf_ref, y_ref):
    h = x_ref[...] + jnp.dot(o_ref[...].astype(BF16), wo_ref[...], preferred_element_type=F32)
    hn = _rms(h, lm_ref[...]).astype(BF16)
    u = jnp.dot(hn, wu_ref[...], preferred_element_type=F32)
    a = jnp.square(jnp.maximum(u, 0.0)).astype(BF16)
    y_ref[...] = _rms(h + jnp.dot(a, wd_ref[...], preferred_element_type=F32), lf_ref[...])


def _out_mlp(x2d, o2d, w_out, ln_mlp, w_up, w_down, ln_final):
    n = x2d.shape[0]
    tm = DENSE_TILE
    const = lambda i: (0, 0)
    single = pl.Buffered(1)
    return pl.pallas_call(
        _out_mlp_body,
        grid=(n // tm,),
        in_specs=[
            pl.BlockSpec((tm, D_MODEL), lambda i: (i, 0)),
            pl.BlockSpec((tm, D_MODEL), lambda i: (i, 0)),
            pl.BlockSpec((D_MODEL, D_MODEL), const, pipeline_mode=single),
            pl.BlockSpec((1, D_MODEL), const),
            pl.BlockSpec((D_MODEL, D_FF), const, pipeline_mode=single),
            pl.BlockSpec((D_FF, D_MODEL), const, pipeline_mode=single),
            pl.BlockSpec((1, D_MODEL), const),
        ],
        out_specs=pl.BlockSpec((tm, D_MODEL), lambda i: (i, 0)),
        out_shape=jax.ShapeDtypeStruct((n, D_MODEL), F32),
        compiler_params=pltpu.CompilerParams(
            dimension_semantics=("arbitrary",), vmem_limit_bytes=VMEM_LIMIT_BYTES),
        name="out_mlp",
    )(x2d, o2d, w_out, ln_mlp, w_up, w_down, ln_final)


def kernel(x_prompt, x_sample, state_hgrn, cache_swa_k, cache_swa_v, ln_mix, w_in, lb_logits,
           hg_norm, sinks, w_out, ln_mlp, w_up, w_down, ln_final):
    depth = w_in.shape[0]
    assert depth == 1 and lb_logits.shape[0] == depth + 1
    bsz, seq, _ = x_prompt.shape
    dbsz, dseq, _ = x_sample.shape
    assert seq % DENSE_TILE == 0 and seq >= WINDOW and (dbsz * dseq) % DENSE_TILE == 0

    w_in_b = w_in[0].astype(BF16)
    w_out_b = w_out[0].astype(BF16)
    w_up_b = w_up[0].astype(BF16)
    w_down_b = w_down[0].astype(BF16)
    ln_mix2 = ln_mix[0].reshape(1, D_MODEL)
    ln_mlp2 = ln_mlp[0].reshape(1, D_MODEL)
    ln_fin2 = ln_final.reshape(1, D_MODEL)
    gn2 = hg_norm[0].reshape(1, HG_DV)
    sink1 = sinks[0]

    def feature_major(c):
        return jnp.transpose(c, (0, 2, 3, 1)).reshape(c.shape[0], SWA_KV_WIDTH, c.shape[1])

    def position_major(c):
        return jnp.transpose(c.reshape(c.shape[0], SWA_KV_HEADS, SWA_HEAD_DIM, c.shape[2]), (0, 3, 1, 2))

    xs = x_sample.reshape(dbsz * dseq, D_MODEL)
    z_s = _inproj(xs, ln_mix2, w_in_b)
    xp = x_prompt.reshape(bsz * seq, D_MODEL)
    y_p, s_p, k_p, v_p, o_s, s_s, ck_s, cv_s = _fused_layer(
        xp, bsz, sink1, ln_mix2, w_in_b, lb_logits, gn2, w_out_b, ln_mlp2, w_up_b, w_down_b, ln_fin2,
        z_s.reshape(dbsz, dseq, N_IN),
        state_hgrn[0], feature_major(cache_swa_k[0]), feature_major(cache_swa_v[0]))
    y_s = _out_mlp(xs, o_s.reshape(-1, D_MODEL), w_out_b, ln_mlp2,
                   w_up_b, w_down_b, ln_fin2)

    return (y_p.reshape(bsz, seq, D_MODEL),
            y_s.reshape(dbsz, dseq, D_MODEL),
            s_p[None],
            position_major(k_p)[None].astype(cache_swa_k.dtype),
            position_major(v_p)[None].astype(cache_swa_v.dtype),
            s_s[None],
            position_major(ck_s)[None],
            position_major(cv_s)[None])
```

```python
import functools

import jax
import jax.numpy as jnp
from jax import lax
from jax.experimental import pallas as pl
from jax.experimental.pallas import tpu as pltpu

F32 = jnp.float32
BF16 = jnp.bfloat16

D_MODEL = 1024
HG_WIDTH = 512
HG_HEADS = 4
HG_DK = 128
HG_DV = 128
SWA_WIDTH = 512
SWA_HEAD_DIM = 64
SWA_Q_HEADS = 8
SWA_KV_HEADS = 2
SWA_GROUP = SWA_Q_HEADS // SWA_KV_HEADS
SWA_KV_WIDTH = SWA_KV_HEADS * SWA_HEAD_DIM
WINDOW = 128
SWA_SCALE = SWA_HEAD_DIM ** -0.5
D_FF = 4 * D_MODEL
EPS = 1e-6
N_HG_IN = 4 * HG_WIDTH
N_SWA_IN = SWA_WIDTH + 2 * SWA_KV_WIDTH
N_IN = N_HG_IN + N_SWA_IN
KV_COLS = slice(N_HG_IN + SWA_WIDTH, N_IN)
NEG_BIG = -1e30

V7X_VMEM_BYTES = 64 * 1024 * 1024
VMEM_LIMIT_BYTES = V7X_VMEM_BYTES - 4 * 1024 * 1024
DENSE_TILE = 512
GLA_CHUNK = 64
FF_CHUNK = 2048
MATMUL_ORDER = "sF U0 F0 F1 D0 sB B0 B1 F2 F3 U1 B2 B3 D1"


def _rms(x, g):
    return x * lax.rsqrt(jnp.mean(x * x, axis=-1, keepdims=True) + EPS) * g


def _sigmoid(x):
    return 1.0 / (1.0 + jnp.exp(-x))


def _dot(a, b):
    return jnp.dot(a.astype(BF16), b.astype(BF16), preferred_element_type=F32)


def _dot_nt(a, b):
    return lax.dot_general(a.astype(BF16), b.astype(BF16), (((1,), (1,)), ((), ())),
                           preferred_element_type=F32)


def _dot_tn(a, b):
    return lax.dot_general(a.astype(BF16), b.astype(BF16), (((0,), (0,)), ((), ())),
                           preferred_element_type=F32)


def _log2(n):
    assert n > 0 and n & (n - 1) == 0, n
    return n.bit_length() - 1


def _cumsum_rows(x):
    n = x.shape[0]
    row = lax.broadcasted_iota(jnp.int32, x.shape, 0)
    s = 1
    while s < n:
        x = x + jnp.where(row >= s, pltpu.roll(x, s, axis=0), 0.0)
        s *= 2
    return x


def _lower_bound(lb_logits):
    m = jnp.max(lb_logits, axis=0, keepdims=True)
    e = jnp.exp(lb_logits - m)
    return e[0:1, :] / jnp.sum(e, axis=0, keepdims=True)


def _gla_front(zh, lb):
    chunk = zh.shape[0]
    zq = zh[:, 0:HG_WIDTH]
    zf = zh[:, HG_WIDTH:2 * HG_WIDTH]
    v = zh[:, 2 * HG_WIDTH:3 * HG_WIDTH].astype(BF16)
    zg = zh[:, 3 * HG_WIDTH:4 * HG_WIDTH]

    q = zq * _sigmoid(zq)
    f = lb + (1.0 - lb) * _sigmoid(zf)
    k = 1.0 - f
    b = _cumsum_rows(jnp.log(f))
    mid = chunk // 2 - 1
    b_mid = b[mid:mid + 1, :]
    b_last = b[chunk - 1:chunk, :]
    qf = q * jnp.exp(b - b_mid)
    kf = k * jnp.exp(b_mid - b)
    qt = q * jnp.exp(b)
    ke = k * jnp.exp(b_last - b)
    dec = jnp.exp(b_last)
    gate = zg * _sigmoid(zg)
    heads = [slice(h * HG_DK, (h + 1) * HG_DK) for h in range(HG_HEADS)]
    kft = kf.T.astype(BF16)
    scores = [_dot(qf[:, sl], kft[sl, :]) for sl in heads]
    kv = [_dot_tn(ke[:, sl], v[:, sl]) for sl in heads]
    return scores, kv, qt.astype(BF16), v, dec, gate


def _decay_columns(dec_rows):
    pad = jnp.zeros((HG_DK - len(dec_rows), HG_WIDTH), F32)
    return jnp.concatenate(list(dec_rows) + [pad], axis=0).T


def _gla_back(front, g_norm, states, decays):
    scores, kv, qt, v, _, gate = front
    chunk = qt.shape[0]
    row = lax.broadcasted_iota(jnp.int32, (chunk, chunk), 0)
    col = lax.broadcasted_iota(jnp.int32, (chunk, chunk), 1)
    causal = row >= col
    outs, new_states = [], []
    for h in range(HG_HEADS):
        sl = slice(h * HG_DK, (h + 1) * HG_DK)
        a = jnp.where(causal, scores[h], 0.0)
        o = _dot(a, v[:, sl]) + _dot(qt[:, sl], states[h])
        new_states.append(states[h] * decays[h] + kv[h])
        outs.append(_rms(o, g_norm) * gate[:, sl])
    return jnp.concatenate(outs, axis=-1), new_states


def _stack_heads(x, kh):
    return jnp.concatenate(
        [x[:, (kh * SWA_GROUP + g) * SWA_HEAD_DIM:(kh * SWA_GROUP + g + 1) * SWA_HEAD_DIM]
         for g in range(SWA_GROUP)], axis=0)


def _sink_column(sink_ref, kh, rows_per_head):
    r = lax.broadcasted_iota(jnp.int32, (SWA_GROUP * rows_per_head, 1), 0)
    col = jnp.full(r.shape, sink_ref[kh * SWA_GROUP], F32)
    for g in range(1, SWA_GROUP):
        col = jnp.where(r >= g * rows_per_head, sink_ref[kh * SWA_GROUP + g], col)
    return col


def _swa_front(zq, kv_cur, kv_prev):
    scores = []
    for kh in range(SWA_KV_HEADS):
        ks = slice(kh * SWA_HEAD_DIM, (kh + 1) * SWA_HEAD_DIM)
        q = (_stack_heads(zq, kh) * SWA_SCALE).astype(BF16)
        scores.append((_dot_nt(q, kv_prev[:, ks]), _dot_nt(q, kv_cur[:, ks])))
    return scores


def _swa_back(scores, kv_cur, kv_prev, sink_ref, no_prev):
    rows = SWA_GROUP * WINDOW
    i = lax.broadcasted_iota(jnp.int32, (rows, WINDOW), 0) & (WINDOW - 1)
    j = lax.broadcasted_iota(jnp.int32, (rows, WINDOW), 1)
    use_prev = j > i
    outs = []
    for kh in range(SWA_KV_HEADS):
        vs = slice(SWA_KV_WIDTH + kh * SWA_HEAD_DIM, SWA_KV_WIDTH + (kh + 1) * SWA_HEAD_DIM)
        sink = _sink_column(sink_ref, kh, WINDOW)
        s_prev, s_cur = scores[kh]
        if no_prev is not None:
            s_prev = jnp.where(no_prev, NEG_BIG, s_prev)
        s = jnp.where(use_prev, s_prev, s_cur)
        m = jnp.maximum(jnp.max(s, axis=-1, keepdims=True), sink)
        p = jnp.exp(s - m)
        den = jnp.sum(p, axis=-1, keepdims=True) + jnp.exp(sink - m)
        o = (_dot(jnp.where(use_prev, p, 0.0), kv_prev[:, vs])
             + _dot(jnp.where(use_prev, 0.0, p), kv_cur[:, vs])) / den
        outs.extend(o[g * WINDOW:(g + 1) * WINDOW] for g in range(SWA_GROUP))
    return jnp.concatenate(outs, axis=-1)


def _swa_sample_front(z_ref, ck_ref, cv_ref, cko_ref, cvo_ref, nb, s_new, wb):
    nq = nb * s_new
    z = z_ref[:, :, N_HG_IN:].reshape(nq, N_SWA_IN)
    k_new = z[:, SWA_WIDTH:SWA_WIDTH + SWA_KV_WIDTH]
    v_new = z[:, SWA_WIDTH + SWA_KV_WIDTH:]
    assert nq <= SWA_KV_WIDTH and wb == SWA_KV_WIDTH
    pad = jnp.zeros((SWA_KV_WIDTH - nq, SWA_KV_WIDTH), F32)
    lane = lax.broadcasted_iota(jnp.int32, (SWA_KV_WIDTH, wb), 1)
    for new, c_ref, co_ref in ((k_new, ck_ref, cko_ref), (v_new, cv_ref, cvo_ref)):
        new_t = jnp.concatenate([new, pad], axis=0).T
        for b in range(nb):
            kept = pltpu.roll(c_ref[b], wb - s_new, axis=1)
            fresh = pltpu.roll(new_t, (wb - s_new - b * s_new) % wb, axis=1)
            co_ref[b] = jnp.where(lane >= wb - s_new, fresh, kept)
    scores = []
    for kh in range(SWA_KV_HEADS):
        cs = slice(kh * SWA_HEAD_DIM, (kh + 1) * SWA_HEAD_DIM)
        kt = jnp.concatenate([ck_ref[b, cs, :] for b in range(nb)], axis=1)
        q = (_stack_heads(z, kh) * SWA_SCALE).astype(BF16)
        scores.append((_dot(q, kt), _dot_nt(q, k_new[:, cs])))
    return scores, v_new.astype(BF16)


def _swa_sample_back(front, sink_ref, cv_ref, o_ref, nb, s_new, wb):
    scores, v_new = front
    nq = nb * s_new
    rows = SWA_GROUP * nq
    ls, lw = _log2(s_new), _log2(wb)
    _log2(nb)
    r = lax.broadcasted_iota(jnp.int32, (rows, nb * wb), 0)
    c = lax.broadcasted_iota(jnp.int32, (rows, nb * wb), 1)
    mask_c = (((c >> lw) == ((r >> ls) & (nb - 1)))
              & ((c & (wb - 1)) > (r & (s_new - 1)) + (wb - WINDOW)))
    r = lax.broadcasted_iota(jnp.int32, (rows, nq), 0)
    c = lax.broadcasted_iota(jnp.int32, (rows, nq), 1)
    mask_n = ((c >> ls) == ((r >> ls) & (nb - 1))) & ((c & (s_new - 1)) <= (r & (s_new - 1)))
    for kh in range(SWA_KV_HEADS):
        cs = slice(kh * SWA_HEAD_DIM, (kh + 1) * SWA_HEAD_DIM)
        vt = jnp.concatenate([cv_ref[b, cs, :] for b in range(nb)], axis=1)
        sink = _sink_column(sink_ref, kh, nq)
        sc = jnp.where(mask_c, scores[kh][0], NEG_BIG)
        sn = jnp.where(mask_n, scores[kh][1], NEG_BIG)
        m = jnp.maximum(jnp.maximum(jnp.max(sc, axis=-1, keepdims=True),
                                    jnp.max(sn, axis=-1, keepdims=True)), sink)
        pc = jnp.where(mask_c, jnp.exp(sc - m), 0.0)
        pn = jnp.where(mask_n, jnp.exp(sn - m), 0.0)
        den = (jnp.sum(pc, axis=-1, keepdims=True) + jnp.sum(pn, axis=-1, keepdims=True)
               + jnp.exp(sink - m))
        o = (_dot_nt(pc, vt) + _dot(pn, v_new[:, cs])) / den
        for g in range(SWA_GROUP):
            c0 = HG_WIDTH + (kh * SWA_GROUP + g) * SWA_HEAD_DIM
            o_ref[:, :, c0:c0 + SWA_HEAD_DIM] = o[g * nq:(g + 1) * nq].reshape(nb, s_new, SWA_HEAD_DIM)


def _sample_front(refs, lb):
    z_ref, _, ck_ref, cv_ref, _, _, cko_ref, cvo_ref = refs
    nb, s_new, _ = z_ref.shape
    gla = [_gla_front(z_ref[j, :, 0:N_HG_IN], lb) for j in range(nb)]
    swa = _swa_sample_front(z_ref, ck_ref, cv_ref, cko_ref, cvo_ref, nb, s_new, ck_ref.shape[2])
    return gla, swa


def _sample_back(front, refs, sink_ref, g_norm):
    z_ref, s0_ref, ck_ref, cv_ref, o_ref, so_ref, _, _ = refs
    nb, s_new, _ = z_ref.shape
    gla, swa = front
    dec_cols = _decay_columns([f[4] for f in gla])
    for j in range(nb):
        o, new_states = _gla_back(
            gla[j], g_norm, [s0_ref[j, h] for h in range(HG_HEADS)],
            [dec_cols[h * HG_DK:(h + 1) * HG_DK, j:j + 1] for h in range(HG_HEADS)])
        o_ref[j, :, 0:HG_WIDTH] = o
        for h in range(HG_HEADS):
            so_ref[j, h] = new_states[h]
    _swa_sample_back(swa, sink_ref, cv_ref, o_ref, nb, s_new, ck_ref.shape[2])


def _prompt_body(sink_ref, xc_ref, lmix_ref, win_ref, lbl_ref, gn_ref, wo_ref, lmlp_ref,
                 wu_ref, wd_ref, lfin_ref, zs_ref, s0_ref, cks_ref, cvs_ref,
                 y_ref, so_ref, ko_ref, vo_ref, os_ref, sso_ref, ckso_ref, cvso_ref,
                 z_ref, o_ref, st_ref, kvp_ref, h_ref, acc_ref, *, tiles_per_seq, n_tiles):
    sample_refs = (zs_ref, s0_ref, cks_ref, cvs_ref, os_ref, sso_ref, ckso_ref, cvso_ref)
    s = pl.program_id(0)
    pos = s % tiles_per_seq
    seq_start = pos == 0
    tm = DENSE_TILE

    @pl.when(seq_start)
    def _reset():
        st_ref[...] = jnp.zeros(st_ref.shape, st_ref.dtype)
        kvp_ref[...] = jnp.zeros(kvp_ref.shape, kvp_ref.dtype)

    n_ff = D_FF // FF_CHUNK
    n_blk = tm // WINDOW
    chunks_per_blk = WINDOW // GLA_CHUNK
    assert n_ff == 2 and n_blk == 4, "MATMUL_ORDER is written for 2 MLP slices and 4 mixer slices"

    def run(do_mlp, do_mix):
        if do_mlp:
            hn = _rms(h_ref[...], lmlp_ref[...]).astype(BF16)
        if do_mix:
            xn = _rms(xc_ref[...], lmix_ref[...]).astype(BF16)
            z_ref[...] = jnp.dot(xn, win_ref[...], preferred_element_type=F32)
            lb = _lower_bound(lbl_ref[...])
            g_norm = gn_ref[...]

        def mlp_up(c):
            cs = slice(c * FF_CHUNK, (c + 1) * FF_CHUNK)
            u = jnp.dot(hn, wu_ref[:, cs], preferred_element_type=F32)
            return jnp.square(jnp.maximum(u, 0.0)).astype(BF16)

        def mlp_down(c, a):
            cs = slice(c * FF_CHUNK, (c + 1) * FF_CHUNK)
            base = h_ref if c == 0 else acc_ref
            acc_ref[...] = base[...] + jnp.dot(a, wd_ref[cs, :], preferred_element_type=F32)

        def kv_blocks(n):
            kv_prev = kvp_ref[...] if n == 0 else z_ref[(n - 1) * WINDOW:n * WINDOW, KV_COLS]
            return z_ref[n * WINDOW:(n + 1) * WINDOW, KV_COLS], kv_prev

        def mix_front(n):
            gla = [_gla_front(z_ref[c * GLA_CHUNK:(c + 1) * GLA_CHUNK, 0:N_HG_IN], lb)
                   for c in range(n * chunks_per_blk, (n + 1) * chunks_per_blk)]
            kv_cur, kv_prev = kv_blocks(n)
            swa = _swa_front(z_ref[n * WINDOW:(n + 1) * WINDOW, N_HG_IN:N_HG_IN + SWA_WIDTH],
                             kv_cur, kv_prev)
            return gla, swa

        def mix_back(n, front):
            gla, swa = front
            dec_cols = _decay_columns([g[4] for g in gla])
            for i, c in enumerate(range(n * chunks_per_blk, (n + 1) * chunks_per_blk)):
                o, new_states = _gla_back(
                    gla[i], g_norm, [st_ref[h] for h in range(HG_HEADS)],
                    [dec_cols[h * HG_DK:(h + 1) * HG_DK, i:i + 1] for h in range(HG_HEADS)])
                for h in range(HG_HEADS):
                    st_ref[h] = new_states[h]
                o_ref[c * GLA_CHUNK:(c + 1) * GLA_CHUNK, 0:HG_WIDTH] = o.astype(BF16)
            kv_cur, kv_prev = kv_blocks(n)
            o_ref[n * WINDOW:(n + 1) * WINDOW, HG_WIDTH:] = _swa_back(
                swa, kv_cur, kv_prev, sink_ref, seq_start if n == 0 else None).astype(BF16)

        acts, fronts = {}, {}
        for step in MATMUL_ORDER.split():
            kind, i = step[0], step[1]
            if kind in "UD":
                if not do_mlp:
                    continue
                if kind == "U":
                    acts[i] = mlp_up(int(i))
                else:
                    mlp_down(int(i), acts.pop(i))
            elif not do_mix:
                continue
            elif kind == "F":
                fronts[i] = mix_front(int(i))
            elif kind == "B":
                mix_back(int(i), fronts.pop(i))
            elif step == "sF":
                fronts[step] = _sample_front(sample_refs, lb)
            else:
                assert step == "sB", step
                _sample_back(fronts.pop("sF"), sample_refs, sink_ref, g_norm)
        assert not acts and not fronts
        if do_mix:
            kvp_ref[...] = z_ref[tm - WINDOW:tm, KV_COLS]
            h_ref[...] = xc_ref[...] + jnp.dot(o_ref[...], wo_ref[...], preferred_element_type=F32)
        if do_mlp:
            y_ref[...] = _rms(acc_ref[...], lfin_ref[...])

    pl.when((s > 0) & (s < n_tiles))(functools.partial(run, True, True))
    pl.when(s == 0)(functools.partial(run, False, True))
    pl.when(s == n_tiles)(functools.partial(run, True, False))

    @pl.when((pos == tiles_per_seq - 1) & (s < n_tiles))
    def _seq_end():
        for h in range(HG_HEADS):
            so_ref[0, h] = st_ref[h]
        ko_ref[0] = z_ref[tm - WINDOW:tm, N_HG_IN + SWA_WIDTH:N_HG_IN + SWA_WIDTH + SWA_KV_WIDTH].T
        vo_ref[0] = z_ref[tm - WINDOW:tm, N_HG_IN + SWA_WIDTH + SWA_KV_WIDTH:N_IN].T


def _fused_layer(x2d, bsz, sinks, ln_mix, w_in, lb_logits, hg_norm, w_out, ln_mlp, w_up, w_down,
                 ln_final, z_s, s0, ck, cv):
    n = x2d.shape[0]
    tm = DENSE_TILE
    n_tiles = n // tm
    tiles_per_seq = n_tiles // bsz
    assert tiles_per_seq * bsz * tm == n
    dbsz, s_new, _ = z_s.shape
    wb = ck.shape[2]
    nb = pl.cdiv(dbsz, n_tiles)
    assert dbsz % nb == 0 and nb <= HG_DK
    n_sblk = dbsz // nb
    const = lambda s: (0, 0)
    single = pl.Buffered(1)
    seq_of = lambda s: jnp.minimum(s, n_tiles - 1) // tiles_per_seq
    srow3 = lambda s: (jnp.minimum(s, n_sblk - 1), 0, 0)
    srow4 = lambda s: (jnp.minimum(s, n_sblk - 1), 0, 0, 0)
    state_spec = pl.BlockSpec((nb, HG_HEADS, HG_DK, HG_DV), srow4)
    cache_spec = pl.BlockSpec((nb, SWA_KV_WIDTH, wb), srow3)
    return pl.pallas_call(
        functools.partial(_prompt_body, tiles_per_seq=tiles_per_seq, n_tiles=n_tiles),
        grid=(n_tiles + 1,),
        in_specs=[
            pl.BlockSpec(memory_space=pltpu.SMEM),
            pl.BlockSpec((tm, D_MODEL), lambda s: (jnp.minimum(s, n_tiles - 1), 0)),
            pl.BlockSpec((1, D_MODEL), const),
            pl.BlockSpec((D_MODEL, N_IN), const, pipeline_mode=single),
            pl.BlockSpec(lb_logits.shape, const),
            pl.BlockSpec((1, HG_DV), const),
            pl.BlockSpec((D_MODEL, D_MODEL), const, pipeline_mode=single),
            pl.BlockSpec((1, D_MODEL), const),
            pl.BlockSpec((D_MODEL, D_FF), const, pipeline_mode=single),
            pl.BlockSpec((D_FF, D_MODEL), const, pipeline_mode=single),
            pl.BlockSpec((1, D_MODEL), const),
            pl.BlockSpec((nb, s_new, N_IN), srow3),
            state_spec,
            cache_spec,
            cache_spec,
        ],
        out_specs=[
            pl.BlockSpec((tm, D_MODEL), lambda s: (jnp.maximum(s - 1, 0), 0)),
            pl.BlockSpec((1, HG_HEADS, HG_DK, HG_DV), lambda s: (seq_of(s), 0, 0, 0)),
            pl.BlockSpec((1, WINDOW, SWA_KV_WIDTH), lambda s: (seq_of(s), 0, 0)),
            pl.BlockSpec((1, WINDOW, SWA_KV_WIDTH), lambda s: (seq_of(s), 0, 0)),
            pl.BlockSpec((nb, s_new, D_MODEL), srow3),
            state_spec,
            cache_spec,
            cache_spec,
        ],
        out_shape=[
            jax.ShapeDtypeStruct((n, D_MODEL), F32),
            jax.ShapeDtypeStruct((bsz, HG_HEADS, HG_DK, HG_DV), F32),
            jax.ShapeDtypeStruct((bsz, WINDOW, SWA_KV_WIDTH), F32),
            jax.ShapeDtypeStruct((bsz, WINDOW, SWA_KV_WIDTH), F32),
            jax.ShapeDtypeStruct((dbsz, s_new, D_MODEL), F32),
            jax.ShapeDtypeStruct(s0.shape, s0.dtype),
            jax.ShapeDtypeStruct(ck.shape, ck.dtype),
            jax.ShapeDtypeStruct(cv.shape, cv.dtype),
        ],
        scratch_shapes=[
            pltpu.VMEM((tm, N_IN), F32),
            pltpu.VMEM((tm, D_MODEL), BF16),
            pltpu.VMEM((HG_HEADS, HG_DK, HG_DV), F32),
            pltpu.VMEM((WINDOW, 2 * SWA_KV_WIDTH), F32),
            pltpu.VMEM((tm, D_MODEL), F32),
            pltpu.VMEM((tm, D_MODEL), F32),
        ],
        compiler_params=pltpu.CompilerParams(
            dimension_semantics=("arbitrary",), vmem_limit_bytes=VMEM_LIMIT_BYTES),
        name="fused_layer",
    )(sinks, x2d, ln_mix, w_in, lb_logits, hg_norm, w_out, ln_mlp, w_up, w_down, ln_final,
      z_s, s0, ck, cv)


def _inproj_body(x_ref, g_ref, w_ref, z_ref):
    xn = _rms(x_ref[...], g_ref[...]).astype(BF16)
    z_ref[...] = jnp.dot(xn, w_ref[...], preferred_element_type=F32)


def _inproj(x2d, ln, w_in_bf16):
    n = x2d.shape[0]
    tm = DENSE_TILE
    return pl.pallas_call(
        _inproj_body,
        grid=(n // tm,),
        in_specs=[
            pl.BlockSpec((tm, D_MODEL), lambda i: (i, 0)),
            pl.BlockSpec((1, D_MODEL), lambda i: (0, 0)),
            pl.BlockSpec((D_MODEL, N_IN), lambda i: (0, 0)),
        ],
        out_specs=pl.BlockSpec((tm, N_IN), lambda i: (i, 0)),
        out_shape=jax.ShapeDtypeStruct((n, N_IN), F32),
        compiler_params=pltpu.CompilerParams(
            dimension_semantics=("arbitrary",), vmem_limit_bytes=VMEM_LIMIT_BYTES),
        name="inproj",
    )(x2d, ln, w_in_bf16)


def _out_mlp_body(x_ref, o_ref, wo_ref, lm_ref, wu_ref, wd_ref, lf_ref, y_ref):
    h = x_ref[...] + jnp.dot(o_ref[...].astype(BF16), wo_ref[...], preferred_element_type=F32)
    hn = _rms(h, lm_ref[...]).astype(BF16)
    u = jnp.dot(hn, wu_ref[...], preferred_element_type=F32)
    a = jnp.square(jnp.maximum(u, 0.0)).astype(BF16)
    y_ref[...] = _rms(h + jnp.dot(a, wd_ref[...], preferred_element_type=F32), lf_ref[...])


def _out_mlp(x2d, o2d, w_out, ln_mlp, w_up, w_down, ln_final):
    n = x2d.shape[0]
    tm = DENSE_TILE
    const = lambda i: (0, 0)
    single = pl.Buffered(1)
    return pl.pallas_call(
        _out_mlp_body,
        grid=(n // tm,),
        in_specs=[
            pl.BlockSpec((tm, D_MODEL), lambda i: (i, 0)),
            pl.BlockSpec((tm, D_MODEL), lambda i: (i, 0)),
            pl.BlockSpec((D_MODEL, D_MODEL), const, pipeline_mode=single),
            pl.BlockSpec((1, D_MODEL), const),
            pl.BlockSpec((D_MODEL, D_FF), const, pipeline_mode=single),
            pl.BlockSpec((D_FF, D_MODEL), const, pipeline_mode=single),
            pl.BlockSpec((1, D_MODEL), const),
        ],
        out_specs=pl.BlockSpec((tm, D_MODEL), lambda i: (i, 0)),
        out_shape=jax.ShapeDtypeStruct((n, D_MODEL), F32),
        compiler_params=pltpu.CompilerParams(
            dimension_semantics=("arbitrary",), vmem_limit_bytes=VMEM_LIMIT_BYTES),
        name="out_mlp",
    )(x2d, o2d, w_out, ln_mlp, w_up, w_down, ln_final)


def kernel(x_prompt, x_sample, state_hgrn, cache_swa_k, cache_swa_v, ln_mix, w_in, lb_logits,
           hg_norm, sinks, w_out, ln_mlp, w_up, w_down, ln_final):
    depth = w_in.shape[0]
    assert depth == 1 and lb_logits.shape[0] == depth + 1
    bsz, seq, _ = x_prompt.shape
    dbsz, dseq, _ = x_sample.shape
    assert seq % DENSE_TILE == 0 and seq >= WINDOW and (dbsz * dseq) % DENSE_TILE == 0

    w_in_b = w_in[0].astype(BF16)
    w_out_b = w_out[0].astype(BF16)
    w_up_b = w_up[0].astype(BF16)
    w_down_b = w_down[0].astype(BF16)
    ln_mix2 = ln_mix[0].reshape(1, D_MODEL)
    ln_mlp2 = ln_mlp[0].reshape(1, D_MODEL)
    ln_fin2 = ln_final.reshape(1, D_MODEL)
    gn2 = hg_norm[0].reshape(1, HG_DV)
    sink1 = sinks[0]

    def feature_major(c):
        return jnp.transpose(c, (0, 2, 3, 1)).reshape(c.shape[0], SWA_KV_WIDTH, c.shape[1])

    def position_major(c):
        return jnp.transpose(c.reshape(c.shape[0], SWA_KV_HEADS, SWA_HEAD_DIM, c.shape[2]), (0, 3, 1, 2))

    xs = x_sample.reshape(dbsz * dseq, D_MODEL)
    z_s = _inproj(xs, ln_mix2, w_in_b)
    xp = x_prompt.reshape(bsz * seq, D_MODEL)
    y_p, s_p, k_p, v_p, o_s, s_s, ck_s, cv_s = _fused_layer(
        xp, bsz, sink1, ln_mix2, w_in_b, lb_logits, gn2, w_out_b, ln_mlp2, w_up_b, w_down_b, ln_fin2,
        z_s.reshape(dbsz, dseq, N_IN),
        state_hgrn[0], feature_major(cache_swa_k[0]), feature_major(cache_swa_v[0]))
    y_s = _out_mlp(xs, o_s.reshape(-1, D_MODEL), w_out_b, ln_mlp2,
                   w_up_b, w_down_b, ln_fin2)

    return (y_p.reshape(bsz, seq, D_MODEL),
            y_s.reshape(dbsz, dseq, D_MODEL),
            s_p[None],
            position_major(k_p)[None].astype(cache_swa_k.dtype),
            position_major(v_p)[None].astype(cache_swa_v.dtype),
            s_s[None],
            position_major(ck_s)[None],
            position_major(cv_s)[None])
```

```python
import functools

import jax
import jax.numpy as jnp
from jax import lax
from jax.experimental import pallas as pl
from jax.experimental.pallas import tpu as pltpu

F32 = jnp.float32
BF16 = jnp.bfloat16

D_MODEL = 1024
HG_WIDTH = 512
HG_HEADS = 4
HG_DK = 128
HG_DV = 128
SWA_WIDTH = 512
SWA_HEAD_DIM = 64
SWA_Q_HEADS = 8
SWA_KV_HEADS = 2
SWA_GROUP = SWA_Q_HEADS // SWA_KV_HEADS
SWA_KV_WIDTH = SWA_KV_HEADS * SWA_HEAD_DIM
WINDOW = 128
SWA_SCALE = SWA_HEAD_DIM ** -0.5
D_FF = 4 * D_MODEL
EPS = 1e-6
N_HG_IN = 4 * HG_WIDTH
N_SWA_IN = SWA_WIDTH + 2 * SWA_KV_WIDTH
N_IN = N_HG_IN + N_SWA_IN
KV_COLS = slice(N_HG_IN + SWA_WIDTH, N_IN)
NEG_BIG = -1e30

V7X_VMEM_BYTES = 64 * 1024 * 1024
VMEM_LIMIT_BYTES = V7X_VMEM_BYTES - 4 * 1024 * 1024
DENSE_TILE = 512
GLA_CHUNK = 64
FF_CHUNK = 2048
MATMUL_ORDER = "sF U0 F0 F1 D0 sB B0 B1 F2 F3 U1 B2 B3 D1"


def _rms(x, g):
    return x * lax.rsqrt(jnp.mean(x * x, axis=-1, keepdims=True) + EPS) * g


def _sigmoid(x):
    return 1.0 / (1.0 + jnp.exp(-x))


def _dot(a, b):
    return jnp.dot(a.astype(BF16), b.astype(BF16), preferred_element_type=F32)


def _dot_nt(a, b):
    return lax.dot_general(a.astype(BF16), b.astype(BF16), (((1,), (1,)), ((), ())),
                           preferred_element_type=F32)


def _dot_tn(a, b):
    return lax.dot_general(a.astype(BF16), b.astype(BF16), (((0,), (0,)), ((), ())),
                           preferred_element_type=F32)


def _log2(n):
    assert n > 0 and n & (n - 1) == 0, n
    return n.bit_length() - 1


def _cumsum_rows(x):
    n = x.shape[0]
    row = lax.broadcasted_iota(jnp.int32, x.shape, 0)
    s = 1
    while s < n:
        x = x + jnp.where(row >= s, pltpu.roll(x, s, axis=0), 0.0)
        s *= 2
    return x


def _lower_bound(lb_logits):
    m = jnp.max(lb_logits, axis=0, keepdims=True)
    e = jnp.exp(lb_logits - m)
    return e[0:1, :] / jnp.sum(e, axis=0, keepdims=True)


def _gla_front(zh, lb):
    chunk = zh.shape[0]
    zq = zh[:, 0:HG_WIDTH]
    zf = zh[:, HG_WIDTH:2 * HG_WIDTH]
    v = zh[:, 2 * HG_WIDTH:3 * HG_WIDTH].astype(BF16)
    zg = zh[:, 3 * HG_WIDTH:4 * HG_WIDTH]

    q = zq * _sigmoid(zq)
    f = lb + (1.0 - lb) * _sigmoid(zf)
    k = 1.0 - f
    b = _cumsum_rows(jnp.log(f))
    mid = chunk // 2 - 1
    b_mid = b[mid:mid + 1, :]
    b_last = b[chunk - 1:chunk, :]
    qf = q * jnp.exp(b - b_mid)
    kf = k * jnp.exp(b_mid - b)
    qt = q * jnp.exp(b)
    ke = k * jnp.exp(b_last - b)
    dec = jnp.exp(b_last)
    gate = zg * _sigmoid(zg)
    heads = [slice(h * HG_DK, (h + 1) * HG_DK) for h in range(HG_HEADS)]
    scores = [_dot_nt(qf[:, sl], kf[:, sl]) for sl in heads]
    kv = [_dot_tn(ke[:, sl], v[:, sl]) for sl in heads]
    return scores, kv, qt.astype(BF16), v, dec, gate


def _decay_columns(dec_rows):
    pad = jnp.zeros((HG_DK - len(dec_rows), HG_WIDTH), F32)
    return jnp.concatenate(list(dec_rows) + [pad], axis=0).T


def _gla_back(front, g_norm, states, decays):
    scores, kv, qt, v, _, gate = front
    chunk = qt.shape[0]
    row = lax.broadcasted_iota(jnp.int32, (chunk, chunk), 0)
    col = lax.broadcasted_iota(jnp.int32, (chunk, chunk), 1)
    causal = row >= col
    outs, new_states = [], []
    for h in range(HG_HEADS):
        sl = slice(h * HG_DK, (h + 1) * HG_DK)
        a = jnp.where(causal, scores[h], 0.0)
        o = _dot(a, v[:, sl]) + _dot(qt[:, sl], states[h])
        new_states.append(states[h] * decays[h] + kv[h])
        outs.append(_rms(o, g_norm) * gate[:, sl])
    return jnp.concatenate(outs, axis=-1), new_states


def _stack_heads(x, kh):
    return jnp.concatenate(
        [x[:, (kh * SWA_GROUP + g) * SWA_HEAD_DIM:(kh * SWA_GROUP + g + 1) * SWA_HEAD_DIM]
         for g in range(SWA_GROUP)], axis=0)


def _sink_column(sink_ref, kh, rows_per_head):
    r = lax.broadcasted_iota(jnp.int32, (SWA_GROUP * rows_per_head, 1), 0)
    col = jnp.full(r.shape, sink_ref[kh * SWA_GROUP], F32)
    for g in range(1, SWA_GROUP):
        col = jnp.where(r >= g * rows_per_head, sink_ref[kh * SWA_GROUP + g], col)
    return col


def _swa_front(zq, kv_cur, kv_prev):
    scores = []
    for kh in range(SWA_KV_HEADS):
        ks = slice(kh * SWA_HEAD_DIM, (kh + 1) * SWA_HEAD_DIM)
        q = (_stack_heads(zq, kh) * SWA_SCALE).astype(BF16)
        scores.append((_dot_nt(q, kv_prev[:, ks]), _dot_nt(q, kv_cur[:, ks])))
    return scores


def _swa_back(scores, kv_cur, kv_prev, sink_ref, no_prev):
    rows = SWA_GROUP * WINDOW
    i = lax.broadcasted_iota(jnp.int32, (rows, WINDOW), 0) & (WINDOW - 1)
    j = lax.broadcasted_iota(jnp.int32, (rows, WINDOW), 1)
    use_prev = j > i
    outs = []
    for kh in range(SWA_KV_HEADS):
        vs = slice(SWA_KV_WIDTH + kh * SWA_HEAD_DIM, SWA_KV_WIDTH + (kh + 1) * SWA_HEAD_DIM)
        sink = _sink_column(sink_ref, kh, WINDOW)
        s_prev, s_cur = scores[kh]
        if no_prev is not None:
            s_prev = jnp.where(no_prev, NEG_BIG, s_prev)
        s = jnp.where(use_prev, s_prev, s_cur)
        m = jnp.maximum(jnp.max(s, axis=-1, keepdims=True), sink)
        p = jnp.exp(s - m)
        den = jnp.sum(p, axis=-1, keepdims=True) + jnp.exp(sink - m)
        o = (_dot(jnp.where(use_prev, p, 0.0), kv_prev[:, vs])
             + _dot(jnp.where(use_prev, 0.0, p), kv_cur[:, vs])) / den
        outs.extend(o[g * WINDOW:(g + 1) * WINDOW] for g in range(SWA_GROUP))
    return jnp.concatenate(outs, axis=-1)


def _swa_sample_front(z_ref, ck_ref, cv_ref, cko_ref, cvo_ref, nb, s_new, wb):
    nq = nb * s_new
    z = z_ref[:, :, N_HG_IN:].reshape(nq, N_SWA_IN)
    k_new = z[:, SWA_WIDTH:SWA_WIDTH + SWA_KV_WIDTH]
    v_new = z[:, SWA_WIDTH + SWA_KV_WIDTH:]
    assert nq <= SWA_KV_WIDTH and wb == SWA_KV_WIDTH
    pad = jnp.zeros((SWA_KV_WIDTH - nq, SWA_KV_WIDTH), F32)
    lane = lax.broadcasted_iota(jnp.int32, (SWA_KV_WIDTH, wb), 1)
    for new, c_ref, co_ref in ((k_new, ck_ref, cko_ref), (v_new, cv_ref, cvo_ref)):
        new_t = jnp.concatenate([new, pad], axis=0).T
        for b in range(nb):
            kept = pltpu.roll(c_ref[b], wb - s_new, axis=1)
            fresh = pltpu.roll(new_t, (wb - s_new - b * s_new) % wb, axis=1)
            co_ref[b] = jnp.where(lane >= wb - s_new, fresh, kept)
    scores = []
    for kh in range(SWA_KV_HEADS):
        cs = slice(kh * SWA_HEAD_DIM, (kh + 1) * SWA_HEAD_DIM)
        kt = jnp.concatenate([ck_ref[b, cs, :] for b in range(nb)], axis=1)
        q = (_stack_heads(z, kh) * SWA_SCALE).astype(BF16)
        scores.append((_dot(q, kt), _dot_nt(q, k_new[:, cs])))
    return scores, v_new.astype(BF16)


def _swa_sample_back(front, sink_ref, cv_ref, o_ref, nb, s_new, wb):
    scores, v_new = front
    nq = nb * s_new
    rows = SWA_GROUP * nq
    ls, lw = _log2(s_new), _log2(wb)
    _log2(nb)
    r = lax.broadcasted_iota(jnp.int32, (rows, nb * wb), 0)
    c = lax.broadcasted_iota(jnp.int32, (rows, nb * wb), 1)
    mask_c = (((c >> lw) == ((r >> ls) & (nb - 1)))
              & ((c & (wb - 1)) > (r & (s_new - 1)) + (wb - WINDOW)))
    r = lax.broadcasted_iota(jnp.int32, (rows, nq), 0)
    c = lax.broadcasted_iota(jnp.int32, (rows, nq), 1)
    mask_n = ((c >> ls) == ((r >> ls) & (nb - 1))) & ((c & (s_new - 1)) <= (r & (s_new - 1)))
    for kh in range(SWA_KV_HEADS):
        cs = slice(kh * SWA_HEAD_DIM, (kh + 1) * SWA_HEAD_DIM)
        vt = jnp.concatenate([cv_ref[b, cs, :] for b in range(nb)], axis=1)
        sink = _sink_column(sink_ref, kh, nq)
        sc = jnp.where(mask_c, scores[kh][0], NEG_BIG)
        sn = jnp.where(mask_n, scores[kh][1], NEG_BIG)
        m = jnp.maximum(jnp.maximum(jnp.max(sc, axis=-1, keepdims=True),
                                    jnp.max(sn, axis=-1, keepdims=True)), sink)
        pc = jnp.where(mask_c, jnp.exp(sc - m), 0.0)
        pn = jnp.where(mask_n, jnp.exp(sn - m), 0.0)
        den = (jnp.sum(pc, axis=-1, keepdims=True) + jnp.sum(pn, axis=-1, keepdims=True)
               + jnp.exp(sink - m))
        o = (_dot_nt(pc, vt) + _dot(pn, v_new[:, cs])) / den
        for g in range(SWA_GROUP):
            c0 = HG_WIDTH + (kh * SWA_GROUP + g) * SWA_HEAD_DIM
            o_ref[:, :, c0:c0 + SWA_HEAD_DIM] = o[g * nq:(g + 1) * nq].reshape(nb, s_new, SWA_HEAD_DIM)


def _sample_front(refs, lb):
    z_ref, _, ck_ref, cv_ref, _, _, cko_ref, cvo_ref = refs
    nb, s_new, _ = z_ref.shape
    gla = [_gla_front(z_ref[j, :, 0:N_HG_IN], lb) for j in range(nb)]
    swa = _swa_sample_front(z_ref, ck_ref, cv_ref, cko_ref, cvo_ref, nb, s_new, ck_ref.shape[2])
    return gla, swa


def _sample_back(front, refs, sink_ref, g_norm):
    z_ref, s0_ref, ck_ref, cv_ref, o_ref, so_ref, _, _ = refs
    nb, s_new, _ = z_ref.shape
    gla, swa = front
    dec_cols = _decay_columns([f[4] for f in gla])
    for j in range(nb):
        o, new_states = _gla_back(
            gla[j], g_norm, [s0_ref[j, h] for h in range(HG_HEADS)],
            [dec_cols[h * HG_DK:(h + 1) * HG_DK, j:j + 1] for h in range(HG_HEADS)])
        o_ref[j, :, 0:HG_WIDTH] = o
        for h in range(HG_HEADS):
            so_ref[j, h] = new_states[h]
    _swa_sample_back(swa, sink_ref, cv_ref, o_ref, nb, s_new, ck_ref.shape[2])


def _prompt_body(sink_ref, xc_ref, lmix_ref, win_ref, lbl_ref, gn_ref, wo_ref, lmlp_ref,
                 wu_ref, wd_ref, lfin_ref, zs_ref, s0_ref, cks_ref, cvs_ref,
                 y_ref, so_ref, ko_ref, vo_ref, os_ref, sso_ref, ckso_ref, cvso_ref,
                 z_ref, o_ref, st_ref, kvp_ref, h_ref, acc_ref, *, tiles_per_seq, n_tiles):
    sample_refs = (zs_ref, s0_ref, cks_ref, cvs_ref, os_ref, sso_ref, ckso_ref, cvso_ref)
    s = pl.program_id(0)
    pos = s % tiles_per_seq
    seq_start = pos == 0
    tm = DENSE_TILE

    @pl.when(seq_start)
    def _reset():
        st_ref[...] = jnp.zeros(st_ref.shape, st_ref.dtype)
        kvp_ref[...] = jnp.zeros(kvp_ref.shape, kvp_ref.dtype)

    n_ff = D_FF // FF_CHUNK
    n_blk = tm // WINDOW
    chunks_per_blk = WINDOW // GLA_CHUNK
    assert n_ff == 2 and n_blk == 4, "MATMUL_ORDER is written for 2 MLP slices and 4 mixer slices"

    def run(do_mlp, do_mix):
        if do_mlp:
            hn = _rms(h_ref[...], lmlp_ref[...]).astype(BF16)
        if do_mix:
            xn = _rms(xc_ref[...], lmix_ref[...]).astype(BF16)
            z_ref[...] = jnp.dot(xn, win_ref[...], preferred_element_type=F32)
            lb = _lower_bound(lbl_ref[...])
            g_norm = gn_ref[...]

        def mlp_up(c):
            cs = slice(c * FF_CHUNK, (c + 1) * FF_CHUNK)
            u = jnp.dot(hn, wu_ref[:, cs], preferred_element_type=F32)
            return jnp.square(jnp.maximum(u, 0.0)).astype(BF16)

        def mlp_down(c, a):
            cs = slice(c * FF_CHUNK, (c + 1) * FF_CHUNK)
            base = h_ref if c == 0 else acc_ref
            acc_ref[...] = base[...] + jnp.dot(a, wd_ref[cs, :], preferred_element_type=F32)

        def kv_blocks(n):
            kv_prev = kvp_ref[...] if n == 0 else z_ref[(n - 1) * WINDOW:n * WINDOW, KV_COLS]
            return z_ref[n * WINDOW:(n + 1) * WINDOW, KV_COLS], kv_prev

        def mix_front(n):
            gla = [_gla_front(z_ref[c * GLA_CHUNK:(c + 1) * GLA_CHUNK, 0:N_HG_IN], lb)
                   for c in range(n * chunks_per_blk, (n + 1) * chunks_per_blk)]
            kv_cur, kv_prev = kv_blocks(n)
            swa = _swa_front(z_ref[n * WINDOW:(n + 1) * WINDOW, N_HG_IN:N_HG_IN + SWA_WIDTH],
                             kv_cur, kv_prev)
            return gla, swa

        def mix_back(n, front):
            gla, swa = front
            dec_cols = _decay_columns([g[4] for g in gla])
            for i, c in enumerate(range(n * chunks_per_blk, (n + 1) * chunks_per_blk)):
                o, new_states = _gla_back(
                    gla[i], g_norm, [st_ref[h] for h in range(HG_HEADS)],
                    [dec_cols[h * HG_DK:(h + 1) * HG_DK, i:i + 1] for h in range(HG_HEADS)])
                for h in range(HG_HEADS):
                    st_ref[h] = new_states[h]
                o_ref[c * GLA_CHUNK:(c + 1) * GLA_CHUNK, 0:HG_WIDTH] = o.astype(BF16)
            kv_cur, kv_prev = kv_blocks(n)
            o_ref[n * WINDOW:(n + 1) * WINDOW, HG_WIDTH:] = _swa_back(
                swa, kv_cur, kv_prev, sink_ref, seq_start if n == 0 else None).astype(BF16)

        acts, fronts = {}, {}
        for step in MATMUL_ORDER.split():
            kind, i = step[0], step[1]
            if kind in "UD":
                if not do_mlp:
                    continue
                if kind == "U":
                    acts[i] = mlp_up(int(i))
                else:
                    mlp_down(int(i), acts.pop(i))
            elif not do_mix:
                continue
            elif kind == "F":
                fronts[i] = mix_front(int(i))
            elif kind == "B":
                mix_back(int(i), fronts.pop(i))
            elif step == "sF":
                fronts[step] = _sample_front(sample_refs, lb)
            else:
                assert step == "sB", step
                _sample_back(fronts.pop("sF"), sample_refs, sink_ref, g_norm)
        assert not acts and not fronts
        if do_mix:
            kvp_ref[...] = z_ref[tm - WINDOW:tm, KV_COLS]
            h_ref[...] = xc_ref[...] + jnp.dot(o_ref[...], wo_ref[...], preferred_element_type=F32)
        if do_mlp:
            y_ref[...] = _rms(acc_ref[...], lfin_ref[...])

    pl.when((s > 0) & (s < n_tiles))(functools.partial(run, True, True))
    pl.when(s == 0)(functools.partial(run, False, True))
    pl.when(s == n_tiles)(functools.partial(run, True, False))

    @pl.when((pos == tiles_per_seq - 1) & (s < n_tiles))
    def _seq_end():
        for h in range(HG_HEADS):
            so_ref[0, h] = st_ref[h]
        ko_ref[0] = z_ref[tm - WINDOW:tm, N_HG_IN + SWA_WIDTH:N_HG_IN + SWA_WIDTH + SWA_KV_WIDTH].T
        vo_ref[0] = z_ref[tm - WINDOW:tm, N_HG_IN + SWA_WIDTH + SWA_KV_WIDTH:N_IN].T


def _fused_layer(x2d, bsz, sinks, ln_mix, w_in, lb_logits, hg_norm, w_out, ln_mlp, w_up, w_down,
                 ln_final, z_s, s0, ck, cv):
    n = x2d.shape[0]
    tm = DENSE_TILE
    n_tiles = n // tm
    tiles_per_seq = n_tiles // bsz
    assert tiles_per_seq * bsz * tm == n
    dbsz, s_new, _ = z_s.shape
    wb = ck.shape[2]
    nb = pl.cdiv(dbsz, n_tiles)
    assert dbsz % nb == 0 and nb <= HG_DK
    n_sblk = dbsz // nb
    const = lambda s: (0, 0)
    single = pl.Buffered(1)
    seq_of = lambda s: jnp.minimum(s, n_tiles - 1) // tiles_per_seq
    srow3 = lambda s: (jnp.minimum(s, n_sblk - 1), 0, 0)
    srow4 = lambda s: (jnp.minimum(s, n_sblk - 1), 0, 0, 0)
    state_spec = pl.BlockSpec((nb, HG_HEADS, HG_DK, HG_DV), srow4)
    cache_spec = pl.BlockSpec((nb, SWA_KV_WIDTH, wb), srow3)
    return pl.pallas_call(
        functools.partial(_prompt_body, tiles_per_seq=tiles_per_seq, n_tiles=n_tiles),
        grid=(n_tiles + 1,),
        in_specs=[
            pl.BlockSpec(memory_space=pltpu.SMEM),
            pl.BlockSpec((tm, D_MODEL), lambda s: (jnp.minimum(s, n_tiles - 1), 0)),
            pl.BlockSpec((1, D_MODEL), const),
            pl.BlockSpec((D_MODEL, N_IN), const, pipeline_mode=single),
            pl.BlockSpec(lb_logits.shape, const),
            pl.BlockSpec((1, HG_DV), const),
            pl.BlockSpec((D_MODEL, D_MODEL), const, pipeline_mode=single),
            pl.BlockSpec((1, D_MODEL), const),
            pl.BlockSpec((D_MODEL, D_FF), const, pipeline_mode=single),
            pl.BlockSpec((D_FF, D_MODEL), const, pipeline_mode=single),
            pl.BlockSpec((1, D_MODEL), const),
            pl.BlockSpec((nb, s_new, N_IN), srow3),
            state_spec,
            cache_spec,
            cache_spec,
        ],
        out_specs=[
            pl.BlockSpec((tm, D_MODEL), lambda s: (jnp.maximum(s - 1, 0), 0)),
            pl.BlockSpec((1, HG_HEADS, HG_DK, HG_DV), lambda s: (seq_of(s), 0, 0, 0)),
            pl.BlockSpec((1, WINDOW, SWA_KV_WIDTH), lambda s: (seq_of(s), 0, 0)),
            pl.BlockSpec((1, WINDOW, SWA_KV_WIDTH), lambda s: (seq_of(s), 0, 0)),
            pl.BlockSpec((nb, s_new, D_MODEL), srow3),
            state_spec,
            cache_spec,
            cache_spec,
        ],
        out_shape=[
            jax.ShapeDtypeStruct((n, D_MODEL), F32),
            jax.ShapeDtypeStruct((bsz, HG_HEADS, HG_DK, HG_DV), F32),
            jax.ShapeDtypeStruct((bsz, WINDOW, SWA_KV_WIDTH), F32),
            jax.ShapeDtypeStruct((bsz, WINDOW, SWA_KV_WIDTH), F32),
            jax.ShapeDtypeStruct((dbsz, s_new, D_MODEL), F32),
            jax.ShapeDtypeStruct(s0.shape, s0.dtype),
            jax.ShapeDtypeStruct(ck.shape, ck.dtype),
            jax.ShapeDtypeStruct(cv.shape, cv.dtype),
        ],
        scratch_shapes=[
            pltpu.VMEM((tm, N_IN), F32),
            pltpu.VMEM((tm, D_MODEL), BF16),
            pltpu.VMEM((HG_HEADS, HG_DK, HG_DV), F32),
            pltpu.VMEM((WINDOW, 2 * SWA_KV_WIDTH), F32),
            pltpu.VMEM((tm, D_MODEL), F32),
            pltpu.VMEM((tm, D_MODEL), F32),
        ],
        compiler_params=pltpu.CompilerParams(
            dimension_semantics=("arbitrary",), vmem_limit_bytes=VMEM_LIMIT_BYTES),
        name="fused_layer",
    )(sinks, x2d, ln_mix, w_in, lb_logits, hg_norm, w_out, ln_mlp, w_up, w_down, ln_final,
      z_s, s0, ck, cv)


CAST_STEPS = 8


def _prep_body(x_ref, g_ref, win_ref, wo_ref, wu_ref, wd_ref,
               z_ref, win_o, wo_o, wu_o, wd_o, win_scr, *, n_cast):
    j = pl.program_id(0)

    @pl.when(j < n_cast)
    def _cast():
        rows = win_ref.shape[0]
        w = win_ref[...].astype(BF16)
        win_o[...] = w
        win_scr[pl.ds(pl.multiple_of(j * rows, rows), rows), :] = w
        wo_o[...] = wo_ref[...].astype(BF16)
        wu_o[...] = wu_ref[...].astype(BF16)
        wd_o[...] = wd_ref[...].astype(BF16)

    @pl.when(j >= n_cast)
    def _project():
        xn = _rms(x_ref[...], g_ref[...]).astype(BF16)
        z_ref[...] = jnp.dot(xn, win_scr[...], preferred_element_type=F32)


def _prep(x2d, ln, w_in, w_out, w_up, w_down):
    n = x2d.shape[0]
    tm = DENSE_TILE
    nc = CAST_STEPS
    weights = (w_in, w_out, w_up, w_down)
    assert all(w.shape[0] % (8 * nc) == 0 for w in weights)
    chunk = lambda s: (jnp.minimum(s, nc - 1), 0)
    tile = lambda s: (jnp.maximum(s - nc, 0), 0)
    w_specs = [pl.BlockSpec((w.shape[0] // nc, w.shape[1]), chunk) for w in weights]
    return pl.pallas_call(
        functools.partial(_prep_body, n_cast=nc),
        grid=(nc + n // tm,),
        in_specs=[pl.BlockSpec((tm, D_MODEL), tile), pl.BlockSpec((1, D_MODEL), lambda s: (0, 0))]
        + w_specs,
        out_specs=[pl.BlockSpec((tm, N_IN), tile)] + w_specs,
        out_shape=[jax.ShapeDtypeStruct((n, N_IN), F32)]
        + [jax.ShapeDtypeStruct(w.shape, BF16) for w in weights],
        scratch_shapes=[pltpu.VMEM(w_in.shape, BF16)],
        compiler_params=pltpu.CompilerParams(
            dimension_semantics=("arbitrary",), vmem_limit_bytes=VMEM_LIMIT_BYTES),
        name="prep",
    )(x2d, ln, w_in, w_out, w_up, w_down)


def _out_mlp_body(x_ref, o_ref, wo_ref, lm_ref, wu_ref, wd_ref, lf_ref, y_ref):
    h = x_ref[...] + jnp.dot(o_ref[...].astype(BF16), wo_ref[...], preferred_element_type=F32)
    hn = _rms(h, lm_ref[...]).astype(BF16)
    u = jnp.dot(hn, wu_ref[...], preferred_element_type=F32)
    a = jnp.square(jnp.maximum(u, 0.0)).astype(BF16)
    y_ref[...] = _rms(h + jnp.dot(a, wd_ref[...], preferred_element_type=F32), lf_ref[...])


def _out_mlp(x2d, o2d, w_out, ln_mlp, w_up, w_down, ln_final):
    n = x2d.shape[0]
    tm = DENSE_TILE
    const = lambda i: (0, 0)
    single = pl.Buffered(1)
    return pl.pallas_call(
        _out_mlp_body,
        grid=(n // tm,),
        in_specs=[
            pl.BlockSpec((tm, D_MODEL), lambda i: (i, 0)),
            pl.BlockSpec((tm, D_MODEL), lambda i: (i, 0)),
            pl.BlockSpec((D_MODEL, D_MODEL), const, pipeline_mode=single),
            pl.BlockSpec((1, D_MODEL), const),
            pl.BlockSpec((D_MODEL, D_FF), const, pipeline_mode=single),
            pl.BlockSpec((D_FF, D_MODEL), const, pipeline_mode=single),
            pl.BlockSpec((1, D_MODEL), const),
        ],
        out_specs=pl.BlockSpec((tm, D_MODEL), lambda i: (i, 0)),
        out_shape=jax.ShapeDtypeStruct((n, D_MODEL), F32),
        compiler_params=pltpu.CompilerParams(
            dimension_semantics=("arbitrary",), vmem_limit_bytes=VMEM_LIMIT_BYTES),
        name="out_mlp",
    )(x2d, o2d, w_out, ln_mlp, w_up, w_down, ln_final)


def kernel(x_prompt, x_sample, state_hgrn, cache_swa_k, cache_swa_v, ln_mix, w_in, lb_logits,
           hg_norm, sinks, w_out, ln_mlp, w_up, w_down, ln_final):
    depth = w_in.shape[0]
    assert depth == 1 and lb_logits.shape[0] == depth + 1
    bsz, seq, _ = x_prompt.shape
    dbsz, dseq, _ = x_sample.shape
    assert seq % DENSE_TILE == 0 and seq >= WINDOW and (dbsz * dseq) % DENSE_TILE == 0

    ln_mix2 = ln_mix[0].reshape(1, D_MODEL)
    ln_mlp2 = ln_mlp[0].reshape(1, D_MODEL)
    ln_fin2 = ln_final.reshape(1, D_MODEL)
    gn2 = hg_norm[0].reshape(1, HG_DV)
    sink1 = sinks[0]

    def feature_major(c):
        return jnp.transpose(c, (0, 2, 3, 1)).reshape(c.shape[0], SWA_KV_WIDTH, c.shape[1])

    def position_major(c):
        return jnp.transpose(c.reshape(c.shape[0], SWA_KV_HEADS, SWA_HEAD_DIM, c.shape[2]), (0, 3, 1, 2))

    xs = x_sample.reshape(dbsz * dseq, D_MODEL)
    z_s, w_in_b, w_out_b, w_up_b, w_down_b = _prep(xs, ln_mix2, w_in[0], w_out[0], w_up[0], w_down[0])
    xp = x_prompt.reshape(bsz * seq, D_MODEL)
    y_p, s_p, k_p, v_p, o_s, s_s, ck_s, cv_s = _fused_layer(
        xp, bsz, sink1, ln_mix2, w_in_b, lb_logits, gn2, w_out_b, ln_mlp2, w_up_b, w_down_b, ln_fin2,
        z_s.reshape(dbsz, dseq, N_IN),
        state_hgrn[0], feature_major(cache_swa_k[0]), feature_major(cache_swa_v[0]))
    y_s = _out_mlp(xs, o_s.reshape(-1, D_MODEL), w_out_b, ln_mlp2,
                   w_up_b, w_down_b, ln_fin2)

    return (y_p.reshape(bsz, seq, D_MODEL),
            y_s.reshape(dbsz, dseq, D_MODEL),
            s_p[None],
            position_major(k_p)[None].astype(cache_swa_k.dtype),
            position_major(v_p)[None].astype(cache_swa_v.dtype),
            s_s[None],
            position_major(ck_s)[None],
            position_major(cv_s)[None])
```

```python
import functools

import jax
import jax.numpy as jnp
from jax import lax
from jax.experimental import pallas as pl
from jax.experimental.pallas import tpu as pltpu

F32 = jnp.float32
BF16 = jnp.bfloat16

D_MODEL = 1024
HG_WIDTH = 512
HG_HEADS = 4
HG_DK = 128
HG_DV = 128
SWA_WIDTH = 512
SWA_HEAD_DIM = 64
SWA_Q_HEADS = 8
SWA_KV_HEADS = 2
SWA_GROUP = SWA_Q_HEADS // SWA_KV_HEADS
SWA_KV_WIDTH = SWA_KV_HEADS * SWA_HEAD_DIM
WINDOW = 128
SWA_SCALE = SWA_HEAD_DIM ** -0.5
D_FF = 4 * D_MODEL
EPS = 1e-6
N_HG_IN = 4 * HG_WIDTH
N_SWA_IN = SWA_WIDTH + 2 * SWA_KV_WIDTH
N_IN = N_HG_IN + N_SWA_IN
KV_COLS = slice(N_HG_IN + SWA_WIDTH, N_IN)
NEG_BIG = -1e30

V7X_VMEM_BYTES = 64 * 1024 * 1024
VMEM_LIMIT_BYTES = V7X_VMEM_BYTES - 4 * 1024 * 1024
DENSE_TILE = 512
GLA_CHUNK = 64
FF_CHUNK = 2048
MATMUL_ORDER = "sF U0 F0 F1 D0 sB B0 B1 F2 F3 U1 B2 B3 D1"


def _rms(x, g):
    return x * lax.rsqrt(jnp.mean(x * x, axis=-1, keepdims=True) + EPS) * g


def _sigmoid(x):
    return 1.0 / (1.0 + jnp.exp(-x))


def _dot(a, b):
    return jnp.dot(a.astype(BF16), b.astype(BF16), preferred_element_type=F32)


def _dot_nt(a, b):
    return lax.dot_general(a.astype(BF16), b.astype(BF16), (((1,), (1,)), ((), ())),
                           preferred_element_type=F32)


def _dot_tn(a, b):
    return lax.dot_general(a.astype(BF16), b.astype(BF16), (((0,), (0,)), ((), ())),
                           preferred_element_type=F32)


def _log2(n):
    assert n > 0 and n & (n - 1) == 0, n
    return n.bit_length() - 1


def _cumsum_rows(x):
    n = x.shape[0]
    row = lax.broadcasted_iota(jnp.int32, x.shape, 0)
    s = 1
    while s < n:
        x = x + jnp.where(row >= s, pltpu.roll(x, s, axis=0), 0.0)
        s *= 2
    return x


def _lower_bound(lb_logits):
    m = jnp.max(lb_logits, axis=0, keepdims=True)
    e = jnp.exp(lb_logits - m)
    return e[0:1, :] / jnp.sum(e, axis=0, keepdims=True)


def _gla_front(zh, lb):
    chunk = zh.shape[0]
    zq = zh[:, 0:HG_WIDTH]
    zf = zh[:, HG_WIDTH:2 * HG_WIDTH]
    v = zh[:, 2 * HG_WIDTH:3 * HG_WIDTH].astype(BF16)
    zg = zh[:, 3 * HG_WIDTH:4 * HG_WIDTH]

    q = zq * _sigmoid(zq)
    f = lb + (1.0 - lb) * _sigmoid(zf)
    k = 1.0 - f
    b = _cumsum_rows(jnp.log(f))
    mid = chunk // 2 - 1
    b_mid = b[mid:mid + 1, :]
    b_last = b[chunk - 1:chunk, :]
    qf = q * jnp.exp(b - b_mid)
    kf = k * jnp.exp(b_mid - b)
    qt = q * jnp.exp(b)
    ke = k * jnp.exp(b_last - b)
    dec = jnp.exp(b_last)
    gate = zg * _sigmoid(zg)
    heads = [slice(h * HG_DK, (h + 1) * HG_DK) for h in range(HG_HEADS)]
    scores = [_dot_nt(qf[:, sl], kf[:, sl]) for sl in heads]
    kv = [_dot_tn(ke[:, sl], v[:, sl]) for sl in heads]
    return scores, kv, qt.astype(BF16), v, dec, gate


def _decay_columns(dec_rows):
    pad = jnp.zeros((HG_DK - len(dec_rows), HG_WIDTH), F32)
    return jnp.concatenate(list(dec_rows) + [pad], axis=0).T


def _gla_back(front, g_norm, states, decays):
    scores, kv, qt, v, _, gate = front
    chunk = qt.shape[0]
    row = lax.broadcasted_iota(jnp.int32, (chunk, chunk), 0)
    col = lax.broadcasted_iota(jnp.int32, (chunk, chunk), 1)
    causal = row >= col
    outs, new_states = [], []
    for h in range(HG_HEADS):
        sl = slice(h * HG_DK, (h + 1) * HG_DK)
        a = jnp.where(causal, scores[h], 0.0)
        o = _dot(a, v[:, sl]) + _dot(qt[:, sl], states[h])
        new_states.append(states[h] * decays[h] + kv[h])
        outs.append(_rms(o, g_norm) * gate[:, sl])
    return jnp.concatenate(outs, axis=-1), new_states


def _stack_heads(x, kh):
    return jnp.concatenate(
        [x[:, (kh * SWA_GROUP + g) * SWA_HEAD_DIM:(kh * SWA_GROUP + g + 1) * SWA_HEAD_DIM]
         for g in range(SWA_GROUP)], axis=0)


def _sink_column(sink_ref, kh, rows_per_head):
    r = lax.broadcasted_iota(jnp.int32, (SWA_GROUP * rows_per_head, 1), 0)
    col = jnp.full(r.shape, sink_ref[kh * SWA_GROUP], F32)
    for g in range(1, SWA_GROUP):
        col = jnp.where(r >= g * rows_per_head, sink_ref[kh * SWA_GROUP + g], col)
    return col


def _swa_front(zq, kv_cur, kv_prev):
    scores = []
    for kh in range(SWA_KV_HEADS):
        ks = slice(kh * SWA_HEAD_DIM, (kh + 1) * SWA_HEAD_DIM)
        q = (_stack_heads(zq, kh) * SWA_SCALE).astype(BF16)
        scores.append((_dot_nt(q, kv_prev[:, ks]), _dot_nt(q, kv_cur[:, ks])))
    return scores


def _swa_back(scores, kv_cur, kv_prev, sink_ref, no_prev):
    rows = SWA_GROUP * WINDOW
    i = lax.broadcasted_iota(jnp.int32, (rows, WINDOW), 0) & (WINDOW - 1)
    j = lax.broadcasted_iota(jnp.int32, (rows, WINDOW), 1)
    use_prev = j > i
    outs = []
    for kh in range(SWA_KV_HEADS):
        vs = slice(SWA_KV_WIDTH + kh * SWA_HEAD_DIM, SWA_KV_WIDTH + (kh + 1) * SWA_HEAD_DIM)
        sink = _sink_column(sink_ref, kh, WINDOW)
        s_prev, s_cur = scores[kh]
        if no_prev is not None:
            s_prev = jnp.where(no_prev, NEG_BIG, s_prev)
        s = jnp.where(use_prev, s_prev, s_cur)
        m = jnp.maximum(jnp.max(s, axis=-1, keepdims=True), sink)
        p = jnp.exp(s - m)
        den = jnp.sum(p, axis=-1, keepdims=True) + jnp.exp(sink - m)
        o = (_dot(jnp.where(use_prev, p, 0.0), kv_prev[:, vs])
             + _dot(jnp.where(use_prev, 0.0, p), kv_cur[:, vs])) / den
        outs.extend(o[g * WINDOW:(g + 1) * WINDOW] for g in range(SWA_GROUP))
    return jnp.concatenate(outs, axis=-1)


def _swa_sample_front(z_ref, ck_ref, cv_ref, cko_ref, cvo_ref, nb, s_new, wb):
    nq = nb * s_new
    z = z_ref[:, :, N_HG_IN:].reshape(nq, N_SWA_IN)
    k_new = z[:, SWA_WIDTH:SWA_WIDTH + SWA_KV_WIDTH]
    v_new = z[:, SWA_WIDTH + SWA_KV_WIDTH:]
    assert nq <= SWA_KV_WIDTH and wb == SWA_KV_WIDTH
    pad = jnp.zeros((SWA_KV_WIDTH - nq, SWA_KV_WIDTH), F32)
    lane = lax.broadcasted_iota(jnp.int32, (SWA_KV_WIDTH, wb), 1)
    for new, c_ref, co_ref in ((k_new, ck_ref, cko_ref), (v_new, cv_ref, cvo_ref)):
        new_t = jnp.concatenate([new, pad], axis=0).T
        for b in range(nb):
            kept = pltpu.roll(c_ref[b], wb - s_new, axis=1)
            fresh = pltpu.roll(new_t, (wb - s_new - b * s_new) % wb, axis=1)
            co_ref[b] = jnp.where(lane >= wb - s_new, fresh, kept)
    scores = []
    for kh in range(SWA_KV_HEADS):
        cs = slice(kh * SWA_HEAD_DIM, (kh + 1) * SWA_HEAD_DIM)
        kt = jnp.concatenate([ck_ref[b, cs, :] for b in range(nb)], axis=1)
        q = (_stack_heads(z, kh) * SWA_SCALE).astype(BF16)
        scores.append((_dot(q, kt), _dot_nt(q, k_new[:, cs])))
    return scores, v_new.astype(BF16)


def _swa_sample_back(front, sink_ref, cv_ref, o_ref, nb, s_new, wb):
    scores, v_new = front
    nq = nb * s_new
    rows = SWA_GROUP * nq
    ls, lw = _log2(s_new), _log2(wb)
    _log2(nb)
    r = lax.broadcasted_iota(jnp.int32, (rows, nb * wb), 0)
    c = lax.broadcasted_iota(jnp.int32, (rows, nb * wb), 1)
    mask_c = (((c >> lw) == ((r >> ls) & (nb - 1)))
              & ((c & (wb - 1)) > (r & (s_new - 1)) + (wb - WINDOW)))
    r = lax.broadcasted_iota(jnp.int32, (rows, nq), 0)
    c = lax.broadcasted_iota(jnp.int32, (rows, nq), 1)
    mask_n = ((c >> ls) == ((r >> ls) & (nb - 1))) & ((c & (s_new - 1)) <= (r & (s_new - 1)))
    for kh in range(SWA_KV_HEADS):
        cs = slice(kh * SWA_HEAD_DIM, (kh + 1) * SWA_HEAD_DIM)
        vt = jnp.concatenate([cv_ref[b, cs, :] for b in range(nb)], axis=1)
        sink = _sink_column(sink_ref, kh, nq)
        sc = jnp.where(mask_c, scores[kh][0], NEG_BIG)
        sn = jnp.where(mask_n, scores[kh][1], NEG_BIG)
        m = jnp.maximum(jnp.maximum(jnp.max(sc, axis=-1, keepdims=True),
                                    jnp.max(sn, axis=-1, keepdims=True)), sink)
        pc = jnp.where(mask_c, jnp.exp(sc - m), 0.0)
        pn = jnp.where(mask_n, jnp.exp(sn - m), 0.0)
        den = (jnp.sum(pc, axis=-1, keepdims=True) + jnp.sum(pn, axis=-1, keepdims=True)
               + jnp.exp(sink - m))
        o = (_dot_nt(pc, vt) + _dot(pn, v_new[:, cs])) / den
        for g in range(SWA_GROUP):
            c0 = HG_WIDTH + (kh * SWA_GROUP + g) * SWA_HEAD_DIM
            o_ref[:, :, c0:c0 + SWA_HEAD_DIM] = o[g * nq:(g + 1) * nq].reshape(nb, s_new, SWA_HEAD_DIM)


def _sample_front(refs, lb):
    z_ref, _, ck_ref, cv_ref, _, _, cko_ref, cvo_ref = refs
    nb, s_new, _ = z_ref.shape
    gla = [_gla_front(z_ref[j, :, 0:N_HG_IN], lb) for j in range(nb)]
    swa = _swa_sample_front(z_ref, ck_ref, cv_ref, cko_ref, cvo_ref, nb, s_new, ck_ref.shape[2])
    return gla, swa


def _sample_back(front, refs, sink_ref, g_norm):
    z_ref, s0_ref, ck_ref, cv_ref, o_ref, so_ref, _, _ = refs
    nb, s_new, _ = z_ref.shape
    gla, swa = front
    dec_cols = _decay_columns([f[4] for f in gla])
    for j in range(nb):
        o, new_states = _gla_back(
            gla[j], g_norm, [s0_ref[j, h] for h in range(HG_HEADS)],
            [dec_cols[h * HG_DK:(h + 1) * HG_DK, j:j + 1] for h in range(HG_HEADS)])
        o_ref[j, :, 0:HG_WIDTH] = o
        for h in range(HG_HEADS):
            so_ref[j, h] = new_states[h]
    _swa_sample_back(swa, sink_ref, cv_ref, o_ref, nb, s_new, ck_ref.shape[2])


def _prompt_body(sink_ref, xc_ref, lmix_ref, win_ref, lbl_ref, gn_ref, wo_ref, lmlp_ref,
                 wu_ref, wd_ref, lfin_ref, zs_ref, s0_ref, cks_ref, cvs_ref,
                 y_ref, so_ref, ko_ref, vo_ref, os_ref, sso_ref, ckso_ref, cvso_ref,
                 z_ref, o_ref, st_ref, kvp_ref, h_ref, acc_ref, *, tiles_per_seq, n_tiles):
    sample_refs = (zs_ref, s0_ref, cks_ref, cvs_ref, os_ref, sso_ref, ckso_ref, cvso_ref)
    s = pl.program_id(0)
    pos = s % tiles_per_seq
    seq_start = pos == 0
    tm = DENSE_TILE

    @pl.when(seq_start)
    def _reset():
        st_ref[...] = jnp.zeros(st_ref.shape, st_ref.dtype)
        kvp_ref[...] = jnp.zeros(kvp_ref.shape, kvp_ref.dtype)

    n_ff = D_FF // FF_CHUNK
    n_blk = tm // WINDOW
    chunks_per_blk = WINDOW // GLA_CHUNK
    assert n_ff == 2 and n_blk == 4, "MATMUL_ORDER is written for 2 MLP slices and 4 mixer slices"

    def run(do_mlp, do_mix):
        if do_mlp:
            hn = _rms(h_ref[...], lmlp_ref[...]).astype(BF16)
        if do_mix:
            xn = _rms(xc_ref[...], lmix_ref[...]).astype(BF16)
            z_ref[...] = jnp.dot(xn, win_ref[...], preferred_element_type=F32)
            lb = _lower_bound(lbl_ref[...])
            g_norm = gn_ref[...]

        def mlp_up(c):
            cs = slice(c * FF_CHUNK, (c + 1) * FF_CHUNK)
            u = jnp.dot(hn, wu_ref[:, cs], preferred_element_type=F32)
            return jnp.square(jnp.maximum(u, 0.0)).astype(BF16)

        def mlp_down(c, a):
            cs = slice(c * FF_CHUNK, (c + 1) * FF_CHUNK)
            base = h_ref if c == 0 else acc_ref
            acc_ref[...] = base[...] + jnp.dot(a, wd_ref[cs, :], preferred_element_type=F32)

        def kv_blocks(n):
            kv_prev = kvp_ref[...] if n == 0 else z_ref[(n - 1) * WINDOW:n * WINDOW, KV_COLS]
            return z_ref[n * WINDOW:(n + 1) * WINDOW, KV_COLS], kv_prev

        def mix_front(n):
            gla = [_gla_front(z_ref[c * GLA_CHUNK:(c + 1) * GLA_CHUNK, 0:N_HG_IN], lb)
                   for c in range(n * chunks_per_blk, (n + 1) * chunks_per_blk)]
            kv_cur, kv_prev = kv_blocks(n)
            swa = _swa_front(z_ref[n * WINDOW:(n + 1) * WINDOW, N_HG_IN:N_HG_IN + SWA_WIDTH],
                             kv_cur, kv_prev)
            return gla, swa

        def mix_back(n, front):
            gla, swa = front
            dec_cols = _decay_columns([g[4] for g in gla])
            for i, c in enumerate(range(n * chunks_per_blk, (n + 1) * chunks_per_blk)):
                o, new_states = _gla_back(
                    gla[i], g_norm, [st_ref[h] for h in range(HG_HEADS)],
                    [dec_cols[h * HG_DK:(h + 1) * HG_DK, i:i + 1] for h in range(HG_HEADS)])
                for h in range(HG_HEADS):
                    st_ref[h] = new_states[h]
                o_ref[c * GLA_CHUNK:(c + 1) * GLA_CHUNK, 0:HG_WIDTH] = o.astype(BF16)
            kv_cur, kv_prev = kv_blocks(n)
            o_ref[n * WINDOW:(n + 1) * WINDOW, HG_WIDTH:] = _swa_back(
                swa, kv_cur, kv_prev, sink_ref, seq_start if n == 0 else None).astype(BF16)

        acts, fronts = {}, {}
        for step in MATMUL_ORDER.split():
            kind, i = step[0], step[1]
            if kind in "UD":
                if not do_mlp:
                    continue
                if kind == "U":
                    acts[i] = mlp_up(int(i))
                else:
                    mlp_down(int(i), acts.pop(i))
            elif not do_mix:
                continue
            elif kind == "F":
                fronts[i] = mix_front(int(i))
            elif kind == "B":
                mix_back(int(i), fronts.pop(i))
            elif step == "sF":
                fronts[step] = _sample_front(sample_refs, lb)
            else:
                assert step == "sB", step
                _sample_back(fronts.pop("sF"), sample_refs, sink_ref, g_norm)
        assert not acts and not fronts
        if do_mix:
            kvp_ref[...] = z_ref[tm - WINDOW:tm, KV_COLS]
            h_ref[...] = xc_ref[...] + jnp.dot(o_ref[...], wo_ref[...], preferred_element_type=F32)
        if do_mlp:
            y_ref[...] = _rms(acc_ref[...], lfin_ref[...])

    pl.when((s > 0) & (s < n_tiles))(functools.partial(run, True, True))
    pl.when(s == 0)(functools.partial(run, False, True))
    pl.when(s == n_tiles)(functools.partial(run, True, False))

    @pl.when((pos == tiles_per_seq - 1) & (s < n_tiles))
    def _seq_end():
        for h in range(HG_HEADS):
            so_ref[0, h] = st_ref[h]
        ko_ref[0] = z_ref[tm - WINDOW:tm, N_HG_IN + SWA_WIDTH:N_HG_IN + SWA_WIDTH + SWA_KV_WIDTH].T
        vo_ref[0] = z_ref[tm - WINDOW:tm, N_HG_IN + SWA_WIDTH + SWA_KV_WIDTH:N_IN].T


def _fused_layer(x2d, bsz, sinks, ln_mix, w_in, lb_logits, hg_norm, w_out, ln_mlp, w_up, w_down,
                 ln_final, z_s, s0, ck, cv):
    n = x2d.shape[0]
    tm = DENSE_TILE
    n_tiles = n // tm
    tiles_per_seq = n_tiles // bsz
    assert tiles_per_seq * bsz * tm == n
    dbsz, s_new, _ = z_s.shape
    wb = ck.shape[2]
    nb = pl.cdiv(dbsz, n_tiles)
    assert dbsz % nb == 0 and nb <= HG_DK
    n_sblk = dbsz // nb
    const = lambda s: (0, 0)
    single = pl.Buffered(1)
    seq_of = lambda s: jnp.minimum(s, n_tiles - 1) // tiles_per_seq
    srow3 = lambda s: (jnp.minimum(s, n_sblk - 1), 0, 0)
    srow4 = lambda s: (jnp.minimum(s, n_sblk - 1), 0, 0, 0)
    state_spec = pl.BlockSpec((nb, HG_HEADS, HG_DK, HG_DV), srow4)
    cache_spec = pl.BlockSpec((nb, SWA_KV_WIDTH, wb), srow3)
    return pl.pallas_call(
        functools.partial(_prompt_body, tiles_per_seq=tiles_per_seq, n_tiles=n_tiles),
        grid=(n_tiles + 1,),
        in_specs=[
            pl.BlockSpec(memory_space=pltpu.SMEM),
            pl.BlockSpec((tm, D_MODEL), lambda s: (jnp.minimum(s, n_tiles - 1), 0)),
            pl.BlockSpec((1, D_MODEL), const),
            pl.BlockSpec((D_MODEL, N_IN), const, pipeline_mode=single),
            pl.BlockSpec(lb_logits.shape, const),
            pl.BlockSpec((1, HG_DV), const),
            pl.BlockSpec((D_MODEL, D_MODEL), const, pipeline_mode=single),
            pl.BlockSpec((1, D_MODEL), const),
            pl.BlockSpec((D_MODEL, D_FF), const, pipeline_mode=single),
            pl.BlockSpec((D_FF, D_MODEL), const, pipeline_mode=single),
            pl.BlockSpec((1, D_MODEL), const),
            pl.BlockSpec((nb, s_new, N_IN), srow3),
            state_spec,
            cache_spec,
            cache_spec,
        ],
        out_specs=[
            pl.BlockSpec((tm, D_MODEL), lambda s: (jnp.maximum(s - 1, 0), 0)),
            pl.BlockSpec((1, HG_HEADS, HG_DK, HG_DV), lambda s: (seq_of(s), 0, 0, 0)),
            pl.BlockSpec((1, WINDOW, SWA_KV_WIDTH), lambda s: (seq_of(s), 0, 0)),
            pl.BlockSpec((1, WINDOW, SWA_KV_WIDTH), lambda s: (seq_of(s), 0, 0)),
            pl.BlockSpec((nb, s_new, D_MODEL), srow3),
            state_spec,
            cache_spec,
            cache_spec,
        ],
        out_shape=[
            jax.ShapeDtypeStruct((n, D_MODEL), F32),
            jax.ShapeDtypeStruct((bsz, HG_HEADS, HG_DK, HG_DV), F32),
            jax.ShapeDtypeStruct((bsz, WINDOW, SWA_KV_WIDTH), F32),
            jax.ShapeDtypeStruct((bsz, WINDOW, SWA_KV_WIDTH), F32),
            jax.ShapeDtypeStruct((dbsz, s_new, D_MODEL), F32),
            jax.ShapeDtypeStruct(s0.shape, s0.dtype),
            jax.ShapeDtypeStruct(ck.shape, ck.dtype),
            jax.ShapeDtypeStruct(cv.shape, cv.dtype),
        ],
        scratch_shapes=[
            pltpu.VMEM((tm, N_IN), F32),
            pltpu.VMEM((tm, D_MODEL), BF16),
            pltpu.VMEM((HG_HEADS, HG_DK, HG_DV), F32),
            pltpu.VMEM((WINDOW, 2 * SWA_KV_WIDTH), F32),
            pltpu.VMEM((tm, D_MODEL), F32),
            pltpu.VMEM((tm, D_MODEL), F32),
        ],
        compiler_params=pltpu.CompilerParams(
            dimension_semantics=("arbitrary",), vmem_limit_bytes=VMEM_LIMIT_BYTES),
        name="fused_layer",
    )(sinks, x2d, ln_mix, w_in, lb_logits, hg_norm, w_out, ln_mlp, w_up, w_down, ln_final,
      z_s, s0, ck, cv)


CAST_STEPS = 8


def _prep_body(x_ref, g_ref, win_ref, wo_ref, wu_ref, wd_ref,
               z_ref, win_o, wo_o, wu_o, wd_o, win_scr, *, n_in, n_tiles):
    j = pl.program_id(0)
    wo_o[...] = wo_ref[...].astype(BF16)
    wu_o[...] = wu_ref[...].astype(BF16)
    wd_o[...] = wd_ref[...].astype(BF16)

    @pl.when(j < n_in)
    def _cast_w_in():
        rows = win_ref.shape[0]
        w = win_ref[...].astype(BF16)
        win_o[...] = w
        win_scr[pl.ds(pl.multiple_of(j * rows, rows), rows), :] = w

    @pl.when((j >= n_in) & (j < n_in + n_tiles))
    def _project():
        xn = _rms(x_ref[...], g_ref[...]).astype(BF16)
        z_ref[...] = jnp.dot(xn, win_scr[...], preferred_element_type=F32)


def _prep(x2d, ln, w_in, w_out, w_up, w_down):
    n = x2d.shape[0]
    tm = DENSE_TILE
    nc, n_in, n_tiles = CAST_STEPS, 2, n // tm
    assert n_in + n_tiles <= nc and w_in.shape[0] % (8 * n_in) == 0
    assert all(w.shape[0] % (8 * nc) == 0 for w in (w_out, w_up, w_down))
    in_chunk = lambda s: (jnp.minimum(s, n_in - 1), 0)
    tile = lambda s: (jnp.clip(s - n_in, 0, n_tiles - 1), 0)
    w_in_spec = pl.BlockSpec((w_in.shape[0] // n_in, w_in.shape[1]), in_chunk)
    w_specs = [pl.BlockSpec((w.shape[0] // nc, w.shape[1]), lambda s: (s, 0)) for w in (w_out, w_up, w_down)]
    return pl.pallas_call(
        functools.partial(_prep_body, n_in=n_in, n_tiles=n_tiles),
        grid=(nc,),
        in_specs=[pl.BlockSpec((tm, D_MODEL), tile), pl.BlockSpec((1, D_MODEL), lambda s: (0, 0)), w_in_spec]
        + w_specs,
        out_specs=[pl.BlockSpec((tm, N_IN), tile), w_in_spec] + w_specs,
        out_shape=[jax.ShapeDtypeStruct((n, N_IN), F32)]
        + [jax.ShapeDtypeStruct(w.shape, BF16) for w in (w_in, w_out, w_up, w_down)],
        scratch_shapes=[pltpu.VMEM(w_in.shape, BF16)],
        compiler_params=pltpu.CompilerParams(
            dimension_semantics=("arbitrary",), vmem_limit_bytes=VMEM_LIMIT_BYTES),
        name="prep",
    )(x2d, ln, w_in, w_out, w_up, w_down)


def _out_mlp_body(x_ref, o_ref, wo_ref, lm_ref, wu_ref, wd_ref, lf_ref, y_ref):
    h = x_ref[...] + jnp.dot(o_ref[...].astype(BF16), wo_ref[...], preferred_element_type=F32)
    hn = _rms(h, lm_ref[...]).astype(BF16)
    u = jnp.dot(hn, wu_ref[...], preferred_element_type=F32)
    a = jnp.square(jnp.maximum(u, 0.0)).astype(BF16)
    y_ref[...] = _rms(h + jnp.dot(a, wd_ref[...], preferred_element_type=F32), lf_ref[...])


def _out_mlp(x2d, o2d, w_out, ln_mlp, w_up, w_down, ln_final):
    n = x2d.shape[0]
    tm = DENSE_TILE
    const = lambda i: (0, 0)
    single = pl.Buffered(1)
    return pl.pallas_call(
        _out_mlp_body,
        grid=(n // tm,),
        in_specs=[
            pl.BlockSpec((tm, D_MODEL), lambda i: (i, 0)),
            pl.BlockSpec((tm, D_MODEL), lambda i: (i, 0)),
            pl.BlockSpec((D_MODEL, D_MODEL), const, pipeline_mode=single),
            pl.BlockSpec((1, D_MODEL), const),
            pl.BlockSpec((D_MODEL, D_FF), const, pipeline_mode=single),
            pl.BlockSpec((D_FF, D_MODEL), const, pipeline_mode=single),
            pl.BlockSpec((1, D_MODEL), const),
        ],
        out_specs=pl.BlockSpec((tm, D_MODEL), lambda i: (i, 0)),
        out_shape=jax.ShapeDtypeStruct((n, D_MODEL), F32),
        compiler_params=pltpu.CompilerParams(
            dimension_semantics=("arbitrary",), vmem_limit_bytes=VMEM_LIMIT_BYTES),
        name="out_mlp",
    )(x2d, o2d, w_out, ln_mlp, w_up, w_down, ln_final)


def kernel(x_prompt, x_sample, state_hgrn, cache_swa_k, cache_swa_v, ln_mix, w_in, lb_logits,
           hg_norm, sinks, w_out, ln_mlp, w_up, w_down, ln_final):
    depth = w_in.shape[0]
    assert depth == 1 and lb_logits.shape[0] == depth + 1
    bsz, seq, _ = x_prompt.shape
    dbsz, dseq, _ = x_sample.shape
    assert seq % DENSE_TILE == 0 and seq >= WINDOW and (dbsz * dseq) % DENSE_TILE == 0

    ln_mix2 = ln_mix[0].reshape(1, D_MODEL)
    ln_mlp2 = ln_mlp[0].reshape(1, D_MODEL)
    ln_fin2 = ln_final.reshape(1, D_MODEL)
    gn2 = hg_norm[0].reshape(1, HG_DV)
    sink1 = sinks[0]

    def feature_major(c):
        return jnp.transpose(c, (0, 2, 3, 1)).reshape(c.shape[0], SWA_KV_WIDTH, c.shape[1])

    def position_major(c):
        return jnp.transpose(c.reshape(c.shape[0], SWA_KV_HEADS, SWA_HEAD_DIM, c.shape[2]), (0, 3, 1, 2))

    xs = x_sample.reshape(dbsz * dseq, D_MODEL)
    z_s, w_in_b, w_out_b, w_up_b, w_down_b = _prep(xs, ln_mix2, w_in[0], w_out[0], w_up[0], w_down[0])
    xp = x_prompt.reshape(bsz * seq, D_MODEL)
    y_p, s_p, k_p, v_p, o_s, s_s, ck_s, cv_s = _fused_layer(
        xp, bsz, sink1, ln_mix2, w_in_b, lb_logits, gn2, w_out_b, ln_mlp2, w_up_b, w_down_b, ln_fin2,
        z_s.reshape(dbsz, dseq, N_IN),
        state_hgrn[0], feature_major(cache_swa_k[0]), feature_major(cache_swa_v[0]))
    y_s = _out_mlp(xs, o_s.reshape(-1, D_MODEL), w_out_b, ln_mlp2,
                   w_up_b, w_down_b, ln_fin2)

    return (y_p.reshape(bsz, seq, D_MODEL),
            y_s.reshape(dbsz, dseq, D_MODEL),
            s_p[None],
            position_major(k_p)[None].astype(cache_swa_k.dtype),
            position_major(v_p)[None].astype(cache_swa_v.dtype),
            s_s[None],
            position_major(ck_s)[None],
            position_major(cv_s)[None])
```

```python
import functools

import jax
import jax.numpy as jnp
from jax import lax
from jax.experimental import pallas as pl
from jax.experimental.pallas import tpu as pltpu

F32 = jnp.float32
BF16 = jnp.bfloat16

D_MODEL = 1024
HG_WIDTH = 512
HG_HEADS = 4
HG_DK = 128
HG_DV = 128
SWA_WIDTH = 512
SWA_HEAD_DIM = 64
SWA_Q_HEADS = 8
SWA_KV_HEADS = 2
SWA_GROUP = SWA_Q_HEADS // SWA_KV_HEADS
SWA_KV_WIDTH = SWA_KV_HEADS * SWA_HEAD_DIM
WINDOW = 128
SWA_SCALE = SWA_HEAD_DIM ** -0.5
D_FF = 4 * D_MODEL
EPS = 1e-6
N_HG_IN = 4 * HG_WIDTH
N_SWA_IN = SWA_WIDTH + 2 * SWA_KV_WIDTH
N_IN = N_HG_IN + N_SWA_IN
KV_COLS = slice(N_HG_IN + SWA_WIDTH, N_IN)
NEG_BIG = -1e30

V7X_VMEM_BYTES = 64 * 1024 * 1024
VMEM_LIMIT_BYTES = V7X_VMEM_BYTES - 4 * 1024 * 1024
DENSE_TILE = 512
GLA_CHUNK = 64
FF_CHUNK = 2048
MATMUL_ORDER = "sS In sG U0 F0 F1 D0 sB B0 B1 F2 F3 U1 B2 B3 D1"


def _rms(x, g):
    return x * lax.rsqrt(jnp.mean(x * x, axis=-1, keepdims=True) + EPS) * g


def _sigmoid(x):
    return 1.0 / (1.0 + jnp.exp(-x))


def _dot(a, b):
    return jnp.dot(a.astype(BF16), b.astype(BF16), preferred_element_type=F32)


def _dot_nt(a, b):
    return lax.dot_general(a.astype(BF16), b.astype(BF16), (((1,), (1,)), ((), ())),
                           preferred_element_type=F32)


def _dot_tn(a, b):
    return lax.dot_general(a.astype(BF16), b.astype(BF16), (((0,), (0,)), ((), ())),
                           preferred_element_type=F32)


def _log2(n):
    assert n > 0 and n & (n - 1) == 0, n
    return n.bit_length() - 1


def _cumsum_rows(x):
    n = x.shape[0]
    row = lax.broadcasted_iota(jnp.int32, x.shape, 0)
    s = 1
    while s < n:
        x = x + jnp.where(row >= s, pltpu.roll(x, s, axis=0), 0.0)
        s *= 2
    return x


def _lower_bound(lb_logits):
    m = jnp.max(lb_logits, axis=0, keepdims=True)
    e = jnp.exp(lb_logits - m)
    return e[0:1, :] / jnp.sum(e, axis=0, keepdims=True)


def _gla_front(zh, lb):
    chunk = zh.shape[0]
    zq = zh[:, 0:HG_WIDTH]
    zf = zh[:, HG_WIDTH:2 * HG_WIDTH]
    v = zh[:, 2 * HG_WIDTH:3 * HG_WIDTH].astype(BF16)
    zg = zh[:, 3 * HG_WIDTH:4 * HG_WIDTH]

    q = zq * _sigmoid(zq)
    f = lb + (1.0 - lb) * _sigmoid(zf)
    k = 1.0 - f
    b = _cumsum_rows(jnp.log(f))
    mid = chunk // 2 - 1
    b_mid = b[mid:mid + 1, :]
    b_last = b[chunk - 1:chunk, :]
    qf = q * jnp.exp(b - b_mid)
    kf = k * jnp.exp(b_mid - b)
    qt = q * jnp.exp(b)
    ke = k * jnp.exp(b_last - b)
    dec = jnp.exp(b_last)
    gate = zg * _sigmoid(zg)
    heads = [slice(h * HG_DK, (h + 1) * HG_DK) for h in range(HG_HEADS)]
    scores = [_dot_nt(qf[:, sl], kf[:, sl]) for sl in heads]
    kv = [_dot_tn(ke[:, sl], v[:, sl]) for sl in heads]
    return scores, kv, qt.astype(BF16), v, dec, gate


def _decay_columns(dec_rows):
    pad = jnp.zeros((HG_DK - len(dec_rows), HG_WIDTH), F32)
    return jnp.concatenate(list(dec_rows) + [pad], axis=0).T


def _gla_back(front, g_norm, states, decays):
    scores, kv, qt, v, _, gate = front
    chunk = qt.shape[0]
    row = lax.broadcasted_iota(jnp.int32, (chunk, chunk), 0)
    col = lax.broadcasted_iota(jnp.int32, (chunk, chunk), 1)
    causal = row >= col
    outs, new_states = [], []
    for h in range(HG_HEADS):
        sl = slice(h * HG_DK, (h + 1) * HG_DK)
        a = jnp.where(causal, scores[h], 0.0)
        o = _dot(a, v[:, sl]) + _dot(qt[:, sl], states[h])
        new_states.append(states[h] * decays[h] + kv[h])
        outs.append(_rms(o, g_norm) * gate[:, sl])
    return jnp.concatenate(outs, axis=-1), new_states


def _stack_heads(x, kh):
    return jnp.concatenate(
        [x[:, (kh * SWA_GROUP + g) * SWA_HEAD_DIM:(kh * SWA_GROUP + g + 1) * SWA_HEAD_DIM]
         for g in range(SWA_GROUP)], axis=0)


def _sink_column(sink_ref, kh, rows_per_head):
    r = lax.broadcasted_iota(jnp.int32, (SWA_GROUP * rows_per_head, 1), 0)
    col = jnp.full(r.shape, sink_ref[kh * SWA_GROUP], F32)
    for g in range(1, SWA_GROUP):
        col = jnp.where(r >= g * rows_per_head, sink_ref[kh * SWA_GROUP + g], col)
    return col


def _swa_front(zq, kv_cur, kv_prev):
    scores = []
    for kh in range(SWA_KV_HEADS):
        ks = slice(kh * SWA_HEAD_DIM, (kh + 1) * SWA_HEAD_DIM)
        q = (_stack_heads(zq, kh) * SWA_SCALE).astype(BF16)
        scores.append((_dot_nt(q, kv_prev[:, ks]), _dot_nt(q, kv_cur[:, ks])))
    return scores


def _swa_back(scores, kv_cur, kv_prev, sink_ref, no_prev):
    rows = SWA_GROUP * WINDOW
    i = lax.broadcasted_iota(jnp.int32, (rows, WINDOW), 0) & (WINDOW - 1)
    j = lax.broadcasted_iota(jnp.int32, (rows, WINDOW), 1)
    use_prev = j > i
    outs = []
    for kh in range(SWA_KV_HEADS):
        vs = slice(SWA_KV_WIDTH + kh * SWA_HEAD_DIM, SWA_KV_WIDTH + (kh + 1) * SWA_HEAD_DIM)
        sink = _sink_column(sink_ref, kh, WINDOW)
        s_prev, s_cur = scores[kh]
        if no_prev is not None:
            s_prev = jnp.where(no_prev, NEG_BIG, s_prev)
        s = jnp.where(use_prev, s_prev, s_cur)
        m = jnp.maximum(jnp.max(s, axis=-1, keepdims=True), sink)
        p = jnp.exp(s - m)
        den = jnp.sum(p, axis=-1, keepdims=True) + jnp.exp(sink - m)
        o = (_dot(jnp.where(use_prev, p, 0.0), kv_prev[:, vs])
             + _dot(jnp.where(use_prev, 0.0, p), kv_cur[:, vs])) / den
        outs.extend(o[g * WINDOW:(g + 1) * WINDOW] for g in range(SWA_GROUP))
    return jnp.concatenate(outs, axis=-1)


def _swa_sample_front(z_ref, ck_ref, cv_ref, cko_ref, cvo_ref, nb, s_new, wb):
    nq = nb * s_new
    z = z_ref[:, :, N_HG_IN:].reshape(nq, N_SWA_IN)
    k_new = z[:, SWA_WIDTH:SWA_WIDTH + SWA_KV_WIDTH]
    v_new = z[:, SWA_WIDTH + SWA_KV_WIDTH:]
    assert nq <= SWA_KV_WIDTH and wb == SWA_KV_WIDTH
    pad = jnp.zeros((SWA_KV_WIDTH - nq, SWA_KV_WIDTH), F32)
    lane = lax.broadcasted_iota(jnp.int32, (SWA_KV_WIDTH, wb), 1)
    for new, c_ref, co_ref in ((k_new, ck_ref, cko_ref), (v_new, cv_ref, cvo_ref)):
        new_t = jnp.concatenate([new, pad], axis=0).T
        for b in range(nb):
            kept = pltpu.roll(c_ref[b], wb - s_new, axis=1)
            fresh = pltpu.roll(new_t, (wb - s_new - b * s_new) % wb, axis=1)
            co_ref[b] = jnp.where(lane >= wb - s_new, fresh, kept)
    scores = []
    for kh in range(SWA_KV_HEADS):
        cs = slice(kh * SWA_HEAD_DIM, (kh + 1) * SWA_HEAD_DIM)
        kt = jnp.concatenate([ck_ref[b, cs, :] for b in range(nb)], axis=1)
        q = (_stack_heads(z, kh) * SWA_SCALE).astype(BF16)
        scores.append((_dot(q, kt), _dot_nt(q, k_new[:, cs])))
    return scores, v_new.astype(BF16)


def _swa_sample_back(front, sink_ref, cv_ref, o_ref, nb, s_new, wb):
    scores, v_new = front
    nq = nb * s_new
    rows = SWA_GROUP * nq
    ls, lw = _log2(s_new), _log2(wb)
    _log2(nb)
    r = lax.broadcasted_iota(jnp.int32, (rows, nb * wb), 0)
    c = lax.broadcasted_iota(jnp.int32, (rows, nb * wb), 1)
    mask_c = (((c >> lw) == ((r >> ls) & (nb - 1)))
              & ((c & (wb - 1)) > (r & (s_new - 1)) + (wb - WINDOW)))
    r = lax.broadcasted_iota(jnp.int32, (rows, nq), 0)
    c = lax.broadcasted_iota(jnp.int32, (rows, nq), 1)
    mask_n = ((c >> ls) == ((r >> ls) & (nb - 1))) & ((c & (s_new - 1)) <= (r & (s_new - 1)))
    for kh in range(SWA_KV_HEADS):
        cs = slice(kh * SWA_HEAD_DIM, (kh + 1) * SWA_HEAD_DIM)
        vt = jnp.concatenate([cv_ref[b, cs, :] for b in range(nb)], axis=1)
        sink = _sink_column(sink_ref, kh, nq)
        sc = jnp.where(mask_c, scores[kh][0], NEG_BIG)
        sn = jnp.where(mask_n, scores[kh][1], NEG_BIG)
        m = jnp.maximum(jnp.maximum(jnp.max(sc, axis=-1, keepdims=True),
                                    jnp.max(sn, axis=-1, keepdims=True)), sink)
        pc = jnp.where(mask_c, jnp.exp(sc - m), 0.0)
        pn = jnp.where(mask_n, jnp.exp(sn - m), 0.0)
        den = (jnp.sum(pc, axis=-1, keepdims=True) + jnp.sum(pn, axis=-1, keepdims=True)
               + jnp.exp(sink - m))
        o = (_dot_nt(pc, vt) + _dot(pn, v_new[:, cs])) / den
        for g in range(SWA_GROUP):
            c0 = HG_WIDTH + (kh * SWA_GROUP + g) * SWA_HEAD_DIM
            o_ref[:, :, c0:c0 + SWA_HEAD_DIM] = o[g * nq:(g + 1) * nq].reshape(nb, s_new, SWA_HEAD_DIM)


def _sample_front_swa(refs):
    z_ref, _, ck_ref, cv_ref, _, _, cko_ref, cvo_ref = refs
    nb, s_new, _ = z_ref.shape
    return _swa_sample_front(z_ref, ck_ref, cv_ref, cko_ref, cvo_ref, nb, s_new, ck_ref.shape[2])


def _sample_front_gla(refs, lb):
    z_ref = refs[0]
    return [_gla_front(z_ref[j, :, 0:N_HG_IN], lb) for j in range(z_ref.shape[0])]


def _sample_back(front, refs, sink_ref, g_norm):
    z_ref, s0_ref, ck_ref, cv_ref, o_ref, so_ref, _, _ = refs
    nb, s_new, _ = z_ref.shape
    gla, swa = front
    dec_cols = _decay_columns([f[4] for f in gla])
    for j in range(nb):
        o, new_states = _gla_back(
            gla[j], g_norm, [s0_ref[j, h] for h in range(HG_HEADS)],
            [dec_cols[h * HG_DK:(h + 1) * HG_DK, j:j + 1] for h in range(HG_HEADS)])
        o_ref[j, :, 0:HG_WIDTH] = o
        for h in range(HG_HEADS):
            so_ref[j, h] = new_states[h]
    _swa_sample_back(swa, sink_ref, cv_ref, o_ref, nb, s_new, ck_ref.shape[2])


def _prompt_body(sink_ref, xc_ref, lmix_ref, win_ref, lbl_ref, gn_ref, wo_ref, lmlp_ref,
                 wu_ref, wd_ref, lfin_ref, zs_ref, s0_ref, cks_ref, cvs_ref,
                 y_ref, so_ref, ko_ref, vo_ref, os_ref, sso_ref, ckso_ref, cvso_ref,
                 z_ref, o_ref, st_ref, kvp_ref, h_ref, acc_ref, *, tiles_per_seq, n_tiles):
    sample_refs = (zs_ref, s0_ref, cks_ref, cvs_ref, os_ref, sso_ref, ckso_ref, cvso_ref)
    s = pl.program_id(0)
    pos = s % tiles_per_seq
    seq_start = pos == 0
    tm = DENSE_TILE

    @pl.when(seq_start)
    def _reset():
        st_ref[...] = jnp.zeros(st_ref.shape, st_ref.dtype)
        kvp_ref[...] = jnp.zeros(kvp_ref.shape, kvp_ref.dtype)

    n_ff = D_FF // FF_CHUNK
    n_blk = tm // WINDOW
    chunks_per_blk = WINDOW // GLA_CHUNK
    assert n_ff == 2 and n_blk == 4, "MATMUL_ORDER is written for 2 MLP slices and 4 mixer slices"

    def run(do_mlp, do_mix):
        if do_mlp:
            hn = _rms(h_ref[...], lmlp_ref[...]).astype(BF16)
        if do_mix:
            lb = _lower_bound(lbl_ref[...])
            g_norm = gn_ref[...]

        def mlp_up(c):
            cs = slice(c * FF_CHUNK, (c + 1) * FF_CHUNK)
            u = jnp.dot(hn, wu_ref[:, cs], preferred_element_type=F32)
            return jnp.square(jnp.maximum(u, 0.0)).astype(BF16)

        def mlp_down(c, a):
            cs = slice(c * FF_CHUNK, (c + 1) * FF_CHUNK)
            base = h_ref if c == 0 else acc_ref
            acc_ref[...] = base[...] + jnp.dot(a, wd_ref[cs, :], preferred_element_type=F32)

        def kv_blocks(n):
            kv_prev = kvp_ref[...] if n == 0 else z_ref[(n - 1) * WINDOW:n * WINDOW, KV_COLS]
            return z_ref[n * WINDOW:(n + 1) * WINDOW, KV_COLS], kv_prev

        def mix_front(n):
            gla = [_gla_front(z_ref[c * GLA_CHUNK:(c + 1) * GLA_CHUNK, 0:N_HG_IN], lb)
                   for c in range(n * chunks_per_blk, (n + 1) * chunks_per_blk)]
            kv_cur, kv_prev = kv_blocks(n)
            swa = _swa_front(z_ref[n * WINDOW:(n + 1) * WINDOW, N_HG_IN:N_HG_IN + SWA_WIDTH],
                             kv_cur, kv_prev)
            return gla, swa

        def mix_back(n, front):
            gla, swa = front
            dec_cols = _decay_columns([g[4] for g in gla])
            for i, c in enumerate(range(n * chunks_per_blk, (n + 1) * chunks_per_blk)):
                o, new_states = _gla_back(
                    gla[i], g_norm, [st_ref[h] for h in range(HG_HEADS)],
                    [dec_cols[h * HG_DK:(h + 1) * HG_DK, i:i + 1] for h in range(HG_HEADS)])
                for h in range(HG_HEADS):
                    st_ref[h] = new_states[h]
                o_ref[c * GLA_CHUNK:(c + 1) * GLA_CHUNK, 0:HG_WIDTH] = o.astype(BF16)
            kv_cur, kv_prev = kv_blocks(n)
            o_ref[n * WINDOW:(n + 1) * WINDOW, HG_WIDTH:] = _swa_back(
                swa, kv_cur, kv_prev, sink_ref, seq_start if n == 0 else None).astype(BF16)

        acts, fronts = {}, {}
        for step in MATMUL_ORDER.split():
            kind, i = step[0], step[1]
            if kind in "UD":
                if not do_mlp:
                    continue
                if kind == "U":
                    acts[i] = mlp_up(int(i))
                else:
                    mlp_down(int(i), acts.pop(i))
            elif not do_mix:
                continue
            elif step == "In":
                xn = _rms(xc_ref[...], lmix_ref[...]).astype(BF16)
                z_ref[...] = jnp.dot(xn, win_ref[...], preferred_element_type=F32)
            elif kind == "F":
                fronts[i] = mix_front(int(i))
            elif kind == "B":
                mix_back(int(i), fronts.pop(i))
            elif step == "sS":
                fronts[step] = _sample_front_swa(sample_refs)
            elif step == "sG":
                fronts[step] = _sample_front_gla(sample_refs, lb)
            else:
                assert step == "sB", step
                _sample_back((fronts.pop("sG"), fronts.pop("sS")), sample_refs, sink_ref, g_norm)
        assert not acts and not fronts
        if do_mix:
            kvp_ref[...] = z_ref[tm - WINDOW:tm, KV_COLS]
            h_ref[...] = xc_ref[...] + jnp.dot(o_ref[...], wo_ref[...], preferred_element_type=F32)
        if do_mlp:
            y_ref[...] = _rms(acc_ref[...], lfin_ref[...])

    pl.when((s > 0) & (s < n_tiles))(functools.partial(run, True, True))
    pl.when(s == 0)(functools.partial(run, False, True))
    pl.when(s == n_tiles)(functools.partial(run, True, False))

    @pl.when((pos == tiles_per_seq - 1) & (s < n_tiles))
    def _seq_end():
        for h in range(HG_HEADS):
            so_ref[0, h] = st_ref[h]
        ko_ref[0] = z_ref[tm - WINDOW:tm, N_HG_IN + SWA_WIDTH:N_HG_IN + SWA_WIDTH + SWA_KV_WIDTH].T
        vo_ref[0] = z_ref[tm - WINDOW:tm, N_HG_IN + SWA_WIDTH + SWA_KV_WIDTH:N_IN].T


def _fused_layer(x2d, bsz, sinks, ln_mix, w_in, lb_logits, hg_norm, w_out, ln_mlp, w_up, w_down,
                 ln_final, z_s, s0, ck, cv):
    n = x2d.shape[0]
    tm = DENSE_TILE
    n_tiles = n // tm
    tiles_per_seq = n_tiles // bsz
    assert tiles_per_seq * bsz * tm == n
    dbsz, s_new, _ = z_s.shape
    wb = ck.shape[2]
    nb = pl.cdiv(dbsz, n_tiles)
    assert dbsz % nb == 0 and nb <= HG_DK
    n_sblk = dbsz // nb
    const = lambda s: (0, 0)
    single = pl.Buffered(1)
    seq_of = lambda s: jnp.minimum(s, n_tiles - 1) // tiles_per_seq
    srow3 = lambda s: (jnp.minimum(s, n_sblk - 1), 0, 0)
    srow4 = lambda s: (jnp.minimum(s, n_sblk - 1), 0, 0, 0)
    state_spec = pl.BlockSpec((nb, HG_HEADS, HG_DK, HG_DV), srow4)
    cache_spec = pl.BlockSpec((nb, SWA_KV_WIDTH, wb), srow3)
    return pl.pallas_call(
        functools.partial(_prompt_body, tiles_per_seq=tiles_per_seq, n_tiles=n_tiles),
        grid=(n_tiles + 1,),
        in_specs=[
            pl.BlockSpec(memory_space=pltpu.SMEM),
            pl.BlockSpec((tm, D_MODEL), lambda s: (jnp.minimum(s, n_tiles - 1), 0)),
            pl.BlockSpec((1, D_MODEL), const),
            pl.BlockSpec((D_MODEL, N_IN), const, pipeline_mode=single),
            pl.BlockSpec(lb_logits.shape, const),
            pl.BlockSpec((1, HG_DV), const),
            pl.BlockSpec((D_MODEL, D_MODEL), const, pipeline_mode=single),
            pl.BlockSpec((1, D_MODEL), const),
            pl.BlockSpec((D_MODEL, D_FF), const, pipeline_mode=single),
            pl.BlockSpec((D_FF, D_MODEL), const, pipeline_mode=single),
            pl.BlockSpec((1, D_MODEL), const),
            pl.BlockSpec((nb, s_new, N_IN), srow3),
            state_spec,
            cache_spec,
            cache_spec,
        ],
        out_specs=[
            pl.BlockSpec((tm, D_MODEL), lambda s: (jnp.maximum(s - 1, 0), 0)),
            pl.BlockSpec((1, HG_HEADS, HG_DK, HG_DV), lambda s: (seq_of(s), 0, 0, 0)),
            pl.BlockSpec((1, WINDOW, SWA_KV_WIDTH), lambda s: (seq_of(s), 0, 0)),
            pl.BlockSpec((1, WINDOW, SWA_KV_WIDTH), lambda s: (seq_of(s), 0, 0)),
            pl.BlockSpec((nb, s_new, D_MODEL), srow3),
            state_spec,
            cache_spec,
            cache_spec,
        ],
        out_shape=[
            jax.ShapeDtypeStruct((n, D_MODEL), F32),
            jax.ShapeDtypeStruct((bsz, HG_HEADS, HG_DK, HG_DV), F32),
            jax.ShapeDtypeStruct((bsz, WINDOW, SWA_KV_WIDTH), F32),
            jax.ShapeDtypeStruct((bsz, WINDOW, SWA_KV_WIDTH), F32),
            jax.ShapeDtypeStruct((dbsz, s_new, D_MODEL), F32),
            jax.ShapeDtypeStruct(s0.shape, s0.dtype),
            jax.ShapeDtypeStruct(ck.shape, ck.dtype),
            jax.ShapeDtypeStruct(cv.shape, cv.dtype),
        ],
        scratch_shapes=[
            pltpu.VMEM((tm, N_IN), F32),
            pltpu.VMEM((tm, D_MODEL), BF16),
            pltpu.VMEM((HG_HEADS, HG_DK, HG_DV), F32),
            pltpu.VMEM((WINDOW, 2 * SWA_KV_WIDTH), F32),
            pltpu.VMEM((tm, D_MODEL), F32),
            pltpu.VMEM((tm, D_MODEL), F32),
        ],
        compiler_params=pltpu.CompilerParams(
            dimension_semantics=("arbitrary",), vmem_limit_bytes=VMEM_LIMIT_BYTES),
        name="fused_layer",
    )(sinks, x2d, ln_mix, w_in, lb_logits, hg_norm, w_out, ln_mlp, w_up, w_down, ln_final,
      z_s, s0, ck, cv)


CAST_STEPS = 8


def _prep_body(x_ref, g_ref, win_ref, wo_ref, wu_ref, wd_ref,
               z_ref, win_o, wo_o, wu_o, wd_o, win_scr, *, n_in, n_tiles):
    j = pl.program_id(0)
    wo_o[...] = wo_ref[...].astype(BF16)
    wu_o[...] = wu_ref[...].astype(BF16)
    wd_o[...] = wd_ref[...].astype(BF16)

    @pl.when(j < n_in)
    def _cast_w_in():
        rows = win_ref.shape[0]
        w = win_ref[...].astype(BF16)
        win_o[...] = w
        win_scr[pl.ds(pl.multiple_of(j * rows, rows), rows), :] = w

    @pl.when((j >= n_in) & (j < n_in + n_tiles))
    def _project():
        xn = _rms(x_ref[...], g_ref[...]).astype(BF16)
        z_ref[...] = jnp.dot(xn, win_scr[...], preferred_element_type=F32)


def _prep(x2d, ln, w_in, w_out, w_up, w_down):
    n = x2d.shape[0]
    tm = DENSE_TILE
    nc, n_in, n_tiles = CAST_STEPS, 2, n // tm
    assert n_in + n_tiles <= nc and w_in.shape[0] % (8 * n_in) == 0
    assert all(w.shape[0] % (8 * nc) == 0 for w in (w_out, w_up, w_down))
    in_chunk = lambda s: (jnp.minimum(s, n_in - 1), 0)
    tile = lambda s: (jnp.clip(s - n_in, 0, n_tiles - 1), 0)
    w_in_spec = pl.BlockSpec((w_in.shape[0] // n_in, w_in.shape[1]), in_chunk)
    w_specs = [pl.BlockSpec((w.shape[0] // nc, w.shape[1]), lambda s: (s, 0)) for w in (w_out, w_up, w_down)]
    return pl.pallas_call(
        functools.partial(_prep_body, n_in=n_in, n_tiles=n_tiles),
        grid=(nc,),
        in_specs=[pl.BlockSpec((tm, D_MODEL), tile), pl.BlockSpec((1, D_MODEL), lambda s: (0, 0)), w_in_spec]
        + w_specs,
        out_specs=[pl.BlockSpec((tm, N_IN), tile), w_in_spec] + w_specs,
        out_shape=[jax.ShapeDtypeStruct((n, N_IN), F32)]
        + [jax.ShapeDtypeStruct(w.shape, BF16) for w in (w_in, w_out, w_up, w_down)],
        scratch_shapes=[pltpu.VMEM(w_in.shape, BF16)],
        compiler_params=pltpu.CompilerParams(
            dimension_semantics=("arbitrary",), vmem_limit_bytes=VMEM_LIMIT_BYTES),
        name="prep",
    )(x2d, ln, w_in, w_out, w_up, w_down)


def _out_mlp_body(x_ref, o_ref, wo_ref, lm_ref, wu_ref, wd_ref, lf_ref, y_ref):
    h = x_ref[...] + jnp.dot(o_ref[...].astype(BF16), wo_ref[...], preferred_element_type=F32)
    hn = _rms(h, lm_ref[...]).astype(BF16)
    u = jnp.dot(hn, wu_ref[...], preferred_element_type=F32)
    a = jnp.square(jnp.maximum(u, 0.0)).astype(BF16)
    y_ref[...] = _rms(h + jnp.dot(a, wd_ref[...], preferred_element_type=F32), lf_ref[...])


def _out_mlp(x2d, o2d, w_out, ln_mlp, w_up, w_down, ln_final):
    n = x2d.shape[0]
    tm = DENSE_TILE
    const = lambda i: (0, 0)
    single = pl.Buffered(1)
    return pl.pallas_call(
        _out_mlp_body,
        grid=(n // tm,),
        in_specs=[
            pl.BlockSpec((tm, D_MODEL), lambda i: (i, 0)),
            pl.BlockSpec((tm, D_MODEL), lambda i: (i, 0)),
            pl.BlockSpec((D_MODEL, D_MODEL), const, pipeline_mode=single),
            pl.BlockSpec((1, D_MODEL), const),
            pl.BlockSpec((D_MODEL, D_FF), const, pipeline_mode=single),
            pl.BlockSpec((D_FF, D_MODEL), const, pipeline_mode=single),
            pl.BlockSpec((1, D_MODEL), const),
        ],
        out_specs=pl.BlockSpec((tm, D_MODEL), lambda i: (i, 0)),
        out_shape=jax.ShapeDtypeStruct((n, D_MODEL), F32),
        compiler_params=pltpu.CompilerParams(
            dimension_semantics=("arbitrary",), vmem_limit_bytes=VMEM_LIMIT_BYTES),
        name="out_mlp",
    )(x2d, o2d, w_out, ln_mlp, w_up, w_down, ln_final)


def kernel(x_prompt, x_sample, state_hgrn, cache_swa_k, cache_swa_v, ln_mix, w_in, lb_logits,
           hg_norm, sinks, w_out, ln_mlp, w_up, w_down, ln_final):
    depth = w_in.shape[0]
    assert depth == 1 and lb_logits.shape[0] == depth + 1
    bsz, seq, _ = x_prompt.shape
    dbsz, dseq, _ = x_sample.shape
    assert seq % DENSE_TILE == 0 and seq >= WINDOW and (dbsz * dseq) % DENSE_TILE == 0

    ln_mix2 = ln_mix[0].reshape(1, D_MODEL)
    ln_mlp2 = ln_mlp[0].reshape(1, D_MODEL)
    ln_fin2 = ln_final.reshape(1, D_MODEL)
    gn2 = hg_norm[0].reshape(1, HG_DV)
    sink1 = sinks[0]

    def feature_major(c):
        return jnp.transpose(c, (0, 2, 3, 1)).reshape(c.shape[0], SWA_KV_WIDTH, c.shape[1])

    def position_major(c):
        return jnp.transpose(c.reshape(c.shape[0], SWA_KV_HEADS, SWA_HEAD_DIM, c.shape[2]), (0, 3, 1, 2))

    xs = x_sample.reshape(dbsz * dseq, D_MODEL)
    z_s, w_in_b, w_out_b, w_up_b, w_down_b = _prep(xs, ln_mix2, w_in[0], w_out[0], w_up[0], w_down[0])
    xp = x_prompt.reshape(bsz * seq, D_MODEL)
    y_p, s_p, k_p, v_p, o_s, s_s, ck_s, cv_s = _fused_layer(
        xp, bsz, sink1, ln_mix2, w_in_b, lb_logits, gn2, w_out_b, ln_mlp2, w_up_b, w_down_b, ln_fin2,
        z_s.reshape(dbsz, dseq, N_IN),
        state_hgrn[0], feature_major(cache_swa_k[0]), feature_major(cache_swa_v[0]))
    y_s = _out_mlp(xs, o_s.reshape(-1, D_MODEL), w_out_b, ln_mlp2,
                   w_up_b, w_down_b, ln_fin2)

    return (y_p.reshape(bsz, seq, D_MODEL),
            y_s.reshape(dbsz, dseq, D_MODEL),
            s_p[None],
            position_major(k_p)[None].astype(cache_swa_k.dtype),
            position_major(v_p)[None].astype(cache_swa_v.dtype),
            s_s[None],
            position_major(ck_s)[None],
            position_major(cv_s)[None])
```

```python
import functools

import jax
import jax.numpy as jnp
from jax import lax
from jax.experimental import pallas as pl
from jax.experimental.pallas import tpu as pltpu

F32 = jnp.float32
BF16 = jnp.bfloat16

D_MODEL = 1024
HG_WIDTH = 512
HG_HEADS = 4
HG_DK = 128
HG_DV = 128
SWA_WIDTH = 512
SWA_HEAD_DIM = 64
SWA_Q_HEADS = 8
SWA_KV_HEADS = 2
SWA_GROUP = SWA_Q_HEADS // SWA_KV_HEADS
SWA_KV_WIDTH = SWA_KV_HEADS * SWA_HEAD_DIM
WINDOW = 128
SWA_SCALE = SWA_HEAD_DIM ** -0.5
D_FF = 4 * D_MODEL
EPS = 1e-6
N_HG_IN = 4 * HG_WIDTH
N_SWA_IN = SWA_WIDTH + 2 * SWA_KV_WIDTH
N_IN = N_HG_IN + N_SWA_IN
KV_COLS = slice(N_HG_IN + SWA_WIDTH, N_IN)
NEG_BIG = -1e30

V7X_VMEM_BYTES = 64 * 1024 * 1024
VMEM_LIMIT_BYTES = V7X_VMEM_BYTES - 4 * 1024 * 1024
DENSE_TILE = 512
GLA_CHUNK = 64
FF_CHUNK = 2048
MATMUL_ORDER = "sS In sG U0 F0 F1 D0 B0 B1 F2 F3 U1 B2 B3 D1 Out sB"


def _rms(x, g):
    return x * lax.rsqrt(jnp.mean(x * x, axis=-1, keepdims=True) + EPS) * g


def _sigmoid(x):
    return 1.0 / (1.0 + jnp.exp(-x))


def _dot(a, b):
    return jnp.dot(a.astype(BF16), b.astype(BF16), preferred_element_type=F32)


def _dot_nt(a, b):
    return lax.dot_general(a.astype(BF16), b.astype(BF16), (((1,), (1,)), ((), ())),
                           preferred_element_type=F32)


def _dot_tn(a, b):
    return lax.dot_general(a.astype(BF16), b.astype(BF16), (((0,), (0,)), ((), ())),
                           preferred_element_type=F32)


def _log2(n):
    assert n > 0 and n & (n - 1) == 0, n
    return n.bit_length() - 1


def _cumsum_rows(x):
    n = x.shape[0]
    row = lax.broadcasted_iota(jnp.int32, x.shape, 0)
    s = 1
    while s < n:
        x = x + jnp.where(row >= s, pltpu.roll(x, s, axis=0), 0.0)
        s *= 2
    return x


def _lower_bound(lb_logits):
    m = jnp.max(lb_logits, axis=0, keepdims=True)
    e = jnp.exp(lb_logits - m)
    return e[0:1, :] / jnp.sum(e, axis=0, keepdims=True)


def _gla_front(zh, lb):
    chunk = zh.shape[0]
    zq = zh[:, 0:HG_WIDTH]
    zf = zh[:, HG_WIDTH:2 * HG_WIDTH]
    v = zh[:, 2 * HG_WIDTH:3 * HG_WIDTH].astype(BF16)
    zg = zh[:, 3 * HG_WIDTH:4 * HG_WIDTH]

    q = zq * _sigmoid(zq)
    f = lb + (1.0 - lb) * _sigmoid(zf)
    k = 1.0 - f
    b = _cumsum_rows(jnp.log(f))
    mid = chunk // 2 - 1
    b_mid = b[mid:mid + 1, :]
    b_last = b[chunk - 1:chunk, :]
    qf = q * jnp.exp(b - b_mid)
    kf = k * jnp.exp(b_mid - b)
    qt = q * jnp.exp(b)
    ke = k * jnp.exp(b_last - b)
    dec = jnp.exp(b_last)
    gate = zg * _sigmoid(zg)
    heads = [slice(h * HG_DK, (h + 1) * HG_DK) for h in range(HG_HEADS)]
    scores = [_dot_nt(qf[:, sl], kf[:, sl]) for sl in heads]
    kv = [_dot_tn(ke[:, sl], v[:, sl]) for sl in heads]
    return scores, kv, qt.astype(BF16), v, dec, gate


def _decay_columns(dec_rows):
    pad = jnp.zeros((HG_DK - len(dec_rows), HG_WIDTH), F32)
    return jnp.concatenate(list(dec_rows) + [pad], axis=0).T


def _gla_back(front, g_norm, states, decays):
    scores, kv, qt, v, _, gate = front
    chunk = qt.shape[0]
    row = lax.broadcasted_iota(jnp.int32, (chunk, chunk), 0)
    col = lax.broadcasted_iota(jnp.int32, (chunk, chunk), 1)
    causal = row >= col
    outs, new_states = [], []
    for h in range(HG_HEADS):
        sl = slice(h * HG_DK, (h + 1) * HG_DK)
        a = jnp.where(causal, scores[h], 0.0)
        o = _dot(a, v[:, sl]) + _dot(qt[:, sl], states[h])
        new_states.append(states[h] * decays[h] + kv[h])
        outs.append(_rms(o, g_norm) * gate[:, sl])
    return jnp.concatenate(outs, axis=-1), new_states


def _stack_heads(x, kh):
    return jnp.concatenate(
        [x[:, (kh * SWA_GROUP + g) * SWA_HEAD_DIM:(kh * SWA_GROUP + g + 1) * SWA_HEAD_DIM]
         for g in range(SWA_GROUP)], axis=0)


def _sink_column(sink_ref, kh, rows_per_head):
    r = lax.broadcasted_iota(jnp.int32, (SWA_GROUP * rows_per_head, 1), 0)
    col = jnp.full(r.shape, sink_ref[kh * SWA_GROUP], F32)
    for g in range(1, SWA_GROUP):
        col = jnp.where(r >= g * rows_per_head, sink_ref[kh * SWA_GROUP + g], col)
    return col


def _swa_front(zq, kv_cur, kv_prev):
    scores = []
    for kh in range(SWA_KV_HEADS):
        ks = slice(kh * SWA_HEAD_DIM, (kh + 1) * SWA_HEAD_DIM)
        q = (_stack_heads(zq, kh) * SWA_SCALE).astype(BF16)
        scores.append((_dot_nt(q, kv_prev[:, ks]), _dot_nt(q, kv_cur[:, ks])))
    return scores


def _swa_back(scores, kv_cur, kv_prev, sink_ref, no_prev):
    rows = SWA_GROUP * WINDOW
    i = lax.broadcasted_iota(jnp.int32, (rows, WINDOW), 0) & (WINDOW - 1)
    j = lax.broadcasted_iota(jnp.int32, (rows, WINDOW), 1)
    use_prev = j > i
    outs = []
    for kh in range(SWA_KV_HEADS):
        vs = slice(SWA_KV_WIDTH + kh * SWA_HEAD_DIM, SWA_KV_WIDTH + (kh + 1) * SWA_HEAD_DIM)
        sink = _sink_column(sink_ref, kh, WINDOW)
        s_prev, s_cur = scores[kh]
        if no_prev is not None:
            s_prev = jnp.where(no_prev, NEG_BIG, s_prev)
        s = jnp.where(use_prev, s_prev, s_cur)
        m = jnp.maximum(jnp.max(s, axis=-1, keepdims=True), sink)
        p = jnp.exp(s - m)
        den = jnp.sum(p, axis=-1, keepdims=True) + jnp.exp(sink - m)
        o = (_dot(jnp.where(use_prev, p, 0.0), kv_prev[:, vs])
             + _dot(jnp.where(use_prev, 0.0, p), kv_cur[:, vs])) / den
        outs.extend(o[g * WINDOW:(g + 1) * WINDOW] for g in range(SWA_GROUP))
    return jnp.concatenate(outs, axis=-1)


def _swa_sample_front(z_ref, ck_ref, cv_ref, cko_ref, cvo_ref, nb, s_new, wb):
    nq = nb * s_new
    z = z_ref[:, :, N_HG_IN:].reshape(nq, N_SWA_IN)
    k_new = z[:, SWA_WIDTH:SWA_WIDTH + SWA_KV_WIDTH]
    v_new = z[:, SWA_WIDTH + SWA_KV_WIDTH:]
    assert nq <= SWA_KV_WIDTH and wb == SWA_KV_WIDTH
    pad = jnp.zeros((SWA_KV_WIDTH - nq, SWA_KV_WIDTH), F32)
    lane = lax.broadcasted_iota(jnp.int32, (SWA_KV_WIDTH, wb), 1)
    for new, c_ref, co_ref in ((k_new, ck_ref, cko_ref), (v_new, cv_ref, cvo_ref)):
        new_t = jnp.concatenate([new, pad], axis=0).T
        for b in range(nb):
            kept = pltpu.roll(c_ref[b], wb - s_new, axis=1)
            fresh = pltpu.roll(new_t, (wb - s_new - b * s_new) % wb, axis=1)
            co_ref[b] = jnp.where(lane >= wb - s_new, fresh, kept)
    scores = []
    for kh in range(SWA_KV_HEADS):
        cs = slice(kh * SWA_HEAD_DIM, (kh + 1) * SWA_HEAD_DIM)
        kt = jnp.concatenate([ck_ref[b, cs, :] for b in range(nb)], axis=1)
        q = (_stack_heads(z, kh) * SWA_SCALE).astype(BF16)
        scores.append((_dot(q, kt), _dot_nt(q, k_new[:, cs])))
    return scores, v_new.astype(BF16)


def _swa_sample_back(front, sink_ref, cv_ref, o_ref, nb, s_new, wb):
    scores, v_new = front
    nq = nb * s_new
    rows = SWA_GROUP * nq
    ls, lw = _log2(s_new), _log2(wb)
    _log2(nb)
    r = lax.broadcasted_iota(jnp.int32, (rows, nb * wb), 0)
    c = lax.broadcasted_iota(jnp.int32, (rows, nb * wb), 1)
    mask_c = (((c >> lw) == ((r >> ls) & (nb - 1)))
              & ((c & (wb - 1)) > (r & (s_new - 1)) + (wb - WINDOW)))
    r = lax.broadcasted_iota(jnp.int32, (rows, nq), 0)
    c = lax.broadcasted_iota(jnp.int32, (rows, nq), 1)
    mask_n = ((c >> ls) == ((r >> ls) & (nb - 1))) & ((c & (s_new - 1)) <= (r & (s_new - 1)))
    for kh in range(SWA_KV_HEADS):
        cs = slice(kh * SWA_HEAD_DIM, (kh + 1) * SWA_HEAD_DIM)
        vt = jnp.concatenate([cv_ref[b, cs, :] for b in range(nb)], axis=1)
        sink = _sink_column(sink_ref, kh, nq)
        sc = jnp.where(mask_c, scores[kh][0], NEG_BIG)
        sn = jnp.where(mask_n, scores[kh][1], NEG_BIG)
        m = jnp.maximum(jnp.maximum(jnp.max(sc, axis=-1, keepdims=True),
                                    jnp.max(sn, axis=-1, keepdims=True)), sink)
        pc = jnp.where(mask_c, jnp.exp(sc - m), 0.0)
        pn = jnp.where(mask_n, jnp.exp(sn - m), 0.0)
        den = (jnp.sum(pc, axis=-1, keepdims=True) + jnp.sum(pn, axis=-1, keepdims=True)
               + jnp.exp(sink - m))
        o = (_dot_nt(pc, vt) + _dot(pn, v_new[:, cs])) / den
        for g in range(SWA_GROUP):
            c0 = HG_WIDTH + (kh * SWA_GROUP + g) * SWA_HEAD_DIM
            o_ref[:, :, c0:c0 + SWA_HEAD_DIM] = o[g * nq:(g + 1) * nq].reshape(nb, s_new, SWA_HEAD_DIM)


def _sample_front_swa(refs):
    z_ref, _, ck_ref, cv_ref, _, _, cko_ref, cvo_ref = refs
    nb, s_new, _ = z_ref.shape
    return _swa_sample_front(z_ref, ck_ref, cv_ref, cko_ref, cvo_ref, nb, s_new, ck_ref.shape[2])


def _sample_front_gla(refs, lb):
    z_ref = refs[0]
    return [_gla_front(z_ref[j, :, 0:N_HG_IN], lb) for j in range(z_ref.shape[0])]


def _sample_back(front, refs, sink_ref, g_norm):
    z_ref, s0_ref, ck_ref, cv_ref, o_ref, so_ref, _, _ = refs
    nb, s_new, _ = z_ref.shape
    gla, swa = front
    dec_cols = _decay_columns([f[4] for f in gla])
    for j in range(nb):
        o, new_states = _gla_back(
            gla[j], g_norm, [s0_ref[j, h] for h in range(HG_HEADS)],
            [dec_cols[h * HG_DK:(h + 1) * HG_DK, j:j + 1] for h in range(HG_HEADS)])
        o_ref[j, :, 0:HG_WIDTH] = o
        for h in range(HG_HEADS):
            so_ref[j, h] = new_states[h]
    _swa_sample_back(swa, sink_ref, cv_ref, o_ref, nb, s_new, ck_ref.shape[2])


def _prompt_body(sink_ref, xc_ref, lmix_ref, win_ref, lbl_ref, gn_ref, wo_ref, lmlp_ref,
                 wu_ref, wd_ref, lfin_ref, zs_ref, s0_ref, cks_ref, cvs_ref,
                 y_ref, so_ref, ko_ref, vo_ref, os_ref, sso_ref, ckso_ref, cvso_ref,
                 z_ref, o_ref, st_ref, kvp_ref, h_ref, acc_ref, *, tiles_per_seq, n_tiles):
    sample_refs = (zs_ref, s0_ref, cks_ref, cvs_ref, os_ref, sso_ref, ckso_ref, cvso_ref)
    s = pl.program_id(0)
    pos = s % tiles_per_seq
    seq_start = pos == 0
    tm = DENSE_TILE

    @pl.when(seq_start)
    def _reset():
        st_ref[...] = jnp.zeros(st_ref.shape, st_ref.dtype)
        kvp_ref[...] = jnp.zeros(kvp_ref.shape, kvp_ref.dtype)

    n_ff = D_FF // FF_CHUNK
    n_blk = tm // WINDOW
    chunks_per_blk = WINDOW // GLA_CHUNK
    assert n_ff == 2 and n_blk == 4, "MATMUL_ORDER is written for 2 MLP slices and 4 mixer slices"

    def run(do_mlp, do_mix):
        if do_mlp:
            hn = _rms(h_ref[...], lmlp_ref[...]).astype(BF16)
        if do_mix:
            lb = _lower_bound(lbl_ref[...])
            g_norm = gn_ref[...]

        def mlp_up(c):
            cs = slice(c * FF_CHUNK, (c + 1) * FF_CHUNK)
            u = jnp.dot(hn, wu_ref[:, cs], preferred_element_type=F32)
            return jnp.square(jnp.maximum(u, 0.0)).astype(BF16)

        def mlp_down(c, a):
            cs = slice(c * FF_CHUNK, (c + 1) * FF_CHUNK)
            base = h_ref if c == 0 else acc_ref
            acc_ref[...] = base[...] + jnp.dot(a, wd_ref[cs, :], preferred_element_type=F32)

        def kv_blocks(n):
            kv_prev = kvp_ref[...] if n == 0 else z_ref[(n - 1) * WINDOW:n * WINDOW, KV_COLS]
            return z_ref[n * WINDOW:(n + 1) * WINDOW, KV_COLS], kv_prev

        def mix_front(n):
            gla = [_gla_front(z_ref[c * GLA_CHUNK:(c + 1) * GLA_CHUNK, 0:N_HG_IN], lb)
                   for c in range(n * chunks_per_blk, (n + 1) * chunks_per_blk)]
            kv_cur, kv_prev = kv_blocks(n)
            swa = _swa_front(z_ref[n * WINDOW:(n + 1) * WINDOW, N_HG_IN:N_HG_IN + SWA_WIDTH],
                             kv_cur, kv_prev)
            return gla, swa

        def mix_back(n, front):
            gla, swa = front
            dec_cols = _decay_columns([g[4] for g in gla])
            for i, c in enumerate(range(n * chunks_per_blk, (n + 1) * chunks_per_blk)):
                o, new_states = _gla_back(
                    gla[i], g_norm, [st_ref[h] for h in range(HG_HEADS)],
                    [dec_cols[h * HG_DK:(h + 1) * HG_DK, i:i + 1] for h in range(HG_HEADS)])
                for h in range(HG_HEADS):
                    st_ref[h] = new_states[h]
                o_ref[c * GLA_CHUNK:(c + 1) * GLA_CHUNK, 0:HG_WIDTH] = o.astype(BF16)
            kv_cur, kv_prev = kv_blocks(n)
            o_ref[n * WINDOW:(n + 1) * WINDOW, HG_WIDTH:] = _swa_back(
                swa, kv_cur, kv_prev, sink_ref, seq_start if n == 0 else None).astype(BF16)

        acts, fronts = {}, {}
        for step in MATMUL_ORDER.split():
            kind, i = step[0], step[1]
            if kind in "UD":
                if not do_mlp:
                    continue
                if kind == "U":
                    acts[i] = mlp_up(int(i))
                else:
                    mlp_down(int(i), acts.pop(i))
            elif not do_mix:
                continue
            elif step == "In":
                xn = _rms(xc_ref[...], lmix_ref[...]).astype(BF16)
                z_ref[...] = jnp.dot(xn, win_ref[...], preferred_element_type=F32)
            elif step == "Out":
                h_ref[...] = xc_ref[...] + jnp.dot(o_ref[...], wo_ref[...], preferred_element_type=F32)
            elif kind == "F":
                fronts[i] = mix_front(int(i))
            elif kind == "B":
                mix_back(int(i), fronts.pop(i))
            elif step == "sS":
                fronts[step] = _sample_front_swa(sample_refs)
            elif step == "sG":
                fronts[step] = _sample_front_gla(sample_refs, lb)
            else:
                assert step == "sB", step
                _sample_back((fronts.pop("sG"), fronts.pop("sS")), sample_refs, sink_ref, g_norm)
        assert not acts and not fronts
        if do_mix:
            kvp_ref[...] = z_ref[tm - WINDOW:tm, KV_COLS]
        if do_mlp:
            y_ref[...] = _rms(acc_ref[...], lfin_ref[...])

    pl.when((s > 0) & (s < n_tiles))(functools.partial(run, True, True))
    pl.when(s == 0)(functools.partial(run, False, True))
    pl.when(s == n_tiles)(functools.partial(run, True, False))

    @pl.when((pos == tiles_per_seq - 1) & (s < n_tiles))
    def _seq_end():
        for h in range(HG_HEADS):
            so_ref[0, h] = st_ref[h]
        ko_ref[0] = z_ref[tm - WINDOW:tm, N_HG_IN + SWA_WIDTH:N_HG_IN + SWA_WIDTH + SWA_KV_WIDTH].T
        vo_ref[0] = z_ref[tm - WINDOW:tm, N_HG_IN + SWA_WIDTH + SWA_KV_WIDTH:N_IN].T


def _fused_layer(x2d, bsz, sinks, ln_mix, w_in, lb_logits, hg_norm, w_out, ln_mlp, w_up, w_down,
                 ln_final, z_s, s0, ck, cv):
    n = x2d.shape[0]
    tm = DENSE_TILE
    n_tiles = n // tm
    tiles_per_seq = n_tiles // bsz
    assert tiles_per_seq * bsz * tm == n
    dbsz, s_new, _ = z_s.shape
    wb = ck.shape[2]
    nb = pl.cdiv(dbsz, n_tiles)
    assert dbsz % nb == 0 and nb <= HG_DK
    n_sblk = dbsz // nb
    const = lambda s: (0, 0)
    single = pl.Buffered(1)
    seq_of = lambda s: jnp.minimum(s, n_tiles - 1) // tiles_per_seq
    srow3 = lambda s: (jnp.minimum(s, n_sblk - 1), 0, 0)
    srow4 = lambda s: (jnp.minimum(s, n_sblk - 1), 0, 0, 0)
    state_spec = pl.BlockSpec((nb, HG_HEADS, HG_DK, HG_DV), srow4)
    cache_spec = pl.BlockSpec((nb, SWA_KV_WIDTH, wb), srow3)
    return pl.pallas_call(
        functools.partial(_prompt_body, tiles_per_seq=tiles_per_seq, n_tiles=n_tiles),
        grid=(n_tiles + 1,),
        in_specs=[
            pl.BlockSpec(memory_space=pltpu.SMEM),
            pl.BlockSpec((tm, D_MODEL), lambda s: (jnp.minimum(s, n_tiles - 1), 0)),
            pl.BlockSpec((1, D_MODEL), const),
            pl.BlockSpec((D_MODEL, N_IN), const, pipeline_mode=single),
            pl.BlockSpec(lb_logits.shape, const),
            pl.BlockSpec((1, HG_DV), const),
            pl.BlockSpec((D_MODEL, D_MODEL), const, pipeline_mode=single),
            pl.BlockSpec((1, D_MODEL), const),
            pl.BlockSpec((D_MODEL, D_FF), const, pipeline_mode=single),
            pl.BlockSpec((D_FF, D_MODEL), const, pipeline_mode=single),
            pl.BlockSpec((1, D_MODEL), const),
            pl.BlockSpec((nb, s_new, N_IN), srow3),
            state_spec,
            cache_spec,
            cache_spec,
        ],
        out_specs=[
            pl.BlockSpec((tm, D_MODEL), lambda s: (jnp.maximum(s - 1, 0), 0)),
            pl.BlockSpec((1, HG_HEADS, HG_DK, HG_DV), lambda s: (seq_of(s), 0, 0, 0)),
            pl.BlockSpec((1, WINDOW, SWA_KV_WIDTH), lambda s: (seq_of(s), 0, 0)),
            pl.BlockSpec((1, WINDOW, SWA_KV_WIDTH), lambda s: (seq_of(s), 0, 0)),
            pl.BlockSpec((nb, s_new, D_MODEL), srow3),
            state_spec,
            cache_spec,
            cache_spec,
        ],
        out_shape=[
            jax.ShapeDtypeStruct((n, D_MODEL), F32),
            jax.ShapeDtypeStruct((bsz, HG_HEADS, HG_DK, HG_DV), F32),
            jax.ShapeDtypeStruct((bsz, WINDOW, SWA_KV_WIDTH), F32),
            jax.ShapeDtypeStruct((bsz, WINDOW, SWA_KV_WIDTH), F32),
            jax.ShapeDtypeStruct((dbsz, s_new, D_MODEL), F32),
            jax.ShapeDtypeStruct(s0.shape, s0.dtype),
            jax.ShapeDtypeStruct(ck.shape, ck.dtype),
            jax.ShapeDtypeStruct(cv.shape, cv.dtype),
        ],
        scratch_shapes=[
            pltpu.VMEM((tm, N_IN), F32),
            pltpu.VMEM((tm, D_MODEL), BF16),
            pltpu.VMEM((HG_HEADS, HG_DK, HG_DV), F32),
            pltpu.VMEM((WINDOW, 2 * SWA_KV_WIDTH), F32),
            pltpu.VMEM((tm, D_MODEL), F32),
            pltpu.VMEM((tm, D_MODEL), F32),
        ],
        compiler_params=pltpu.CompilerParams(
            dimension_semantics=("arbitrary",), vmem_limit_bytes=VMEM_LIMIT_BYTES),
        name="fused_layer",
    )(sinks, x2d, ln_mix, w_in, lb_logits, hg_norm, w_out, ln_mlp, w_up, w_down, ln_final,
      z_s, s0, ck, cv)


CAST_STEPS = 8


def _prep_body(x_ref, g_ref, win_ref, wo_ref, wu_ref, wd_ref,
               z_ref, win_o, wo_o, wu_o, wd_o, win_scr, *, n_in, n_tiles):
    j = pl.program_id(0)
    wo_o[...] = wo_ref[...].astype(BF16)
    wu_o[...] = wu_ref[...].astype(BF16)
    wd_o[...] = wd_ref[...].astype(BF16)

    @pl.when(j < n_in)
    def _cast_w_in():
        rows = win_ref.shape[0]
        w = win_ref[...].astype(BF16)
        win_o[...] = w
        win_scr[pl.ds(pl.multiple_of(j * rows, rows), rows), :] = w

    @pl.when((j >= n_in) & (j < n_in + n_tiles))
    def _project():
        xn = _rms(x_ref[...], g_ref[...]).astype(BF16)
        z_ref[...] = jnp.dot(xn, win_scr[...], preferred_element_type=F32)


def _prep(x2d, ln, w_in, w_out, w_up, w_down):
    n = x2d.shape[0]
    tm = DENSE_TILE
    nc, n_in, n_tiles = CAST_STEPS, 2, n // tm
    assert n_in + n_tiles <= nc and w_in.shape[0] % (8 * n_in) == 0
    assert all(w.shape[0] % (8 * nc) == 0 for w in (w_out, w_up, w_down))
    in_chunk = lambda s: (jnp.minimum(s, n_in - 1), 0)
    tile = lambda s: (jnp.clip(s - n_in, 0, n_tiles - 1), 0)
    w_in_spec = pl.BlockSpec((w_in.shape[0] // n_in, w_in.shape[1]), in_chunk)
    w_specs = [pl.BlockSpec((w.shape[0] // nc, w.shape[1]), lambda s: (s, 0)) for w in (w_out, w_up, w_down)]
    return pl.pallas_call(
        functools.partial(_prep_body, n_in=n_in, n_tiles=n_tiles),
        grid=(nc,),
        in_specs=[pl.BlockSpec((tm, D_MODEL), tile), pl.BlockSpec((1, D_MODEL), lambda s: (0, 0)), w_in_spec]
        + w_specs,
        out_specs=[pl.BlockSpec((tm, N_IN), tile), w_in_spec] + w_specs,
        out_shape=[jax.ShapeDtypeStruct((n, N_IN), F32)]
        + [jax.ShapeDtypeStruct(w.shape, BF16) for w in (w_in, w_out, w_up, w_down)],
        scratch_shapes=[pltpu.VMEM(w_in.shape, BF16)],
        compiler_params=pltpu.CompilerParams(
            dimension_semantics=("arbitrary",), vmem_limit_bytes=VMEM_LIMIT_BYTES),
        name="prep",
    )(x2d, ln, w_in, w_out, w_up, w_down)


def _out_mlp_body(x_ref, o_ref, wo_ref, lm_ref, wu_ref, wd_ref, lf_ref, y_ref):
    h = x_ref[...] + jnp.dot(o_ref[...].astype(BF16), wo_ref[...], preferred_element_type=F32)
    hn = _rms(h, lm_ref[...]).astype(BF16)
    u = jnp.dot(hn, wu_ref[...], preferred_element_type=F32)
    a = jnp.square(jnp.maximum(u, 0.0)).astype(BF16)
    y_ref[...] = _rms(h + jnp.dot(a, wd_ref[...], preferred_element_type=F32), lf_ref[...])


def _out_mlp(x2d, o2d, w_out, ln_mlp, w_up, w_down, ln_final):
    n = x2d.shape[0]
    tm = DENSE_TILE
    const = lambda i: (0, 0)
    single = pl.Buffered(1)
    return pl.pallas_call(
        _out_mlp_body,
        grid=(n // tm,),
        in_specs=[
            pl.BlockSpec((tm, D_MODEL), lambda i: (i, 0)),
            pl.BlockSpec((tm, D_MODEL), lambda i: (i, 0)),
            pl.BlockSpec((D_MODEL, D_MODEL), const, pipeline_mode=single),
            pl.BlockSpec((1, D_MODEL), const),
            pl.BlockSpec((D_MODEL, D_FF), const, pipeline_mode=single),
            pl.BlockSpec((D_FF, D_MODEL), const, pipeline_mode=single),
            pl.BlockSpec((1, D_MODEL), const),
        ],
        out_specs=pl.BlockSpec((tm, D_MODEL), lambda i: (i, 0)),
        out_shape=jax.ShapeDtypeStruct((n, D_MODEL), F32),
        compiler_params=pltpu.CompilerParams(
            dimension_semantics=("arbitrary",), vmem_limit_bytes=VMEM_LIMIT_BYTES),
        name="out_mlp",
    )(x2d, o2d, w_out, ln_mlp, w_up, w_down, ln_final)


def kernel(x_prompt, x_sample, state_hgrn, cache_swa_k, cache_swa_v, ln_mix, w_in, lb_logits,
           hg_norm, sinks, w_out, ln_mlp, w_up, w_down, ln_final):
    depth = w_in.shape[0]
    assert depth == 1 and lb_logits.shape[0] == depth + 1
    bsz, seq, _ = x_prompt.shape
    dbsz, dseq, _ = x_sample.shape
    assert seq % DENSE_TILE == 0 and seq >= WINDOW and (dbsz * dseq) % DENSE_TILE == 0

    ln_mix2 = ln_mix[0].reshape(1, D_MODEL)
    ln_mlp2 = ln_mlp[0].reshape(1, D_MODEL)
    ln_fin2 = ln_final.reshape(1, D_MODEL)
    gn2 = hg_norm[0].reshape(1, HG_DV)
    sink1 = sinks[0]

    def feature_major(c):
        return jnp.transpose(c, (0, 2, 3, 1)).reshape(c.shape[0], SWA_KV_WIDTH, c.shape[1])

    def position_major(c):
        return jnp.transpose(c.reshape(c.shape[0], SWA_KV_HEADS, SWA_HEAD_DIM, c.shape[2]), (0, 3, 1, 2))

    xs = x_sample.reshape(dbsz * dseq, D_MODEL)
    z_s, w_in_b, w_out_b, w_up_b, w_down_b = _prep(xs, ln_mix2, w_in[0], w_out[0], w_up[0], w_down[0])
    xp = x_prompt.reshape(bsz * seq, D_MODEL)
    y_p, s_p, k_p, v_p, o_s, s_s, ck_s, cv_s = _fused_layer(
        xp, bsz, sink1, ln_mix2, w_in_b, lb_logits, gn2, w_out_b, ln_mlp2, w_up_b, w_down_b, ln_fin2,
        z_s.reshape(dbsz, dseq, N_IN),
        state_hgrn[0], feature_major(cache_swa_k[0]), feature_major(cache_swa_v[0]))
    y_s = _out_mlp(xs, o_s.reshape(-1, D_MODEL), w_out_b, ln_mlp2,
                   w_up_b, w_down_b, ln_fin2)

    return (y_p.reshape(bsz, seq, D_MODEL),
            y_s.reshape(dbsz, dseq, D_MODEL),
            s_p[None],
            position_major(k_p)[None].astype(cache_swa_k.dtype),
            position_major(v_p)[None].astype(cache_swa_v.dtype),
            s_s[None],
            position_major(ck_s)[None],
            position_major(cv_s)[None])
```

```python
import functools

import jax
import jax.numpy as jnp
from jax import lax
from jax.experimental import pallas as pl
from jax.experimental.pallas import tpu as pltpu

F32 = jnp.float32
BF16 = jnp.bfloat16

D_MODEL = 1024
HG_WIDTH = 512
HG_HEADS = 4
HG_DK = 128
HG_DV = 128
SWA_WIDTH = 512
SWA_HEAD_DIM = 64
SWA_Q_HEADS = 8
SWA_KV_HEADS = 2
SWA_GROUP = SWA_Q_HEADS // SWA_KV_HEADS
SWA_KV_WIDTH = SWA_KV_HEADS * SWA_HEAD_DIM
WINDOW = 128
SWA_SCALE = SWA_HEAD_DIM ** -0.5
D_FF = 4 * D_MODEL
EPS = 1e-6
N_HG_IN = 4 * HG_WIDTH
N_SWA_IN = SWA_WIDTH + 2 * SWA_KV_WIDTH
N_IN = N_HG_IN + N_SWA_IN
KV_COLS = slice(N_HG_IN + SWA_WIDTH, N_IN)
NEG_BIG = -1e30

V7X_VMEM_BYTES = 64 * 1024 * 1024
VMEM_LIMIT_BYTES = V7X_VMEM_BYTES - 4 * 1024 * 1024
DENSE_TILE = 512
GLA_CHUNK = 64
FF_CHUNK = 2048
MATMUL_ORDER = "sS In sG U0 F0 F1 sB D0 B0 B1 F2 F3 U1 B2 B3 D1"


def _rms(x, g):
    return x * lax.rsqrt(jnp.mean(x * x, axis=-1, keepdims=True) + EPS) * g


def _sigmoid(x):
    return 1.0 / (1.0 + jnp.exp(-x))


def _dot(a, b):
    return jnp.dot(a.astype(BF16), b.astype(BF16), preferred_element_type=F32)


def _dot_nt(a, b):
    return lax.dot_general(a.astype(BF16), b.astype(BF16), (((1,), (1,)), ((), ())),
                           preferred_element_type=F32)


def _dot_tn(a, b):
    return lax.dot_general(a.astype(BF16), b.astype(BF16), (((0,), (0,)), ((), ())),
                           preferred_element_type=F32)


def _log2(n):
    assert n > 0 and n & (n - 1) == 0, n
    return n.bit_length() - 1


def _cumsum_rows(x):
    n = x.shape[0]
    row = lax.broadcasted_iota(jnp.int32, x.shape, 0)
    s = 1
    while s < n:
        x = x + jnp.where(row >= s, pltpu.roll(x, s, axis=0), 0.0)
        s *= 2
    return x


def _lower_bound(lb_logits):
    m = jnp.max(lb_logits, axis=0, keepdims=True)
    e = jnp.exp(lb_logits - m)
    return e[0:1, :] / jnp.sum(e, axis=0, keepdims=True)


def _gla_front(zh, lb):
    chunk = zh.shape[0]
    zq = zh[:, 0:HG_WIDTH]
    zf = zh[:, HG_WIDTH:2 * HG_WIDTH]
    v = zh[:, 2 * HG_WIDTH:3 * HG_WIDTH].astype(BF16)
    zg = zh[:, 3 * HG_WIDTH:4 * HG_WIDTH]

    q = zq * _sigmoid(zq)
    f = lb + (1.0 - lb) * _sigmoid(zf)
    k = 1.0 - f
    b = _cumsum_rows(jnp.log(f))
    mid = chunk // 2 - 1
    b_mid = b[mid:mid + 1, :]
    b_last = b[chunk - 1:chunk, :]
    qf = q * jnp.exp(b - b_mid)
    kf = k * jnp.exp(b_mid - b)
    qt = q * jnp.exp(b)
    ke = k * jnp.exp(b_last - b)
    dec = jnp.exp(b_last)
    gate = zg * _sigmoid(zg)
    heads = [slice(h * HG_DK, (h + 1) * HG_DK) for h in range(HG_HEADS)]
    scores = [_dot_nt(qf[:, sl], kf[:, sl]) for sl in heads]
    kv = [_dot_tn(ke[:, sl], v[:, sl]) for sl in heads]
    return scores, kv, qt.astype(BF16), v, dec, gate


def _decay_columns(dec_rows):
    pad = jnp.zeros((HG_DK - len(dec_rows), HG_WIDTH), F32)
    return jnp.concatenate(list(dec_rows) + [pad], axis=0).T


def _gla_back(front, g_norm, states, decays):
    scores, kv, qt, v, _, gate = front
    chunk = qt.shape[0]
    row = lax.broadcasted_iota(jnp.int32, (chunk, chunk), 0)
    col = lax.broadcasted_iota(jnp.int32, (chunk, chunk), 1)
    causal = row >= col
    outs, new_states = [], []
    for h in range(HG_HEADS):
        sl = slice(h * HG_DK, (h + 1) * HG_DK)
        a = jnp.where(causal, scores[h], 0.0)
        o = _dot(a, v[:, sl]) + _dot(qt[:, sl], states[h])
        new_states.append(states[h] * decays[h] + kv[h])
        outs.append(_rms(o, g_norm) * gate[:, sl])
    return jnp.concatenate(outs, axis=-1), new_states


def _stack_heads(x, kh):
    return jnp.concatenate(
        [x[:, (kh * SWA_GROUP + g) * SWA_HEAD_DIM:(kh * SWA_GROUP + g + 1) * SWA_HEAD_DIM]
         for g in range(SWA_GROUP)], axis=0)


def _sink_column(sink_ref, kh, rows_per_head):
    r = lax.broadcasted_iota(jnp.int32, (SWA_GROUP * rows_per_head, 1), 0)
    col = jnp.full(r.shape, sink_ref[kh * SWA_GROUP], F32)
    for g in range(1, SWA_GROUP):
        col = jnp.where(r >= g * rows_per_head, sink_ref[kh * SWA_GROUP + g], col)
    return col


def _swa_front(zq, kv_cur, kv_prev):
    scores = []
    for kh in range(SWA_KV_HEADS):
        ks = slice(kh * SWA_HEAD_DIM, (kh + 1) * SWA_HEAD_DIM)
        q = (_stack_heads(zq, kh) * SWA_SCALE).astype(BF16)
        scores.append((_dot_nt(q, kv_prev[:, ks]), _dot_nt(q, kv_cur[:, ks])))
    return scores


def _swa_back(scores, kv_cur, kv_prev, sink_ref, no_prev):
    rows = SWA_GROUP * WINDOW
    i = lax.broadcasted_iota(jnp.int32, (rows, WINDOW), 0) & (WINDOW - 1)
    j = lax.broadcasted_iota(jnp.int32, (rows, WINDOW), 1)
    use_prev = j > i
    outs = []
    for kh in range(SWA_KV_HEADS):
        vs = slice(SWA_KV_WIDTH + kh * SWA_HEAD_DIM, SWA_KV_WIDTH + (kh + 1) * SWA_HEAD_DIM)
        sink = _sink_column(sink_ref, kh, WINDOW)
        s_prev, s_cur = scores[kh]
        if no_prev is not None:
            s_prev = jnp.where(no_prev, NEG_BIG, s_prev)
        s = jnp.where(use_prev, s_prev, s_cur)
        m = jnp.maximum(jnp.max(s, axis=-1, keepdims=True), sink)
        p = jnp.exp(s - m)
        den = jnp.sum(p, axis=-1, keepdims=True) + jnp.exp(sink - m)
        o = (_dot(jnp.where(use_prev, p, 0.0), kv_prev[:, vs])
             + _dot(jnp.where(use_prev, 0.0, p), kv_cur[:, vs])) / den
        outs.extend(o[g * WINDOW:(g + 1) * WINDOW] for g in range(SWA_GROUP))
    return jnp.concatenate(outs, axis=-1)


def _swa_sample_front(z_ref, ck_ref, cv_ref, cko_ref, cvo_ref, nb, s_new, wb):
    nq = nb * s_new
    z = z_ref[:, :, N_HG_IN:].reshape(nq, N_SWA_IN)
    k_new = z[:, SWA_WIDTH:SWA_WIDTH + SWA_KV_WIDTH]
    v_new = z[:, SWA_WIDTH + SWA_KV_WIDTH:]
    assert nq <= SWA_KV_WIDTH and wb == SWA_KV_WIDTH
    pad = jnp.zeros((SWA_KV_WIDTH - nq, SWA_KV_WIDTH), F32)
    lane = lax.broadcasted_iota(jnp.int32, (SWA_KV_WIDTH, wb), 1)
    for new, c_ref, co_ref in ((k_new, ck_ref, cko_ref), (v_new, cv_ref, cvo_ref)):
        new_t = jnp.concatenate([new, pad], axis=0).T
        for b in range(nb):
            kept = pltpu.roll(c_ref[b], wb - s_new, axis=1)
            fresh = pltpu.roll(new_t, (wb - s_new - b * s_new) % wb, axis=1)
            co_ref[b] = jnp.where(lane >= wb - s_new, fresh, kept)
    scores = []
    for kh in range(SWA_KV_HEADS):
        cs = slice(kh * SWA_HEAD_DIM, (kh + 1) * SWA_HEAD_DIM)
        kt = jnp.concatenate([ck_ref[b, cs, :] for b in range(nb)], axis=1)
        q = (_stack_heads(z, kh) * SWA_SCALE).astype(BF16)
        scores.append((_dot(q, kt), _dot_nt(q, k_new[:, cs])))
    return scores, v_new.astype(BF16)


def _swa_sample_back(front, sink_ref, cv_ref, o_ref, nb, s_new, wb):
    scores, v_new = front
    nq = nb * s_new
    rows = SWA_GROUP * nq
    ls, lw = _log2(s_new), _log2(wb)
    _log2(nb)
    r = lax.broadcasted_iota(jnp.int32, (rows, nb * wb), 0)
    c = lax.broadcasted_iota(jnp.int32, (rows, nb * wb), 1)
    mask_c = (((c >> lw) == ((r >> ls) & (nb - 1)))
              & ((c & (wb - 1)) > (r & (s_new - 1)) + (wb - WINDOW)))
    r = lax.broadcasted_iota(jnp.int32, (rows, nq), 0)
    c = lax.broadcasted_iota(jnp.int32, (rows, nq), 1)
    mask_n = ((c >> ls) == ((r >> ls) & (nb - 1))) & ((c & (s_new - 1)) <= (r & (s_new - 1)))
    for kh in range(SWA_KV_HEADS):
        cs = slice(kh * SWA_HEAD_DIM, (kh + 1) * SWA_HEAD_DIM)
        vt = jnp.concatenate([cv_ref[b, cs, :] for b in range(nb)], axis=1)
        sink = _sink_column(sink_ref, kh, nq)
        sc = jnp.where(mask_c, scores[kh][0], NEG_BIG)
        sn = jnp.where(mask_n, scores[kh][1], NEG_BIG)
        m = jnp.maximum(jnp.maximum(jnp.max(sc, axis=-1, keepdims=True),
                                    jnp.max(sn, axis=-1, keepdims=True)), sink)
        pc = jnp.where(mask_c, jnp.exp(sc - m), 0.0)
        pn = jnp.where(mask_n, jnp.exp(sn - m), 0.0)
        den = (jnp.sum(pc, axis=-1, keepdims=True) + jnp.sum(pn, axis=-1, keepdims=True)
               + jnp.exp(sink - m))
        o = (_dot_nt(pc, vt) + _dot(pn, v_new[:, cs])) / den
        for g in range(SWA_GROUP):
            c0 = HG_WIDTH + (kh * SWA_GROUP + g) * SWA_HEAD_DIM
            o_ref[:, :, c0:c0 + SWA_HEAD_DIM] = o[g * nq:(g + 1) * nq].reshape(nb, s_new, SWA_HEAD_DIM)


def _sample_front_swa(refs):
    z_ref, _, ck_ref, cv_ref, _, _, cko_ref, cvo_ref = refs
    nb, s_new, _ = z_ref.shape
    return _swa_sample_front(z_ref, ck_ref, cv_ref, cko_ref, cvo_ref, nb, s_new, ck_ref.shape[2])


def _sample_front_gla(refs, lb):
    z_ref = refs[0]
    return [_gla_front(z_ref[j, :, 0:N_HG_IN], lb) for j in range(z_ref.shape[0])]


def _sample_back(front, refs, sink_ref, g_norm):
    z_ref, s0_ref, ck_ref, cv_ref, o_ref, so_ref, _, _ = refs
    nb, s_new, _ = z_ref.shape
    gla, swa = front
    dec_cols = _decay_columns([f[4] for f in gla])
    for j in range(nb):
        o, new_states = _gla_back(
            gla[j], g_norm, [s0_ref[j, h] for h in range(HG_HEADS)],
            [dec_cols[h * HG_DK:(h + 1) * HG_DK, j:j + 1] for h in range(HG_HEADS)])
        o_ref[j, :, 0:HG_WIDTH] = o
        for h in range(HG_HEADS):
            so_ref[j, h] = new_states[h]
    _swa_sample_back(swa, sink_ref, cv_ref, o_ref, nb, s_new, ck_ref.shape[2])


def _prompt_body(sink_ref, xc_ref, lmix_ref, win_ref, lbl_ref, gn_ref, wo_ref, lmlp_ref,
                 wu_ref, wd_ref, lfin_ref, zs_ref, s0_ref, cks_ref, cvs_ref,
                 y_ref, so_ref, ko_ref, vo_ref, os_ref, sso_ref, ckso_ref, cvso_ref,
                 z_ref, o_ref, st_ref, kvp_ref, h_ref, acc_ref, *, tiles_per_seq, n_tiles):
    sample_refs = (zs_ref, s0_ref, cks_ref, cvs_ref, os_ref, sso_ref, ckso_ref, cvso_ref)
    s = pl.program_id(0)
    pos = s % tiles_per_seq
    seq_start = pos == 0
    tm = DENSE_TILE

    @pl.when(seq_start)
    def _reset():
        st_ref[...] = jnp.zeros(st_ref.shape, st_ref.dtype)
        kvp_ref[...] = jnp.zeros(kvp_ref.shape, kvp_ref.dtype)

    n_ff = D_FF // FF_CHUNK
    n_blk = tm // WINDOW
    chunks_per_blk = WINDOW // GLA_CHUNK
    assert n_ff == 2 and n_blk == 4, "MATMUL_ORDER is written for 2 MLP slices and 4 mixer slices"

    def run(do_mlp, do_mix):
        if do_mlp:
            hn = _rms(h_ref[...], lmlp_ref[...]).astype(BF16)
        if do_mix:
            lb = _lower_bound(lbl_ref[...])
            g_norm = gn_ref[...]

        def mlp_up(c):
            cs = slice(c * FF_CHUNK, (c + 1) * FF_CHUNK)
            u = jnp.dot(hn, wu_ref[:, cs], preferred_element_type=F32)
            return jnp.square(jnp.maximum(u, 0.0)).astype(BF16)

        def mlp_down(c, a):
            cs = slice(c * FF_CHUNK, (c + 1) * FF_CHUNK)
            base = h_ref if c == 0 else acc_ref
            acc_ref[...] = base[...] + jnp.dot(a, wd_ref[cs, :], preferred_element_type=F32)

        def kv_blocks(n):
            kv_prev = kvp_ref[...] if n == 0 else z_ref[(n - 1) * WINDOW:n * WINDOW, KV_COLS]
            return z_ref[n * WINDOW:(n + 1) * WINDOW, KV_COLS], kv_prev

        def mix_front(n):
            gla = [_gla_front(z_ref[c * GLA_CHUNK:(c + 1) * GLA_CHUNK, 0:N_HG_IN], lb)
                   for c in range(n * chunks_per_blk, (n + 1) * chunks_per_blk)]
            kv_cur, kv_prev = kv_blocks(n)
            swa = _swa_front(z_ref[n * WINDOW:(n + 1) * WINDOW, N_HG_IN:N_HG_IN + SWA_WIDTH],
                             kv_cur, kv_prev)
            return gla, swa

        def mix_back(n, front):
            gla, swa = front
            dec_cols = _decay_columns([g[4] for g in gla])
            for i, c in enumerate(range(n * chunks_per_blk, (n + 1) * chunks_per_blk)):
                o, new_states = _gla_back(
                    gla[i], g_norm, [st_ref[h] for h in range(HG_HEADS)],
                    [dec_cols[h * HG_DK:(h + 1) * HG_DK, i:i + 1] for h in range(HG_HEADS)])
                for h in range(HG_HEADS):
                    st_ref[h] = new_states[h]
                o_ref[c * GLA_CHUNK:(c + 1) * GLA_CHUNK, 0:HG_WIDTH] = o.astype(BF16)
            kv_cur, kv_prev = kv_blocks(n)
            o_ref[n * WINDOW:(n + 1) * WINDOW, HG_WIDTH:] = _swa_back(
                swa, kv_cur, kv_prev, sink_ref, seq_start if n == 0 else None).astype(BF16)

        acts, fronts = {}, {}
        for step in MATMUL_ORDER.split():
            kind, i = step[0], step[1]
            if kind in "UD":
                if not do_mlp:
                    continue
                if kind == "U":
                    acts[i] = mlp_up(int(i))
                else:
                    mlp_down(int(i), acts.pop(i))
            elif not do_mix:
                continue
            elif step == "In":
                xn = _rms(xc_ref[...], lmix_ref[...]).astype(BF16)
                z_ref[...] = jnp.dot(xn, win_ref[...], preferred_element_type=F32)
            elif kind == "F":
                fronts[i] = mix_front(int(i))
            elif kind == "B":
                mix_back(int(i), fronts.pop(i))
            elif step == "sS":
                fronts[step] = _sample_front_swa(sample_refs)
            elif step == "sG":
                fronts[step] = _sample_front_gla(sample_refs, lb)
            else:
                assert step == "sB", step
                _sample_back((fronts.pop("sG"), fronts.pop("sS")), sample_refs, sink_ref, g_norm)
        assert not acts and not fronts
        if do_mix:
            kvp_ref[...] = z_ref[tm - WINDOW:tm, KV_COLS]
            h_ref[...] = xc_ref[...] + jnp.dot(o_ref[...], wo_ref[...], preferred_element_type=F32)
        if do_mlp:
            y_ref[...] = _rms(acc_ref[...], lfin_ref[...])

    pl.when((s > 0) & (s < n_tiles))(functools.partial(run, True, True))
    pl.when(s == 0)(functools.partial(run, False, True))
    pl.when(s == n_tiles)(functools.partial(run, True, False))

    @pl.when((pos == tiles_per_seq - 1) & (s < n_tiles))
    def _seq_end():
        for h in range(HG_HEADS):
            so_ref[0, h] = st_ref[h]
        ko_ref[0] = z_ref[tm - WINDOW:tm, N_HG_IN + SWA_WIDTH:N_HG_IN + SWA_WIDTH + SWA_KV_WIDTH].T
        vo_ref[0] = z_ref[tm - WINDOW:tm, N_HG_IN + SWA_WIDTH + SWA_KV_WIDTH:N_IN].T


def _fused_layer(x2d, bsz, sinks, ln_mix, w_in, lb_logits, hg_norm, w_out, ln_mlp, w_up, w_down,
                 ln_final, z_s, s0, ck, cv):
    n = x2d.shape[0]
    tm = DENSE_TILE
    n_tiles = n // tm
    tiles_per_seq = n_tiles // bsz
    assert tiles_per_seq * bsz * tm == n
    dbsz, s_new, _ = z_s.shape
    wb = ck.shape[2]
    nb = pl.cdiv(dbsz, n_tiles)
    assert dbsz % nb == 0 and nb <= HG_DK
    n_sblk = dbsz // nb
    const = lambda s: (0, 0)
    single = pl.Buffered(1)
    seq_of = lambda s: jnp.minimum(s, n_tiles - 1) // tiles_per_seq
    srow3 = lambda s: (jnp.minimum(s, n_sblk - 1), 0, 0)
    srow4 = lambda s: (jnp.minimum(s, n_sblk - 1), 0, 0, 0)
    state_spec = pl.BlockSpec((nb, HG_HEADS, HG_DK, HG_DV), srow4)
    cache_spec = pl.BlockSpec((nb, SWA_KV_WIDTH, wb), srow3)
    return pl.pallas_call(
        functools.partial(_prompt_body, tiles_per_seq=tiles_per_seq, n_tiles=n_tiles),
        grid=(n_tiles + 1,),
        in_specs=[
            pl.BlockSpec(memory_space=pltpu.SMEM),
            pl.BlockSpec((tm, D_MODEL), lambda s: (jnp.minimum(s, n_tiles - 1), 0)),
            pl.BlockSpec((1, D_MODEL), const),
            pl.BlockSpec((D_MODEL, N_IN), const, pipeline_mode=single),
            pl.BlockSpec(lb_logits.shape, const),
            pl.BlockSpec((1, HG_DV), const),
            pl.BlockSpec((D_MODEL, D_MODEL), const, pipeline_mode=single),
            pl.BlockSpec((1, D_MODEL), const),
            pl.BlockSpec((D_MODEL, D_FF), const, pipeline_mode=single),
            pl.BlockSpec((D_FF, D_MODEL), const, pipeline_mode=single),
            pl.BlockSpec((1, D_MODEL), const),
            pl.BlockSpec((nb, s_new, N_IN), srow3),
            state_spec,
            cache_spec,
            cache_spec,
        ],
        out_specs=[
            pl.BlockSpec((tm, D_MODEL), lambda s: (jnp.maximum(s - 1, 0), 0)),
            pl.BlockSpec((1, HG_HEADS, HG_DK, HG_DV), lambda s: (seq_of(s), 0, 0, 0)),
            pl.BlockSpec((1, WINDOW, SWA_KV_WIDTH), lambda s: (seq_of(s), 0, 0)),
            pl.BlockSpec((1, WINDOW, SWA_KV_WIDTH), lambda s: (seq_of(s), 0, 0)),
            pl.BlockSpec((nb, s_new, D_MODEL), srow3),
            state_spec,
            cache_spec,
            cache_spec,
        ],
        out_shape=[
            jax.ShapeDtypeStruct((n, D_MODEL), F32),
            jax.ShapeDtypeStruct((bsz, HG_HEADS, HG_DK, HG_DV), F32),
            jax.ShapeDtypeStruct((bsz, WINDOW, SWA_KV_WIDTH), F32),
            jax.ShapeDtypeStruct((bsz, WINDOW, SWA_KV_WIDTH), F32),
            jax.ShapeDtypeStruct((dbsz, s_new, D_MODEL), F32),
            jax.ShapeDtypeStruct(s0.shape, s0.dtype),
            jax.ShapeDtypeStruct(ck.shape, ck.dtype),
            jax.ShapeDtypeStruct(cv.shape, cv.dtype),
        ],
        scratch_shapes=[
            pltpu.VMEM((tm, N_IN), F32),
            pltpu.VMEM((tm, D_MODEL), BF16),
            pltpu.VMEM((HG_HEADS, HG_DK, HG_DV), F32),
            pltpu.VMEM((WINDOW, 2 * SWA_KV_WIDTH), F32),
            pltpu.VMEM((tm, D_MODEL), F32),
            pltpu.VMEM((tm, D_MODEL), F32),
        ],
        compiler_params=pltpu.CompilerParams(
            dimension_semantics=("arbitrary",), vmem_limit_bytes=VMEM_LIMIT_BYTES),
        name="fused_layer",
    )(sinks, x2d, ln_mix, w_in, lb_logits, hg_norm, w_out, ln_mlp, w_up, w_down, ln_final,
      z_s, s0, ck, cv)


CAST_STEPS = 8


def _prep_body(x_ref, g_ref, win_ref, wo_ref, wu_ref, wd_ref,
               z_ref, win_o, wo_o, wu_o, wd_o, win_scr, *, n_in, n_tiles):
    j = pl.program_id(0)
    wo_o[...] = wo_ref[...].astype(BF16)
    wu_o[...] = wu_ref[...].astype(BF16)
    wd_o[...] = wd_ref[...].astype(BF16)

    @pl.when(j < n_in)
    def _cast_w_in():
        rows = win_ref.shape[0]
        w = win_ref[...].astype(BF16)
        win_o[...] = w
        win_scr[pl.ds(pl.multiple_of(j * rows, rows), rows), :] = w

    @pl.when((j >= n_in) & (j < n_in + n_tiles))
    def _project():
        xn = _rms(x_ref[...], g_ref[...]).astype(BF16)
        z_ref[...] = jnp.dot(xn, win_scr[...], preferred_element_type=F32)


def _prep(x2d, ln, w_in, w_out, w_up, w_down):
    n = x2d.shape[0]
    tm = DENSE_TILE
    nc, n_in, n_tiles = CAST_STEPS, 2, n // tm
    assert n_in + n_tiles <= nc and w_in.shape[0] % (8 * n_in) == 0
    assert all(w.shape[0] % (8 * nc) == 0 for w in (w_out, w_up, w_down))
    in_chunk = lambda s: (jnp.minimum(s, n_in - 1), 0)
    tile = lambda s: (jnp.clip(s - n_in, 0, n_tiles - 1), 0)
    w_in_spec = pl.BlockSpec((w_in.shape[0] // n_in, w_in.shape[1]), in_chunk)
    w_specs = [pl.BlockSpec((w.shape[0] // nc, w.shape[1]), lambda s: (s, 0)) for w in (w_out, w_up, w_down)]
    return pl.pallas_call(
        functools.partial(_prep_body, n_in=n_in, n_tiles=n_tiles),
        grid=(nc,),
        in_specs=[pl.BlockSpec((tm, D_MODEL), tile), pl.BlockSpec((1, D_MODEL), lambda s: (0, 0)), w_in_spec]
        + w_specs,
        out_specs=[pl.BlockSpec((tm, N_IN), tile), w_in_spec] + w_specs,
        out_shape=[jax.ShapeDtypeStruct((n, N_IN), F32)]
        + [jax.ShapeDtypeStruct(w.shape, BF16) for w in (w_in, w_out, w_up, w_down)],
        scratch_shapes=[pltpu.VMEM(w_in.shape, BF16)],
        compiler_params=pltpu.CompilerParams(
            dimension_semantics=("arbitrary",), vmem_limit_bytes=VMEM_LIMIT_BYTES),
        name="prep",
    )(x2d, ln, w_in, w_out, w_up, w_down)


def _out_mlp_body(x_ref, o_ref, wo_ref, lm_ref, wu_ref, wd_ref, lf_ref, y_ref):
    h = x_ref[...] + jnp.dot(o_ref[...].astype(BF16), wo_ref[...], preferred_element_type=F32)
    hn = _rms(h, lm_ref[...]).astype(BF16)
    u = jnp.dot(hn, wu_ref[...], preferred_element_type=F32)
    a = jnp.square(jnp.maximum(u, 0.0)).astype(BF16)
    y_ref[...] = _rms(h + jnp.dot(a, wd_ref[...], preferred_element_type=F32), lf_ref[...])


def _out_mlp(x2d, o2d, w_out, ln_mlp, w_up, w_down, ln_final):
    n = x2d.shape[0]
    tm = DENSE_TILE
    const = lambda i: (0, 0)
    single = pl.Buffered(1)
    return pl.pallas_call(
        _out_mlp_body,
        grid=(n // tm,),
        in_specs=[
            pl.BlockSpec((tm, D_MODEL), lambda i: (i, 0)),
            pl.BlockSpec((tm, D_MODEL), lambda i: (i, 0)),
            pl.BlockSpec((D_MODEL, D_MODEL), const, pipeline_mode=single),
            pl.BlockSpec((1, D_MODEL), const),
            pl.BlockSpec((D_MODEL, D_FF), const, pipeline_mode=single),
            pl.BlockSpec((D_FF, D_MODEL), const, pipeline_mode=single),
            pl.BlockSpec((1, D_MODEL), const),
        ],
        out_specs=pl.BlockSpec((tm, D_MODEL), lambda i: (i, 0)),
        out_shape=jax.ShapeDtypeStruct((n, D_MODEL), F32),
        compiler_params=pltpu.CompilerParams(
            dimension_semantics=("arbitrary",), vmem_limit_bytes=VMEM_LIMIT_BYTES),
        name="out_mlp",
    )(x2d, o2d, w_out, ln_mlp, w_up, w_down, ln_final)


def kernel(x_prompt, x_sample, state_hgrn, cache_swa_k, cache_swa_v, ln_mix, w_in, lb_logits,
           hg_norm, sinks, w_out, ln_mlp, w_up, w_down, ln_final):
    depth = w_in.shape[0]
    assert depth == 1 and lb_logits.shape[0] == depth + 1
    bsz, seq, _ = x_prompt.shape
    dbsz, dseq, _ = x_sample.shape
    assert seq % DENSE_TILE == 0 and seq >= WINDOW and (dbsz * dseq) % DENSE_TILE == 0

    ln_mix2 = ln_mix[0].reshape(1, D_MODEL)
    ln_mlp2 = ln_mlp[0].reshape(1, D_MODEL)
    ln_fin2 = ln_final.reshape(1, D_MODEL)
    gn2 = hg_norm[0].reshape(1, HG_DV)
    sink1 = sinks[0]

    def feature_major(c):
        return jnp.transpose(c, (0, 2, 3, 1)).reshape(c.shape[0], SWA_KV_WIDTH, c.shape[1])

    def position_major(c):
        return jnp.transpose(c.reshape(c.shape[0], SWA_KV_HEADS, SWA_HEAD_DIM, c.shape[2]), (0, 3, 1, 2))

    xs = x_sample.reshape(dbsz * dseq, D_MODEL)
    z_s, w_in_b, w_out_b, w_up_b, w_down_b = _prep(xs, ln_mix2, w_in[0], w_out[0], w_up[0], w_down[0])
    xp = x_prompt.reshape(bsz * seq, D_MODEL)
    y_p, s_p, k_p, v_p, o_s, s_s, ck_s, cv_s = _fused_layer(
        xp, bsz, sink1, ln_mix2, w_in_b, lb_logits, gn2, w_out_b, ln_mlp2, w_up_b, w_down_b, ln_fin2,
        z_s.reshape(dbsz, dseq, N_IN),
        state_hgrn[0], feature_major(cache_swa_k[0]), feature_major(cache_swa_v[0]))
    y_s = _out_mlp(xs, o_s.reshape(-1, D_MODEL), w_out_b, ln_mlp2,
                   w_up_b, w_down_b, ln_fin2)

    return (y_p.reshape(bsz, seq, D_MODEL),
            y_s.reshape(dbsz, dseq, D_MODEL),
            s_p[None],
            position_major(k_p)[None].astype(cache_swa_k.dtype),
            position_major(v_p)[None].astype(cache_swa_v.dtype),
            s_s[None],
            position_major(ck_s)[None],
            position_major(cv_s)[None])
```

```python
import functools

import jax
import jax.numpy as jnp
from jax import lax
from jax.experimental import pallas as pl
from jax.experimental.pallas import tpu as pltpu

F32 = jnp.float32
BF16 = jnp.bfloat16

D_MODEL = 1024
HG_WIDTH = 512
HG_HEADS = 4
HG_DK = 128
HG_DV = 128
SWA_WIDTH = 512
SWA_HEAD_DIM = 64
SWA_Q_HEADS = 8
SWA_KV_HEADS = 2
SWA_GROUP = SWA_Q_HEADS // SWA_KV_HEADS
SWA_KV_WIDTH = SWA_KV_HEADS * SWA_HEAD_DIM
WINDOW = 128
SWA_SCALE = SWA_HEAD_DIM ** -0.5
D_FF = 4 * D_MODEL
EPS = 1e-6
N_HG_IN = 4 * HG_WIDTH
N_SWA_IN = SWA_WIDTH + 2 * SWA_KV_WIDTH
N_IN = N_HG_IN + N_SWA_IN
KV_COLS = slice(N_HG_IN + SWA_WIDTH, N_IN)
NEG_BIG = -1e30

V7X_VMEM_BYTES = 64 * 1024 * 1024
VMEM_LIMIT_BYTES = V7X_VMEM_BYTES - 4 * 1024 * 1024
DENSE_TILE = 512
GLA_CHUNK = 64
FF_CHUNK = 2048
SAMPLE_FF_CHUNK = 1024
MATMUL_ORDER = "sS In sG U0 F0 F1 D0 sB B0 B1 F2 F3 U1 B2 B3 D1"


def _rms(x, g):
    return x * lax.rsqrt(jnp.mean(x * x, axis=-1, keepdims=True) + EPS) * g


def _sigmoid(x):
    return 1.0 / (1.0 + jnp.exp(-x))


def _dot(a, b):
    return jnp.dot(a.astype(BF16), b.astype(BF16), preferred_element_type=F32)


def _dot_nt(a, b):
    return lax.dot_general(a.astype(BF16), b.astype(BF16), (((1,), (1,)), ((), ())),
                           preferred_element_type=F32)


def _dot_tn(a, b):
    return lax.dot_general(a.astype(BF16), b.astype(BF16), (((0,), (0,)), ((), ())),
                           preferred_element_type=F32)


def _log2(n):
    assert n > 0 and n & (n - 1) == 0, n
    return n.bit_length() - 1


def _cumsum_rows(x):
    n = x.shape[0]
    row = lax.broadcasted_iota(jnp.int32, x.shape, 0)
    s = 1
    while s < n:
        x = x + jnp.where(row >= s, pltpu.roll(x, s, axis=0), 0.0)
        s *= 2
    return x


def _lower_bound(lb_logits):
    m = jnp.max(lb_logits, axis=0, keepdims=True)
    e = jnp.exp(lb_logits - m)
    return e[0:1, :] / jnp.sum(e, axis=0, keepdims=True)


def _gla_front(zh, lb):
    chunk = zh.shape[0]
    zq = zh[:, 0:HG_WIDTH]
    zf = zh[:, HG_WIDTH:2 * HG_WIDTH]
    v = zh[:, 2 * HG_WIDTH:3 * HG_WIDTH].astype(BF16)
    zg = zh[:, 3 * HG_WIDTH:4 * HG_WIDTH]

    q = zq * _sigmoid(zq)
    f = lb + (1.0 - lb) * _sigmoid(zf)
    k = 1.0 - f
    b = _cumsum_rows(jnp.log(f))
    mid = chunk // 2 - 1
    b_mid = b[mid:mid + 1, :]
    b_last = b[chunk - 1:chunk, :]
    qf = q * jnp.exp(b - b_mid)
    kf = k * jnp.exp(b_mid - b)
    qt = q * jnp.exp(b)
    ke = k * jnp.exp(b_last - b)
    dec = jnp.exp(b_last)
    gate = zg * _sigmoid(zg)
    heads = [slice(h * HG_DK, (h + 1) * HG_DK) for h in range(HG_HEADS)]
    scores = [_dot_nt(qf[:, sl], kf[:, sl]) for sl in heads]
    kv = [_dot_tn(ke[:, sl], v[:, sl]) for sl in heads]
    return scores, kv, qt.astype(BF16), v, dec, gate


def _decay_columns(dec_rows):
    pad = jnp.zeros((HG_DK - len(dec_rows), HG_WIDTH), F32)
    return jnp.concatenate(list(dec_rows) + [pad], axis=0).T


def _gla_back(front, g_norm, states, decays):
    scores, kv, qt, v, _, gate = front
    chunk = qt.shape[0]
    row = lax.broadcasted_iota(jnp.int32, (chunk, chunk), 0)
    col = lax.broadcasted_iota(jnp.int32, (chunk, chunk), 1)
    causal = row >= col
    outs, new_states = [], []
    for h in range(HG_HEADS):
        sl = slice(h * HG_DK, (h + 1) * HG_DK)
        a = jnp.where(causal, scores[h], 0.0)
        o = _dot(a, v[:, sl]) + _dot(qt[:, sl], states[h])
        new_states.append(states[h] * decays[h] + kv[h])
        outs.append(_rms(o, g_norm) * gate[:, sl])
    return jnp.concatenate(outs, axis=-1), new_states


def _stack_heads(x, kh):
    return jnp.concatenate(
        [x[:, (kh * SWA_GROUP + g) * SWA_HEAD_DIM:(kh * SWA_GROUP + g + 1) * SWA_HEAD_DIM]
         for g in range(SWA_GROUP)], axis=0)


def _sink_column(sink_ref, kh, rows_per_head):
    r = lax.broadcasted_iota(jnp.int32, (SWA_GROUP * rows_per_head, 1), 0)
    col = jnp.full(r.shape, sink_ref[kh * SWA_GROUP], F32)
    for g in range(1, SWA_GROUP):
        col = jnp.where(r >= g * rows_per_head, sink_ref[kh * SWA_GROUP + g], col)
    return col


def _swa_front(zq, kv_cur, kv_prev):
    scores = []
    for kh in range(SWA_KV_HEADS):
        ks = slice(kh * SWA_HEAD_DIM, (kh + 1) * SWA_HEAD_DIM)
        q = (_stack_heads(zq, kh) * SWA_SCALE).astype(BF16)
        scores.append((_dot_nt(q, kv_prev[:, ks]), _dot_nt(q, kv_cur[:, ks])))
    return scores


def _swa_back(scores, kv_cur, kv_prev, sink_ref, no_prev):
    rows = SWA_GROUP * WINDOW
    i = lax.broadcasted_iota(jnp.int32, (rows, WINDOW), 0) & (WINDOW - 1)
    j = lax.broadcasted_iota(jnp.int32, (rows, WINDOW), 1)
    use_prev = j > i
    outs = []
    for kh in range(SWA_KV_HEADS):
        vs = slice(SWA_KV_WIDTH + kh * SWA_HEAD_DIM, SWA_KV_WIDTH + (kh + 1) * SWA_HEAD_DIM)
        sink = _sink_column(sink_ref, kh, WINDOW)
        s_prev, s_cur = scores[kh]
        if no_prev is not None:
            s_prev = jnp.where(no_prev, NEG_BIG, s_prev)
        s = jnp.where(use_prev, s_prev, s_cur)
        m = jnp.maximum(jnp.max(s, axis=-1, keepdims=True), sink)
        p = jnp.exp(s - m)
        den = jnp.sum(p, axis=-1, keepdims=True) + jnp.exp(sink - m)
        o = (_dot(jnp.where(use_prev, p, 0.0), kv_prev[:, vs])
             + _dot(jnp.where(use_prev, 0.0, p), kv_cur[:, vs])) / den
        outs.extend(o[g * WINDOW:(g + 1) * WINDOW] for g in range(SWA_GROUP))
    return jnp.concatenate(outs, axis=-1)


def _swa_sample_front(z_ref, ck_ref, cv_ref, cko_ref, cvo_ref, nb, s_new, wb):
    nq = nb * s_new
    z = z_ref[:, :, N_HG_IN:].reshape(nq, N_SWA_IN)
    k_new = z[:, SWA_WIDTH:SWA_WIDTH + SWA_KV_WIDTH]
    v_new = z[:, SWA_WIDTH + SWA_KV_WIDTH:]
    assert nq <= SWA_KV_WIDTH and wb == SWA_KV_WIDTH
    pad = jnp.zeros((SWA_KV_WIDTH - nq, SWA_KV_WIDTH), F32)
    lane = lax.broadcasted_iota(jnp.int32, (SWA_KV_WIDTH, wb), 1)
    for new, c_ref, co_ref in ((k_new, ck_ref, cko_ref), (v_new, cv_ref, cvo_ref)):
        new_t = jnp.concatenate([new, pad], axis=0).T
        for b in range(nb):
            kept = pltpu.roll(c_ref[b], wb - s_new, axis=1)
            fresh = pltpu.roll(new_t, (wb - s_new - b * s_new) % wb, axis=1)
            co_ref[b] = jnp.where(lane >= wb - s_new, fresh, kept)
    scores = []
    for kh in range(SWA_KV_HEADS):
        cs = slice(kh * SWA_HEAD_DIM, (kh + 1) * SWA_HEAD_DIM)
        kt = jnp.concatenate([ck_ref[b, cs, :] for b in range(nb)], axis=1)
        q = (_stack_heads(z, kh) * SWA_SCALE).astype(BF16)
        scores.append((_dot(q, kt), _dot_nt(q, k_new[:, cs])))
    return scores, v_new.astype(BF16)


def _swa_sample_back(front, sink_ref, cv_ref, o_ref, nb, s_new, wb):
    scores, v_new = front
    nq = nb * s_new
    rows = SWA_GROUP * nq
    ls, lw = _log2(s_new), _log2(wb)
    _log2(nb)
    r = lax.broadcasted_iota(jnp.int32, (rows, nb * wb), 0)
    c = lax.broadcasted_iota(jnp.int32, (rows, nb * wb), 1)
    mask_c = (((c >> lw) == ((r >> ls) & (nb - 1)))
              & ((c & (wb - 1)) > (r & (s_new - 1)) + (wb - WINDOW)))
    r = lax.broadcasted_iota(jnp.int32, (rows, nq), 0)
    c = lax.broadcasted_iota(jnp.int32, (rows, nq), 1)
    mask_n = ((c >> ls) == ((r >> ls) & (nb - 1))) & ((c & (s_new - 1)) <= (r & (s_new - 1)))
    for kh in range(SWA_KV_HEADS):
        cs = slice(kh * SWA_HEAD_DIM, (kh + 1) * SWA_HEAD_DIM)
        vt = jnp.concatenate([cv_ref[b, cs, :] for b in range(nb)], axis=1)
        sink = _sink_column(sink_ref, kh, nq)
        sc = jnp.where(mask_c, scores[kh][0], NEG_BIG)
        sn = jnp.where(mask_n, scores[kh][1], NEG_BIG)
        m = jnp.maximum(jnp.maximum(jnp.max(sc, axis=-1, keepdims=True),
                                    jnp.max(sn, axis=-1, keepdims=True)), sink)
        pc = jnp.where(mask_c, jnp.exp(sc - m), 0.0)
        pn = jnp.where(mask_n, jnp.exp(sn - m), 0.0)
        den = (jnp.sum(pc, axis=-1, keepdims=True) + jnp.sum(pn, axis=-1, keepdims=True)
               + jnp.exp(sink - m))
        o = (_dot_nt(pc, vt) + _dot(pn, v_new[:, cs])) / den
        for g in range(SWA_GROUP):
            c0 = HG_WIDTH + (kh * SWA_GROUP + g) * SWA_HEAD_DIM
            o_ref[:, :, c0:c0 + SWA_HEAD_DIM] = o[g * nq:(g + 1) * nq].reshape(nb, s_new, SWA_HEAD_DIM)


def _sample_front_swa(refs):
    z_ref, _, ck_ref, cv_ref, _, _, cko_ref, cvo_ref = refs
    nb, s_new, _ = z_ref.shape
    return _swa_sample_front(z_ref, ck_ref, cv_ref, cko_ref, cvo_ref, nb, s_new, ck_ref.shape[2])


def _sample_front_gla(refs, lb):
    z_ref = refs[0]
    return [_gla_front(z_ref[j, :, 0:N_HG_IN], lb) for j in range(z_ref.shape[0])]


def _sample_back(front, refs, sink_ref, g_norm):
    z_ref, s0_ref, ck_ref, cv_ref, o_ref, so_ref, _, _ = refs
    nb, s_new, _ = z_ref.shape
    gla, swa = front
    dec_cols = _decay_columns([f[4] for f in gla])
    for j in range(nb):
        o, new_states = _gla_back(
            gla[j], g_norm, [s0_ref[j, h] for h in range(HG_HEADS)],
            [dec_cols[h * HG_DK:(h + 1) * HG_DK, j:j + 1] for h in range(HG_HEADS)])
        o_ref[j, :, 0:HG_WIDTH] = o
        for h in range(HG_HEADS):
            so_ref[j, h] = new_states[h]
    _swa_sample_back(swa, sink_ref, cv_ref, o_ref, nb, s_new, ck_ref.shape[2])


def _prompt_body(sink_ref, xc_ref, lmix_ref, win_ref, lbl_ref, gn_ref, wo_ref, lmlp_ref,
                 wu_ref, wd_ref, lfin_ref, zs_ref, s0_ref, cks_ref, cvs_ref,
                 y_ref, so_ref, ko_ref, vo_ref, os_ref, sso_ref, ckso_ref, cvso_ref,
                 z_ref, o_ref, st_ref, kvp_ref, h_ref, acc_ref, *, tiles_per_seq, n_tiles):
    sample_refs = (zs_ref, s0_ref, cks_ref, cvs_ref, os_ref, sso_ref, ckso_ref, cvso_ref)
    s = pl.program_id(0)
    pos = s % tiles_per_seq
    seq_start = pos == 0
    tm = DENSE_TILE

    @pl.when(seq_start)
    def _reset():
        st_ref[...] = jnp.zeros(st_ref.shape, st_ref.dtype)
        kvp_ref[...] = jnp.zeros(kvp_ref.shape, kvp_ref.dtype)

    n_ff = D_FF // FF_CHUNK
    n_blk = tm // WINDOW
    chunks_per_blk = WINDOW // GLA_CHUNK
    assert n_ff == 2 and n_blk == 4, "MATMUL_ORDER is written for 2 MLP slices and 4 mixer slices"

    def run(do_mlp, do_mix):
        if do_mlp:
            hn = _rms(h_ref[...], lmlp_ref[...]).astype(BF16)
        if do_mix:
            lb = _lower_bound(lbl_ref[...])
            g_norm = gn_ref[...]

        def mlp_up(c):
            cs = slice(c * FF_CHUNK, (c + 1) * FF_CHUNK)
            u = jnp.dot(hn, wu_ref[:, cs], preferred_element_type=F32)
            return jnp.square(jnp.maximum(u, 0.0)).astype(BF16)

        def mlp_down(c, a):
            cs = slice(c * FF_CHUNK, (c + 1) * FF_CHUNK)
            base = h_ref if c == 0 else acc_ref
            acc_ref[...] = base[...] + jnp.dot(a, wd_ref[cs, :], preferred_element_type=F32)

        def kv_blocks(n):
            kv_prev = kvp_ref[...] if n == 0 else z_ref[(n - 1) * WINDOW:n * WINDOW, KV_COLS]
            return z_ref[n * WINDOW:(n + 1) * WINDOW, KV_COLS], kv_prev

        def mix_front(n):
            gla = [_gla_front(z_ref[c * GLA_CHUNK:(c + 1) * GLA_CHUNK, 0:N_HG_IN], lb)
                   for c in range(n * chunks_per_blk, (n + 1) * chunks_per_blk)]
            kv_cur, kv_prev = kv_blocks(n)
            swa = _swa_front(z_ref[n * WINDOW:(n + 1) * WINDOW, N_HG_IN:N_HG_IN + SWA_WIDTH],
                             kv_cur, kv_prev)
            return gla, swa

        def mix_back(n, front):
            gla, swa = front
            dec_cols = _decay_columns([g[4] for g in gla])
            for i, c in enumerate(range(n * chunks_per_blk, (n + 1) * chunks_per_blk)):
                o, new_states = _gla_back(
                    gla[i], g_norm, [st_ref[h] for h in range(HG_HEADS)],
                    [dec_cols[h * HG_DK:(h + 1) * HG_DK, i:i + 1] for h in range(HG_HEADS)])
                for h in range(HG_HEADS):
                    st_ref[h] = new_states[h]
                o_ref[c * GLA_CHUNK:(c + 1) * GLA_CHUNK, 0:HG_WIDTH] = o.astype(BF16)
            kv_cur, kv_prev = kv_blocks(n)
            o_ref[n * WINDOW:(n + 1) * WINDOW, HG_WIDTH:] = _swa_back(
                swa, kv_cur, kv_prev, sink_ref, seq_start if n == 0 else None).astype(BF16)

        acts, fronts = {}, {}
        for step in MATMUL_ORDER.split():
            kind, i = step[0], step[1]
            if kind in "UD":
                if not do_mlp:
                    continue
                if kind == "U":
                    acts[i] = mlp_up(int(i))
                else:
                    mlp_down(int(i), acts.pop(i))
            elif not do_mix:
                continue
            elif step == "In":
                xn = _rms(xc_ref[...], lmix_ref[...]).astype(BF16)
                z_ref[...] = jnp.dot(xn, win_ref[...], preferred_element_type=F32)
            elif kind == "F":
                fronts[i] = mix_front(int(i))
            elif kind == "B":
                mix_back(int(i), fronts.pop(i))
            elif step == "sS":
                fronts[step] = _sample_front_swa(sample_refs)
            elif step == "sG":
                fronts[step] = _sample_front_gla(sample_refs, lb)
            else:
                assert step == "sB", step
                _sample_back((fronts.pop("sG"), fronts.pop("sS")), sample_refs, sink_ref, g_norm)
        assert not acts and not fronts
        if do_mix:
            kvp_ref[...] = z_ref[tm - WINDOW:tm, KV_COLS]
            h_ref[...] = xc_ref[...] + jnp.dot(o_ref[...], wo_ref[...], preferred_element_type=F32)
        if do_mlp:
            y_ref[...] = _rms(acc_ref[...], lfin_ref[...])

    pl.when((s > 0) & (s < n_tiles))(functools.partial(run, True, True))
    pl.when(s == 0)(functools.partial(run, False, True))
    pl.when(s == n_tiles)(functools.partial(run, True, False))

    @pl.when((pos == tiles_per_seq - 1) & (s < n_tiles))
    def _seq_end():
        for h in range(HG_HEADS):
            so_ref[0, h] = st_ref[h]
        ko_ref[0] = z_ref[tm - WINDOW:tm, N_HG_IN + SWA_WIDTH:N_HG_IN + SWA_WIDTH + SWA_KV_WIDTH].T
        vo_ref[0] = z_ref[tm - WINDOW:tm, N_HG_IN + SWA_WIDTH + SWA_KV_WIDTH:N_IN].T


def _fused_layer(x2d, bsz, sinks, ln_mix, w_in, lb_logits, hg_norm, w_out, ln_mlp, w_up, w_down,
                 ln_final, z_s, s0, ck, cv):
    n = x2d.shape[0]
    tm = DENSE_TILE
    n_tiles = n // tm
    tiles_per_seq = n_tiles // bsz
    assert tiles_per_seq * bsz * tm == n
    dbsz, s_new, _ = z_s.shape
    wb = ck.shape[2]
    nb = pl.cdiv(dbsz, n_tiles)
    assert dbsz % nb == 0 and nb <= HG_DK
    n_sblk = dbsz // nb
    const = lambda s: (0, 0)
    single = pl.Buffered(1)
    seq_of = lambda s: jnp.minimum(s, n_tiles - 1) // tiles_per_seq
    srow3 = lambda s: (jnp.minimum(s, n_sblk - 1), 0, 0)
    srow4 = lambda s: (jnp.minimum(s, n_sblk - 1), 0, 0, 0)
    state_spec = pl.BlockSpec((nb, HG_HEADS, HG_DK, HG_DV), srow4)
    cache_spec = pl.BlockSpec((nb, SWA_KV_WIDTH, wb), srow3)
    return pl.pallas_call(
        functools.partial(_prompt_body, tiles_per_seq=tiles_per_seq, n_tiles=n_tiles),
        grid=(n_tiles + 1,),
        in_specs=[
            pl.BlockSpec(memory_space=pltpu.SMEM),
            pl.BlockSpec((tm, D_MODEL), lambda s: (jnp.minimum(s, n_tiles - 1), 0)),
            pl.BlockSpec((1, D_MODEL), const),
            pl.BlockSpec((D_MODEL, N_IN), const, pipeline_mode=single),
            pl.BlockSpec(lb_logits.shape, const),
            pl.BlockSpec((1, HG_DV), const),
            pl.BlockSpec((D_MODEL, D_MODEL), const, pipeline_mode=single),
            pl.BlockSpec((1, D_MODEL), const),
            pl.BlockSpec((D_MODEL, D_FF), const, pipeline_mode=single),
            pl.BlockSpec((D_FF, D_MODEL), const, pipeline_mode=single),
            pl.BlockSpec((1, D_MODEL), const),
            pl.BlockSpec((nb, s_new, N_IN), srow3),
            state_spec,
            cache_spec,
            cache_spec,
        ],
        out_specs=[
            pl.BlockSpec((tm, D_MODEL), lambda s: (jnp.maximum(s - 1, 0), 0)),
            pl.BlockSpec((1, HG_HEADS, HG_DK, HG_DV), lambda s: (seq_of(s), 0, 0, 0)),
            pl.BlockSpec((1, WINDOW, SWA_KV_WIDTH), lambda s: (seq_of(s), 0, 0)),
            pl.BlockSpec((1, WINDOW, SWA_KV_WIDTH), lambda s: (seq_of(s), 0, 0)),
            pl.BlockSpec((nb, s_new, D_MODEL), srow3),
            state_spec,
            cache_spec,
            cache_spec,
        ],
        out_shape=[
            jax.ShapeDtypeStruct((n, D_MODEL), F32),
            jax.ShapeDtypeStruct((bsz, HG_HEADS, HG_DK, HG_DV), F32),
            jax.ShapeDtypeStruct((bsz, WINDOW, SWA_KV_WIDTH), F32),
            jax.ShapeDtypeStruct((bsz, WINDOW, SWA_KV_WIDTH), F32),
            jax.ShapeDtypeStruct((dbsz, s_new, D_MODEL), F32),
            jax.ShapeDtypeStruct(s0.shape, s0.dtype),
            jax.ShapeDtypeStruct(ck.shape, ck.dtype),
            jax.ShapeDtypeStruct(cv.shape, cv.dtype),
        ],
        scratch_shapes=[
            pltpu.VMEM((tm, N_IN), F32),
            pltpu.VMEM((tm, D_MODEL), BF16),
            pltpu.VMEM((HG_HEADS, HG_DK, HG_DV), F32),
            pltpu.VMEM((WINDOW, 2 * SWA_KV_WIDTH), F32),
            pltpu.VMEM((tm, D_MODEL), F32),
            pltpu.VMEM((tm, D_MODEL), F32),
        ],
        compiler_params=pltpu.CompilerParams(
            dimension_semantics=("arbitrary",), vmem_limit_bytes=VMEM_LIMIT_BYTES),
        name="fused_layer",
    )(sinks, x2d, ln_mix, w_in, lb_logits, hg_norm, w_out, ln_mlp, w_up, w_down, ln_final,
      z_s, s0, ck, cv)


CAST_STEPS = 8


def _prep_body(x_ref, g_ref, win_ref, wo_ref, wu_ref, wd_ref,
               z_ref, win_o, wo_o, wu_o, wd_o, win_scr, *, n_in, n_tiles):
    j = pl.program_id(0)
    wo_o[...] = wo_ref[...].astype(BF16)
    wu_o[...] = wu_ref[...].astype(BF16)
    wd_o[...] = wd_ref[...].astype(BF16)

    @pl.when(j < n_in)
    def _cast_w_in():
        rows = win_ref.shape[0]
        w = win_ref[...].astype(BF16)
        win_o[...] = w
        win_scr[pl.ds(pl.multiple_of(j * rows, rows), rows), :] = w

    @pl.when((j >= n_in) & (j < n_in + n_tiles))
    def _project():
        xn = _rms(x_ref[...], g_ref[...]).astype(BF16)
        z_ref[...] = jnp.dot(xn, win_scr[...], preferred_element_type=F32)


def _prep(x2d, ln, w_in, w_out, w_up, w_down):
    n = x2d.shape[0]
    tm = DENSE_TILE
    nc, n_in, n_tiles = CAST_STEPS, 2, n // tm
    assert n_in + n_tiles <= nc and w_in.shape[0] % (8 * n_in) == 0
    assert all(w.shape[0] % (8 * nc) == 0 for w in (w_out, w_up, w_down))
    in_chunk = lambda s: (jnp.minimum(s, n_in - 1), 0)
    tile = lambda s: (jnp.clip(s - n_in, 0, n_tiles - 1), 0)
    w_in_spec = pl.BlockSpec((w_in.shape[0] // n_in, w_in.shape[1]), in_chunk)
    w_specs = [pl.BlockSpec((w.shape[0] // nc, w.shape[1]), lambda s: (s, 0)) for w in (w_out, w_up, w_down)]
    return pl.pallas_call(
        functools.partial(_prep_body, n_in=n_in, n_tiles=n_tiles),
        grid=(nc,),
        in_specs=[pl.BlockSpec((tm, D_MODEL), tile), pl.BlockSpec((1, D_MODEL), lambda s: (0, 0)), w_in_spec]
        + w_specs,
        out_specs=[pl.BlockSpec((tm, N_IN), tile), w_in_spec] + w_specs,
        out_shape=[jax.ShapeDtypeStruct((n, N_IN), F32)]
        + [jax.ShapeDtypeStruct(w.shape, BF16) for w in (w_in, w_out, w_up, w_down)],
        scratch_shapes=[pltpu.VMEM(w_in.shape, BF16)],
        compiler_params=pltpu.CompilerParams(
            dimension_semantics=("arbitrary",), vmem_limit_bytes=VMEM_LIMIT_BYTES),
        name="prep",
    )(x2d, ln, w_in, w_out, w_up, w_down)


def _out_mlp_body(x_ref, o_ref, wo_ref, lm_ref, wu_ref, wd_ref, lf_ref, y_ref, hn_ref, acc_ref):
    c = pl.program_id(0)

    @pl.when(c == 0)
    def _out_proj():
        h = x_ref[...] + jnp.dot(o_ref[...].astype(BF16), wo_ref[...], preferred_element_type=F32)
        acc_ref[...] = h
        hn_ref[...] = _rms(h, lm_ref[...]).astype(BF16)

    u = jnp.dot(hn_ref[...], wu_ref[...], preferred_element_type=F32)
    a = jnp.square(jnp.maximum(u, 0.0)).astype(BF16)
    acc_ref[...] += jnp.dot(a, wd_ref[...], preferred_element_type=F32)

    @pl.when(c == pl.num_programs(0) - 1)
    def _final():
        y_ref[...] = _rms(acc_ref[...], lf_ref[...])


def _out_mlp(x2d, o2d, w_out, ln_mlp, w_up, w_down, ln_final):
    n = x2d.shape[0]
    const = lambda c: (0, 0)
    single = pl.Buffered(1)
    return pl.pallas_call(
        _out_mlp_body,
        grid=(D_FF // SAMPLE_FF_CHUNK,),
        in_specs=[
            pl.BlockSpec((n, D_MODEL), const, pipeline_mode=single),
            pl.BlockSpec((n, D_MODEL), const, pipeline_mode=single),
            pl.BlockSpec((D_MODEL, D_MODEL), const, pipeline_mode=single),
            pl.BlockSpec((1, D_MODEL), const),
            pl.BlockSpec((D_MODEL, SAMPLE_FF_CHUNK), lambda c: (0, c)),
            pl.BlockSpec((SAMPLE_FF_CHUNK, D_MODEL), lambda c: (c, 0)),
            pl.BlockSpec((1, D_MODEL), const),
        ],
        out_specs=pl.BlockSpec((n, D_MODEL), const),
        out_shape=jax.ShapeDtypeStruct((n, D_MODEL), F32),
        scratch_shapes=[pltpu.VMEM((n, D_MODEL), BF16), pltpu.VMEM((n, D_MODEL), F32)],
        compiler_params=pltpu.CompilerParams(
            dimension_semantics=("arbitrary",), vmem_limit_bytes=VMEM_LIMIT_BYTES),
        name="out_mlp",
    )(x2d, o2d, w_out, ln_mlp, w_up, w_down, ln_final)


def kernel(x_prompt, x_sample, state_hgrn, cache_swa_k, cache_swa_v, ln_mix, w_in, lb_logits,
           hg_norm, sinks, w_out, ln_mlp, w_up, w_down, ln_final):
    depth = w_in.shape[0]
    assert depth == 1 and lb_logits.shape[0] == depth + 1
    bsz, seq, _ = x_prompt.shape
    dbsz, dseq, _ = x_sample.shape
    assert seq % DENSE_TILE == 0 and seq >= WINDOW and (dbsz * dseq) % DENSE_TILE == 0

    ln_mix2 = ln_mix[0].reshape(1, D_MODEL)
    ln_mlp2 = ln_mlp[0].reshape(1, D_MODEL)
    ln_fin2 = ln_final.reshape(1, D_MODEL)
    gn2 = hg_norm[0].reshape(1, HG_DV)
    sink1 = sinks[0]

    def feature_major(c):
        return jnp.transpose(c, (0, 2, 3, 1)).reshape(c.shape[0], SWA_KV_WIDTH, c.shape[1])

    def position_major(c):
        return jnp.transpose(c.reshape(c.shape[0], SWA_KV_HEADS, SWA_HEAD_DIM, c.shape[2]), (0, 3, 1, 2))

    xs = x_sample.reshape(dbsz * dseq, D_MODEL)
    z_s, w_in_b, w_out_b, w_up_b, w_down_b = _prep(xs, ln_mix2, w_in[0], w_out[0], w_up[0], w_down[0])
    xp = x_prompt.reshape(bsz * seq, D_MODEL)
    y_p, s_p, k_p, v_p, o_s, s_s, ck_s, cv_s = _fused_layer(
        xp, bsz, sink1, ln_mix2, w_in_b, lb_logits, gn2, w_out_b, ln_mlp2, w_up_b, w_down_b, ln_fin2,
        z_s.reshape(dbsz, dseq, N_IN),
        state_hgrn[0], feature_major(cache_swa_k[0]), feature_major(cache_swa_v[0]))
    y_s = _out_mlp(xs, o_s.reshape(-1, D_MODEL), w_out_b, ln_mlp2,
                   w_up_b, w_down_b, ln_fin2)

    return (y_p.reshape(bsz, seq, D_MODEL),
            y_s.reshape(dbsz, dseq, D_MODEL),
            s_p[None],
            position_major(k_p)[None].astype(cache_swa_k.dtype),
            position_major(v_p)[None].astype(cache_swa_v.dtype),
            s_s[None],
            position_major(ck_s)[None],
            position_major(cv_s)[None])
```

```python
import functools

import jax
import jax.numpy as jnp
from jax import lax
from jax.experimental import pallas as pl
from jax.experimental.pallas import tpu as pltpu

F32 = jnp.float32
BF16 = jnp.bfloat16

D_MODEL = 1024
HG_WIDTH = 512
HG_HEADS = 4
HG_DK = 128
HG_DV = 128
SWA_WIDTH = 512
SWA_HEAD_DIM = 64
SWA_Q_HEADS = 8
SWA_KV_HEADS = 2
SWA_GROUP = SWA_Q_HEADS // SWA_KV_HEADS
SWA_KV_WIDTH = SWA_KV_HEADS * SWA_HEAD_DIM
WINDOW = 128
SWA_SCALE = SWA_HEAD_DIM ** -0.5
D_FF = 4 * D_MODEL
EPS = 1e-6
N_HG_IN = 4 * HG_WIDTH
N_SWA_IN = SWA_WIDTH + 2 * SWA_KV_WIDTH
N_IN = N_HG_IN + N_SWA_IN
KV_COLS = slice(N_HG_IN + SWA_WIDTH, N_IN)
NEG_BIG = -1e30

V7X_VMEM_BYTES = 64 * 1024 * 1024
VMEM_LIMIT_BYTES = V7X_VMEM_BYTES - 4 * 1024 * 1024
DENSE_TILE = 512
GLA_CHUNK = 64
FF_CHUNK = 1024
SAMPLE_FF_CHUNK = 1024
MATMUL_ORDER = "sS In sG U0 F0 D0 sB B0 F1 U1 B1 F2 D1 B2 F3 U2 B3 D2 U3 D3"


def _rms(x, g):
    return x * lax.rsqrt(jnp.mean(x * x, axis=-1, keepdims=True) + EPS) * g


def _sigmoid(x):
    return 1.0 / (1.0 + jnp.exp(-x))


def _dot(a, b):
    return jnp.dot(a.astype(BF16), b.astype(BF16), preferred_element_type=F32)


def _dot_nt(a, b):
    return lax.dot_general(a.astype(BF16), b.astype(BF16), (((1,), (1,)), ((), ())),
                           preferred_element_type=F32)


def _dot_tn(a, b):
    return lax.dot_general(a.astype(BF16), b.astype(BF16), (((0,), (0,)), ((), ())),
                           preferred_element_type=F32)


def _log2(n):
    assert n > 0 and n & (n - 1) == 0, n
    return n.bit_length() - 1


def _cumsum_rows(x):
    n = x.shape[0]
    row = lax.broadcasted_iota(jnp.int32, x.shape, 0)
    s = 1
    while s < n:
        x = x + jnp.where(row >= s, pltpu.roll(x, s, axis=0), 0.0)
        s *= 2
    return x


def _lower_bound(lb_logits):
    m = jnp.max(lb_logits, axis=0, keepdims=True)
    e = jnp.exp(lb_logits - m)
    return e[0:1, :] / jnp.sum(e, axis=0, keepdims=True)


def _gla_front(zh, lb):
    chunk = zh.shape[0]
    zq = zh[:, 0:HG_WIDTH]
    zf = zh[:, HG_WIDTH:2 * HG_WIDTH]
    v = zh[:, 2 * HG_WIDTH:3 * HG_WIDTH].astype(BF16)
    zg = zh[:, 3 * HG_WIDTH:4 * HG_WIDTH]

    q = zq * _sigmoid(zq)
    f = lb + (1.0 - lb) * _sigmoid(zf)
    k = 1.0 - f
    b = _cumsum_rows(jnp.log(f))
    mid = chunk // 2 - 1
    b_mid = b[mid:mid + 1, :]
    b_last = b[chunk - 1:chunk, :]
    qf = q * jnp.exp(b - b_mid)
    kf = k * jnp.exp(b_mid - b)
    qt = q * jnp.exp(b)
    ke = k * jnp.exp(b_last - b)
    dec = jnp.exp(b_last)
    gate = zg * _sigmoid(zg)
    heads = [slice(h * HG_DK, (h + 1) * HG_DK) for h in range(HG_HEADS)]
    scores = [_dot_nt(qf[:, sl], kf[:, sl]) for sl in heads]
    kv = [_dot_tn(ke[:, sl], v[:, sl]) for sl in heads]
    return scores, kv, qt.astype(BF16), v, dec, gate


def _decay_columns(dec_rows):
    pad = jnp.zeros((HG_DK - len(dec_rows), HG_WIDTH), F32)
    return jnp.concatenate(list(dec_rows) + [pad], axis=0).T


def _gla_back(front, g_norm, states, decays):
    scores, kv, qt, v, _, gate = front
    chunk = qt.shape[0]
    row = lax.broadcasted_iota(jnp.int32, (chunk, chunk), 0)
    col = lax.broadcasted_iota(jnp.int32, (chunk, chunk), 1)
    causal = row >= col
    outs, new_states = [], []
    for h in range(HG_HEADS):
        sl = slice(h * HG_DK, (h + 1) * HG_DK)
        a = jnp.where(causal, scores[h], 0.0)
        o = _dot(a, v[:, sl]) + _dot(qt[:, sl], states[h])
        new_states.append(states[h] * decays[h] + kv[h])
        outs.append(_rms(o, g_norm) * gate[:, sl])
    return jnp.concatenate(outs, axis=-1), new_states


def _stack_heads(x, kh):
    return jnp.concatenate(
        [x[:, (kh * SWA_GROUP + g) * SWA_HEAD_DIM:(kh * SWA_GROUP + g + 1) * SWA_HEAD_DIM]
         for g in range(SWA_GROUP)], axis=0)


def _sink_column(sink_ref, kh, rows_per_head):
    r = lax.broadcasted_iota(jnp.int32, (SWA_GROUP * rows_per_head, 1), 0)
    col = jnp.full(r.shape, sink_ref[kh * SWA_GROUP], F32)
    for g in range(1, SWA_GROUP):
        col = jnp.where(r >= g * rows_per_head, sink_ref[kh * SWA_GROUP + g], col)
    return col


def _swa_front(zq, kv_cur, kv_prev):
    scores = []
    for kh in range(SWA_KV_HEADS):
        ks = slice(kh * SWA_HEAD_DIM, (kh + 1) * SWA_HEAD_DIM)
        q = (_stack_heads(zq, kh) * SWA_SCALE).astype(BF16)
        scores.append((_dot_nt(q, kv_prev[:, ks]), _dot_nt(q, kv_cur[:, ks])))
    return scores


def _swa_back(scores, kv_cur, kv_prev, sink_ref, no_prev):
    rows = SWA_GROUP * WINDOW
    i = lax.broadcasted_iota(jnp.int32, (rows, WINDOW), 0) & (WINDOW - 1)
    j = lax.broadcasted_iota(jnp.int32, (rows, WINDOW), 1)
    use_prev = j > i
    outs = []
    for kh in range(SWA_KV_HEADS):
        vs = slice(SWA_KV_WIDTH + kh * SWA_HEAD_DIM, SWA_KV_WIDTH + (kh + 1) * SWA_HEAD_DIM)
        sink = _sink_column(sink_ref, kh, WINDOW)
        s_prev, s_cur = scores[kh]
        if no_prev is not None:
            s_prev = jnp.where(no_prev, NEG_BIG, s_prev)
        s = jnp.where(use_prev, s_prev, s_cur)
        m = jnp.maximum(jnp.max(s, axis=-1, keepdims=True), sink)
        p = jnp.exp(s - m)
        den = jnp.sum(p, axis=-1, keepdims=True) + jnp.exp(sink - m)
        o = (_dot(jnp.where(use_prev, p, 0.0), kv_prev[:, vs])
             + _dot(jnp.where(use_prev, 0.0, p), kv_cur[:, vs])) / den
        outs.extend(o[g * WINDOW:(g + 1) * WINDOW] for g in range(SWA_GROUP))
    return jnp.concatenate(outs, axis=-1)


def _swa_sample_front(z_ref, ck_ref, cv_ref, cko_ref, cvo_ref, nb, s_new, wb):
    nq = nb * s_new
    z = z_ref[:, :, N_HG_IN:].reshape(nq, N_SWA_IN)
    k_new = z[:, SWA_WIDTH:SWA_WIDTH + SWA_KV_WIDTH]
    v_new = z[:, SWA_WIDTH + SWA_KV_WIDTH:]
    assert nq <= SWA_KV_WIDTH and wb == SWA_KV_WIDTH
    pad = jnp.zeros((SWA_KV_WIDTH - nq, SWA_KV_WIDTH), F32)
    lane = lax.broadcasted_iota(jnp.int32, (SWA_KV_WIDTH, wb), 1)
    for new, c_ref, co_ref in ((k_new, ck_ref, cko_ref), (v_new, cv_ref, cvo_ref)):
        new_t = jnp.concatenate([new, pad], axis=0).T
        for b in range(nb):
            kept = pltpu.roll(c_ref[b], wb - s_new, axis=1)
            fresh = pltpu.roll(new_t, (wb - s_new - b * s_new) % wb, axis=1)
            co_ref[b] = jnp.where(lane >= wb - s_new, fresh, kept)
    scores = []
    for kh in range(SWA_KV_HEADS):
        cs = slice(kh * SWA_HEAD_DIM, (kh + 1) * SWA_HEAD_DIM)
        kt = jnp.concatenate([ck_ref[b, cs, :] for b in range(nb)], axis=1)
        q = (_stack_heads(z, kh) * SWA_SCALE).astype(BF16)
        scores.append((_dot(q, kt), _dot_nt(q, k_new[:, cs])))
    return scores, v_new.astype(BF16)


def _swa_sample_back(front, sink_ref, cv_ref, o_ref, nb, s_new, wb):
    scores, v_new = front
    nq = nb * s_new
    rows = SWA_GROUP * nq
    ls, lw = _log2(s_new), _log2(wb)
    _log2(nb)
    r = lax.broadcasted_iota(jnp.int32, (rows, nb * wb), 0)
    c = lax.broadcasted_iota(jnp.int32, (rows, nb * wb), 1)
    mask_c = (((c >> lw) == ((r >> ls) & (nb - 1)))
              & ((c & (wb - 1)) > (r & (s_new - 1)) + (wb - WINDOW)))
    r = lax.broadcasted_iota(jnp.int32, (rows, nq), 0)
    c = lax.broadcasted_iota(jnp.int32, (rows, nq), 1)
    mask_n = ((c >> ls) == ((r >> ls) & (nb - 1))) & ((c & (s_new - 1)) <= (r & (s_new - 1)))
    for kh in range(SWA_KV_HEADS):
        cs = slice(kh * SWA_HEAD_DIM, (kh + 1) * SWA_HEAD_DIM)
        vt = jnp.concatenate([cv_ref[b, cs, :] for b in range(nb)], axis=1)
        sink = _sink_column(sink_ref, kh, nq)
        sc = jnp.where(mask_c, scores[kh][0], NEG_BIG)
        sn = jnp.where(mask_n, scores[kh][1], NEG_BIG)
        m = jnp.maximum(jnp.maximum(jnp.max(sc, axis=-1, keepdims=True),
                                    jnp.max(sn, axis=-1, keepdims=True)), sink)
        pc = jnp.where(mask_c, jnp.exp(sc - m), 0.0)
        pn = jnp.where(mask_n, jnp.exp(sn - m), 0.0)
        den = (jnp.sum(pc, axis=-1, keepdims=True) + jnp.sum(pn, axis=-1, keepdims=True)
               + jnp.exp(sink - m))
        o = (_dot_nt(pc, vt) + _dot(pn, v_new[:, cs])) / den
        for g in range(SWA_GROUP):
            c0 = HG_WIDTH + (kh * SWA_GROUP + g) * SWA_HEAD_DIM
            o_ref[:, :, c0:c0 + SWA_HEAD_DIM] = o[g * nq:(g + 1) * nq].reshape(nb, s_new, SWA_HEAD_DIM)


def _sample_front_swa(refs):
    z_ref, _, ck_ref, cv_ref, _, _, cko_ref, cvo_ref = refs
    nb, s_new, _ = z_ref.shape
    return _swa_sample_front(z_ref, ck_ref, cv_ref, cko_ref, cvo_ref, nb, s_new, ck_ref.shape[2])


def _sample_front_gla(refs, lb):
    z_ref = refs[0]
    return [_gla_front(z_ref[j, :, 0:N_HG_IN], lb) for j in range(z_ref.shape[0])]


def _sample_back(front, refs, sink_ref, g_norm):
    z_ref, s0_ref, ck_ref, cv_ref, o_ref, so_ref, _, _ = refs
    nb, s_new, _ = z_ref.shape
    gla, swa = front
    dec_cols = _decay_columns([f[4] for f in gla])
    for j in range(nb):
        o, new_states = _gla_back(
            gla[j], g_norm, [s0_ref[j, h] for h in range(HG_HEADS)],
            [dec_cols[h * HG_DK:(h + 1) * HG_DK, j:j + 1] for h in range(HG_HEADS)])
        o_ref[j, :, 0:HG_WIDTH] = o
        for h in range(HG_HEADS):
            so_ref[j, h] = new_states[h]
    _swa_sample_back(swa, sink_ref, cv_ref, o_ref, nb, s_new, ck_ref.shape[2])


def _prompt_body(sink_ref, xc_ref, lmix_ref, win_ref, lbl_ref, gn_ref, wo_ref, lmlp_ref,
                 wu_ref, wd_ref, lfin_ref, zs_ref, s0_ref, cks_ref, cvs_ref,
                 y_ref, so_ref, ko_ref, vo_ref, os_ref, sso_ref, ckso_ref, cvso_ref,
                 z_ref, o_ref, st_ref, kvp_ref, h_ref, acc_ref, *, tiles_per_seq, n_tiles):
    sample_refs = (zs_ref, s0_ref, cks_ref, cvs_ref, os_ref, sso_ref, ckso_ref, cvso_ref)
    s = pl.program_id(0)
    pos = s % tiles_per_seq
    seq_start = pos == 0
    tm = DENSE_TILE

    @pl.when(seq_start)
    def _reset():
        st_ref[...] = jnp.zeros(st_ref.shape, st_ref.dtype)
        kvp_ref[...] = jnp.zeros(kvp_ref.shape, kvp_ref.dtype)

    n_ff = D_FF // FF_CHUNK
    n_blk = tm // WINDOW
    chunks_per_blk = WINDOW // GLA_CHUNK
    assert n_ff == 4 and n_blk == 4, "MATMUL_ORDER is written for 4 MLP slices and 4 mixer slices"

    def run(do_mlp, do_mix):
        if do_mlp:
            hn = _rms(h_ref[...], lmlp_ref[...]).astype(BF16)
        if do_mix:
            lb = _lower_bound(lbl_ref[...])
            g_norm = gn_ref[...]

        def mlp_up(c):
            cs = slice(c * FF_CHUNK, (c + 1) * FF_CHUNK)
            u = jnp.dot(hn, wu_ref[:, cs], preferred_element_type=F32)
            return jnp.square(jnp.maximum(u, 0.0)).astype(BF16)

        def mlp_down(c, a):
            cs = slice(c * FF_CHUNK, (c + 1) * FF_CHUNK)
            base = h_ref if c == 0 else acc_ref
            acc_ref[...] = base[...] + jnp.dot(a, wd_ref[cs, :], preferred_element_type=F32)

        def kv_blocks(n):
            kv_prev = kvp_ref[...] if n == 0 else z_ref[(n - 1) * WINDOW:n * WINDOW, KV_COLS]
            return z_ref[n * WINDOW:(n + 1) * WINDOW, KV_COLS], kv_prev

        def mix_front(n):
            gla = [_gla_front(z_ref[c * GLA_CHUNK:(c + 1) * GLA_CHUNK, 0:N_HG_IN], lb)
                   for c in range(n * chunks_per_blk, (n + 1) * chunks_per_blk)]
            kv_cur, kv_prev = kv_blocks(n)
            swa = _swa_front(z_ref[n * WINDOW:(n + 1) * WINDOW, N_HG_IN:N_HG_IN + SWA_WIDTH],
                             kv_cur, kv_prev)
            return gla, swa

        def mix_back(n, front):
            gla, swa = front
            dec_cols = _decay_columns([g[4] for g in gla])
            for i, c in enumerate(range(n * chunks_per_blk, (n + 1) * chunks_per_blk)):
                o, new_states = _gla_back(
                    gla[i], g_norm, [st_ref[h] for h in range(HG_HEADS)],
                    [dec_cols[h * HG_DK:(h + 1) * HG_DK, i:i + 1] for h in range(HG_HEADS)])
                for h in range(HG_HEADS):
                    st_ref[h] = new_states[h]
                o_ref[c * GLA_CHUNK:(c + 1) * GLA_CHUNK, 0:HG_WIDTH] = o.astype(BF16)
            kv_cur, kv_prev = kv_blocks(n)
            o_ref[n * WINDOW:(n + 1) * WINDOW, HG_WIDTH:] = _swa_back(
                swa, kv_cur, kv_prev, sink_ref, seq_start if n == 0 else None).astype(BF16)

        acts, fronts = {}, {}
        for step in MATMUL_ORDER.split():
            kind, i = step[0], step[1]
            if kind in "UD":
                if not do_mlp:
                    continue
                if kind == "U":
                    acts[i] = mlp_up(int(i))
                else:
                    mlp_down(int(i), acts.pop(i))
            elif not do_mix:
                continue
            elif step == "In":
                xn = _rms(xc_ref[...], lmix_ref[...]).astype(BF16)
                z_ref[...] = jnp.dot(xn, win_ref[...], preferred_element_type=F32)
            elif kind == "F":
                fronts[i] = mix_front(int(i))
            elif kind == "B":
                mix_back(int(i), fronts.pop(i))
            elif step == "sS":
                fronts[step] = _sample_front_swa(sample_refs)
            elif step == "sG":
                fronts[step] = _sample_front_gla(sample_refs, lb)
            else:
                assert step == "sB", step
                _sample_back((fronts.pop("sG"), fronts.pop("sS")), sample_refs, sink_ref, g_norm)
        assert not acts and not fronts
        if do_mix:
            kvp_ref[...] = z_ref[tm - WINDOW:tm, KV_COLS]
            h_ref[...] = xc_ref[...] + jnp.dot(o_ref[...], wo_ref[...], preferred_element_type=F32)
        if do_mlp:
            y_ref[...] = _rms(acc_ref[...], lfin_ref[...])

    pl.when((s > 0) & (s < n_tiles))(functools.partial(run, True, True))
    pl.when(s == 0)(functools.partial(run, False, True))
    pl.when(s == n_tiles)(functools.partial(run, True, False))

    @pl.when((pos == tiles_per_seq - 1) & (s < n_tiles))
    def _seq_end():
        for h in range(HG_HEADS):
            so_ref[0, h] = st_ref[h]
        ko_ref[0] = z_ref[tm - WINDOW:tm, N_HG_IN + SWA_WIDTH:N_HG_IN + SWA_WIDTH + SWA_KV_WIDTH].T
        vo_ref[0] = z_ref[tm - WINDOW:tm, N_HG_IN + SWA_WIDTH + SWA_KV_WIDTH:N_IN].T


def _fused_layer(x2d, bsz, sinks, ln_mix, w_in, lb_logits, hg_norm, w_out, ln_mlp, w_up, w_down,
                 ln_final, z_s, s0, ck, cv):
    n = x2d.shape[0]
    tm = DENSE_TILE
    n_tiles = n // tm
    tiles_per_seq = n_tiles // bsz
    assert tiles_per_seq * bsz * tm == n
    dbsz, s_new, _ = z_s.shape
    wb = ck.shape[2]
    nb = pl.cdiv(dbsz, n_tiles)
    assert dbsz % nb == 0 and nb <= HG_DK
    n_sblk = dbsz // nb
    const = lambda s: (0, 0)
    single = pl.Buffered(1)
    seq_of = lambda s: jnp.minimum(s, n_tiles - 1) // tiles_per_seq
    srow3 = lambda s: (jnp.minimum(s, n_sblk - 1), 0, 0)
    srow4 = lambda s: (jnp.minimum(s, n_sblk - 1), 0, 0, 0)
    state_spec = pl.BlockSpec((nb, HG_HEADS, HG_DK, HG_DV), srow4)
    cache_spec = pl.BlockSpec((nb, SWA_KV_WIDTH, wb), srow3)
    return pl.pallas_call(
        functools.partial(_prompt_body, tiles_per_seq=tiles_per_seq, n_tiles=n_tiles),
        grid=(n_tiles + 1,),
        in_specs=[
            pl.BlockSpec(memory_space=pltpu.SMEM),
            pl.BlockSpec((tm, D_MODEL), lambda s: (jnp.minimum(s, n_tiles - 1), 0)),
            pl.BlockSpec((1, D_MODEL), const),
            pl.BlockSpec((D_MODEL, N_IN), const, pipeline_mode=single),
            pl.BlockSpec(lb_logits.shape, const),
            pl.BlockSpec((1, HG_DV), const),
            pl.BlockSpec((D_MODEL, D_MODEL), const, pipeline_mode=single),
            pl.BlockSpec((1, D_MODEL), const),
            pl.BlockSpec((D_MODEL, D_FF), const, pipeline_mode=single),
            pl.BlockSpec((D_FF, D_MODEL), const, pipeline_mode=single),
            pl.BlockSpec((1, D_MODEL), const),
            pl.BlockSpec((nb, s_new, N_IN), srow3),
            state_spec,
            cache_spec,
            cache_spec,
        ],
        out_specs=[
            pl.BlockSpec((tm, D_MODEL), lambda s: (jnp.maximum(s - 1, 0), 0)),
            pl.BlockSpec((1, HG_HEADS, HG_DK, HG_DV), lambda s: (seq_of(s), 0, 0, 0)),
            pl.BlockSpec((1, WINDOW, SWA_KV_WIDTH), lambda s: (seq_of(s), 0, 0)),
            pl.BlockSpec((1, WINDOW, SWA_KV_WIDTH), lambda s: (seq_of(s), 0, 0)),
            pl.BlockSpec((nb, s_new, D_MODEL), srow3),
            state_spec,
            cache_spec,
            cache_spec,
        ],
        out_shape=[
            jax.ShapeDtypeStruct((n, D_MODEL), F32),
            jax.ShapeDtypeStruct((bsz, HG_HEADS, HG_DK, HG_DV), F32),
            jax.ShapeDtypeStruct((bsz, WINDOW, SWA_KV_WIDTH), F32),
            jax.ShapeDtypeStruct((bsz, WINDOW, SWA_KV_WIDTH), F32),
            jax.ShapeDtypeStruct((dbsz, s_new, D_MODEL), F32),
            jax.ShapeDtypeStruct(s0.shape, s0.dtype),
            jax.ShapeDtypeStruct(ck.shape, ck.dtype),
            jax.ShapeDtypeStruct(cv.shape, cv.dtype),
        ],
        scratch_shapes=[
            pltpu.VMEM((tm, N_IN), F32),
            pltpu.VMEM((tm, D_MODEL), BF16),
            pltpu.VMEM((HG_HEADS, HG_DK, HG_DV), F32),
            pltpu.VMEM((WINDOW, 2 * SWA_KV_WIDTH), F32),
            pltpu.VMEM((tm, D_MODEL), F32),
            pltpu.VMEM((tm, D_MODEL), F32),
        ],
        compiler_params=pltpu.CompilerParams(
            dimension_semantics=("arbitrary",), vmem_limit_bytes=VMEM_LIMIT_BYTES),
        name="fused_layer",
    )(sinks, x2d, ln_mix, w_in, lb_logits, hg_norm, w_out, ln_mlp, w_up, w_down, ln_final,
      z_s, s0, ck, cv)


CAST_STEPS = 8


def _prep_body(x_ref, g_ref, win_ref, wo_ref, wu_ref, wd_ref,
               z_ref, win_o, wo_o, wu_o, wd_o, win_scr, *, n_in, n_tiles):
    j = pl.program_id(0)
    wo_o[...] = wo_ref[...].astype(BF16)
    wu_o[...] = wu_ref[...].astype(BF16)
    wd_o[...] = wd_ref[...].astype(BF16)

    @pl.when(j < n_in)
    def _cast_w_in():
        rows = win_ref.shape[0]
        w = win_ref[...].astype(BF16)
        win_o[...] = w
        win_scr[pl.ds(pl.multiple_of(j * rows, rows), rows), :] = w

    @pl.when((j >= n_in) & (j < n_in + n_tiles))
    def _project():
        xn = _rms(x_ref[...], g_ref[...]).astype(BF16)
        z_ref[...] = jnp.dot(xn, win_scr[...], preferred_element_type=F32)


def _prep(x2d, ln, w_in, w_out, w_up, w_down):
    n = x2d.shape[0]
    tm = DENSE_TILE
    nc, n_in, n_tiles = CAST_STEPS, 2, n // tm
    assert n_in + n_tiles <= nc and w_in.shape[0] % (8 * n_in) == 0
    assert all(w.shape[0] % (8 * nc) == 0 for w in (w_out, w_up, w_down))
    in_chunk = lambda s: (jnp.minimum(s, n_in - 1), 0)
    tile = lambda s: (jnp.clip(s - n_in, 0, n_tiles - 1), 0)
    w_in_spec = pl.BlockSpec((w_in.shape[0] // n_in, w_in.shape[1]), in_chunk)
    w_specs = [pl.BlockSpec((w.shape[0] // nc, w.shape[1]), lambda s: (s, 0)) for w in (w_out, w_up, w_down)]
    return pl.pallas_call(
        functools.partial(_prep_body, n_in=n_in, n_tiles=n_tiles),
        grid=(nc,),
        in_specs=[pl.BlockSpec((tm, D_MODEL), tile), pl.BlockSpec((1, D_MODEL), lambda s: (0, 0)), w_in_spec]
        + w_specs,
        out_specs=[pl.BlockSpec((tm, N_IN), tile), w_in_spec] + w_specs,
        out_shape=[jax.ShapeDtypeStruct((n, N_IN), F32)]
        + [jax.ShapeDtypeStruct(w.shape, BF16) for w in (w_in, w_out, w_up, w_down)],
        scratch_shapes=[pltpu.VMEM(w_in.shape, BF16)],
        compiler_params=pltpu.CompilerParams(
            dimension_semantics=("arbitrary",), vmem_limit_bytes=VMEM_LIMIT_BYTES),
        name="prep",
    )(x2d, ln, w_in, w_out, w_up, w_down)


def _out_mlp_body(x_ref, o_ref, wo_ref, lm_ref, wu_ref, wd_ref, lf_ref, y_ref, hn_ref, acc_ref):
    c = pl.program_id(0)

    @pl.when(c == 0)
    def _out_proj():
        h = x_ref[...] + jnp.dot(o_ref[...].astype(BF16), wo_ref[...], preferred_element_type=F32)
        acc_ref[...] = h
        hn_ref[...] = _rms(h, lm_ref[...]).astype(BF16)

    u = jnp.dot(hn_ref[...], wu_ref[...], preferred_element_type=F32)
    a = jnp.square(jnp.maximum(u, 0.0)).astype(BF16)
    acc_ref[...] += jnp.dot(a, wd_ref[...], preferred_element_type=F32)

    @pl.when(c == pl.num_programs(0) - 1)
    def _final():
        y_ref[...] = _rms(acc_ref[...], lf_ref[...])


def _out_mlp(x2d, o2d, w_out, ln_mlp, w_up, w_down, ln_final):
    n = x2d.shape[0]
    const = lambda c: (0, 0)
    single = pl.Buffered(1)
    return pl.pallas_call(
        _out_mlp_body,
        grid=(D_FF // SAMPLE_FF_CHUNK,),
        in_specs=[
            pl.BlockSpec((n, D_MODEL), const, pipeline_mode=single),
            pl.BlockSpec((n, D_MODEL), const, pipeline_mode=single),
            pl.BlockSpec((D_MODEL, D_MODEL), const, pipeline_mode=single),
            pl.BlockSpec((1, D_MODEL), const),
            pl.BlockSpec((D_MODEL, SAMPLE_FF_CHUNK), lambda c: (0, c)),
            pl.BlockSpec((SAMPLE_FF_CHUNK, D_MODEL), lambda c: (c, 0)),
            pl.BlockSpec((1, D_MODEL), const),
        ],
        out_specs=pl.BlockSpec((n, D_MODEL), const),
        out_shape=jax.ShapeDtypeStruct((n, D_MODEL), F32),
        scratch_shapes=[pltpu.VMEM((n, D_MODEL), BF16), pltpu.VMEM((n, D_MODEL), F32)],
        compiler_params=pltpu.CompilerParams(
            dimension_semantics=("arbitrary",), vmem_limit_bytes=VMEM_LIMIT_BYTES),
        name="out_mlp",
    )(x2d, o2d, w_out, ln_mlp, w_up, w_down, ln_final)


def kernel(x_prompt, x_sample, state_hgrn, cache_swa_k, cache_swa_v, ln_mix, w_in, lb_logits,
           hg_norm, sinks, w_out, ln_mlp, w_up, w_down, ln_final):
    depth = w_in.shape[0]
    assert depth == 1 and lb_logits.shape[0] == depth + 1
    bsz, seq, _ = x_prompt.shape
    dbsz, dseq, _ = x_sample.shape
    assert seq % DENSE_TILE == 0 and seq >= WINDOW and (dbsz * dseq) % DENSE_TILE == 0

    ln_mix2 = ln_mix[0].reshape(1, D_MODEL)
    ln_mlp2 = ln_mlp[0].reshape(1, D_MODEL)
    ln_fin2 = ln_final.reshape(1, D_MODEL)
    gn2 = hg_norm[0].reshape(1, HG_DV)
    sink1 = sinks[0]

    def feature_major(c):
        return jnp.transpose(c, (0, 2, 3, 1)).reshape(c.shape[0], SWA_KV_WIDTH, c.shape[1])

    def position_major(c):
        return jnp.transpose(c.reshape(c.shape[0], SWA_KV_HEADS, SWA_HEAD_DIM, c.shape[2]), (0, 3, 1, 2))

    xs = x_sample.reshape(dbsz * dseq, D_MODEL)
    z_s, w_in_b, w_out_b, w_up_b, w_down_b = _prep(xs, ln_mix2, w_in[0], w_out[0], w_up[0], w_down[0])
    xp = x_prompt.reshape(bsz * seq, D_MODEL)
    y_p, s_p, k_p, v_p, o_s, s_s, ck_s, cv_s = _fused_layer(
        xp, bsz, sink1, ln_mix2, w_in_b, lb_logits, gn2, w_out_b, ln_mlp2, w_up_b, w_down_b, ln_fin2,
        z_s.reshape(dbsz, dseq, N_IN),
        state_hgrn[0], feature_major(cache_swa_k[0]), feature_major(cache_swa_v[0]))
    y_s = _out_mlp(xs, o_s.reshape(-1, D_MODEL), w_out_b, ln_mlp2,
                   w_up_b, w_down_b, ln_fin2)

    return (y_p.reshape(bsz, seq, D_MODEL),
            y_s.reshape(dbsz, dseq, D_MODEL),
            s_p[None],
            position_major(k_p)[None].astype(cache_swa_k.dtype),
            position_major(v_p)[None].astype(cache_swa_v.dtype),
            s_s[None],
            position_major(ck_s)[None],
            position_major(cv_s)[None])
```

```python
import functools

import jax
import jax.numpy as jnp
from jax import lax
from jax.experimental import pallas as pl
from jax.experimental.pallas import tpu as pltpu

F32 = jnp.float32
BF16 = jnp.bfloat16

D_MODEL = 1024
HG_WIDTH = 512
HG_HEADS = 4
HG_DK = 128
HG_DV = 128
SWA_WIDTH = 512
SWA_HEAD_DIM = 64
SWA_Q_HEADS = 8
SWA_KV_HEADS = 2
SWA_GROUP = SWA_Q_HEADS // SWA_KV_HEADS
SWA_KV_WIDTH = SWA_KV_HEADS * SWA_HEAD_DIM
WINDOW = 128
SWA_SCALE = SWA_HEAD_DIM ** -0.5
D_FF = 4 * D_MODEL
EPS = 1e-6
N_HG_IN = 4 * HG_WIDTH
N_SWA_IN = SWA_WIDTH + 2 * SWA_KV_WIDTH
N_IN = N_HG_IN + N_SWA_IN
KV_COLS = slice(N_HG_IN + SWA_WIDTH, N_IN)
NEG_BIG = -1e30

V7X_VMEM_BYTES = 64 * 1024 * 1024
VMEM_LIMIT_BYTES = V7X_VMEM_BYTES - 4 * 1024 * 1024
DENSE_TILE = 512
GLA_CHUNK = 64
FF_CHUNK = 2048
SAMPLE_FF_CHUNK = 1024
MATMUL_ORDER = "sS In sG U0 F0 F1 F2 D0 sB B0 B1 B2 F3 U1 B3 D1"


def _rms(x, g):
    return x * lax.rsqrt(jnp.mean(x * x, axis=-1, keepdims=True) + EPS) * g


def _sigmoid(x):
    return 1.0 / (1.0 + jnp.exp(-x))


def _dot(a, b):
    return jnp.dot(a.astype(BF16), b.astype(BF16), preferred_element_type=F32)


def _dot_nt(a, b):
    return lax.dot_general(a.astype(BF16), b.astype(BF16), (((1,), (1,)), ((), ())),
                           preferred_element_type=F32)


def _dot_tn(a, b):
    return lax.dot_general(a.astype(BF16), b.astype(BF16), (((0,), (0,)), ((), ())),
                           preferred_element_type=F32)


def _log2(n):
    assert n > 0 and n & (n - 1) == 0, n
    return n.bit_length() - 1


def _cumsum_rows(x):
    n = x.shape[0]
    row = lax.broadcasted_iota(jnp.int32, x.shape, 0)
    s = 1
    while s < n:
        x = x + jnp.where(row >= s, pltpu.roll(x, s, axis=0), 0.0)
        s *= 2
    return x


def _lower_bound(lb_logits):
    m = jnp.max(lb_logits, axis=0, keepdims=True)
    e = jnp.exp(lb_logits - m)
    return e[0:1, :] / jnp.sum(e, axis=0, keepdims=True)


def _gla_front(zh, lb):
    chunk = zh.shape[0]
    zq = zh[:, 0:HG_WIDTH]
    zf = zh[:, HG_WIDTH:2 * HG_WIDTH]
    v = zh[:, 2 * HG_WIDTH:3 * HG_WIDTH].astype(BF16)
    zg = zh[:, 3 * HG_WIDTH:4 * HG_WIDTH]

    q = zq * _sigmoid(zq)
    f = lb + (1.0 - lb) * _sigmoid(zf)
    k = 1.0 - f
    b = _cumsum_rows(jnp.log(f))
    mid = chunk // 2 - 1
    b_mid = b[mid:mid + 1, :]
    b_last = b[chunk - 1:chunk, :]
    qf = q * jnp.exp(b - b_mid)
    kf = k * jnp.exp(b_mid - b)
    qt = q * jnp.exp(b)
    ke = k * jnp.exp(b_last - b)
    dec = jnp.exp(b_last)
    gate = zg * _sigmoid(zg)
    heads = [slice(h * HG_DK, (h + 1) * HG_DK) for h in range(HG_HEADS)]
    scores = [_dot_nt(qf[:, sl], kf[:, sl]) for sl in heads]
    kv = [_dot_tn(ke[:, sl], v[:, sl]) for sl in heads]
    return scores, kv, qt.astype(BF16), v, dec, gate


def _decay_columns(dec_rows):
    pad = jnp.zeros((HG_DK - len(dec_rows), HG_WIDTH), F32)
    return jnp.concatenate(list(dec_rows) + [pad], axis=0).T


def _gla_back(front, g_norm, states, decays):
    scores, kv, qt, v, _, gate = front
    chunk = qt.shape[0]
    row = lax.broadcasted_iota(jnp.int32, (chunk, chunk), 0)
    col = lax.broadcasted_iota(jnp.int32, (chunk, chunk), 1)
    causal = row >= col
    outs, new_states = [], []
    for h in range(HG_HEADS):
        sl = slice(h * HG_DK, (h + 1) * HG_DK)
        a = jnp.where(causal, scores[h], 0.0)
        o = _dot(a, v[:, sl]) + _dot(qt[:, sl], states[h])
        new_states.append(states[h] * decays[h] + kv[h])
        outs.append(_rms(o, g_norm) * gate[:, sl])
    return jnp.concatenate(outs, axis=-1), new_states


def _stack_heads(x, kh):
    return jnp.concatenate(
        [x[:, (kh * SWA_GROUP + g) * SWA_HEAD_DIM:(kh * SWA_GROUP + g + 1) * SWA_HEAD_DIM]
         for g in range(SWA_GROUP)], axis=0)


def _sink_column(sink_ref, kh, rows_per_head):
    r = lax.broadcasted_iota(jnp.int32, (SWA_GROUP * rows_per_head, 1), 0)
    col = jnp.full(r.shape, sink_ref[kh * SWA_GROUP], F32)
    for g in range(1, SWA_GROUP):
        col = jnp.where(r >= g * rows_per_head, sink_ref[kh * SWA_GROUP + g], col)
    return col


def _swa_front(zq, kv_cur, kv_prev):
    scores = []
    for kh in range(SWA_KV_HEADS):
        ks = slice(kh * SWA_HEAD_DIM, (kh + 1) * SWA_HEAD_DIM)
        q = (_stack_heads(zq, kh) * SWA_SCALE).astype(BF16)
        scores.append((_dot_nt(q, kv_prev[:, ks]), _dot_nt(q, kv_cur[:, ks])))
    return scores


def _swa_back(scores, kv_cur, kv_prev, sink_ref, no_prev):
    rows = SWA_GROUP * WINDOW
    i = lax.broadcasted_iota(jnp.int32, (rows, WINDOW), 0) & (WINDOW - 1)
    j = lax.broadcasted_iota(jnp.int32, (rows, WINDOW), 1)
    use_prev = j > i
    outs = []
    for kh in range(SWA_KV_HEADS):
        vs = slice(SWA_KV_WIDTH + kh * SWA_HEAD_DIM, SWA_KV_WIDTH + (kh + 1) * SWA_HEAD_DIM)
        sink = _sink_column(sink_ref, kh, WINDOW)
        s_prev, s_cur = scores[kh]
        if no_prev is not None:
            s_prev = jnp.where(no_prev, NEG_BIG, s_prev)
        s = jnp.where(use_prev, s_prev, s_cur)
        m = jnp.maximum(jnp.max(s, axis=-1, keepdims=True), sink)
        p = jnp.exp(s - m)
        den = jnp.sum(p, axis=-1, keepdims=True) + jnp.exp(sink - m)
        o = (_dot(jnp.where(use_prev, p, 0.0), kv_prev[:, vs])
             + _dot(jnp.where(use_prev, 0.0, p), kv_cur[:, vs])) / den
        outs.extend(o[g * WINDOW:(g + 1) * WINDOW] for g in range(SWA_GROUP))
    return jnp.concatenate(outs, axis=-1)


def _swa_sample_front(z_ref, ck_ref, cv_ref, cko_ref, cvo_ref, nb, s_new, wb):
    nq = nb * s_new
    z = z_ref[:, :, N_HG_IN:].reshape(nq, N_SWA_IN)
    k_new = z[:, SWA_WIDTH:SWA_WIDTH + SWA_KV_WIDTH]
    v_new = z[:, SWA_WIDTH + SWA_KV_WIDTH:]
    assert nq <= SWA_KV_WIDTH and wb == SWA_KV_WIDTH
    pad = jnp.zeros((SWA_KV_WIDTH - nq, SWA_KV_WIDTH), F32)
    lane = lax.broadcasted_iota(jnp.int32, (SWA_KV_WIDTH, wb), 1)
    for new, c_ref, co_ref in ((k_new, ck_ref, cko_ref), (v_new, cv_ref, cvo_ref)):
        new_t = jnp.concatenate([new, pad], axis=0).T
        for b in range(nb):
            kept = pltpu.roll(c_ref[b], wb - s_new, axis=1)
            fresh = pltpu.roll(new_t, (wb - s_new - b * s_new) % wb, axis=1)
            co_ref[b] = jnp.where(lane >= wb - s_new, fresh, kept)
    scores = []
    for kh in range(SWA_KV_HEADS):
        cs = slice(kh * SWA_HEAD_DIM, (kh + 1) * SWA_HEAD_DIM)
        kt = jnp.concatenate([ck_ref[b, cs, :] for b in range(nb)], axis=1)
        q = (_stack_heads(z, kh) * SWA_SCALE).astype(BF16)
        scores.append((_dot(q, kt), _dot_nt(q, k_new[:, cs])))
    return scores, v_new.astype(BF16)


def _swa_sample_back(front, sink_ref, cv_ref, o_ref, nb, s_new, wb):
    scores, v_new = front
    nq = nb * s_new
    rows = SWA_GROUP * nq
    ls, lw = _log2(s_new), _log2(wb)
    _log2(nb)
    r = lax.broadcasted_iota(jnp.int32, (rows, nb * wb), 0)
    c = lax.broadcasted_iota(jnp.int32, (rows, nb * wb), 1)
    mask_c = (((c >> lw) == ((r >> ls) & (nb - 1)))
              & ((c & (wb - 1)) > (r & (s_new - 1)) + (wb - WINDOW)))
    r = lax.broadcasted_iota(jnp.int32, (rows, nq), 0)
    c = lax.broadcasted_iota(jnp.int32, (rows, nq), 1)
    mask_n = ((c >> ls) == ((r >> ls) & (nb - 1))) & ((c & (s_new - 1)) <= (r & (s_new - 1)))
    for kh in range(SWA_KV_HEADS):
        cs = slice(kh * SWA_HEAD_DIM, (kh + 1) * SWA_HEAD_DIM)
        vt = jnp.concatenate([cv_ref[b, cs, :] for b in range(nb)], axis=1)
        sink = _sink_column(sink_ref, kh, nq)
        sc = jnp.where(mask_c, scores[kh][0], NEG_BIG)
        sn = jnp.where(mask_n, scores[kh][1], NEG_BIG)
        m = jnp.maximum(jnp.maximum(jnp.max(sc, axis=-1, keepdims=True),
                                    jnp.max(sn, axis=-1, keepdims=True)), sink)
        pc = jnp.where(mask_c, jnp.exp(sc - m), 0.0)
        pn = jnp.where(mask_n, jnp.exp(sn - m), 0.0)
        den = (jnp.sum(pc, axis=-1, keepdims=True) + jnp.sum(pn, axis=-1, keepdims=True)
               + jnp.exp(sink - m))
        o = (_dot_nt(pc, vt) + _dot(pn, v_new[:, cs])) / den
        for g in range(SWA_GROUP):
            c0 = HG_WIDTH + (kh * SWA_GROUP + g) * SWA_HEAD_DIM
            o_ref[:, :, c0:c0 + SWA_HEAD_DIM] = o[g * nq:(g + 1) * nq].reshape(nb, s_new, SWA_HEAD_DIM)


def _sample_front_swa(refs):
    z_ref, _, ck_ref, cv_ref, _, _, cko_ref, cvo_ref = refs
    nb, s_new, _ = z_ref.shape
    return _swa_sample_front(z_ref, ck_ref, cv_ref, cko_ref, cvo_ref, nb, s_new, ck_ref.shape[2])


def _sample_front_gla(refs, lb):
    z_ref = refs[0]
    return [_gla_front(z_ref[j, :, 0:N_HG_IN], lb) for j in range(z_ref.shape[0])]


def _sample_back(front, refs, sink_ref, g_norm):
    z_ref, s0_ref, ck_ref, cv_ref, o_ref, so_ref, _, _ = refs
    nb, s_new, _ = z_ref.shape
    gla, swa = front
    dec_cols = _decay_columns([f[4] for f in gla])
    for j in range(nb):
        o, new_states = _gla_back(
            gla[j], g_norm, [s0_ref[j, h] for h in range(HG_HEADS)],
            [dec_cols[h * HG_DK:(h + 1) * HG_DK, j:j + 1] for h in range(HG_HEADS)])
        o_ref[j, :, 0:HG_WIDTH] = o
        for h in range(HG_HEADS):
            so_ref[j, h] = new_states[h]
    _swa_sample_back(swa, sink_ref, cv_ref, o_ref, nb, s_new, ck_ref.shape[2])


def _prompt_body(sink_ref, xc_ref, lmix_ref, win_ref, lbl_ref, gn_ref, wo_ref, lmlp_ref,
                 wu_ref, wd_ref, lfin_ref, zs_ref, s0_ref, cks_ref, cvs_ref,
                 y_ref, so_ref, ko_ref, vo_ref, os_ref, sso_ref, ckso_ref, cvso_ref,
                 z_ref, o_ref, st_ref, kvp_ref, h_ref, acc_ref, *, tiles_per_seq, n_tiles):
    sample_refs = (zs_ref, s0_ref, cks_ref, cvs_ref, os_ref, sso_ref, ckso_ref, cvso_ref)
    s = pl.program_id(0)
    pos = s % tiles_per_seq
    seq_start = pos == 0
    tm = DENSE_TILE

    @pl.when(seq_start)
    def _reset():
        st_ref[...] = jnp.zeros(st_ref.shape, st_ref.dtype)
        kvp_ref[...] = jnp.zeros(kvp_ref.shape, kvp_ref.dtype)

    n_ff = D_FF // FF_CHUNK
    n_blk = tm // WINDOW
    chunks_per_blk = WINDOW // GLA_CHUNK
    assert n_ff == 2 and n_blk == 4, "MATMUL_ORDER is written for 2 MLP slices and 4 mixer slices"

    def run(do_mlp, do_mix):
        if do_mlp:
            hn = _rms(h_ref[...], lmlp_ref[...]).astype(BF16)
        if do_mix:
            lb = _lower_bound(lbl_ref[...])
            g_norm = gn_ref[...]

        def mlp_up(c):
            cs = slice(c * FF_CHUNK, (c + 1) * FF_CHUNK)
            u = jnp.dot(hn, wu_ref[:, cs], preferred_element_type=F32)
            return jnp.square(jnp.maximum(u, 0.0)).astype(BF16)

        def mlp_down(c, a):
            cs = slice(c * FF_CHUNK, (c + 1) * FF_CHUNK)
            base = h_ref if c == 0 else acc_ref
            acc_ref[...] = base[...] + jnp.dot(a, wd_ref[cs, :], preferred_element_type=F32)

        def kv_blocks(n):
            kv_prev = kvp_ref[...] if n == 0 else z_ref[(n - 1) * WINDOW:n * WINDOW, KV_COLS]
            return z_ref[n * WINDOW:(n + 1) * WINDOW, KV_COLS], kv_prev

        def mix_front(n):
            gla = [_gla_front(z_ref[c * GLA_CHUNK:(c + 1) * GLA_CHUNK, 0:N_HG_IN], lb)
                   for c in range(n * chunks_per_blk, (n + 1) * chunks_per_blk)]
            kv_cur, kv_prev = kv_blocks(n)
            swa = _swa_front(z_ref[n * WINDOW:(n + 1) * WINDOW, N_HG_IN:N_HG_IN + SWA_WIDTH],
                             kv_cur, kv_prev)
            return gla, swa

        def mix_back(n, front):
            gla, swa = front
            dec_cols = _decay_columns([g[4] for g in gla])
            for i, c in enumerate(range(n * chunks_per_blk, (n + 1) * chunks_per_blk)):
                o, new_states = _gla_back(
                    gla[i], g_norm, [st_ref[h] for h in range(HG_HEADS)],
                    [dec_cols[h * HG_DK:(h + 1) * HG_DK, i:i + 1] for h in range(HG_HEADS)])
                for h in range(HG_HEADS):
                    st_ref[h] = new_states[h]
                o_ref[c * GLA_CHUNK:(c + 1) * GLA_CHUNK, 0:HG_WIDTH] = o.astype(BF16)
            kv_cur, kv_prev = kv_blocks(n)
            o_ref[n * WINDOW:(n + 1) * WINDOW, HG_WIDTH:] = _swa_back(
                swa, kv_cur, kv_prev, sink_ref, seq_start if n == 0 else None).astype(BF16)

        acts, fronts = {}, {}
        for step in MATMUL_ORDER.split():
            kind, i = step[0], step[1]
            if kind in "UD":
                if not do_mlp:
                    continue
                if kind == "U":
                    acts[i] = mlp_up(int(i))
                else:
                    mlp_down(int(i), acts.pop(i))
            elif not do_mix:
                continue
            elif step == "In":
                xn = _rms(xc_ref[...], lmix_ref[...]).astype(BF16)
                z_ref[...] = jnp.dot(xn, win_ref[...], preferred_element_type=F32)
            elif kind == "F":
                fronts[i] = mix_front(int(i))
            elif kind == "B":
                mix_back(int(i), fronts.pop(i))
            elif step == "sS":
                fronts[step] = _sample_front_swa(sample_refs)
            elif step == "sG":
                fronts[step] = _sample_front_gla(sample_refs, lb)
            else:
                assert step == "sB", step
                _sample_back((fronts.pop("sG"), fronts.pop("sS")), sample_refs, sink_ref, g_norm)
        assert not acts and not fronts
        if do_mix:
            kvp_ref[...] = z_ref[tm - WINDOW:tm, KV_COLS]
            h_ref[...] = xc_ref[...] + jnp.dot(o_ref[...], wo_ref[...], preferred_element_type=F32)
        if do_mlp:
            y_ref[...] = _rms(acc_ref[...], lfin_ref[...])

    pl.when((s > 0) & (s < n_tiles))(functools.partial(run, True, True))
    pl.when(s == 0)(functools.partial(run, False, True))
    pl.when(s == n_tiles)(functools.partial(run, True, False))

    @pl.when((pos == tiles_per_seq - 1) & (s < n_tiles))
    def _seq_end():
        for h in range(HG_HEADS):
            so_ref[0, h] = st_ref[h]
        ko_ref[0] = z_ref[tm - WINDOW:tm, N_HG_IN + SWA_WIDTH:N_HG_IN + SWA_WIDTH + SWA_KV_WIDTH].T
        vo_ref[0] = z_ref[tm - WINDOW:tm, N_HG_IN + SWA_WIDTH + SWA_KV_WIDTH:N_IN].T


def _fused_layer(x2d, bsz, sinks, ln_mix, w_in, lb_logits, hg_norm, w_out, ln_mlp, w_up, w_down,
                 ln_final, z_s, s0, ck, cv):
    n = x2d.shape[0]
    tm = DENSE_TILE
    n_tiles = n // tm
    tiles_per_seq = n_tiles // bsz
    assert tiles_per_seq * bsz * tm == n
    dbsz, s_new, _ = z_s.shape
    wb = ck.shape[2]
    nb = pl.cdiv(dbsz, n_tiles)
    assert dbsz % nb == 0 and nb <= HG_DK
    n_sblk = dbsz // nb
    const = lambda s: (0, 0)
    single = pl.Buffered(1)
    seq_of = lambda s: jnp.minimum(s, n_tiles - 1) // tiles_per_seq
    srow3 = lambda s: (jnp.minimum(s, n_sblk - 1), 0, 0)
    srow4 = lambda s: (jnp.minimum(s, n_sblk - 1), 0, 0, 0)
    state_spec = pl.BlockSpec((nb, HG_HEADS, HG_DK, HG_DV), srow4)
    cache_spec = pl.BlockSpec((nb, SWA_KV_WIDTH, wb), srow3)
    return pl.pallas_call(
        functools.partial(_prompt_body, tiles_per_seq=tiles_per_seq, n_tiles=n_tiles),
        grid=(n_tiles + 1,),
        in_specs=[
            pl.BlockSpec(memory_space=pltpu.SMEM),
            pl.BlockSpec((tm, D_MODEL), lambda s: (jnp.minimum(s, n_tiles - 1), 0)),
            pl.BlockSpec((1, D_MODEL), const),
            pl.BlockSpec((D_MODEL, N_IN), const, pipeline_mode=single),
            pl.BlockSpec(lb_logits.shape, const),
            pl.BlockSpec((1, HG_DV), const),
            pl.BlockSpec((D_MODEL, D_MODEL), const, pipeline_mode=single),
            pl.BlockSpec((1, D_MODEL), const),
            pl.BlockSpec((D_MODEL, D_FF), const, pipeline_mode=single),
            pl.BlockSpec((D_FF, D_MODEL), const, pipeline_mode=single),
            pl.BlockSpec((1, D_MODEL), const),
            pl.BlockSpec((nb, s_new, N_IN), srow3),
            state_spec,
            cache_spec,
            cache_spec,
        ],
        out_specs=[
            pl.BlockSpec((tm, D_MODEL), lambda s: (jnp.maximum(s - 1, 0), 0)),
            pl.BlockSpec((1, HG_HEADS, HG_DK, HG_DV), lambda s: (seq_of(s), 0, 0, 0)),
            pl.BlockSpec((1, WINDOW, SWA_KV_WIDTH), lambda s: (seq_of(s), 0, 0)),
            pl.BlockSpec((1, WINDOW, SWA_KV_WIDTH), lambda s: (seq_of(s), 0, 0)),
            pl.BlockSpec((nb, s_new, D_MODEL), srow3),
            state_spec,
            cache_spec,
            cache_spec,
        ],
        out_shape=[
            jax.ShapeDtypeStruct((n, D_MODEL), F32),
            jax.ShapeDtypeStruct((bsz, HG_HEADS, HG_DK, HG_DV), F32),
            jax.ShapeDtypeStruct((bsz, WINDOW, SWA_KV_WIDTH), F32),
            jax.ShapeDtypeStruct((bsz, WINDOW, SWA_KV_WIDTH), F32),
            jax.ShapeDtypeStruct((dbsz, s_new, D_MODEL), F32),
            jax.ShapeDtypeStruct(s0.shape, s0.dtype),
            jax.ShapeDtypeStruct(ck.shape, ck.dtype),
            jax.ShapeDtypeStruct(cv.shape, cv.dtype),
        ],
        scratch_shapes=[
            pltpu.VMEM((tm, N_IN), F32),
            pltpu.VMEM((tm, D_MODEL), BF16),
            pltpu.VMEM((HG_HEADS, HG_DK, HG_DV), F32),
            pltpu.VMEM((WINDOW, 2 * SWA_KV_WIDTH), F32),
            pltpu.VMEM((tm, D_MODEL), F32),
            pltpu.VMEM((tm, D_MODEL), F32),
        ],
        compiler_params=pltpu.CompilerParams(
            dimension_semantics=("arbitrary",), vmem_limit_bytes=VMEM_LIMIT_BYTES),
        name="fused_layer",
    )(sinks, x2d, ln_mix, w_in, lb_logits, hg_norm, w_out, ln_mlp, w_up, w_down, ln_final,
      z_s, s0, ck, cv)


CAST_STEPS = 8


def _prep_body(x_ref, g_ref, win_ref, wo_ref, wu_ref, wd_ref,
               z_ref, win_o, wo_o, wu_o, wd_o, win_scr, *, n_in, n_tiles):
    j = pl.program_id(0)
    wo_o[...] = wo_ref[...].astype(BF16)
    wu_o[...] = wu_ref[...].astype(BF16)
    wd_o[...] = wd_ref[...].astype(BF16)

    @pl.when(j < n_in)
    def _cast_w_in():
        rows = win_ref.shape[0]
        w = win_ref[...].astype(BF16)
        win_o[...] = w
        win_scr[pl.ds(pl.multiple_of(j * rows, rows), rows), :] = w

    @pl.when((j >= n_in) & (j < n_in + n_tiles))
    def _project():
        xn = _rms(x_ref[...], g_ref[...]).astype(BF16)
        z_ref[...] = jnp.dot(xn, win_scr[...], preferred_element_type=F32)


def _prep(x2d, ln, w_in, w_out, w_up, w_down):
    n = x2d.shape[0]
    tm = DENSE_TILE
    nc, n_in, n_tiles = CAST_STEPS, 2, n // tm
    assert n_in + n_tiles <= nc and w_in.shape[0] % (8 * n_in) == 0
    assert all(w.shape[0] % (8 * nc) == 0 for w in (w_out, w_up, w_down))
    in_chunk = lambda s: (jnp.minimum(s, n_in - 1), 0)
    tile = lambda s: (jnp.clip(s - n_in, 0, n_tiles - 1), 0)
    w_in_spec = pl.BlockSpec((w_in.shape[0] // n_in, w_in.shape[1]), in_chunk)
    w_specs = [pl.BlockSpec((w.shape[0] // nc, w.shape[1]), lambda s: (s, 0)) for w in (w_out, w_up, w_down)]
    return pl.pallas_call(
        functools.partial(_prep_body, n_in=n_in, n_tiles=n_tiles),
        grid=(nc,),
        in_specs=[pl.BlockSpec((tm, D_MODEL), tile), pl.BlockSpec((1, D_MODEL), lambda s: (0, 0)), w_in_spec]
        + w_specs,
        out_specs=[pl.BlockSpec((tm, N_IN), tile), w_in_spec] + w_specs,
        out_shape=[jax.ShapeDtypeStruct((n, N_IN), F32)]
        + [jax.ShapeDtypeStruct(w.shape, BF16) for w in (w_in, w_out, w_up, w_down)],
        scratch_shapes=[pltpu.VMEM(w_in.shape, BF16)],
        compiler_params=pltpu.CompilerParams(
            dimension_semantics=("arbitrary",), vmem_limit_bytes=VMEM_LIMIT_BYTES),
        name="prep",
    )(x2d, ln, w_in, w_out, w_up, w_down)


def _out_mlp_body(x_ref, o_ref, wo_ref, lm_ref, wu_ref, wd_ref, lf_ref, y_ref, hn_ref, acc_ref):
    c = pl.program_id(0)

    @pl.when(c == 0)
    def _out_proj():
        h = x_ref[...] + jnp.dot(o_ref[...].astype(BF16), wo_ref[...], preferred_element_type=F32)
        acc_ref[...] = h
        hn_ref[...] = _rms(h, lm_ref[...]).astype(BF16)

    u = jnp.dot(hn_ref[...], wu_ref[...], preferred_element_type=F32)
    a = jnp.square(jnp.maximum(u, 0.0)).astype(BF16)
    acc_ref[...] += jnp.dot(a, wd_ref[...], preferred_element_type=F32)

    @pl.when(c == pl.num_programs(0) - 1)
    def _final():
        y_ref[...] = _rms(acc_ref[...], lf_ref[...])


def _out_mlp(x2d, o2d, w_out, ln_mlp, w_up, w_down, ln_final):
    n = x2d.shape[0]
    const = lambda c: (0, 0)
    single = pl.Buffered(1)
    return pl.pallas_call(
        _out_mlp_body,
        grid=(D_FF // SAMPLE_FF_CHUNK,),
        in_specs=[
            pl.BlockSpec((n, D_MODEL), const, pipeline_mode=single),
            pl.BlockSpec((n, D_MODEL), const, pipeline_mode=single),
            pl.BlockSpec((D_MODEL, D_MODEL), const, pipeline_mode=single),
            pl.BlockSpec((1, D_MODEL), const),
            pl.BlockSpec((D_MODEL, SAMPLE_FF_CHUNK), lambda c: (0, c)),
            pl.BlockSpec((SAMPLE_FF_CHUNK, D_MODEL), lambda c: (c, 0)),
            pl.BlockSpec((1, D_MODEL), const),
        ],
        out_specs=pl.BlockSpec((n, D_MODEL), const),
        out_shape=jax.ShapeDtypeStruct((n, D_MODEL), F32),
        scratch_shapes=[pltpu.VMEM((n, D_MODEL), BF16), pltpu.VMEM((n, D_MODEL), F32)],
        compiler_params=pltpu.CompilerParams(
            dimension_semantics=("arbitrary",), vmem_limit_bytes=VMEM_LIMIT_BYTES),
        name="out_mlp",
    )(x2d, o2d, w_out, ln_mlp, w_up, w_down, ln_final)


def kernel(x_prompt, x_sample, state_hgrn, cache_swa_k, cache_swa_v, ln_mix, w_in, lb_logits,
           hg_norm, sinks, w_out, ln_mlp, w_up, w_down, ln_final):
    depth = w_in.shape[0]
    assert depth == 1 and lb_logits.shape[0] == depth + 1
    bsz, seq, _ = x_prompt.shape
    dbsz, dseq, _ = x_sample.shape
    assert seq % DENSE_TILE == 0 and seq >= WINDOW and (dbsz * dseq) % DENSE_TILE == 0

    ln_mix2 = ln_mix[0].reshape(1, D_MODEL)
    ln_mlp2 = ln_mlp[0].reshape(1, D_MODEL)
    ln_fin2 = ln_final.reshape(1, D_MODEL)
    gn2 = hg_norm[0].reshape(1, HG_DV)
    sink1 = sinks[0]

    def feature_major(c):
        return jnp.transpose(c, (0, 2, 3, 1)).reshape(c.shape[0], SWA_KV_WIDTH, c.shape[1])

    def position_major(c):
        return jnp.transpose(c.reshape(c.shape[0], SWA_KV_HEADS, SWA_HEAD_DIM, c.shape[2]), (0, 3, 1, 2))

    xs = x_sample.reshape(dbsz * dseq, D_MODEL)
    z_s, w_in_b, w_out_b, w_up_b, w_down_b = _prep(xs, ln_mix2, w_in[0], w_out[0], w_up[0], w_down[0])
    xp = x_prompt.reshape(bsz * seq, D_MODEL)
    y_p, s_p, k_p, v_p, o_s, s_s, ck_s, cv_s = _fused_layer(
        xp, bsz, sink1, ln_mix2, w_in_b, lb_logits, gn2, w_out_b, ln_mlp2, w_up_b, w_down_b, ln_fin2,
        z_s.reshape(dbsz, dseq, N_IN),
        state_hgrn[0], feature_major(cache_swa_k[0]), feature_major(cache_swa_v[0]))
    y_s = _out_mlp(xs, o_s.reshape(-1, D_MODEL), w_out_b, ln_mlp2,
                   w_up_b, w_down_b, ln_fin2)

    return (y_p.reshape(bsz, seq, D_MODEL),
            y_s.reshape(dbsz, dseq, D_MODEL),
            s_p[None],
            position_major(k_p)[None].astype(cache_swa_k.dtype),
            position_major(v_p)[None].astype(cache_swa_v.dtype),
            s_s[None],
            position_major(ck_s)[None],
            position_major(cv_s)[None])
```
